```python
import math
import jax, jax.numpy as jnp
from jax import lax
import numpy as np

D_MODEL = 1024
BATCH = 16
SEQ = 2048
DEPTH = 1

HEAD_DIM = 64
ATTN_WIDTH = D_MODEL // 2
N_ATTN_HEADS = ATTN_WIDTH // HEAD_DIM
DILATED_CONFIGS = ((128, 1), (512, 4), (2048, 16))
SGU_WIDTH = D_MODEL // 4
N_SGU_GROUPS = 4
SGU_GROUP = SGU_WIDTH // N_SGU_GROUPS
SGU_CHUNK = 128
MEM_WIDTH = D_MODEL // 4
N_MEM_HEADS = 4
MEM_HEAD_DIM = MEM_WIDTH // N_MEM_HEADS
N_MEM = 256
MIX_WIDTH = ATTN_WIDTH + SGU_WIDTH + MEM_WIDTH
IN_COLS = 4 * ATTN_WIDTH + 3 * SGU_WIDTH + 2 * MEM_WIDTH
EPS = 1e-6
NEG_INF = -1e30

kernel_name = "hybrid_dilated_sgu_memory_encoder"


def rmsnorm(x, g):
    xf = x.astype(jnp.float32)
    y = xf * lax.rsqrt(jnp.mean(xf * xf, axis=-1, keepdims=True) + EPS) * g.astype(jnp.float32)
    return y.astype(x.dtype)


def alibi_slopes(n):
    return jnp.power(2.0, -8.0 * (jnp.arange(n, dtype=jnp.float32) + 1.0) / n)


def dilated_window_attention(q, k, v, slopes, window, dilation):
    B, S, H, E = q.shape
    radius = window // (2 * dilation)
    blk = radius
    L = S // dilation
    N = B * dilation
    nb = -(-L // blk)
    Lp = nb * blk

    def to_sub(t):
        return t.reshape(B, L, dilation, H, E).transpose(0, 2, 1, 3, 4).reshape(N, L, H, E)

    qs = jnp.pad(to_sub(q), ((0, 0), (0, Lp - L), (0, 0), (0, 0))).reshape(N, nb, blk, H, E)

    def neighbour_blocks(t):
        tp = jnp.pad(to_sub(t), ((0, 0), (blk, Lp - L + blk), (0, 0), (0, 0))).reshape(N, nb + 2, blk, H, E)
        return jnp.concatenate([tp[:, :-2], tp[:, 1:-1], tp[:, 2:]], axis=2)

    kb = neighbour_blocks(k)
    vb = neighbour_blocks(v)
    q_idx = jnp.arange(Lp).reshape(nb, blk)
    k_idx = (jnp.arange(nb) * blk)[:, None] - blk + jnp.arange(3 * blk)[None, :]
    rel = jnp.abs(k_idx[:, None, :] - q_idx[:, :, None])
    valid = (rel <= radius) & (k_idx[:, None, :] >= 0) & (k_idx[:, None, :] < L)
    dist = (rel * dilation).astype(jnp.float32)

    s = jnp.einsum('nbqhe,nbkhe->nhbqk', qs.astype(jnp.float32), kb.astype(jnp.float32)) * (E ** -0.5)
    s = s - slopes[None, :, None, None, None] * dist[None, None]
    s = jnp.where(valid[None, None], s, NEG_INF)
    m = jnp.max(s, axis=-1, keepdims=True)
    p = jnp.exp(s - m)
    l = jnp.sum(p, axis=-1)
    o = jnp.einsum('nhbqk,nbkhe->nbqhe', p, vb.astype(jnp.float32))
    o = o / jnp.transpose(l, (0, 2, 3, 1))[..., None]
    lse = m[..., 0] + jnp.log(l)

    o = o.reshape(N, Lp, H, E)[:, :L].reshape(B, dilation, L, H, E).transpose(0, 2, 1, 3, 4).reshape(B, S, H, E)
    lse = lse.reshape(N, H, Lp)[..., :L].reshape(B, dilation, H, L).transpose(0, 3, 1, 2).reshape(B, S, H)
    return o, lse


def mixture_of_dilations(q, k, v):
    slopes = alibi_slopes(q.shape[2])
    outs, lses = [], []
    for window, dilation in DILATED_CONFIGS:
        o, lse = dilated_window_attention(q, k, v, slopes, window, dilation)
        outs.append(o)
        lses.append(lse)
    w = jax.nn.softmax(jnp.stack(lses, axis=0), axis=0)
    return jnp.sum(w[..., None] * jnp.stack(outs, axis=0), axis=0)


def chunked_spatial_gating(u, v, g_v, w_s, b_s):
    B, S, _ = v.shape
    n = S // SGU_CHUNK
    vn = rmsnorm(v, g_v).reshape(B, n, SGU_CHUNK, N_SGU_GROUPS, SGU_GROUP).astype(jnp.float32)
    mixed = jnp.einsum('gts,bnsgc->bntgc', w_s.astype(jnp.float32), vn)
    mixed = mixed + b_s.astype(jnp.float32).T[None, None, :, :, None]
    return u.astype(jnp.float32) * mixed.reshape(B, S, SGU_WIDTH)


def memory_cross_attention(q, mem, g_mem, w_mem_kv):
    B, M, _ = mem.shape
    kv = rmsnorm(mem, g_mem) @ w_mem_kv
    k, v = jnp.split(kv, 2, axis=-1)
    k = k.reshape(B, M, N_MEM_HEADS, MEM_HEAD_DIM).astype(jnp.float32)
    v = v.reshape(B, M, N_MEM_HEADS, MEM_HEAD_DIM).astype(jnp.float32)
    s = jnp.einsum('bshe,bmhe->bhsm', q.astype(jnp.float32), k) * (MEM_HEAD_DIM ** -0.5)
    p = jax.nn.softmax(s, axis=-1)
    return jnp.einsum('bhsm,bmhe->bshe', p, v)


def hybrid_layer(x, mem, g_norm, w_in, w_s, b_s, g_v, g_mem, w_mem_kv, w_out):
    B, S, _ = x.shape
    h = rmsnorm(x, g_norm)
    proj = h @ w_in
    offs = np.cumsum([ATTN_WIDTH] * 4 + [SGU_WIDTH] * 3 + [MEM_WIDTH])
    qa, ka, va, za, ub, vb, zb, qm, zm = jnp.split(proj, list(offs), axis=-1)

    shp = (B, S, N_ATTN_HEADS, HEAD_DIM)
    a = mixture_of_dilations(qa.reshape(shp), ka.reshape(shp), va.reshape(shp)).reshape(B, S, ATTN_WIDTH)
    sg = chunked_spatial_gating(jax.nn.gelu(ub), jax.nn.gelu(vb), g_v, w_s, b_s)
    mo = memory_cross_attention(qm.reshape(B, S, N_MEM_HEADS, MEM_HEAD_DIM), mem, g_mem, w_mem_kv)
    mo = mo.reshape(B, S, MEM_WIDTH)

    gated = jnp.concatenate([
        jax.nn.silu(za.astype(jnp.float32)) * a,
        jax.nn.silu(zb.astype(jnp.float32)) * sg,
        jax.nn.silu(zm.astype(jnp.float32)) * mo,
    ], axis=-1).astype(x.dtype)
    return x + gated @ w_out


def setup_inputs(seed: int = 0) -> dict:
    key = jax.random.key(seed)
    ks = jax.random.split(key, 12)
    f32 = jnp.float32
    x = jax.random.normal(ks[0], (BATCH, SEQ, D_MODEL), f32)
    mem = jax.random.normal(ks[1], (BATCH, N_MEM, D_MODEL), f32)
    g_norm = 1.0 + 0.02 * jax.random.normal(ks[2], (DEPTH, D_MODEL), f32)
    w_in = jax.random.normal(ks[3], (DEPTH, D_MODEL, IN_COLS), f32) * D_MODEL ** -0.5
    w_sgu_spatial = jax.random.normal(ks[4], (DEPTH, N_SGU_GROUPS, SGU_CHUNK, SGU_CHUNK), f32) * (0.5 * SGU_CHUNK ** -0.5)
    b_sgu_spatial = 1.0 + 0.02 * jax.random.normal(ks[5], (DEPTH, N_SGU_GROUPS, SGU_CHUNK), f32)
    g_sgu_v = 1.0 + 0.02 * jax.random.normal(ks[6], (DEPTH, SGU_WIDTH), f32)
    g_mem = 1.0 + 0.02 * jax.random.normal(ks[7], (DEPTH, D_MODEL), f32)
    w_mem_kv = jax.random.normal(ks[8], (DEPTH, D_MODEL, 2 * MEM_WIDTH), f32) * D_MODEL ** -0.5
    w_out = jax.random.normal(ks[9], (DEPTH, MIX_WIDTH, D_MODEL), f32) * MIX_WIDTH ** -0.5
    g_final = 1.0 + 0.02 * jax.random.normal(ks[10], (D_MODEL,), f32)
    return {"x": x, "mem": mem, "g_norm": g_norm, "w_in": w_in,
            "w_sgu_spatial": w_sgu_spatial, "b_sgu_spatial": b_sgu_spatial, "g_sgu_v": g_sgu_v,
            "g_mem": g_mem, "w_mem_kv": w_mem_kv, "w_out": w_out, "g_final": g_final}


def reference(x, mem, g_norm, w_in, w_sgu_spatial, b_sgu_spatial, g_sgu_v, g_mem, w_mem_kv, w_out, g_final):
    h = x
    for layer in range(DEPTH):
        h = hybrid_layer(h, mem, g_norm[layer], w_in[layer], w_sgu_spatial[layer], b_sgu_spatial[layer],
                         g_sgu_v[layer], g_mem[layer], w_mem_kv[layer], w_out[layer])
    return rmsnorm(h, g_final)
```

```python
import functools

import jax
import jax.numpy as jnp
from jax import lax
from jax.experimental import pallas as pl
from jax.experimental.pallas import tpu as pltpu

F32 = jnp.float32
BF16 = jnp.bfloat16

EPS = 1e-6
NEG_INF = -1e30

HEAD_DIM = 64
N_ATTN_HEADS = 8
ATTN_WIDTH = HEAD_DIM * N_ATTN_HEADS
DILATED_CONFIGS = ((128, 1), (512, 4), (2048, 16))
RADIUS = 64
SGU_WIDTH = 256
N_SGU_GROUPS = 4
SGU_GROUP = SGU_WIDTH // N_SGU_GROUPS
SGU_CHUNK = 128
MEM_WIDTH = 256
N_MEM_HEADS = 4
MEM_HEAD_DIM = MEM_WIDTH // N_MEM_HEADS

LANES = 128
Q_BLOCK = 128
KEY_WINDOW = 2 * Q_BLOCK
HEADS_PER_STEP = LANES // HEAD_DIM
VMEM_LIMIT_BYTES = 56 * 1024 * 1024


def _rms(x, g):
    return x * lax.rsqrt(jnp.mean(x * x, axis=-1, keepdims=True) + EPS) * g


def _silu(x):
    return x / (1.0 + jnp.exp(-x))


def _memkv_kernel(mem_ref, g_ref, w_ref, o_ref):
    h = _rms(mem_ref[0], g_ref[...]).astype(BF16)
    o_ref[0] = jnp.dot(h, w_ref[...], preferred_element_type=F32).astype(BF16)


def _memkv(mem, g_mem, w_kv):
    b, m, d = mem.shape
    n = w_kv.shape[1]
    return pl.pallas_call(
        _memkv_kernel,
        grid=(b,),
        in_specs=[
            pl.BlockSpec((1, m, d), lambda i: (i, 0, 0)),
            pl.BlockSpec((1, d), lambda i: (0, 0)),
            pl.BlockSpec((d, n), lambda i: (0, 0)),
        ],
        out_specs=pl.BlockSpec((1, m, n), lambda i: (i, 0, 0)),
        out_shape=jax.ShapeDtypeStruct((b, m, n), BF16),
        compiler_params=pltpu.CompilerParams(
            dimension_semantics=("arbitrary",), vmem_limit_bytes=VMEM_LIMIT_BYTES),
        name="memkv",
    )(mem, g_mem, w_kv)


def _inproj_kernel(x_ref, g_ref, w_ref, q_ref, k_ref, v_ref, r_ref):
    h = _rms(x_ref[...], g_ref[...]).astype(BF16)
    a = ATTN_WIDTH
    q_ref[...] = jnp.dot(h, w_ref[:, 0:a], preferred_element_type=F32) * (HEAD_DIM ** -0.5)
    k_ref[...] = jnp.dot(h, w_ref[:, a:2 * a], preferred_element_type=F32)
    v_ref[...] = jnp.dot(h, w_ref[:, 2 * a:3 * a], preferred_element_type=F32)
    r_ref[...] = jnp.dot(h, w_ref[:, 3 * a:], preferred_element_type=F32).astype(BF16)


def _inproj(x2d, g_norm, w_in, tm):
    n, d = x2d.shape
    cols = w_in.shape[1]
    rest = cols - 3 * ATTN_WIDTH
    row = lambda i: (i, 0)
    fixed = lambda i: (0, 0)
    return pl.pallas_call(
        _inproj_kernel,
        grid=(n // tm,),
        in_specs=[
            pl.BlockSpec((tm, d), row),
            pl.BlockSpec((1, d), fixed),
            pl.BlockSpec((d, cols), fixed),
        ],
        out_specs=[
            pl.BlockSpec((tm, ATTN_WIDTH), row),
            pl.BlockSpec((tm, ATTN_WIDTH), row),
            pl.BlockSpec((tm, ATTN_WIDTH), row),
            pl.BlockSpec((tm, rest), row),
        ],
        out_shape=[
            jax.ShapeDtypeStruct((n, ATTN_WIDTH), F32),
            jax.ShapeDtypeStruct((n, ATTN_WIDTH), F32),
            jax.ShapeDtypeStruct((n, ATTN_WIDTH), F32),
            jax.ShapeDtypeStruct((n, rest), BF16),
        ],
        compiler_params=pltpu.CompilerParams(
            dimension_semantics=("arbitrary",), vmem_limit_bytes=VMEM_LIMIT_BYTES),
        name="inproj",
    )(x2d, g_norm, w_in)


def _bias_tile(offset, dilation, first_head):
    shape = (HEADS_PER_STEP * Q_BLOCK, KEY_WINDOW)
    row = lax.broadcasted_iota(jnp.int32, shape, 0)
    col = lax.broadcasted_iota(jnp.int32, shape, 1)
    second = row >= Q_BLOCK
    qpos = jnp.where(second, row - Q_BLOCK, row) + offset
    rel = jnp.abs(qpos - col)
    head = (first_head + jnp.where(second, 1, 0)).astype(F32)
    slope = jnp.exp2(-8.0 * (head + 1.0) / N_ATTN_HEADS)
    dist = (rel * dilation).astype(F32)
    return jnp.where(rel <= RADIUS, -slope * dist, NEG_INF)


def _attend(q2, k_win, v_win, bias):
    s = lax.dot_general(q2, k_win, (((1,), (1,)), ((), ())), preferred_element_type=F32) + bias
    m = jnp.max(s, axis=1, keepdims=True)
    p = jnp.exp(s - m)
    l = jnp.sum(p, axis=1, keepdims=True)
    pv = jnp.dot(p.astype(BF16), v_win, preferred_element_type=F32)
    o = pv / l
    lse = m + jnp.log(l)
    lane = lax.broadcasted_iota(jnp.int32, (Q_BLOCK, LANES), 1)
    first = lane < HEAD_DIM
    o_sel = jnp.where(first, o[:Q_BLOCK], o[Q_BLOCK:])
    lse_sel = jnp.where(first, lse[:Q_BLOCK], lse[Q_BLOCK:])
    return o_sel, lse_sel


def _attn_kernel(q_ref, k_ref, v_ref, a_ref, q_scr, k_scr, v_scr, bias_scr, o_scr, lse_scr, *, seq):
    n_cfg = len(DILATED_CONFIGS)
    first_head = pl.program_id(1) * HEADS_PER_STEP

    offsets = (0, RADIUS, Q_BLOCK)
    for c, (_, dil) in enumerate(DILATED_CONFIGS):
        for o, off in enumerate(offsets):
            bias_scr[c * len(offsets) + o] = _bias_tile(off, dil, first_head)

    lane = lax.broadcasted_iota(jnp.int32, (1, LANES), 1)
    first = lane < HEAD_DIM
    for c, (_, dil) in enumerate(DILATED_CONFIGS):
        cls = seq // dil
        for r in range(dil):
            dst = pl.ds(r * cls, cls)
            if dil == 1:
                src = pl.ds(0, cls)
            else:
                src = pl.ds(r, cls, stride=dil)
            q = q_ref[0, src, :]
            q_scr[c, 0, dst, :] = jnp.where(first, q, 0.0).astype(BF16)
            q_scr[c, 1, dst, :] = jnp.where(first, 0.0, q).astype(BF16)
            k_scr[c, dst, :] = k_ref[0, src, :].astype(BF16)
            v_scr[c, dst, :] = v_ref[0, src, :].astype(BF16)

    for c, (_, dil) in enumerate(DILATED_CONFIGS):
        cls = seq // dil
        blocks = cls // Q_BLOCK
        window = min(KEY_WINDOW, cls)

        def body(idx, carry, c=c, dil=dil, cls=cls, blocks=blocks, window=window):
            r = idx // blocks
            j = idx - r * blocks
            base = r * cls
            qs = pl.multiple_of(base + j * Q_BLOCK, Q_BLOCK)
            ws = pl.multiple_of(base + jnp.clip(j * Q_BLOCK - RADIUS, 0, cls - window), RADIUS)
            if blocks == 1:
                sel = 0
            else:
                sel = jnp.where(j == 0, 0, jnp.where(j == blocks - 1, 2, 1))
            q2 = jnp.concatenate([q_scr[c, 0, pl.ds(qs, Q_BLOCK), :], q_scr[c, 1, pl.ds(qs, Q_BLOCK), :]], axis=0)
            k_win = k_scr[c, pl.ds(ws, window), :]
            v_win = v_scr[c, pl.ds(ws, window), :]
            bias = bias_scr[c * len(offsets) + sel]
            if window < KEY_WINDOW:
                bias = bias[:, :window]
            o, lse = _attend(q2, k_win, v_win, bias)
            start = j * (Q_BLOCK * dil) + r
            if dil == 1:
                rows = pl.ds(pl.multiple_of(start, Q_BLOCK), Q_BLOCK)
            else:
                rows = pl.ds(start, Q_BLOCK, stride=dil)
            o_scr[c, rows, :] = o
            lse_scr[c, rows, :] = lse
            return carry

        lax.fori_loop(0, dil * blocks, body, 0)

    def merge(i, carry):
        rows = pl.ds(pl.multiple_of(i * Q_BLOCK, Q_BLOCK), Q_BLOCK)
        lses = [lse_scr[c, rows, :] for c in range(n_cfg)]
        top = functools.reduce(jnp.maximum, lses)
        es = [jnp.exp(x - top) for x in lses]
        num = functools.reduce(lambda a, b: a + b, [e * o_scr[c, rows, :] for c, e in enumerate(es)])
        den = functools.reduce(lambda a, b: a + b, es)
        a_ref[0, rows, :] = (num / den).astype(a_ref.dtype)
        return carry

    lax.fori_loop(0, seq // Q_BLOCK, merge, 0)


def _attn(q, k, v):
    b, s, w = q.shape
    n_cfg = len(DILATED_CONFIGS)
    blk = pl.BlockSpec((1, s, LANES), lambda i, j: (i, 0, j))
    return pl.pallas_call(
        functools.partial(_attn_kernel, seq=s),
        grid=(b, w // LANES),
        in_specs=[blk, blk, blk],
        out_specs=blk,
        out_shape=jax.ShapeDtypeStruct((b, s, w), BF16),
        scratch_shapes=[
            pltpu.VMEM((n_cfg, HEADS_PER_STEP, s, LANES), BF16),
            pltpu.VMEM((n_cfg, s, LANES), BF16),
            pltpu.VMEM((n_cfg, s, LANES), BF16),
            pltpu.VMEM((n_cfg * 3, HEADS_PER_STEP * Q_BLOCK, KEY_WINDOW), F32),
            pltpu.VMEM((n_cfg, s, LANES), F32),
            pltpu.VMEM((n_cfg, s, LANES), F32),
        ],
        compiler_params=pltpu.CompilerParams(
            dimension_semantics=("arbitrary", "arbitrary"), vmem_limit_bytes=VMEM_LIMIT_BYTES),
        name="attn",
    )(q, k, v)


def _mix_kernel(a_ref, r_ref, kv_ref, x_ref, ws_ref, bs_ref, gv_ref, wo_ref, gf_ref, o_ref):
    tm = x_ref.shape[1]
    c0 = 0
    za = r_ref[0, :, c0:c0 + ATTN_WIDTH].astype(F32); c0 += ATTN_WIDTH
    ub = r_ref[0, :, c0:c0 + SGU_WIDTH].astype(F32); c0 += SGU_WIDTH
    vb = r_ref[0, :, c0:c0 + SGU_WIDTH].astype(F32); c0 += SGU_WIDTH
    zb = r_ref[0, :, c0:c0 + SGU_WIDTH].astype(F32); c0 += SGU_WIDTH
    qm = r_ref[0, :, c0:c0 + MEM_WIDTH]; c0 += MEM_WIDTH
    zm = r_ref[0, :, c0:c0 + MEM_WIDTH].astype(F32)

    ga = (_silu(za) * a_ref[0].astype(F32)).astype(BF16)

    u = jax.nn.gelu(ub)
    vn = _rms(jax.nn.gelu(vb), gv_ref[...])
    group = lax.broadcasted_iota(jnp.int32, (SGU_CHUNK, SGU_WIDTH), 1) // SGU_GROUP
    mixed = []
    for c in range(tm // SGU_CHUNK):
        vc = vn[c * SGU_CHUNK:(c + 1) * SGU_CHUNK]
        acc = bs_ref[...]
        for g in range(N_SGU_GROUPS):
            vg = jnp.where(group == g, vc, 0.0).astype(BF16)
            acc = acc + jnp.dot(ws_ref[g], vg, preferred_element_type=F32)
        mixed.append(acc)
    mixed = jnp.concatenate(mixed, axis=0)
    gb = (_silu(zb) * (u * mixed)).astype(BF16)

    k_mem = kv_ref[0, :, 0:MEM_WIDTH]
    v_mem = kv_ref[0, :, MEM_WIDTH:2 * MEM_WIDTH]
    mhead = lax.broadcasted_iota(jnp.int32, (tm, MEM_WIDTH), 1) // MEM_HEAD_DIM
    zero = jnp.zeros_like(qm)
    q4 = jnp.concatenate([jnp.where(mhead == h, qm, zero) for h in range(N_MEM_HEADS)], axis=0)
    s = lax.dot_general(q4, k_mem, (((1,), (1,)), ((), ())), preferred_element_type=F32) * (MEM_HEAD_DIM ** -0.5)
    m = jnp.max(s, axis=1, keepdims=True)
    p = jnp.exp(s - m)
    l = jnp.sum(p, axis=1, keepdims=True)
    pv = jnp.dot(p.astype(BF16), v_mem, preferred_element_type=F32) / l
    mo = pv[0:tm]
    for h in range(1, N_MEM_HEADS):
        mo = jnp.where(mhead == h, pv[h * tm:(h + 1) * tm], mo)
    gm = (_silu(zm) * mo).astype(BF16)

    e1 = ATTN_WIDTH
    e2 = e1 + SGU_WIDTH
    y = jnp.dot(ga, wo_ref[0:e1, :], preferred_element_type=F32)
    y = y + jnp.dot(gb, wo_ref[e1:e2, :], preferred_element_type=F32)
    y = y + jnp.dot(gm, wo_ref[e2:, :], preferred_element_type=F32)
    o_ref[0] = _rms(x_ref[0] + y, gf_ref[...])


def _mix(a, rest, kv, x, w_s, b_tile, g_v, w_out, g_final, tm):
    b, s, d = x.shape
    tile = lambda w: pl.BlockSpec((1, tm, w), lambda i, j: (i, j, 0))
    fixed2 = lambda shp: pl.BlockSpec(shp, lambda i, j: (0, 0))
    return pl.pallas_call(
        _mix_kernel,
        grid=(b, s // tm),
        in_specs=[
            tile(a.shape[2]),
            tile(rest.shape[2]),
            pl.BlockSpec((1,) + kv.shape[1:], lambda i, j: (i, 0, 0)),
            tile(d),
            pl.BlockSpec(w_s.shape, lambda i, j: (0, 0, 0)),
            fixed2(b_tile.shape),
            fixed2(g_v.shape),
            fixed2(w_out.shape),
            fixed2(g_final.shape),
        ],
        out_specs=tile(d),
        out_shape=jax.ShapeDtypeStruct((b, s, d), x.dtype),
        compiler_params=pltpu.CompilerParams(
            dimension_semantics=("arbitrary", "arbitrary"), vmem_limit_bytes=VMEM_LIMIT_BYTES),
        name="mix",
    )(a, rest, kv, x, w_s, b_tile, g_v, w_out, g_final)


def kernel(x, mem, g_norm, w_in, w_sgu_spatial, b_sgu_spatial, g_sgu_v, g_mem, w_mem_kv, w_out, g_final):
    assert g_norm.shape[0] == 1, "the final norm is fused into the single layer's last kernel"
    b, s, d = x.shape
    kv = _memkv(mem, g_mem[0][None, :], w_mem_kv[0].astype(BF16))
    q, k, v, rest = _inproj(x.reshape(b * s, d), g_norm[0][None, :], w_in[0].astype(BF16), tm=512)
    a = _attn(q.reshape(b, s, -1), k.reshape(b, s, -1), v.reshape(b, s, -1))
    b_tile = jnp.repeat(b_sgu_spatial[0].T, SGU_GROUP, axis=1)
    return _mix(a, rest.reshape(b, s, -1), kv, x, w_sgu_spatial[0].astype(BF16), b_tile,
                g_sgu_v[0][None, :], w_out[0].astype(BF16), g_final[None, :], tm=256)
```

```python
import functools

import jax
import jax.numpy as jnp
from jax import lax
from jax.experimental import pallas as pl
from jax.experimental.pallas import tpu as pltpu

F32 = jnp.float32
BF16 = jnp.bfloat16

EPS = 1e-6
NEG_INF = -1e30

HEAD_DIM = 64
N_ATTN_HEADS = 8
ATTN_WIDTH = HEAD_DIM * N_ATTN_HEADS
DILATED_CONFIGS = ((128, 1), (512, 4), (2048, 16))
RADIUS = 64
SGU_WIDTH = 256
N_SGU_GROUPS = 4
SGU_GROUP = SGU_WIDTH // N_SGU_GROUPS
SGU_CHUNK = 128
MEM_WIDTH = 256
N_MEM_HEADS = 4
MEM_HEAD_DIM = MEM_WIDTH // N_MEM_HEADS

LANES = 128
Q_BLOCK = 128
KEY_WINDOW = 2 * Q_BLOCK
HEADS_PER_STEP = LANES // HEAD_DIM
VMEM_LIMIT_BYTES = 56 * 1024 * 1024


def _rms(x, g):
    return x * lax.rsqrt(jnp.mean(x * x, axis=-1, keepdims=True) + EPS) * g


def _silu(x):
    return x / (1.0 + jnp.exp(-x))


def _memkv_kernel(mem_ref, g_ref, w_ref, o_ref):
    h = _rms(mem_ref[0], g_ref[...]).astype(BF16)
    o_ref[0] = jnp.dot(h, w_ref[...], preferred_element_type=F32).astype(BF16)


def _memkv(mem, g_mem, w_kv):
    b, m, d = mem.shape
    n = w_kv.shape[1]
    return pl.pallas_call(
        _memkv_kernel,
        grid=(b,),
        in_specs=[
            pl.BlockSpec((1, m, d), lambda i: (i, 0, 0)),
            pl.BlockSpec((1, d), lambda i: (0, 0)),
            pl.BlockSpec((d, n), lambda i: (0, 0)),
        ],
        out_specs=pl.BlockSpec((1, m, n), lambda i: (i, 0, 0)),
        out_shape=jax.ShapeDtypeStruct((b, m, n), BF16),
        compiler_params=pltpu.CompilerParams(
            dimension_semantics=("arbitrary",), vmem_limit_bytes=VMEM_LIMIT_BYTES),
        name="memkv",
    )(mem, g_mem, w_kv)


def _inproj_kernel(x_ref, g_ref, w_ref, q_ref, k_ref, v_ref, r_ref):
    h = _rms(x_ref[...], g_ref[...]).astype(BF16)
    a = ATTN_WIDTH
    q_ref[...] = jnp.dot(h, w_ref[:, 0:a], preferred_element_type=F32) * (HEAD_DIM ** -0.5)
    k_ref[...] = jnp.dot(h, w_ref[:, a:2 * a], preferred_element_type=F32)
    v_ref[...] = jnp.dot(h, w_ref[:, 2 * a:3 * a], preferred_element_type=F32)
    r_ref[...] = jnp.dot(h, w_ref[:, 3 * a:], preferred_element_type=F32).astype(BF16)


def _inproj(x2d, g_norm, w_in, tm):
    n, d = x2d.shape
    cols = w_in.shape[1]
    rest = cols - 3 * ATTN_WIDTH
    row = lambda i: (i, 0)
    fixed = lambda i: (0, 0)
    return pl.pallas_call(
        _inproj_kernel,
        grid=(n // tm,),
        in_specs=[
            pl.BlockSpec((tm, d), row),
            pl.BlockSpec((1, d), fixed),
            pl.BlockSpec((d, cols), fixed),
        ],
        out_specs=[
            pl.BlockSpec((tm, ATTN_WIDTH), row),
            pl.BlockSpec((tm, ATTN_WIDTH), row),
            pl.BlockSpec((tm, ATTN_WIDTH), row),
            pl.BlockSpec((tm, rest), row),
        ],
        out_shape=[
            jax.ShapeDtypeStruct((n, ATTN_WIDTH), F32),
            jax.ShapeDtypeStruct((n, ATTN_WIDTH), F32),
            jax.ShapeDtypeStruct((n, ATTN_WIDTH), F32),
            jax.ShapeDtypeStruct((n, rest), BF16),
        ],
        compiler_params=pltpu.CompilerParams(
            dimension_semantics=("arbitrary",), vmem_limit_bytes=VMEM_LIMIT_BYTES),
        name="inproj",
    )(x2d, g_norm, w_in)


def _bias_tile(offset, dilation, first_head):
    shape = (HEADS_PER_STEP * Q_BLOCK, KEY_WINDOW)
    row = lax.broadcasted_iota(jnp.int32, shape, 0)
    col = lax.broadcasted_iota(jnp.int32, shape, 1)
    second = row >= Q_BLOCK
    qpos = jnp.where(second, row - Q_BLOCK, row) + offset
    rel = jnp.abs(qpos - col)
    head = (first_head + jnp.where(second, 1, 0)).astype(F32)
    slope = jnp.exp2(-8.0 * (head + 1.0) / N_ATTN_HEADS)
    dist = (rel * dilation).astype(F32)
    return jnp.where(rel <= RADIUS, -slope * dist, NEG_INF)


def _attend(q2, k_win, v_win, bias):
    s = lax.dot_general(q2, k_win, (((1,), (1,)), ((), ())), preferred_element_type=F32) + bias
    m = jnp.max(s, axis=1, keepdims=True)
    p = jnp.exp(s - m)
    l = jnp.sum(p, axis=1, keepdims=True)
    pv = jnp.dot(p.astype(BF16), v_win, preferred_element_type=F32)
    o = pv / l
    lse = m + jnp.log(l)
    lane = lax.broadcasted_iota(jnp.int32, (Q_BLOCK, LANES), 1)
    first = lane < HEAD_DIM
    o_sel = jnp.where(first, o[:Q_BLOCK], o[Q_BLOCK:])
    lse_sel = jnp.where(first, lse[:Q_BLOCK], lse[Q_BLOCK:])
    return o_sel, lse_sel


def _attn_kernel(q_ref, k_ref, v_ref, a_ref, q_scr, k_scr, v_scr, bias_scr, o_scr, lse_scr, *, seq):
    n_cfg = len(DILATED_CONFIGS)
    first_head = pl.program_id(1) * HEADS_PER_STEP

    offsets = (0, RADIUS, Q_BLOCK)
    for c, (_, dil) in enumerate(DILATED_CONFIGS):
        for o, off in enumerate(offsets):
            bias_scr[c * len(offsets) + o] = _bias_tile(off, dil, first_head)

    lane = lax.broadcasted_iota(jnp.int32, (1, LANES), 1)
    first = lane < HEAD_DIM
    for c, (_, dil) in enumerate(DILATED_CONFIGS):
        cls = seq // dil
        for r in range(dil):
            dst = pl.ds(r * cls, cls)
            if dil == 1:
                src = pl.ds(0, cls)
            else:
                src = pl.ds(r, cls, stride=dil)
            q = q_ref[0, src, :]
            q_scr[c, 0, dst, :] = jnp.where(first, q, 0.0).astype(BF16)
            q_scr[c, 1, dst, :] = jnp.where(first, 0.0, q).astype(BF16)
            k_scr[c, dst, :] = k_ref[0, src, :].astype(BF16)
            v_scr[c, dst, :] = v_ref[0, src, :].astype(BF16)

    for c, (_, dil) in enumerate(DILATED_CONFIGS):
        cls = seq // dil
        blocks = cls // Q_BLOCK
        window = min(KEY_WINDOW, cls)

        def body(idx, carry, c=c, dil=dil, cls=cls, blocks=blocks, window=window):
            r = idx // blocks
            j = idx - r * blocks
            base = r * cls
            qs = pl.multiple_of(base + j * Q_BLOCK, Q_BLOCK)
            ws = pl.multiple_of(base + jnp.clip(j * Q_BLOCK - RADIUS, 0, cls - window), RADIUS)
            if blocks == 1:
                sel = 0
            else:
                sel = jnp.where(j == 0, 0, jnp.where(j == blocks - 1, 2, 1))
            q2 = jnp.concatenate([q_scr[c, 0, pl.ds(qs, Q_BLOCK), :], q_scr[c, 1, pl.ds(qs, Q_BLOCK), :]], axis=0)
            k_win = k_scr[c, pl.ds(ws, window), :]
            v_win = v_scr[c, pl.ds(ws, window), :]
            bias = bias_scr[c * len(offsets) + sel]
            if window < KEY_WINDOW:
                bias = bias[:, :window]
            o, lse = _attend(q2, k_win, v_win, bias)
            start = j * (Q_BLOCK * dil) + r
            if dil == 1:
                rows = pl.ds(pl.multiple_of(start, Q_BLOCK), Q_BLOCK)
            else:
                rows = pl.ds(start, Q_BLOCK, stride=dil)
            o_scr[c, rows, :] = o
            lse_scr[c, rows, :] = lse
            return carry

        lax.fori_loop(0, dil * blocks, body, 0, unroll=4)

    def merge(i, carry):
        rows = pl.ds(pl.multiple_of(i * Q_BLOCK, Q_BLOCK), Q_BLOCK)
        lses = [lse_scr[c, rows, :] for c in range(n_cfg)]
        top = functools.reduce(jnp.maximum, lses)
        es = [jnp.exp(x - top) for x in lses]
        num = functools.reduce(lambda a, b: a + b, [e * o_scr[c, rows, :] for c, e in enumerate(es)])
        den = functools.reduce(lambda a, b: a + b, es)
        a_ref[0, rows, :] = (num / den).astype(a_ref.dtype)
        return carry

    lax.fori_loop(0, seq // Q_BLOCK, merge, 0)


def _attn(q, k, v):
    b, s, w = q.shape
    n_cfg = len(DILATED_CONFIGS)
    blk = pl.BlockSpec((1, s, LANES), lambda i, j: (i, 0, j))
    return pl.pallas_call(
        functools.partial(_attn_kernel, seq=s),
        grid=(b, w // LANES),
        in_specs=[blk, blk, blk],
        out_specs=blk,
        out_shape=jax.ShapeDtypeStruct((b, s, w), BF16),
        scratch_shapes=[
            pltpu.VMEM((n_cfg, HEADS_PER_STEP, s, LANES), BF16),
            pltpu.VMEM((n_cfg, s, LANES), BF16),
            pltpu.VMEM((n_cfg, s, LANES), BF16),
            pltpu.VMEM((n_cfg * 3, HEADS_PER_STEP * Q_BLOCK, KEY_WINDOW), F32),
            pltpu.VMEM((n_cfg, s, LANES), F32),
            pltpu.VMEM((n_cfg, s, LANES), F32),
        ],
        compiler_params=pltpu.CompilerParams(
            dimension_semantics=("arbitrary", "arbitrary"), vmem_limit_bytes=VMEM_LIMIT_BYTES),
        name="attn",
    )(q, k, v)


def _mix_kernel(a_ref, r_ref, kv_ref, x_ref, ws_ref, bs_ref, gv_ref, wo_ref, gf_ref, o_ref):
    tm = x_ref.shape[1]
    c0 = 0
    za = r_ref[0, :, c0:c0 + ATTN_WIDTH].astype(F32); c0 += ATTN_WIDTH
    ub = r_ref[0, :, c0:c0 + SGU_WIDTH].astype(F32); c0 += SGU_WIDTH
    vb = r_ref[0, :, c0:c0 + SGU_WIDTH].astype(F32); c0 += SGU_WIDTH
    zb = r_ref[0, :, c0:c0 + SGU_WIDTH].astype(F32); c0 += SGU_WIDTH
    qm = r_ref[0, :, c0:c0 + MEM_WIDTH]; c0 += MEM_WIDTH
    zm = r_ref[0, :, c0:c0 + MEM_WIDTH].astype(F32)

    ga = (_silu(za) * a_ref[0].astype(F32)).astype(BF16)

    u = jax.nn.gelu(ub)
    vn = _rms(jax.nn.gelu(vb), gv_ref[...])
    group = lax.broadcasted_iota(jnp.int32, (SGU_CHUNK, SGU_WIDTH), 1) // SGU_GROUP
    mixed = []
    for c in range(tm // SGU_CHUNK):
        vc = vn[c * SGU_CHUNK:(c + 1) * SGU_CHUNK]
        acc = bs_ref[...]
        for g in range(N_SGU_GROUPS):
            vg = jnp.where(group == g, vc, 0.0).astype(BF16)
            acc = acc + jnp.dot(ws_ref[g], vg, preferred_element_type=F32)
        mixed.append(acc)
    mixed = jnp.concatenate(mixed, axis=0)
    gb = (_silu(zb) * (u * mixed)).astype(BF16)

    k_mem = kv_ref[0, :, 0:MEM_WIDTH]
    v_mem = kv_ref[0, :, MEM_WIDTH:2 * MEM_WIDTH]
    mhead = lax.broadcasted_iota(jnp.int32, (tm, MEM_WIDTH), 1) // MEM_HEAD_DIM
    zero = jnp.zeros_like(qm)
    q4 = jnp.concatenate([jnp.where(mhead == h, qm, zero) for h in range(N_MEM_HEADS)], axis=0)
    s = lax.dot_general(q4, k_mem, (((1,), (1,)), ((), ())), preferred_element_type=F32) * (MEM_HEAD_DIM ** -0.5)
    m = jnp.max(s, axis=1, keepdims=True)
    p = jnp.exp(s - m)
    l = jnp.sum(p, axis=1, keepdims=True)
    pv = jnp.dot(p.astype(BF16), v_mem, preferred_element_type=F32) / l
    mo = pv[0:tm]
    for h in range(1, N_MEM_HEADS):
        mo = jnp.where(mhead == h, pv[h * tm:(h + 1) * tm], mo)
    gm = (_silu(zm) * mo).astype(BF16)

    e1 = ATTN_WIDTH
    e2 = e1 + SGU_WIDTH
    y = jnp.dot(ga, wo_ref[0:e1, :], preferred_element_type=F32)
    y = y + jnp.dot(gb, wo_ref[e1:e2, :], preferred_element_type=F32)
    y = y + jnp.dot(gm, wo_ref[e2:, :], preferred_element_type=F32)
    o_ref[0] = _rms(x_ref[0] + y, gf_ref[...])


def _mix(a, rest, kv, x, w_s, b_tile, g_v, w_out, g_final, tm):
    b, s, d = x.shape
    tile = lambda w: pl.BlockSpec((1, tm, w), lambda i, j: (i, j, 0))
    fixed2 = lambda shp: pl.BlockSpec(shp, lambda i, j: (0, 0))
    return pl.pallas_call(
        _mix_kernel,
        grid=(b, s // tm),
        in_specs=[
            tile(a.shape[2]),
            tile(rest.shape[2]),
            pl.BlockSpec((1,) + kv.shape[1:], lambda i, j: (i, 0, 0)),
            tile(d),
            pl.BlockSpec(w_s.shape, lambda i, j: (0, 0, 0)),
            fixed2(b_tile.shape),
            fixed2(g_v.shape),
            fixed2(w_out.shape),
            fixed2(g_final.shape),
        ],
        out_specs=tile(d),
        out_shape=jax.ShapeDtypeStruct((b, s, d), x.dtype),
        compiler_params=pltpu.CompilerParams(
            dimension_semantics=("arbitrary", "arbitrary"), vmem_limit_bytes=VMEM_LIMIT_BYTES),
        name="mix",
    )(a, rest, kv, x, w_s, b_tile, g_v, w_out, g_final)


def kernel(x, mem, g_norm, w_in, w_sgu_spatial, b_sgu_spatial, g_sgu_v, g_mem, w_mem_kv, w_out, g_final):
    assert g_norm.shape[0] == 1, "the final norm is fused into the single layer's last kernel"
    b, s, d = x.shape
    kv = _memkv(mem, g_mem[0][None, :], w_mem_kv[0].astype(BF16))
    q, k, v, rest = _inproj(x.reshape(b * s, d), g_norm[0][None, :], w_in[0].astype(BF16), tm=512)
    a = _attn(q.reshape(b, s, -1), k.reshape(b, s, -1), v.reshape(b, s, -1))
    b_tile = jnp.repeat(b_sgu_spatial[0].T, SGU_GROUP, axis=1)
    return _mix(a, rest.reshape(b, s, -1), kv, x, w_sgu_spatial[0].astype(BF16), b_tile,
                g_sgu_v[0][None, :], w_out[0].astype(BF16), g_final[None, :], tm=256)
```

```python
import functools
from typing import NamedTuple

import jax
import jax.numpy as jnp
from jax import lax
from jax.experimental import pallas as pl
from jax.experimental.pallas import tpu as pltpu

F32 = jnp.float32
BF16 = jnp.bfloat16

EPS = 1e-6
NEG_INF = -1e30

HEAD_DIM = 64
N_ATTN_HEADS = 8
ATTN_WIDTH = HEAD_DIM * N_ATTN_HEADS
DILATED_CONFIGS = ((128, 1), (512, 4), (2048, 16))
RADIUS = 64
SGU_WIDTH = 256
N_SGU_GROUPS = 4
SGU_GROUP = SGU_WIDTH // N_SGU_GROUPS
SGU_CHUNK = 128
MEM_WIDTH = 256
N_MEM_HEADS = 4
MEM_HEAD_DIM = MEM_WIDTH // N_MEM_HEADS

LANES = 128
SUBLANES = 8
Q_BLOCK = 128
KEY_WINDOW = 2 * Q_BLOCK
HEADS_PER_STEP = LANES // HEAD_DIM
GROUP = 8
VMEM_LIMIT_BYTES = 56 * 1024 * 1024


def _rms(x, g):
    return x * lax.rsqrt(jnp.mean(x * x, axis=-1, keepdims=True) + EPS) * g


def _silu(x):
    return x / (1.0 + jnp.exp(-x))


def _memkv_kernel(mem_ref, g_ref, w_ref, o_ref):
    h = _rms(mem_ref[0], g_ref[...]).astype(BF16)
    o_ref[0] = jnp.dot(h, w_ref[...], preferred_element_type=F32).astype(BF16)


def _memkv(mem, g_mem, w_kv):
    b, m, d = mem.shape
    n = w_kv.shape[1]
    return pl.pallas_call(
        _memkv_kernel,
        grid=(b,),
        in_specs=[
            pl.BlockSpec((1, m, d), lambda i: (i, 0, 0)),
            pl.BlockSpec((1, d), lambda i: (0, 0)),
            pl.BlockSpec((d, n), lambda i: (0, 0)),
        ],
        out_specs=pl.BlockSpec((1, m, n), lambda i: (i, 0, 0)),
        out_shape=jax.ShapeDtypeStruct((b, m, n), BF16),
        compiler_params=pltpu.CompilerParams(
            dimension_semantics=("arbitrary",), vmem_limit_bytes=VMEM_LIMIT_BYTES),
        name="memkv",
    )(mem, g_mem, w_kv)


def _inproj_kernel(x_ref, g_ref, w_ref, q_ref, k_ref, v_ref, r_ref):
    h = _rms(x_ref[...], g_ref[...]).astype(BF16)
    a = ATTN_WIDTH
    q_ref[...] = jnp.dot(h, w_ref[:, 0:a], preferred_element_type=F32) * (HEAD_DIM ** -0.5)
    k_ref[...] = jnp.dot(h, w_ref[:, a:2 * a], preferred_element_type=F32)
    v_ref[...] = jnp.dot(h, w_ref[:, 2 * a:3 * a], preferred_element_type=F32)
    r_ref[...] = jnp.dot(h, w_ref[:, 3 * a:], preferred_element_type=F32).astype(BF16)


def _inproj(x2d, g_norm, w_in, tm):
    n, d = x2d.shape
    cols = w_in.shape[1]
    rest = cols - 3 * ATTN_WIDTH
    row = lambda i: (i, 0)
    fixed = lambda i: (0, 0)
    return pl.pallas_call(
        _inproj_kernel,
        grid=(n // tm,),
        in_specs=[
            pl.BlockSpec((tm, d), row),
            pl.BlockSpec((1, d), fixed),
            pl.BlockSpec((d, cols), fixed),
        ],
        out_specs=[
            pl.BlockSpec((tm, ATTN_WIDTH), row),
            pl.BlockSpec((tm, ATTN_WIDTH), row),
            pl.BlockSpec((tm, ATTN_WIDTH), row),
            pl.BlockSpec((tm, rest), row),
        ],
        out_shape=[
            jax.ShapeDtypeStruct((n, ATTN_WIDTH), F32),
            jax.ShapeDtypeStruct((n, ATTN_WIDTH), F32),
            jax.ShapeDtypeStruct((n, ATTN_WIDTH), F32),
            jax.ShapeDtypeStruct((n, rest), BF16),
        ],
        compiler_params=pltpu.CompilerParams(
            dimension_semantics=("arbitrary",), vmem_limit_bytes=VMEM_LIMIT_BYTES),
        name="inproj",
    )(x2d, g_norm, w_in)


SHIFT, EDGE, WHOLE = "shift", "edge", "whole"


def _bias_tile(kind, dilation, first_head):
    shape = (KEY_WINDOW, HEADS_PER_STEP * Q_BLOCK)
    key = lax.broadcasted_iota(jnp.int32, shape, 0)
    col = lax.broadcasted_iota(jnp.int32, shape, 1)
    second = col >= Q_BLOCK
    qi = jnp.where(second, col - Q_BLOCK, col)
    if kind == SHIFT:
        rel = jnp.abs(qi + RADIUS - key)
        valid = rel <= RADIUS
    elif kind == WHOLE:
        rel = jnp.abs(qi - key)
        valid = rel <= RADIUS
    else:
        upper = key >= Q_BLOCK
        rel = jnp.abs(qi - jnp.where(upper, key - Q_BLOCK, key))
        valid = (rel <= RADIUS) & (upper == (qi >= RADIUS))
    head = (first_head + jnp.where(second, 1, 0)).astype(F32)
    slope = jnp.exp2(-8.0 * (head + 1.0) / N_ATTN_HEADS)
    dist = (rel * dilation).astype(F32)
    return jnp.where(valid, -slope * dist, NEG_INF)


class _Block(NamedTuple):
    cfg: int
    tile: int
    q_rows: tuple
    k_rows: tuple
    out_rows: tuple


def _attn_plan(seq):
    blocks, tiles = [], []
    for c, (_, dil) in enumerate(DILATED_CONFIGS):
        cls = seq // dil
        if cls == Q_BLOCK:
            tiles.append((WHOLE, dil))
            whole = len(tiles) - 1
        else:
            tiles.append((SHIFT, dil))
            tiles.append((EDGE, dil))
            shift, edge = len(tiles) - 2, len(tiles) - 1
        for r in range(dil):
            base = r * cls
            if cls == Q_BLOCK:
                blocks.append(_Block(c, whole, ((base, Q_BLOCK),), ((base, Q_BLOCK),), ((r, Q_BLOCK),)))
                continue
            for j in range(cls // Q_BLOCK - 1):
                u = RADIUS + j * Q_BLOCK
                blocks.append(_Block(c, shift, ((base + u, Q_BLOCK),), ((base + j * Q_BLOCK, KEY_WINDOW),),
                                     ((dil * u + r, Q_BLOCK),)))
            last = cls - RADIUS
            blocks.append(_Block(c, edge, ((base, RADIUS), (base + last, RADIUS)),
                                 ((base, Q_BLOCK), (base + cls - Q_BLOCK, Q_BLOCK)),
                                 ((r, RADIUS), (dil * last + r, RADIUS))))
    groups = [blocks[i:i + GROUP] for i in range(0, len(blocks), GROUP)]
    for g in groups:
        assert len({(b.cfg, sum(n for _, n in b.k_rows)) for b in g}) == 1
    return groups, tiles


def _rows(ref, lead, ranges):
    parts = [ref[lead + (pl.ds(s, n), slice(None))] for s, n in ranges]
    return parts[0] if len(parts) == 1 else jnp.concatenate(parts, axis=0)


def _score_stage(group, slot, zero, q_scr, k_scr, bias_scr, s_scr):
    for b, blk in enumerate(group):
        q2 = jnp.concatenate([_rows(q_scr, (blk.cfg + zero, h), blk.q_rows) for h in range(HEADS_PER_STEP)],
                             axis=0)
        k_win = _rows(k_scr, (blk.cfg + zero,), blk.k_rows)
        window = k_win.shape[0]
        s = lax.dot_general(k_win, q2, (((1,), (1,)), ((), ())), preferred_element_type=F32)
        cols = HEADS_PER_STEP * Q_BLOCK
        s_scr[slot, 0:window, b * cols:(b + 1) * cols] = s + bias_scr[blk.tile, 0:window, :]


def _value_stage(group, slot, zero, vt_scr, s_scr, o_scr, lse_scr):
    cfg = group[0].cfg
    dil = DILATED_CONFIGS[cfg][1]
    window = sum(n for _, n in group[0].k_rows)
    cols = HEADS_PER_STEP * Q_BLOCK
    ps, rs, lses = [], [], []
    for i in range(len(group) * HEADS_PER_STEP):
        s = s_scr[slot + zero, 0:window, i * Q_BLOCK:(i + 1) * Q_BLOCK]
        m = jnp.max(s, axis=0, keepdims=True)
        p = jnp.exp(s - m)
        l = jnp.sum(p, axis=0, keepdims=True)
        ps.append(p.astype(BF16))
        rs.append(1.0 / l)
        lses.append(m + jnp.log(l))
    lse = jnp.concatenate(lses + [jnp.zeros((LANES - len(lses), Q_BLOCK), F32)], axis=0).T
    first = lax.broadcasted_iota(jnp.int32, (Q_BLOCK, LANES), 1) < HEAD_DIM
    for b, blk in enumerate(group):
        vt = jnp.concatenate([vt_scr[cfg + zero, :, s0:s0 + n] for s0, n in blk.k_rows], axis=1)
        c0 = b * HEADS_PER_STEP
        p = jnp.concatenate(ps[c0:c0 + HEADS_PER_STEP], axis=1)
        ot = jnp.dot(vt, p, preferred_element_type=F32)
        ot = jnp.concatenate(
            [ot[h * HEAD_DIM:(h + 1) * HEAD_DIM, h * Q_BLOCK:(h + 1) * Q_BLOCK] * rs[c0 + h]
             for h in range(HEADS_PER_STEP)], axis=0)
        o = ot.T
        lse_b = jnp.where(first, lse[:, c0:c0 + 1], lse[:, c0 + 1:c0 + 2])
        at = 0
        for start, n in blk.out_rows:
            dst = pl.ds(start, n) if dil == 1 else pl.ds(start, n, stride=dil)
            o_scr[cfg, dst, :] = o[at:at + n]
            lse_scr[cfg, dst, :] = lse_b[at:at + n]
            at += n


def _attn_kernel(q_ref, k_ref, v_ref, a_ref, q_scr, k_scr, vt_scr, tmp_scr, bias_scr, s_scr, o_scr, lse_scr,
                 *, seq):
    n_cfg = len(DILATED_CONFIGS)
    groups, tiles = _attn_plan(seq)

    @pl.when(pl.program_id(1) == 0)
    def _():
        first_head = pl.program_id(0) * HEADS_PER_STEP
        for t, (kind, dil) in enumerate(tiles):
            bias_scr[t] = _bias_tile(kind, dil, first_head)

    first = lax.broadcasted_iota(jnp.int32, (1, LANES), 1) < HEAD_DIM

    def put(c, dst, q, k, v):
        q_scr[c, 0, dst, :] = jnp.where(first, q, 0.0).astype(BF16)
        q_scr[c, 1, dst, :] = jnp.where(first, 0.0, q).astype(BF16)
        k_scr[c, dst, :] = k.astype(BF16)
        for i in range(v.shape[0] // LANES):
            at = dst.start + i * LANES
            vt_scr[c, :, at:at + LANES] = v[i * LANES:(i + 1) * LANES].T.astype(BF16)

    refs = (q_ref, k_ref, v_ref)
    for c, (_, dil) in enumerate(DILATED_CONFIGS):
        cls = seq // dil
        if dil == 1:
            put(c, slice(0, seq), *(ref[0] for ref in refs))
        elif dil == 4:
            for r in range(dil):
                dst = slice(r * cls, (r + 1) * cls)
                vals = [ref[0, pl.ds(r, cls, stride=dil), :] for ref in refs]
                for t, val in enumerate(vals):
                    tmp_scr[t, dst, :] = val
                put(c, dst, *vals)
        else:
            sub = dil // 4
            for r in range(dil):
                src = pl.ds((r % 4) * (seq // 4) + r // 4, cls, stride=sub)
                put(c, slice(r * cls, (r + 1) * cls), *(tmp_scr[t, src, :] for t in range(len(refs))))

    zero = jnp.minimum(pl.program_id(1), 0)
    _score_stage(groups[0], 0, zero, q_scr, k_scr, bias_scr, s_scr)
    for i, group in enumerate(groups):
        if i + 1 < len(groups):
            _score_stage(groups[i + 1], (i + 1) % 2, zero, q_scr, k_scr, bias_scr, s_scr)
        _value_stage(group, i % 2, zero, vt_scr, s_scr, o_scr, lse_scr)

    def merge(i, carry):
        rows = pl.ds(pl.multiple_of(i * Q_BLOCK, Q_BLOCK), Q_BLOCK)
        lses = [lse_scr[c, rows, :] for c in range(n_cfg)]
        top = functools.reduce(jnp.maximum, lses)
        es = [jnp.exp(x - top) for x in lses]
        num = functools.reduce(lambda a, b: a + b, [e * o_scr[c, rows, :] for c, e in enumerate(es)])
        den = functools.reduce(lambda a, b: a + b, es)
        a_ref[0, rows, :] = (num / den).astype(a_ref.dtype)
        return carry

    lax.fori_loop(0, seq // Q_BLOCK, merge, 0)


def _attn(q, k, v):
    b, s, w = q.shape
    n_cfg = len(DILATED_CONFIGS)
    assert [d for _, d in DILATED_CONFIGS] == [1, 4, 16] and s % (16 * Q_BLOCK) == 0
    n_tiles = len(_attn_plan(s)[1])
    blk = pl.BlockSpec((1, s, LANES), lambda j, i: (i, 0, j))
    return pl.pallas_call(
        functools.partial(_attn_kernel, seq=s),
        grid=(w // LANES, b),
        in_specs=[blk, blk, blk],
        out_specs=blk,
        out_shape=jax.ShapeDtypeStruct((b, s, w), BF16),
        scratch_shapes=[
            pltpu.VMEM((n_cfg, HEADS_PER_STEP, s, LANES), BF16),
            pltpu.VMEM((n_cfg, s, LANES), BF16),
            pltpu.VMEM((n_cfg, LANES, s), BF16),
            pltpu.VMEM((3, s, LANES), F32),
            pltpu.VMEM((n_tiles, KEY_WINDOW, HEADS_PER_STEP * Q_BLOCK), F32),
            pltpu.VMEM((2, KEY_WINDOW, GROUP * HEADS_PER_STEP * Q_BLOCK), F32),
            pltpu.VMEM((n_cfg, s, LANES), F32),
            pltpu.VMEM((n_cfg, s, LANES), F32),
        ],
        compiler_params=pltpu.CompilerParams(
            dimension_semantics=("arbitrary", "arbitrary"), vmem_limit_bytes=VMEM_LIMIT_BYTES),
        name="attn",
    )(q, k, v)


def _mix_kernel(a_ref, r_ref, kv_ref, x_ref, ws_ref, bs_ref, gv_ref, wo_ref, gf_ref, o_ref):
    tm = x_ref.shape[1]
    c0 = 0
    za = r_ref[0, :, c0:c0 + ATTN_WIDTH].astype(F32); c0 += ATTN_WIDTH
    ub = r_ref[0, :, c0:c0 + SGU_WIDTH].astype(F32); c0 += SGU_WIDTH
    vb = r_ref[0, :, c0:c0 + SGU_WIDTH].astype(F32); c0 += SGU_WIDTH
    zb = r_ref[0, :, c0:c0 + SGU_WIDTH].astype(F32); c0 += SGU_WIDTH
    qm = r_ref[0, :, c0:c0 + MEM_WIDTH]; c0 += MEM_WIDTH
    zm = r_ref[0, :, c0:c0 + MEM_WIDTH].astype(F32)

    ga = (_silu(za) * a_ref[0].astype(F32)).astype(BF16)

    u = jax.nn.gelu(ub)
    vn = _rms(jax.nn.gelu(vb), gv_ref[...])
    group = lax.broadcasted_iota(jnp.int32, (SGU_CHUNK, SGU_WIDTH), 1) // SGU_GROUP
    mixed = []
    for c in range(tm // SGU_CHUNK):
        vc = vn[c * SGU_CHUNK:(c + 1) * SGU_CHUNK]
        acc = bs_ref[...]
        for g in range(N_SGU_GROUPS):
            vg = jnp.where(group == g, vc, 0.0).astype(BF16)
            acc = acc + jnp.dot(ws_ref[g], vg, preferred_element_type=F32)
        mixed.append(acc)
    mixed = jnp.concatenate(mixed, axis=0)
    gb = (_silu(zb) * (u * mixed)).astype(BF16)

    k_mem = kv_ref[0, :, 0:MEM_WIDTH]
    v_mem = kv_ref[0, :, MEM_WIDTH:2 * MEM_WIDTH]
    mhead = lax.broadcasted_iota(jnp.int32, (tm, MEM_WIDTH), 1) // MEM_HEAD_DIM
    zero = jnp.zeros_like(qm)
    q4 = jnp.concatenate([jnp.where(mhead == h, qm, zero) for h in range(N_MEM_HEADS)], axis=0)
    s = lax.dot_general(q4, k_mem, (((1,), (1,)), ((), ())), preferred_element_type=F32) * (MEM_HEAD_DIM ** -0.5)
    m = jnp.max(s, axis=1, keepdims=True)
    p = jnp.exp(s - m)
    l = jnp.sum(p, axis=1, keepdims=True)
    pv = jnp.dot(p.astype(BF16), v_mem, preferred_element_type=F32) / l
    mo = pv[0:tm]
    for h in range(1, N_MEM_HEADS):
        mo = jnp.where(mhead == h, pv[h * tm:(h + 1) * tm], mo)
    gm = (_silu(zm) * mo).astype(BF16)

    e1 = ATTN_WIDTH
    e2 = e1 + SGU_WIDTH
    y = jnp.dot(ga, wo_ref[0:e1, :], preferred_element_type=F32)
    y = y + jnp.dot(gb, wo_ref[e1:e2, :], preferred_element_type=F32)
    y = y + jnp.dot(gm, wo_ref[e2:, :], preferred_element_type=F32)
    o_ref[0] = _rms(x_ref[0] + y, gf_ref[...])


def _mix(a, rest, kv, x, w_s, b_tile, g_v, w_out, g_final, tm):
    b, s, d = x.shape
    tile = lambda w: pl.BlockSpec((1, tm, w), lambda i, j: (i, j, 0))
    fixed2 = lambda shp: pl.BlockSpec(shp, lambda i, j: (0, 0))
    return pl.pallas_call(
        _mix_kernel,
        grid=(b, s // tm),
        in_specs=[
            tile(a.shape[2]),
            tile(rest.shape[2]),
            pl.BlockSpec((1,) + kv.shape[1:], lambda i, j: (i, 0, 0)),
            tile(d),
            pl.BlockSpec(w_s.shape, lambda i, j: (0, 0, 0)),
            fixed2(b_tile.shape),
            fixed2(g_v.shape),
            fixed2(w_out.shape),
            fixed2(g_final.shape),
        ],
        out_specs=tile(d),
        out_shape=jax.ShapeDtypeStruct((b, s, d), x.dtype),
        compiler_params=pltpu.CompilerParams(
            dimension_semantics=("arbitrary", "arbitrary"), vmem_limit_bytes=VMEM_LIMIT_BYTES),
        name="mix",
    )(a, rest, kv, x, w_s, b_tile, g_v, w_out, g_final)


def kernel(x, mem, g_norm, w_in, w_sgu_spatial, b_sgu_spatial, g_sgu_v, g_mem, w_mem_kv, w_out, g_final):
    assert g_norm.shape[0] == 1, "the final norm is fused into the single layer's last kernel"
    b, s, d = x.shape
    kv = _memkv(mem, g_mem[0][None, :], w_mem_kv[0].astype(BF16))
    q, k, v, rest = _inproj(x.reshape(b * s, d), g_norm[0][None, :], w_in[0].astype(BF16), tm=512)
    a = _attn(q.reshape(b, s, -1), k.reshape(b, s, -1), v.reshape(b, s, -1))
    b_tile = jnp.repeat(b_sgu_spatial[0].T, SGU_GROUP, axis=1)
    return _mix(a, rest.reshape(b, s, -1), kv, x, w_sgu_spatial[0].astype(BF16), b_tile,
                g_sgu_v[0][None, :], w_out[0].astype(BF16), g_final[None, :], tm=256)
```

```python
import functools
import math
from typing import NamedTuple

import jax
import jax.numpy as jnp
from jax import lax
from jax.experimental import pallas as pl
from jax.experimental.pallas import tpu as pltpu

F32 = jnp.float32
BF16 = jnp.bfloat16

EPS = 1e-6
NEG_INF = -1e30
LOG2_E = math.log2(math.e)

HEAD_DIM = 64
N_ATTN_HEADS = 8
ATTN_WIDTH = HEAD_DIM * N_ATTN_HEADS
DILATED_CONFIGS = ((128, 1), (512, 4), (2048, 16))
RADIUS = 64
SGU_WIDTH = 256
N_SGU_GROUPS = 4
SGU_GROUP = SGU_WIDTH // N_SGU_GROUPS
SGU_CHUNK = 128
MEM_WIDTH = 256
N_MEM_HEADS = 4
MEM_HEAD_DIM = MEM_WIDTH // N_MEM_HEADS

LANES = 128
BF16_ROWS = 16
Q_BLOCK = 128
KEY_WINDOW = 2 * Q_BLOCK
HEADS_PER_STEP = LANES // HEAD_DIM
SCORE_LAG = 4
SCORE_SLOTS = 2 * SCORE_LAG
VMEM_LIMIT_BYTES = 56 * 1024 * 1024


def _rms(x, g):
    return x * lax.rsqrt(jnp.mean(x * x, axis=-1, keepdims=True) + EPS) * g


def _silu(x):
    return x / (1.0 + jnp.exp(-x))


def _memkv_kernel(mem_ref, g_ref, w_ref, o_ref):
    h = _rms(mem_ref[0], g_ref[...]).astype(BF16)
    o_ref[0] = jnp.dot(h, w_ref[...], preferred_element_type=F32).astype(BF16)


def _memkv(mem, g_mem, w_kv):
    b, m, d = mem.shape
    n = w_kv.shape[1]
    return pl.pallas_call(
        _memkv_kernel,
        grid=(b,),
        in_specs=[
            pl.BlockSpec((1, m, d), lambda i: (i, 0, 0)),
            pl.BlockSpec((1, d), lambda i: (0, 0)),
            pl.BlockSpec((d, n), lambda i: (0, 0)),
        ],
        out_specs=pl.BlockSpec((1, m, n), lambda i: (i, 0, 0)),
        out_shape=jax.ShapeDtypeStruct((b, m, n), BF16),
        compiler_params=pltpu.CompilerParams(
            dimension_semantics=("arbitrary",), vmem_limit_bytes=VMEM_LIMIT_BYTES),
        name="memkv",
    )(mem, g_mem, w_kv)


def _inproj_kernel(x_ref, g_ref, w_ref, q_ref, k_ref, v_ref, r_ref):
    h = _rms(x_ref[...], g_ref[...]).astype(BF16)
    a = ATTN_WIDTH
    q_ref[...] = jnp.dot(h, w_ref[:, 0:a], preferred_element_type=F32) * (HEAD_DIM ** -0.5 * LOG2_E)
    k_ref[...] = jnp.dot(h, w_ref[:, a:2 * a], preferred_element_type=F32)
    v_ref[...] = jnp.dot(h, w_ref[:, 2 * a:3 * a], preferred_element_type=F32)
    r_ref[...] = jnp.dot(h, w_ref[:, 3 * a:], preferred_element_type=F32).astype(BF16)


def _inproj(x2d, g_norm, w_in, tm):
    n, d = x2d.shape
    cols = w_in.shape[1]
    rest = cols - 3 * ATTN_WIDTH
    row = lambda i: (i, 0)
    fixed = lambda i: (0, 0)
    return pl.pallas_call(
        _inproj_kernel,
        grid=(n // tm,),
        in_specs=[
            pl.BlockSpec((tm, d), row),
            pl.BlockSpec((1, d), fixed),
            pl.BlockSpec((d, cols), fixed),
        ],
        out_specs=[
            pl.BlockSpec((tm, ATTN_WIDTH), row),
            pl.BlockSpec((tm, ATTN_WIDTH), row),
            pl.BlockSpec((tm, ATTN_WIDTH), row),
            pl.BlockSpec((tm, rest), row),
        ],
        out_shape=[
            jax.ShapeDtypeStruct((n, ATTN_WIDTH), F32),
            jax.ShapeDtypeStruct((n, ATTN_WIDTH), F32),
            jax.ShapeDtypeStruct((n, ATTN_WIDTH), F32),
            jax.ShapeDtypeStruct((n, rest), BF16),
        ],
        compiler_params=pltpu.CompilerParams(
            dimension_semantics=("arbitrary",), vmem_limit_bytes=VMEM_LIMIT_BYTES),
        name="inproj",
    )(x2d, g_norm, w_in)


SHIFT, EDGE, WHOLE = "shift", "edge", "whole"


def _bias_tile(kind, dilation, first_head):
    shape = (KEY_WINDOW, HEADS_PER_STEP * Q_BLOCK)
    key = lax.broadcasted_iota(jnp.int32, shape, 0)
    col = lax.broadcasted_iota(jnp.int32, shape, 1)
    second = col >= Q_BLOCK
    qi = jnp.where(second, col - Q_BLOCK, col)
    if kind == SHIFT:
        rel = jnp.abs(qi + RADIUS - key)
        valid = rel <= RADIUS
    elif kind == WHOLE:
        rel = jnp.abs(qi - key)
        valid = rel <= RADIUS
    else:
        upper = key >= Q_BLOCK
        rel = jnp.abs(qi - jnp.where(upper, key - Q_BLOCK, key))
        valid = (rel <= RADIUS) & (upper == (qi >= RADIUS))
    head = (first_head + jnp.where(second, 1, 0)).astype(F32)
    slope = jnp.exp2(-8.0 * (head + 1.0) / N_ATTN_HEADS)
    dist = (rel * dilation).astype(F32)
    return jnp.where(valid, -slope * dist * LOG2_E, NEG_INF)


class _Block(NamedTuple):
    cfg: int
    tile: int
    q_rows: tuple
    k_rows: tuple
    out_rows: tuple


def _attn_plan(seq):
    blocks, tiles = [], []
    for c, (_, dil) in enumerate(DILATED_CONFIGS):
        cls = seq // dil
        if cls == Q_BLOCK:
            tiles.append((WHOLE, dil))
            whole = len(tiles) - 1
        else:
            tiles.append((SHIFT, dil))
            tiles.append((EDGE, dil))
            shift, edge = len(tiles) - 2, len(tiles) - 1
        for r in range(dil):
            base = r * cls
            if cls == Q_BLOCK:
                blocks.append(_Block(c, whole, ((base, Q_BLOCK),), ((base, Q_BLOCK),), ((r, Q_BLOCK),)))
                continue
            for j in range(cls // Q_BLOCK - 1):
                u = RADIUS + j * Q_BLOCK
                blocks.append(_Block(c, shift, ((base + u, Q_BLOCK),), ((base + j * Q_BLOCK, KEY_WINDOW),),
                                     ((dil * u + r, Q_BLOCK),)))
            last = cls - RADIUS
            blocks.append(_Block(c, edge, ((base, RADIUS), (base + last, RADIUS)),
                                 ((base, Q_BLOCK), (base + cls - Q_BLOCK, Q_BLOCK)),
                                 ((r, RADIUS), (dil * last + r, RADIUS))))
    return blocks, tiles


def _rows(ref, lead, ranges):
    parts = [ref[lead + (pl.ds(s, n), slice(None))] for s, n in ranges]
    return parts[0] if len(parts) == 1 else jnp.concatenate(parts, axis=0)


def _score_block(blk, zero, q_scr, k_scr, bias_scr, s_scr):
    q2 = jnp.concatenate([_rows(q_scr, (blk.cfg + zero, h), blk.q_rows) for h in range(HEADS_PER_STEP)], axis=0)
    k_win = _rows(k_scr, (blk.cfg + zero,), blk.k_rows)
    window = k_win.shape[0]
    s = lax.dot_general(k_win, q2, (((1,), (1,)), ((), ())), preferred_element_type=F32)
    s_scr[0, 0:window, :] = s + bias_scr[blk.tile, 0:window, :]


def _value_block(blk, zero, vt_scr, s_scr, o_scr, lse_scr):
    cfg = blk.cfg
    dil = DILATED_CONFIGS[cfg][1]
    window = sum(n for _, n in blk.k_rows)
    ms, ps = [], []
    for h in range(HEADS_PER_STEP):
        s = s_scr[zero, 0:window, h * Q_BLOCK:(h + 1) * Q_BLOCK]
        m = jnp.max(s, axis=0, keepdims=True)
        ms.append(m)
        ps.append(jnp.exp2(s - m).astype(BF16))
    vt = jnp.concatenate([vt_scr[cfg + zero, :, s0:s0 + n] for s0, n in blk.k_rows], axis=1)
    ot = jnp.dot(vt, jnp.concatenate(ps, axis=1), preferred_element_type=F32)
    outs, lses = [], []
    for h in range(HEADS_PER_STEP):
        cols = slice(h * Q_BLOCK, (h + 1) * Q_BLOCK)
        l = ot[LANES:LANES + 1, cols]
        outs.append(ot[h * HEAD_DIM:(h + 1) * HEAD_DIM, cols] * (1.0 / l))
        lses.append(ms[h] + jnp.log2(l))
    o = jnp.concatenate(outs, axis=0).T
    lse = jnp.concatenate([jnp.broadcast_to(x, (HEAD_DIM, Q_BLOCK)) for x in lses], axis=0).T
    at = 0
    for start, n in blk.out_rows:
        dst = pl.ds(start, n) if dil == 1 else pl.ds(start, n, stride=dil)
        o_scr[cfg, dst, :] = o[at:at + n]
        lse_scr[cfg, dst, :] = lse[at:at + n]
        at += n


def _attn_kernel(q_ref, k_ref, v_ref, a_ref, q_scr, k_scr, vt_scr, tmp_scr, bias_scr, o_scr, lse_scr, *s_scrs,
                 seq):
    n_cfg = len(DILATED_CONFIGS)
    blocks, tiles = _attn_plan(seq)

    @pl.when(pl.program_id(1) == 0)
    def _():
        first_head = pl.program_id(0) * HEADS_PER_STEP
        for t, (kind, dil) in enumerate(tiles):
            bias_scr[t] = _bias_tile(kind, dil, first_head)

    first = lax.broadcasted_iota(jnp.int32, (1, LANES), 1) < HEAD_DIM

    def put(c, dst, q, k, v):
        q_scr[c, 0, dst, :] = jnp.where(first, q, 0.0).astype(BF16)
        q_scr[c, 1, dst, :] = jnp.where(first, 0.0, q).astype(BF16)
        k_scr[c, dst, :] = k.astype(BF16)
        for i in range(v.shape[0] // LANES):
            at = dst.start + i * LANES
            vt_scr[c, 0:LANES, at:at + LANES] = v[i * LANES:(i + 1) * LANES].T.astype(BF16)

    ones_row = lax.broadcasted_iota(jnp.int32, (BF16_ROWS, seq), 0) == 0
    refs = (q_ref, k_ref, v_ref)
    for c, (_, dil) in enumerate(DILATED_CONFIGS):
        vt_scr[c, LANES:LANES + BF16_ROWS, :] = jnp.where(ones_row, 1.0, 0.0).astype(BF16)
        cls = seq // dil
        if dil == 1:
            put(c, slice(0, seq), *(ref[0] for ref in refs))
        elif dil == 4:
            for r in range(dil):
                dst = slice(r * cls, (r + 1) * cls)
                vals = [ref[0, pl.ds(r, cls, stride=dil), :] for ref in refs]
                for t, val in enumerate(vals):
                    tmp_scr[t, dst, :] = val
                put(c, dst, *vals)
        else:
            sub = dil // 4
            for r in range(dil):
                src = pl.ds((r % 4) * (seq // 4) + r // 4, cls, stride=sub)
                put(c, slice(r * cls, (r + 1) * cls), *(tmp_scr[t, src, :] for t in range(len(refs))))

    zero = jnp.minimum(pl.program_id(1), 0)
    for t in range(len(blocks) + SCORE_LAG):
        if t < len(blocks):
            _score_block(blocks[t], zero, q_scr, k_scr, bias_scr, s_scrs[t % SCORE_SLOTS])
        if t >= SCORE_LAG:
            u = t - SCORE_LAG
            _value_block(blocks[u], zero, vt_scr, s_scrs[u % SCORE_SLOTS], o_scr, lse_scr)

    def merge(i, carry):
        rows = pl.ds(pl.multiple_of(i * Q_BLOCK, Q_BLOCK), Q_BLOCK)
        lses = [lse_scr[c, rows, :] for c in range(n_cfg)]
        top = functools.reduce(jnp.maximum, lses)
        es = [jnp.exp2(x - top) for x in lses]
        num = functools.reduce(lambda a, b: a + b, [e * o_scr[c, rows, :] for c, e in enumerate(es)])
        den = functools.reduce(lambda a, b: a + b, es)
        a_ref[0, rows, :] = (num / den).astype(a_ref.dtype)
        return carry

    lax.fori_loop(0, seq // Q_BLOCK, merge, 0)


def _attn(q, k, v):
    b, s, w = q.shape
    n_cfg = len(DILATED_CONFIGS)
    assert [d for _, d in DILATED_CONFIGS] == [1, 4, 16] and s % (16 * Q_BLOCK) == 0
    n_tiles = len(_attn_plan(s)[1])
    blk = pl.BlockSpec((1, s, LANES), lambda j, i: (i, 0, j))
    return pl.pallas_call(
        functools.partial(_attn_kernel, seq=s),
        grid=(w // LANES, b),
        in_specs=[blk, blk, blk],
        out_specs=blk,
        out_shape=jax.ShapeDtypeStruct((b, s, w), BF16),
        scratch_shapes=[
            pltpu.VMEM((n_cfg, HEADS_PER_STEP, s, LANES), BF16),
            pltpu.VMEM((n_cfg, s, LANES), BF16),
            pltpu.VMEM((n_cfg, LANES + BF16_ROWS, s), BF16),
            pltpu.VMEM((3, s, LANES), F32),
            pltpu.VMEM((n_tiles, KEY_WINDOW, HEADS_PER_STEP * Q_BLOCK), F32),
            pltpu.VMEM((n_cfg, s, LANES), F32),
            pltpu.VMEM((n_cfg, s, LANES), F32),
        ] + [pltpu.VMEM((1, KEY_WINDOW, HEADS_PER_STEP * Q_BLOCK), F32)] * SCORE_SLOTS,
        compiler_params=pltpu.CompilerParams(
            dimension_semantics=("arbitrary", "arbitrary"), vmem_limit_bytes=VMEM_LIMIT_BYTES),
        name="attn",
    )(q, k, v)


def _mix_kernel(a_ref, r_ref, kv_ref, x_ref, ws_ref, bs_ref, gv_ref, wo_ref, gf_ref, o_ref):
    tm = x_ref.shape[1]
    c0 = 0
    za = r_ref[0, :, c0:c0 + ATTN_WIDTH].astype(F32); c0 += ATTN_WIDTH
    ub = r_ref[0, :, c0:c0 + SGU_WIDTH].astype(F32); c0 += SGU_WIDTH
    vb = r_ref[0, :, c0:c0 + SGU_WIDTH].astype(F32); c0 += SGU_WIDTH
    zb = r_ref[0, :, c0:c0 + SGU_WIDTH].astype(F32); c0 += SGU_WIDTH
    qm = r_ref[0, :, c0:c0 + MEM_WIDTH]; c0 += MEM_WIDTH
    zm = r_ref[0, :, c0:c0 + MEM_WIDTH].astype(F32)

    ga = (_silu(za) * a_ref[0].astype(F32)).astype(BF16)

    u = jax.nn.gelu(ub)
    vn = _rms(jax.nn.gelu(vb), gv_ref[...])
    group = lax.broadcasted_iota(jnp.int32, (SGU_CHUNK, SGU_WIDTH), 1) // SGU_GROUP
    mixed = []
    for c in range(tm // SGU_CHUNK):
        vc = vn[c * SGU_CHUNK:(c + 1) * SGU_CHUNK]
        acc = bs_ref[...]
        for g in range(N_SGU_GROUPS):
            vg = jnp.where(group == g, vc, 0.0).astype(BF16)
            acc = acc + jnp.dot(ws_ref[g], vg, preferred_element_type=F32)
        mixed.append(acc)
    mixed = jnp.concatenate(mixed, axis=0)
    gb = (_silu(zb) * (u * mixed)).astype(BF16)

    k_mem = kv_ref[0, :, 0:MEM_WIDTH]
    v_mem = kv_ref[0, :, MEM_WIDTH:2 * MEM_WIDTH]
    mhead = lax.broadcasted_iota(jnp.int32, (tm, MEM_WIDTH), 1) // MEM_HEAD_DIM
    zero = jnp.zeros_like(qm)
    q4 = jnp.concatenate([jnp.where(mhead == h, qm, zero) for h in range(N_MEM_HEADS)], axis=0)
    s = lax.dot_general(q4, k_mem, (((1,), (1,)), ((), ())), preferred_element_type=F32) * (MEM_HEAD_DIM ** -0.5)
    m = jnp.max(s, axis=1, keepdims=True)
    p = jnp.exp(s - m)
    l = jnp.sum(p, axis=1, keepdims=True)
    pv = jnp.dot(p.astype(BF16), v_mem, preferred_element_type=F32) / l
    mo = pv[0:tm]
    for h in range(1, N_MEM_HEADS):
        mo = jnp.where(mhead == h, pv[h * tm:(h + 1) * tm], mo)
    gm = (_silu(zm) * mo).astype(BF16)

    e1 = ATTN_WIDTH
    e2 = e1 + SGU_WIDTH
    y = jnp.dot(ga, wo_ref[0:e1, :], preferred_element_type=F32)
    y = y + jnp.dot(gb, wo_ref[e1:e2, :], preferred_element_type=F32)
    y = y + jnp.dot(gm, wo_ref[e2:, :], preferred_element_type=F32)
    o_ref[0] = _rms(x_ref[0] + y, gf_ref[...])


def _mix(a, rest, kv, x, w_s, b_tile, g_v, w_out, g_final, tm):
    b, s, d = x.shape
    tile = lambda w: pl.BlockSpec((1, tm, w), lambda i, j: (i, j, 0))
    fixed2 = lambda shp: pl.BlockSpec(shp, lambda i, j: (0, 0))
    return pl.pallas_call(
        _mix_kernel,
        grid=(b, s // tm),
        in_specs=[
            tile(a.shape[2]),
            tile(rest.shape[2]),
            pl.BlockSpec((1,) + kv.shape[1:], lambda i, j: (i, 0, 0)),
            tile(d),
            pl.BlockSpec(w_s.shape, lambda i, j: (0, 0, 0)),
            fixed2(b_tile.shape),
            fixed2(g_v.shape),
            fixed2(w_out.shape),
            fixed2(g_final.shape),
        ],
        out_specs=tile(d),
        out_shape=jax.ShapeDtypeStruct((b, s, d), x.dtype),
        compiler_params=pltpu.CompilerParams(
            dimension_semantics=("arbitrary", "arbitrary"), vmem_limit_bytes=VMEM_LIMIT_BYTES),
        name="mix",
    )(a, rest, kv, x, w_s, b_tile, g_v, w_out, g_final)


def kernel(x, mem, g_norm, w_in, w_sgu_spatial, b_sgu_spatial, g_sgu_v, g_mem, w_mem_kv, w_out, g_final):
    assert g_norm.shape[0] == 1, "the final norm is fused into the single layer's last kernel"
    b, s, d = x.shape
    kv = _memkv(mem, g_mem[0][None, :], w_mem_kv[0].astype(BF16))
    q, k, v, rest = _inproj(x.reshape(b * s, d), g_norm[0][None, :], w_in[0].astype(BF16), tm=512)
    a = _attn(q.reshape(b, s, -1), k.reshape(b, s, -1), v.reshape(b, s, -1))
    b_tile = jnp.repeat(b_sgu_spatial[0].T, SGU_GROUP, axis=1)
    return _mix(a, rest.reshape(b, s, -1), kv, x, w_sgu_spatial[0].astype(BF16), b_tile,
                g_sgu_v[0][None, :], w_out[0].astype(BF16), g_final[None, :], tm=256)
```

```python
import functools
import math
from typing import NamedTuple

import jax
import jax.numpy as jnp
from jax import lax
from jax.experimental import pallas as pl
from jax.experimental.pallas import tpu as pltpu

F32 = jnp.float32
BF16 = jnp.bfloat16

EPS = 1e-6
NEG_INF = -1e30
LOG2_E = math.log2(math.e)

HEAD_DIM = 64
N_ATTN_HEADS = 8
ATTN_WIDTH = HEAD_DIM * N_ATTN_HEADS
DILATED_CONFIGS = ((128, 1), (512, 4), (2048, 16))
RADIUS = 64
SGU_WIDTH = 256
N_SGU_GROUPS = 4
SGU_GROUP = SGU_WIDTH // N_SGU_GROUPS
SGU_CHUNK = 128
MEM_WIDTH = 256
N_MEM_HEADS = 4
MEM_HEAD_DIM = MEM_WIDTH // N_MEM_HEADS
ACT_SPLITS = (ATTN_WIDTH, SGU_WIDTH, SGU_WIDTH, MEM_WIDTH, MEM_WIDTH)
ACT_WIDTH = sum(ACT_SPLITS)

LANES = 128
BF16_ROWS = 16
Q_BLOCK = 128
KEY_WINDOW = 2 * Q_BLOCK
HEADS_PER_STEP = LANES // HEAD_DIM
SCORE_LAG = 4
SCORE_SLOTS = 2 * SCORE_LAG
ROW_TILE = 512
VMEM_LIMIT_BYTES = 56 * 1024 * 1024


def _rms(x, g):
    return x * lax.rsqrt(jnp.mean(x * x, axis=-1, keepdims=True) + EPS) * g


def _silu(x):
    return x / (1.0 + jnp.exp(-x))


def _ones_rows(cols):
    first = lax.broadcasted_iota(jnp.int32, (BF16_ROWS, cols), 0) == 0
    return jnp.where(first, 1.0, 0.0).astype(BF16)


def _memkv_kernel(mem_ref, g_ref, w_ref, k_ref, vt_ref):
    h = _rms(mem_ref[0], g_ref[...]).astype(BF16)
    kv = jnp.dot(h, w_ref[...], preferred_element_type=F32)
    k_ref[0] = kv[:, 0:MEM_WIDTH].astype(BF16)
    vt_ref[0, 0:MEM_WIDTH, :] = kv[:, MEM_WIDTH:].T.astype(BF16)
    vt_ref[0, MEM_WIDTH:, :] = _ones_rows(kv.shape[0])


def _memkv(mem, g_mem, w_kv):
    b, m, d = mem.shape
    n = w_kv.shape[1]
    return pl.pallas_call(
        _memkv_kernel,
        grid=(b,),
        in_specs=[
            pl.BlockSpec((1, m, d), lambda i: (i, 0, 0)),
            pl.BlockSpec((1, d), lambda i: (0, 0)),
            pl.BlockSpec((d, n), lambda i: (0, 0)),
        ],
        out_specs=[
            pl.BlockSpec((1, m, MEM_WIDTH), lambda i: (i, 0, 0)),
            pl.BlockSpec((1, MEM_WIDTH + BF16_ROWS, m), lambda i: (i, 0, 0)),
        ],
        out_shape=[
            jax.ShapeDtypeStruct((b, m, MEM_WIDTH), BF16),
            jax.ShapeDtypeStruct((b, MEM_WIDTH + BF16_ROWS, m), BF16),
        ],
        compiler_params=pltpu.CompilerParams(
            dimension_semantics=("arbitrary",), vmem_limit_bytes=VMEM_LIMIT_BYTES),
        name="memkv",
    )(mem, g_mem, w_kv)


def _inproj_kernel(x_ref, g_ref, gv_ref, w_ref, q_ref, k_ref, v_ref, act_ref):
    h = _rms(x_ref[...], g_ref[...]).astype(BF16)
    at = [0]

    def proj(width):
        lo = at[0]
        at[0] = lo + width
        return jnp.dot(h, w_ref[:, lo:lo + width], preferred_element_type=F32)

    q_ref[...] = proj(ATTN_WIDTH) * (HEAD_DIM ** -0.5 * LOG2_E)
    k_ref[...] = proj(ATTN_WIDTH)
    v_ref[...] = proj(ATTN_WIDTH)
    za = proj(ATTN_WIDTH)
    ub = proj(SGU_WIDTH)
    vb = proj(SGU_WIDTH)
    zb = proj(SGU_WIDTH)
    qm = proj(MEM_WIDTH)
    zm = proj(MEM_WIDTH)
    acts = (_silu(za), _silu(zb) * jax.nn.gelu(ub), _rms(jax.nn.gelu(vb), gv_ref[...]),
            qm * (MEM_HEAD_DIM ** -0.5 * LOG2_E), _silu(zm))
    lo = 0
    for width, val in zip(ACT_SPLITS, acts):
        act_ref[:, lo:lo + width] = val.astype(BF16)
        lo += width


def _inproj(x2d, g_norm, g_v, w_in):
    n, d = x2d.shape
    cols = w_in.shape[1]
    assert cols == 4 * ATTN_WIDTH + 3 * SGU_WIDTH + 2 * MEM_WIDTH
    row = lambda i: (i, 0)
    fixed = lambda i: (0, 0)
    return pl.pallas_call(
        _inproj_kernel,
        grid=(n // ROW_TILE,),
        in_specs=[
            pl.BlockSpec((ROW_TILE, d), row),
            pl.BlockSpec((1, d), fixed),
            pl.BlockSpec((1, SGU_WIDTH), fixed),
            pl.BlockSpec((d, cols), fixed),
        ],
        out_specs=[
            pl.BlockSpec((ROW_TILE, ATTN_WIDTH), row),
            pl.BlockSpec((ROW_TILE, ATTN_WIDTH), row),
            pl.BlockSpec((ROW_TILE, ATTN_WIDTH), row),
            pl.BlockSpec((ROW_TILE, ACT_WIDTH), row),
        ],
        out_shape=[
            jax.ShapeDtypeStruct((n, ATTN_WIDTH), F32),
            jax.ShapeDtypeStruct((n, ATTN_WIDTH), F32),
            jax.ShapeDtypeStruct((n, ATTN_WIDTH), F32),
            jax.ShapeDtypeStruct((n, ACT_WIDTH), BF16),
        ],
        compiler_params=pltpu.CompilerParams(
            dimension_semantics=("arbitrary",), vmem_limit_bytes=VMEM_LIMIT_BYTES),
        name="inproj",
    )(x2d, g_norm, g_v, w_in)


SHIFT, EDGE, WHOLE = "shift", "edge", "whole"


def _bias_tile(kind, dilation, first_head):
    shape = (KEY_WINDOW, HEADS_PER_STEP * Q_BLOCK)
    key = lax.broadcasted_iota(jnp.int32, shape, 0)
    col = lax.broadcasted_iota(jnp.int32, shape, 1)
    second = col >= Q_BLOCK
    qi = jnp.where(second, col - Q_BLOCK, col)
    if kind == SHIFT:
        rel = jnp.abs(qi + RADIUS - key)
        valid = rel <= RADIUS
    elif kind == WHOLE:
        rel = jnp.abs(qi - key)
        valid = rel <= RADIUS
    else:
        upper = key >= Q_BLOCK
        rel = jnp.abs(qi - jnp.where(upper, key - Q_BLOCK, key))
        valid = (rel <= RADIUS) & (upper == (qi >= RADIUS))
    head = (first_head + jnp.where(second, 1, 0)).astype(F32)
    slope = jnp.exp2(-8.0 * (head + 1.0) / N_ATTN_HEADS)
    dist = (rel * dilation).astype(F32)
    return jnp.where(valid, -slope * dist * LOG2_E, NEG_INF)


class _Block(NamedTuple):
    cfg: int
    tile: int
    q_rows: tuple
    k_rows: tuple
    out_rows: tuple


def _attn_plan(seq):
    blocks, tiles = [], []
    for c, (_, dil) in enumerate(DILATED_CONFIGS):
        cls = seq // dil
        if cls == Q_BLOCK:
            tiles.append((WHOLE, dil))
            whole = len(tiles) - 1
        else:
            tiles.append((SHIFT, dil))
            tiles.append((EDGE, dil))
            shift, edge = len(tiles) - 2, len(tiles) - 1
        for r in range(dil):
            base = r * cls
            if cls == Q_BLOCK:
                blocks.append(_Block(c, whole, ((base, Q_BLOCK),), ((base, Q_BLOCK),), ((r, Q_BLOCK),)))
                continue
            for j in range(cls // Q_BLOCK - 1):
                u = RADIUS + j * Q_BLOCK
                blocks.append(_Block(c, shift, ((base + u, Q_BLOCK),), ((base + j * Q_BLOCK, KEY_WINDOW),),
                                     ((dil * u + r, Q_BLOCK),)))
            last = cls - RADIUS
            blocks.append(_Block(c, edge, ((base, RADIUS), (base + last, RADIUS)),
                                 ((base, Q_BLOCK), (base + cls - Q_BLOCK, Q_BLOCK)),
                                 ((r, RADIUS), (dil * last + r, RADIUS))))
    return blocks, tiles


def _rows(ref, lead, ranges):
    parts = [ref[lead + (pl.ds(s, n), slice(None))] for s, n in ranges]
    return parts[0] if len(parts) == 1 else jnp.concatenate(parts, axis=0)


def _score_block(blk, zero, q_scr, k_scr, bias_scr, s_scr):
    q2 = jnp.concatenate([_rows(q_scr, (blk.cfg + zero, h), blk.q_rows) for h in range(HEADS_PER_STEP)], axis=0)
    k_win = _rows(k_scr, (blk.cfg + zero,), blk.k_rows)
    window = k_win.shape[0]
    s = lax.dot_general(k_win, q2, (((1,), (1,)), ((), ())), preferred_element_type=F32)
    s_scr[0, 0:window, :] = s + bias_scr[blk.tile, 0:window, :]


def _value_block(blk, zero, vt_scr, s_scr, o_scr, lse_scr):
    cfg = blk.cfg
    dil = DILATED_CONFIGS[cfg][1]
    window = sum(n for _, n in blk.k_rows)
    ms, ps = [], []
    for h in range(HEADS_PER_STEP):
        s = s_scr[zero, 0:window, h * Q_BLOCK:(h + 1) * Q_BLOCK]
        m = jnp.max(s, axis=0, keepdims=True)
        ms.append(m)
        ps.append(jnp.exp2(s - m).astype(BF16))
    vt = jnp.concatenate([vt_scr[cfg + zero, :, s0:s0 + n] for s0, n in blk.k_rows], axis=1)
    ot = jnp.dot(vt, jnp.concatenate(ps, axis=1), preferred_element_type=F32)
    outs, lses = [], []
    for h in range(HEADS_PER_STEP):
        cols = slice(h * Q_BLOCK, (h + 1) * Q_BLOCK)
        l = ot[LANES:LANES + 1, cols]
        outs.append(ot[h * HEAD_DIM:(h + 1) * HEAD_DIM, cols] * (1.0 / l))
        lses.append(ms[h] + jnp.log2(l))
    o = jnp.concatenate(outs, axis=0).T
    lse = jnp.concatenate([jnp.broadcast_to(x, (HEAD_DIM, Q_BLOCK)) for x in lses], axis=0).T
    at = 0
    for start, n in blk.out_rows:
        dst = pl.ds(start, n) if dil == 1 else pl.ds(start, n, stride=dil)
        o_scr[cfg, dst, :] = o[at:at + n]
        lse_scr[cfg, dst, :] = lse[at:at + n]
        at += n


def _attn_kernel(q_ref, k_ref, v_ref, a_ref, q_scr, k_scr, vt_scr, tmp_scr, bias_scr, o_scr, lse_scr, *s_scrs,
                 seq):
    n_cfg = len(DILATED_CONFIGS)
    blocks, tiles = _attn_plan(seq)

    @pl.when(pl.program_id(1) == 0)
    def _():
        first_head = pl.program_id(0) * HEADS_PER_STEP
        for t, (kind, dil) in enumerate(tiles):
            bias_scr[t] = _bias_tile(kind, dil, first_head)

    first = lax.broadcasted_iota(jnp.int32, (1, LANES), 1) < HEAD_DIM

    def put(c, dst, q, k, v):
        q_scr[c, 0, dst, :] = jnp.where(first, q, 0.0).astype(BF16)
        q_scr[c, 1, dst, :] = jnp.where(first, 0.0, q).astype(BF16)
        k_scr[c, dst, :] = k.astype(BF16)
        for i in range(v.shape[0] // LANES):
            at = dst.start + i * LANES
            vt_scr[c, 0:LANES, at:at + LANES] = v[i * LANES:(i + 1) * LANES].T.astype(BF16)

    refs = (q_ref, k_ref, v_ref)
    for c, (_, dil) in enumerate(DILATED_CONFIGS):
        vt_scr[c, LANES:LANES + BF16_ROWS, :] = _ones_rows(seq)
        cls = seq // dil
        if dil == 1:
            put(c, slice(0, seq), *(ref[0] for ref in refs))
        elif dil == 4:
            for r in range(dil):
                dst = slice(r * cls, (r + 1) * cls)
                vals = [ref[0, pl.ds(r, cls, stride=dil), :] for ref in refs]
                for t, val in enumerate(vals):
                    tmp_scr[t, dst, :] = val
                put(c, dst, *vals)
        else:
            sub = dil // 4
            for r in range(dil):
                src = pl.ds((r % 4) * (seq // 4) + r // 4, cls, stride=sub)
                put(c, slice(r * cls, (r + 1) * cls), *(tmp_scr[t, src, :] for t in range(len(refs))))

    zero = jnp.minimum(pl.program_id(1), 0)
    for t in range(len(blocks) + SCORE_LAG):
        if t < len(blocks):
            _score_block(blocks[t], zero, q_scr, k_scr, bias_scr, s_scrs[t % SCORE_SLOTS])
        if t >= SCORE_LAG:
            u = t - SCORE_LAG
            _value_block(blocks[u], zero, vt_scr, s_scrs[u % SCORE_SLOTS], o_scr, lse_scr)

    def merge(i, carry):
        rows = pl.ds(pl.multiple_of(i * Q_BLOCK, Q_BLOCK), Q_BLOCK)
        lses = [lse_scr[c, rows, :] for c in range(n_cfg)]
        top = functools.reduce(jnp.maximum, lses)
        es = [jnp.exp2(x - top) for x in lses]
        num = functools.reduce(lambda a, b: a + b, [e * o_scr[c, rows, :] for c, e in enumerate(es)])
        den = functools.reduce(lambda a, b: a + b, es)
        a_ref[0, rows, :] = (num / den).astype(a_ref.dtype)
        return carry

    lax.fori_loop(0, seq // Q_BLOCK, merge, 0)


def _attn(q, k, v):
    b, s, w = q.shape
    n_cfg = len(DILATED_CONFIGS)
    assert [d for _, d in DILATED_CONFIGS] == [1, 4, 16] and s % (16 * Q_BLOCK) == 0
    n_tiles = len(_attn_plan(s)[1])
    blk = pl.BlockSpec((1, s, LANES), lambda j, i: (i, 0, j))
    return pl.pallas_call(
        functools.partial(_attn_kernel, seq=s),
        grid=(w // LANES, b),
        in_specs=[blk, blk, blk],
        out_specs=blk,
        out_shape=jax.ShapeDtypeStruct((b, s, w), BF16),
        scratch_shapes=[
            pltpu.VMEM((n_cfg, HEADS_PER_STEP, s, LANES), BF16),
            pltpu.VMEM((n_cfg, s, LANES), BF16),
            pltpu.VMEM((n_cfg, LANES + BF16_ROWS, s), BF16),
            pltpu.VMEM((3, s, LANES), F32),
            pltpu.VMEM((n_tiles, KEY_WINDOW, HEADS_PER_STEP * Q_BLOCK), F32),
            pltpu.VMEM((n_cfg, s, LANES), F32),
            pltpu.VMEM((n_cfg, s, LANES), F32),
        ] + [pltpu.VMEM((1, KEY_WINDOW, HEADS_PER_STEP * Q_BLOCK), F32)] * SCORE_SLOTS,
        compiler_params=pltpu.CompilerParams(
            dimension_semantics=("arbitrary", "arbitrary"), vmem_limit_bytes=VMEM_LIMIT_BYTES),
        name="attn",
    )(q, k, v)


def _mix_kernel(a_ref, act_ref, km_ref, vtm_ref, x_ref, ws_ref, bs_ref, wo_ref, gf_ref, o_ref):
    tm = x_ref.shape[1]
    lo = 0
    cols = []
    for width in ACT_SPLITS:
        cols.append(slice(lo, lo + width))
        lo += width
    sa, gu, vn, qm, sm = (act_ref[0, :, c] for c in cols)

    ga = sa * a_ref[0]

    group = lax.broadcasted_iota(jnp.int32, (SGU_CHUNK, SGU_WIDTH), 1) // SGU_GROUP
    mixed = []
    for c in range(tm // SGU_CHUNK):
        vc = vn[c * SGU_CHUNK:(c + 1) * SGU_CHUNK]
        stacked = jnp.concatenate([jnp.where(group == g, vc, jnp.zeros_like(vc)) for g in range(N_SGU_GROUPS)],
                                  axis=0)
        mixed.append(jnp.dot(ws_ref[...], stacked, preferred_element_type=F32) + bs_ref[...])
    gb = (gu.astype(F32) * jnp.concatenate(mixed, axis=0)).astype(BF16)

    mhead = lax.broadcasted_iota(jnp.int32, (tm, MEM_WIDTH), 1) // MEM_HEAD_DIM
    q4 = jnp.concatenate([jnp.where(mhead == h, qm, jnp.zeros_like(qm)) for h in range(N_MEM_HEADS)], axis=0)
    s = lax.dot_general(km_ref[0], q4, (((1,), (1,)), ((), ())), preferred_element_type=F32)
    p = jnp.exp2(s - jnp.max(s, axis=0, keepdims=True)).astype(BF16)
    ot = jnp.dot(vtm_ref[0], p, preferred_element_type=F32)
    mo = []
    for h in range(N_MEM_HEADS):
        c = slice(h * tm, (h + 1) * tm)
        mo.append(ot[h * MEM_HEAD_DIM:(h + 1) * MEM_HEAD_DIM, c] * (1.0 / ot[MEM_WIDTH:MEM_WIDTH + 1, c]))
    mo = jnp.concatenate(mo, axis=0).T
    gm = (sm.astype(F32) * mo).astype(BF16)

    e1 = ATTN_WIDTH
    e2 = e1 + SGU_WIDTH
    y = jnp.dot(ga, wo_ref[0:e1, :], preferred_element_type=F32)
    y = y + jnp.dot(gb, wo_ref[e1:e2, :], preferred_element_type=F32)
    y = y + jnp.dot(gm, wo_ref[e2:, :], preferred_element_type=F32)
    o_ref[0] = _rms(x_ref[0] + y, gf_ref[...])


def _mix(a, act, k_mem, vt_mem, x, w_s, b_tile, w_out, g_final):
    b, s, d = x.shape
    tile = lambda w: pl.BlockSpec((1, ROW_TILE, w), lambda i, j: (i, j, 0))
    per_batch = lambda arr: pl.BlockSpec((1,) + arr.shape[1:], lambda i, j: (i, 0, 0))
    fixed = lambda arr: pl.BlockSpec(arr.shape, lambda i, j: (0, 0))
    return pl.pallas_call(
        _mix_kernel,
        grid=(b, s // ROW_TILE),
        in_specs=[tile(a.shape[2]), tile(act.shape[2]), per_batch(k_mem), per_batch(vt_mem), tile(d),
                  fixed(w_s), fixed(b_tile), fixed(w_out), fixed(g_final)],
        out_specs=tile(d),
        out_shape=jax.ShapeDtypeStruct((b, s, d), x.dtype),
        compiler_params=pltpu.CompilerParams(
            dimension_semantics=("arbitrary", "arbitrary"), vmem_limit_bytes=VMEM_LIMIT_BYTES),
        name="mix",
    )(a, act, k_mem, vt_mem, x, w_s, b_tile, w_out, g_final)


def kernel(x, mem, g_norm, w_in, w_sgu_spatial, b_sgu_spatial, g_sgu_v, g_mem, w_mem_kv, w_out, g_final):
    assert g_norm.shape[0] == 1, "the final norm is fused into the single layer's last kernel"
    b, s, d = x.shape
    k_mem, vt_mem = _memkv(mem, g_mem[0][None, :], w_mem_kv[0].astype(BF16))
    q, k, v, act = _inproj(x.reshape(b * s, d), g_norm[0][None, :], g_sgu_v[0][None, :], w_in[0].astype(BF16))
    a = _attn(q.reshape(b, s, -1), k.reshape(b, s, -1), v.reshape(b, s, -1))
    w_s = jnp.concatenate(list(w_sgu_spatial[0].astype(BF16)), axis=1)
    b_tile = jnp.repeat(b_sgu_spatial[0].T, SGU_GROUP, axis=1)
    return _mix(a, act.reshape(b, s, -1), k_mem, vt_mem, x, w_s, b_tile, w_out[0].astype(BF16), g_final[None, :])
```

```python
import functools
import math
from typing import NamedTuple

import jax
import jax.numpy as jnp
from jax import lax
from jax.experimental import pallas as pl
from jax.experimental.pallas import tpu as pltpu

F32 = jnp.float32
BF16 = jnp.bfloat16

EPS = 1e-6
NEG_INF = -1e30
LOG2_E = math.log2(math.e)

HEAD_DIM = 64
N_ATTN_HEADS = 8
ATTN_WIDTH = HEAD_DIM * N_ATTN_HEADS
DILATED_CONFIGS = ((128, 1), (512, 4), (2048, 16))
RADIUS = 64
SGU_WIDTH = 256
N_SGU_GROUPS = 4
SGU_GROUP = SGU_WIDTH // N_SGU_GROUPS
SGU_CHUNK = 128
MEM_WIDTH = 256
N_MEM_HEADS = 4
MEM_HEAD_DIM = MEM_WIDTH // N_MEM_HEADS
ACT_SPLITS = (ATTN_WIDTH, SGU_WIDTH, SGU_WIDTH, MEM_WIDTH, MEM_WIDTH)
ACT_WIDTH = sum(ACT_SPLITS)

LANES = 128
BF16_ROWS = 16
Q_BLOCK = 128
KEY_WINDOW = 2 * Q_BLOCK
HEADS_PER_STEP = LANES // HEAD_DIM
SCORE_LAG = 4
SCORE_SLOTS = 2 * SCORE_LAG
ROW_TILE = 512
VMEM_LIMIT_BYTES = 56 * 1024 * 1024


def _rms(x, g):
    return x * lax.rsqrt(jnp.mean(x * x, axis=-1, keepdims=True) + EPS) * g


def _silu(x):
    return x / (1.0 + jnp.exp(-x))


def _ones_rows(cols):
    first = lax.broadcasted_iota(jnp.int32, (BF16_ROWS, cols), 0) == 0
    return jnp.where(first, 1.0, 0.0).astype(BF16)


def _memkv_kernel(mem_ref, g_ref, w_ref, k_ref, vt_ref):
    h = _rms(mem_ref[0], g_ref[...]).astype(BF16)
    kv = jnp.dot(h, w_ref[...], preferred_element_type=F32)
    k_ref[0] = kv[:, 0:MEM_WIDTH].astype(BF16)
    vt_ref[0, 0:MEM_WIDTH, :] = kv[:, MEM_WIDTH:].T.astype(BF16)
    vt_ref[0, MEM_WIDTH:, :] = _ones_rows(kv.shape[0])


def _memkv(mem, g_mem, w_kv):
    b, m, d = mem.shape
    n = w_kv.shape[1]
    return pl.pallas_call(
        _memkv_kernel,
        grid=(b,),
        in_specs=[
            pl.BlockSpec((1, m, d), lambda i: (i, 0, 0)),
            pl.BlockSpec((1, d), lambda i: (0, 0)),
            pl.BlockSpec((d, n), lambda i: (0, 0)),
        ],
        out_specs=[
            pl.BlockSpec((1, m, MEM_WIDTH), lambda i: (i, 0, 0)),
            pl.BlockSpec((1, MEM_WIDTH + BF16_ROWS, m), lambda i: (i, 0, 0)),
        ],
        out_shape=[
            jax.ShapeDtypeStruct((b, m, MEM_WIDTH), BF16),
            jax.ShapeDtypeStruct((b, MEM_WIDTH + BF16_ROWS, m), BF16),
        ],
        compiler_params=pltpu.CompilerParams(
            dimension_semantics=("arbitrary",), vmem_limit_bytes=VMEM_LIMIT_BYTES),
        name="memkv",
    )(mem, g_mem, w_kv)


def _inproj_kernel(x_ref, g_ref, gv_ref, w_ref, q_ref, k_ref, v_ref, act_ref):
    h = _rms(x_ref[...], g_ref[...]).astype(BF16)
    at = [0]

    def proj(width):
        lo = at[0]
        at[0] = lo + width
        return jnp.dot(h, w_ref[:, lo:lo + width], preferred_element_type=F32)

    q_ref[...] = proj(ATTN_WIDTH) * (HEAD_DIM ** -0.5 * LOG2_E)
    k_ref[...] = proj(ATTN_WIDTH)
    v_ref[...] = proj(ATTN_WIDTH)
    za = proj(ATTN_WIDTH)
    ub = proj(SGU_WIDTH)
    vb = proj(SGU_WIDTH)
    zb = proj(SGU_WIDTH)
    qm = proj(MEM_WIDTH)
    zm = proj(MEM_WIDTH)
    acts = (_silu(za), _silu(zb) * jax.nn.gelu(ub), _rms(jax.nn.gelu(vb), gv_ref[...]),
            qm * (MEM_HEAD_DIM ** -0.5 * LOG2_E), _silu(zm))
    lo = 0
    for width, val in zip(ACT_SPLITS, acts):
        act_ref[:, lo:lo + width] = val.astype(BF16)
        lo += width


def _inproj(x2d, g_norm, g_v, w_in):
    n, d = x2d.shape
    cols = w_in.shape[1]
    assert cols == 4 * ATTN_WIDTH + 3 * SGU_WIDTH + 2 * MEM_WIDTH
    row = lambda i: (i, 0)
    fixed = lambda i: (0, 0)
    return pl.pallas_call(
        _inproj_kernel,
        grid=(n // ROW_TILE,),
        in_specs=[
            pl.BlockSpec((ROW_TILE, d), row),
            pl.BlockSpec((1, d), fixed),
            pl.BlockSpec((1, SGU_WIDTH), fixed),
            pl.BlockSpec((d, cols), fixed),
        ],
        out_specs=[
            pl.BlockSpec((ROW_TILE, ATTN_WIDTH), row),
            pl.BlockSpec((ROW_TILE, ATTN_WIDTH), row),
            pl.BlockSpec((ROW_TILE, ATTN_WIDTH), row),
            pl.BlockSpec((ROW_TILE, ACT_WIDTH), row),
        ],
        out_shape=[
            jax.ShapeDtypeStruct((n, ATTN_WIDTH), F32),
            jax.ShapeDtypeStruct((n, ATTN_WIDTH), F32),
            jax.ShapeDtypeStruct((n, ATTN_WIDTH), F32),
            jax.ShapeDtypeStruct((n, ACT_WIDTH), BF16),
        ],
        compiler_params=pltpu.CompilerParams(
            dimension_semantics=("arbitrary",), vmem_limit_bytes=VMEM_LIMIT_BYTES),
        name="inproj",
    )(x2d, g_norm, g_v, w_in)


SHIFT, EDGE, WHOLE = "shift", "edge", "whole"


def _bias_tile(kind, dilation, first_head):
    shape = (KEY_WINDOW, HEADS_PER_STEP * Q_BLOCK)
    key = lax.broadcasted_iota(jnp.int32, shape, 0)
    col = lax.broadcasted_iota(jnp.int32, shape, 1)
    second = col >= Q_BLOCK
    qi = jnp.where(second, col - Q_BLOCK, col)
    if kind == SHIFT:
        rel = jnp.abs(qi + RADIUS - key)
        valid = rel <= RADIUS
    elif kind == WHOLE:
        rel = jnp.abs(qi - key)
        valid = rel <= RADIUS
    else:
        upper = key >= Q_BLOCK
        rel = jnp.abs(qi - jnp.where(upper, key - Q_BLOCK, key))
        valid = (rel <= RADIUS) & (upper == (qi >= RADIUS))
    head = (first_head + jnp.where(second, 1, 0)).astype(F32)
    slope = jnp.exp2(-8.0 * (head + 1.0) / N_ATTN_HEADS)
    dist = (rel * dilation).astype(F32)
    return jnp.where(valid, -slope * dist * LOG2_E, NEG_INF)


class _Block(NamedTuple):
    cfg: int
    tile: int
    q_rows: tuple
    k_rows: tuple
    out_rows: tuple
    merge: tuple = ()


def _cfg_order():
    dils = [d for _, d in DILATED_CONFIGS]
    return sorted(range(len(dils)), key=lambda c: (dils[c] == 1, dils[c]))


def _attn_plan(seq):
    blocks, tiles = [], []
    for c in _cfg_order():
        dil = DILATED_CONFIGS[c][1]
        cls = seq // dil
        if cls == Q_BLOCK:
            tiles.append((WHOLE, dil))
            whole = len(tiles) - 1
        else:
            tiles.append((SHIFT, dil))
            tiles.append((EDGE, dil))
            shift, edge = len(tiles) - 2, len(tiles) - 1
        for r in range(dil):
            base = r * cls
            if cls == Q_BLOCK:
                blocks.append(_Block(c, whole, ((base, Q_BLOCK),), ((base, Q_BLOCK),), ((r, Q_BLOCK),)))
                continue
            last = cls - RADIUS
            blocks.append(_Block(c, edge, ((base, RADIUS), (base + last, RADIUS)),
                                 ((base, Q_BLOCK), (base + cls - Q_BLOCK, Q_BLOCK)),
                                 ((r, RADIUS), (dil * last + r, RADIUS))))
            n_shift = cls // Q_BLOCK - 1
            for j in range(n_shift):
                u = RADIUS + j * Q_BLOCK
                done = ((j,) + ((j + 1,) if j == n_shift - 1 else ())) if dil == 1 else ()
                blocks.append(_Block(c, shift, ((base + u, Q_BLOCK),), ((base + j * Q_BLOCK, KEY_WINDOW),),
                                     ((dil * u + r, Q_BLOCK),), done))
    return blocks, tiles


def _rows(ref, lead, ranges):
    parts = [ref[lead + (pl.ds(s, n), slice(None))] for s, n in ranges]
    return parts[0] if len(parts) == 1 else jnp.concatenate(parts, axis=0)


def _score_block(blk, zero, q_scr, k_scr, bias_scr, s_scr):
    q2 = jnp.concatenate([_rows(q_scr, (h + zero,), blk.q_rows) for h in range(HEADS_PER_STEP)], axis=0)
    k_win = _rows(k_scr, (zero,), blk.k_rows)
    window = k_win.shape[0]
    s = lax.dot_general(k_win, q2, (((1,), (1,)), ((), ())), preferred_element_type=F32)
    s_scr[0, 0:window, :] = s + bias_scr[blk.tile, 0:window, :]


def _value_block(blk, zero, vt_scr, s_scr, o_scr, lse_scr):
    cfg = blk.cfg
    dil = DILATED_CONFIGS[cfg][1]
    window = sum(n for _, n in blk.k_rows)
    ms, ps = [], []
    for h in range(HEADS_PER_STEP):
        s = s_scr[zero, 0:window, h * Q_BLOCK:(h + 1) * Q_BLOCK]
        m = jnp.max(s, axis=0, keepdims=True)
        ms.append(m)
        ps.append(jnp.exp2(s - m).astype(BF16))
    vt = jnp.concatenate([vt_scr[zero, :, s0:s0 + n] for s0, n in blk.k_rows], axis=1)
    ot = jnp.dot(vt, jnp.concatenate(ps, axis=1), preferred_element_type=F32)
    outs, lses = [], []
    for h in range(HEADS_PER_STEP):
        cols = slice(h * Q_BLOCK, (h + 1) * Q_BLOCK)
        l = ot[LANES:LANES + 1, cols]
        outs.append(ot[h * HEAD_DIM:(h + 1) * HEAD_DIM, cols] * (1.0 / l))
        lses.append(ms[h] + jnp.log2(l))
    o = jnp.concatenate(outs, axis=0).T
    lse = jnp.concatenate([jnp.broadcast_to(x, (HEAD_DIM, Q_BLOCK)) for x in lses], axis=0).T
    at = 0
    for start, n in blk.out_rows:
        dst = pl.ds(start, n) if dil == 1 else pl.ds(start, n, stride=dil)
        o_scr[cfg, dst, :] = o[at:at + n]
        lse_scr[cfg, dst, :] = lse[at:at + n]
        at += n


def _attn_kernel(q_ref, k_ref, v_ref, a_ref, tmp_scr, bias_scr, o_scr, lse_scr, *scrs, seq):
    n_cfg = len(DILATED_CONFIGS)
    q_scrs, k_scrs, vt_scrs = scrs[0:n_cfg], scrs[n_cfg:2 * n_cfg], scrs[2 * n_cfg:3 * n_cfg]
    s_scrs = scrs[3 * n_cfg:]
    blocks, tiles = _attn_plan(seq)

    @pl.when(pl.program_id(1) == 0)
    def _():
        first_head = pl.program_id(0) * HEADS_PER_STEP
        for t, (kind, dil) in enumerate(tiles):
            bias_scr[t] = _bias_tile(kind, dil, first_head)

    first = lax.broadcasted_iota(jnp.int32, (1, LANES), 1) < HEAD_DIM

    def put(c, dst, q, k, v):
        q_scrs[c][0, dst, :] = jnp.where(first, q, 0.0).astype(BF16)
        q_scrs[c][1, dst, :] = jnp.where(first, 0.0, q).astype(BF16)
        k_scrs[c][0, dst, :] = k.astype(BF16)
        for i in range(v.shape[0] // LANES):
            at = dst.start + i * LANES
            vt_scrs[c][0, 0:LANES, at:at + LANES] = v[i * LANES:(i + 1) * LANES].T.astype(BF16)

    refs = (q_ref, k_ref, v_ref)
    for c in _cfg_order():
        dil = DILATED_CONFIGS[c][1]
        vt_scrs[c][0, LANES:LANES + BF16_ROWS, :] = _ones_rows(seq)
        cls = seq // dil
        if dil == 1:
            put(c, slice(0, seq), *(ref[0] for ref in refs))
        elif dil == 4:
            for r in range(dil):
                dst = slice(r * cls, (r + 1) * cls)
                vals = [ref[0, pl.ds(r, cls, stride=dil), :] for ref in refs]
                for t, val in enumerate(vals):
                    tmp_scr[t, dst, :] = val
                put(c, dst, *vals)
        else:
            sub = dil // 4
            for r in range(dil):
                src = pl.ds((r % 4) * (seq // 4) + r // 4, cls, stride=sub)
                put(c, slice(r * cls, (r + 1) * cls), *(tmp_scr[t, src, :] for t in range(len(refs))))

    def merge(j):
        rows = pl.ds(j * Q_BLOCK, Q_BLOCK)
        lses = [lse_scr[c, rows, :] for c in range(n_cfg)]
        top = functools.reduce(jnp.maximum, lses)
        es = [jnp.exp2(x - top) for x in lses]
        num = functools.reduce(lambda a, b: a + b, [e * o_scr[c, rows, :] for c, e in enumerate(es)])
        den = functools.reduce(lambda a, b: a + b, es)
        a_ref[0, rows, :] = (num / den).astype(a_ref.dtype)

    zero = jnp.minimum(pl.program_id(1), 0)
    for t in range(len(blocks) + SCORE_LAG):
        if t < len(blocks):
            blk = blocks[t]
            _score_block(blk, zero, q_scrs[blk.cfg], k_scrs[blk.cfg], bias_scr, s_scrs[t % SCORE_SLOTS])
        if t >= SCORE_LAG:
            u = t - SCORE_LAG
            blk = blocks[u]
            _value_block(blk, zero, vt_scrs[blk.cfg], s_scrs[u % SCORE_SLOTS], o_scr, lse_scr)
            for j in blk.merge:
                merge(j)


def _attn(q, k, v):
    b, s, w = q.shape
    n_cfg = len(DILATED_CONFIGS)
    assert [d for _, d in DILATED_CONFIGS] == [1, 4, 16] and s % (16 * Q_BLOCK) == 0
    n_tiles = len(_attn_plan(s)[1])
    blk = pl.BlockSpec((1, s, LANES), lambda j, i: (i, 0, j))
    return pl.pallas_call(
        functools.partial(_attn_kernel, seq=s),
        grid=(w // LANES, b),
        in_specs=[blk, blk, blk],
        out_specs=blk,
        out_shape=jax.ShapeDtypeStruct((b, s, w), BF16),
        scratch_shapes=[
            pltpu.VMEM((3, s, LANES), F32),
            pltpu.VMEM((n_tiles, KEY_WINDOW, HEADS_PER_STEP * Q_BLOCK), F32),
            pltpu.VMEM((n_cfg, s, LANES), F32),
            pltpu.VMEM((n_cfg, s, LANES), F32),
        ] + [pltpu.VMEM((HEADS_PER_STEP, s, LANES), BF16)] * n_cfg
        + [pltpu.VMEM((1, s, LANES), BF16)] * n_cfg
        + [pltpu.VMEM((1, LANES + BF16_ROWS, s), BF16)] * n_cfg
        + [pltpu.VMEM((1, KEY_WINDOW, HEADS_PER_STEP * Q_BLOCK), F32)] * SCORE_SLOTS,
        compiler_params=pltpu.CompilerParams(
            dimension_semantics=("arbitrary", "arbitrary"), vmem_limit_bytes=VMEM_LIMIT_BYTES),
        name="attn",
    )(q, k, v)


def _mix_kernel(a_ref, act_ref, km_ref, vtm_ref, x_ref, ws_ref, bs_ref, wo_ref, gf_ref, o_ref):
    tm = x_ref.shape[1]
    lo = 0
    cols = []
    for width in ACT_SPLITS:
        cols.append(slice(lo, lo + width))
        lo += width
    sa, gu, vn, qm, sm = (act_ref[0, :, c] for c in cols)

    ga = sa * a_ref[0]

    group = lax.broadcasted_iota(jnp.int32, (SGU_CHUNK, SGU_WIDTH), 1) // SGU_GROUP
    mixed = []
    for c in range(tm // SGU_CHUNK):
        vc = vn[c * SGU_CHUNK:(c + 1) * SGU_CHUNK]
        stacked = jnp.concatenate([jnp.where(group == g, vc, jnp.zeros_like(vc)) for g in range(N_SGU_GROUPS)],
                                  axis=0)
        mixed.append(jnp.dot(ws_ref[...], stacked, preferred_element_type=F32) + bs_ref[...])
    gb = (gu.astype(F32) * jnp.concatenate(mixed, axis=0)).astype(BF16)

    mhead = lax.broadcasted_iota(jnp.int32, (tm, MEM_WIDTH), 1) // MEM_HEAD_DIM
    q4 = jnp.concatenate([jnp.where(mhead == h, qm, jnp.zeros_like(qm)) for h in range(N_MEM_HEADS)], axis=0)
    s = lax.dot_general(km_ref[0], q4, (((1,), (1,)), ((), ())), preferred_element_type=F32)
    p = jnp.exp2(s - jnp.max(s, axis=0, keepdims=True)).astype(BF16)
    ot = jnp.dot(vtm_ref[0], p, preferred_element_type=F32)
    mo = []
    for h in range(N_MEM_HEADS):
        c = slice(h * tm, (h + 1) * tm)
        mo.append(ot[h * MEM_HEAD_DIM:(h + 1) * MEM_HEAD_DIM, c] * (1.0 / ot[MEM_WIDTH:MEM_WIDTH + 1, c]))
    mo = jnp.concatenate(mo, axis=0).T
    gm = (sm.astype(F32) * mo).astype(BF16)

    e1 = ATTN_WIDTH
    e2 = e1 + SGU_WIDTH
    y = jnp.dot(ga, wo_ref[0:e1, :], preferred_element_type=F32)
    y = y + jnp.dot(gb, wo_ref[e1:e2, :], preferred_element_type=F32)
    y = y + jnp.dot(gm, wo_ref[e2:, :], preferred_element_type=F32)
    o_ref[0] = _rms(x_ref[0] + y, gf_ref[...])


def _mix(a, act, k_mem, vt_mem, x, w_s, b_tile, w_out, g_final):
    b, s, d = x.shape
    tile = lambda w: pl.BlockSpec((1, ROW_TILE, w), lambda i, j: (i, j, 0))
    per_batch = lambda arr: pl.BlockSpec((1,) + arr.shape[1:], lambda i, j: (i, 0, 0))
    fixed = lambda arr: pl.BlockSpec(arr.shape, lambda i, j: (0, 0))
    return pl.pallas_call(
        _mix_kernel,
        grid=(b, s // ROW_TILE),
        in_specs=[tile(a.shape[2]), tile(act.shape[2]), per_batch(k_mem), per_batch(vt_mem), tile(d),
                  fixed(w_s), fixed(b_tile), fixed(w_out), fixed(g_final)],
        out_specs=tile(d),
        out_shape=jax.ShapeDtypeStruct((b, s, d), x.dtype),
        compiler_params=pltpu.CompilerParams(
            dimension_semantics=("arbitrary", "arbitrary"), vmem_limit_bytes=VMEM_LIMIT_BYTES),
        name="mix",
    )(a, act, k_mem, vt_mem, x, w_s, b_tile, w_out, g_final)


def kernel(x, mem, g_norm, w_in, w_sgu_spatial, b_sgu_spatial, g_sgu_v, g_mem, w_mem_kv, w_out, g_final):
    assert g_norm.shape[0] == 1, "the final norm is fused into the single layer's last kernel"
    b, s, d = x.shape
    k_mem, vt_mem = _memkv(mem, g_mem[0][None, :], w_mem_kv[0].astype(BF16))
    q, k, v, act = _inproj(x.reshape(b * s, d), g_norm[0][None, :], g_sgu_v[0][None, :], w_in[0].astype(BF16))
    a = _attn(q.reshape(b, s, -1), k.reshape(b, s, -1), v.reshape(b, s, -1))
    w_s = jnp.concatenate(list(w_sgu_spatial[0].astype(BF16)), axis=1)
    b_tile = jnp.repeat(b_sgu_spatial[0].T, SGU_GROUP, axis=1)
    return _mix(a, act.reshape(b, s, -1), k_mem, vt_mem, x, w_s, b_tile, w_out[0].astype(BF16), g_final[None, :])
```

```python
import functools
import math
from typing import NamedTuple

import jax
import jax.numpy as jnp
from jax import lax
from jax.experimental import pallas as pl
from jax.experimental.pallas import tpu as pltpu

F32 = jnp.float32
BF16 = jnp.bfloat16

EPS = 1e-6
NEG_INF = -1e30
LOG2_E = math.log2(math.e)

HEAD_DIM = 64
N_ATTN_HEADS = 8
ATTN_WIDTH = HEAD_DIM * N_ATTN_HEADS
DILATED_CONFIGS = ((128, 1), (512, 4), (2048, 16))
RADIUS = 64
SGU_WIDTH = 256
N_SGU_GROUPS = 4
SGU_GROUP = SGU_WIDTH // N_SGU_GROUPS
SGU_CHUNK = 128
MEM_WIDTH = 256
N_MEM_HEADS = 4
MEM_HEAD_DIM = MEM_WIDTH // N_MEM_HEADS
ACT_SPLITS = (ATTN_WIDTH, SGU_WIDTH, SGU_WIDTH, MEM_WIDTH, MEM_WIDTH)
ACT_WIDTH = sum(ACT_SPLITS)

LANES = 128
BF16_ROWS = 16
Q_BLOCK = 128
KEY_WINDOW = 2 * Q_BLOCK
HEADS_PER_STEP = LANES // HEAD_DIM
SCORE_LAG = 4
SCORE_SLOTS = 2 * SCORE_LAG
ROW_TILE = 512
VMEM_LIMIT_BYTES = 56 * 1024 * 1024


def _rms(x, g):
    return x * lax.rsqrt(jnp.mean(x * x, axis=-1, keepdims=True) + EPS) * g


def _silu(x):
    return x / (1.0 + jnp.exp(-x))


def _ones_rows(cols):
    first = lax.broadcasted_iota(jnp.int32, (BF16_ROWS, cols), 0) == 0
    return jnp.where(first, 1.0, 0.0).astype(BF16)


def _memkv_kernel(mem_ref, g_ref, w_ref, k_ref, vt_ref):
    h = _rms(mem_ref[0], g_ref[...]).astype(BF16)
    kv = jnp.dot(h, w_ref[...], preferred_element_type=F32)
    k_ref[0] = kv[:, 0:MEM_WIDTH].astype(BF16)
    vt_ref[0, 0:MEM_WIDTH, :] = kv[:, MEM_WIDTH:].T.astype(BF16)
    vt_ref[0, MEM_WIDTH:, :] = _ones_rows(kv.shape[0])


def _memkv(mem, g_mem, w_kv):
    b, m, d = mem.shape
    n = w_kv.shape[1]
    return pl.pallas_call(
        _memkv_kernel,
        grid=(b,),
        in_specs=[
            pl.BlockSpec((1, m, d), lambda i: (i, 0, 0)),
            pl.BlockSpec((1, d), lambda i: (0, 0)),
            pl.BlockSpec((d, n), lambda i: (0, 0)),
        ],
        out_specs=[
            pl.BlockSpec((1, m, MEM_WIDTH), lambda i: (i, 0, 0)),
            pl.BlockSpec((1, MEM_WIDTH + BF16_ROWS, m), lambda i: (i, 0, 0)),
        ],
        out_shape=[
            jax.ShapeDtypeStruct((b, m, MEM_WIDTH), BF16),
            jax.ShapeDtypeStruct((b, MEM_WIDTH + BF16_ROWS, m), BF16),
        ],
        compiler_params=pltpu.CompilerParams(
            dimension_semantics=("arbitrary",), vmem_limit_bytes=VMEM_LIMIT_BYTES),
        name="memkv",
    )(mem, g_mem, w_kv)


def _inproj_kernel(x_ref, g_ref, gv_ref, w_ref, q_ref, k_ref, v_ref, act_ref):
    h = _rms(x_ref[...], g_ref[...]).astype(BF16)
    at = [0]

    def proj(width):
        lo = at[0]
        at[0] = lo + width
        return jnp.dot(h, w_ref[:, lo:lo + width], preferred_element_type=F32)

    q_ref[...] = proj(ATTN_WIDTH) * (HEAD_DIM ** -0.5 * LOG2_E)
    k_ref[...] = proj(ATTN_WIDTH)
    v_ref[...] = proj(ATTN_WIDTH)
    za = proj(ATTN_WIDTH)
    ub = proj(SGU_WIDTH)
    vb = proj(SGU_WIDTH)
    zb = proj(SGU_WIDTH)
    qm = proj(MEM_WIDTH)
    zm = proj(MEM_WIDTH)
    acts = (_silu(za), _silu(zb) * jax.nn.gelu(ub), _rms(jax.nn.gelu(vb), gv_ref[...]),
            qm * (MEM_HEAD_DIM ** -0.5 * LOG2_E), _silu(zm))
    lo = 0
    for width, val in zip(ACT_SPLITS, acts):
        act_ref[:, lo:lo + width] = val.astype(BF16)
        lo += width


def _inproj(x2d, g_norm, g_v, w_in):
    n, d = x2d.shape
    cols = w_in.shape[1]
    assert cols == 4 * ATTN_WIDTH + 3 * SGU_WIDTH + 2 * MEM_WIDTH
    row = lambda i: (i, 0)
    fixed = lambda i: (0, 0)
    return pl.pallas_call(
        _inproj_kernel,
        grid=(n // ROW_TILE,),
        in_specs=[
            pl.BlockSpec((ROW_TILE, d), row),
            pl.BlockSpec((1, d), fixed),
            pl.BlockSpec((1, SGU_WIDTH), fixed),
            pl.BlockSpec((d, cols), fixed),
        ],
        out_specs=[
            pl.BlockSpec((ROW_TILE, ATTN_WIDTH), row),
            pl.BlockSpec((ROW_TILE, ATTN_WIDTH), row),
            pl.BlockSpec((ROW_TILE, ATTN_WIDTH), row),
            pl.BlockSpec((ROW_TILE, ACT_WIDTH), row),
        ],
        out_shape=[
            jax.ShapeDtypeStruct((n, ATTN_WIDTH), F32),
            jax.ShapeDtypeStruct((n, ATTN_WIDTH), F32),
            jax.ShapeDtypeStruct((n, ATTN_WIDTH), F32),
            jax.ShapeDtypeStruct((n, ACT_WIDTH), BF16),
        ],
        compiler_params=pltpu.CompilerParams(
            dimension_semantics=("arbitrary",), vmem_limit_bytes=VMEM_LIMIT_BYTES),
        name="inproj",
    )(x2d, g_norm, g_v, w_in)


SHIFT, EDGE, WHOLE = "shift", "edge", "whole"


def _bias_tile(kind, dilation, first_head):
    shape = (KEY_WINDOW, HEADS_PER_STEP * Q_BLOCK)
    key = lax.broadcasted_iota(jnp.int32, shape, 0)
    col = lax.broadcasted_iota(jnp.int32, shape, 1)
    second = col >= Q_BLOCK
    qi = jnp.where(second, col - Q_BLOCK, col)
    if kind == SHIFT:
        rel = jnp.abs(qi + RADIUS - key)
        valid = rel <= RADIUS
    elif kind == WHOLE:
        rel = jnp.abs(qi - key)
        valid = rel <= RADIUS
    else:
        upper = key >= Q_BLOCK
        rel = jnp.abs(qi - jnp.where(upper, key - Q_BLOCK, key))
        valid = (rel <= RADIUS) & (upper == (qi >= RADIUS))
    head = (first_head + jnp.where(second, 1, 0)).astype(F32)
    slope = jnp.exp2(-8.0 * (head + 1.0) / N_ATTN_HEADS)
    dist = (rel * dilation).astype(F32)
    return jnp.where(valid, -slope * dist * LOG2_E, NEG_INF)


class _Block(NamedTuple):
    cfg: int
    tile: int
    q_rows: tuple
    k_rows: tuple
    out_rows: tuple
    out_stride: int


def _cfg_order():
    return sorted(range(len(DILATED_CONFIGS)), key=lambda c: -DILATED_CONFIGS[c][1])


def _attn_plan(seq):
    blocks, tiles = [], []
    order = _cfg_order()
    for n, c in enumerate(order):
        dil = DILATED_CONFIGS[c][1]
        nxt = DILATED_CONFIGS[order[n + 1]][1] if n + 1 < len(order) else 1
        step = dil // nxt
        cls = seq // dil
        if cls == Q_BLOCK:
            tiles.append((WHOLE, dil))
            whole = len(tiles) - 1
        else:
            tiles.append((SHIFT, dil))
            tiles.append((EDGE, dil))
            shift, edge = len(tiles) - 2, len(tiles) - 1
        for r in range(dil):
            base = r * cls
            sink = (r % nxt) * (seq // nxt) + r // nxt
            if cls == Q_BLOCK:
                blocks.append(_Block(c, whole, ((base, Q_BLOCK),), ((base, Q_BLOCK),), ((sink, Q_BLOCK),), step))
                continue
            last = cls - RADIUS
            blocks.append(_Block(c, edge, ((base, RADIUS), (base + last, RADIUS)),
                                 ((base, Q_BLOCK), (base + cls - Q_BLOCK, Q_BLOCK)),
                                 ((sink, RADIUS), (sink + step * last, RADIUS)), step))
            for j in range(cls // Q_BLOCK - 1):
                u = RADIUS + j * Q_BLOCK
                blocks.append(_Block(c, shift, ((base + u, Q_BLOCK),), ((base + j * Q_BLOCK, KEY_WINDOW),),
                                     ((sink + step * u, Q_BLOCK),), step))
    return blocks, tiles


def _rows(ref, lead, ranges):
    parts = [ref[lead + (pl.ds(s, n), slice(None))] for s, n in ranges]
    return parts[0] if len(parts) == 1 else jnp.concatenate(parts, axis=0)


def _score_block(blk, zero, q_scr, k_scr, bias_scr, s_scr):
    q2 = jnp.concatenate([_rows(q_scr, (h + zero,), blk.q_rows) for h in range(HEADS_PER_STEP)], axis=0)
    k_win = _rows(k_scr, (zero,), blk.k_rows)
    window = k_win.shape[0]
    s = lax.dot_general(k_win, q2, (((1,), (1,)), ((), ())), preferred_element_type=F32)
    s_scr[0, 0:window, :] = s + bias_scr[blk.tile, 0:window, :]


def _value_block(blk, zero, vt_scr, s_scr, prev, sink, a_ref):
    window = sum(n for _, n in blk.k_rows)
    ms, ps = [], []
    for h in range(HEADS_PER_STEP):
        cols = slice(h * Q_BLOCK, (h + 1) * Q_BLOCK)
        s = s_scr[zero, 0:window, cols]
        m = jnp.max(s, axis=0, keepdims=True)
        ms.append(m)
        ps.append(jnp.exp2(s - m).astype(BF16))
    vt = jnp.concatenate([vt_scr[zero, :, s0:s0 + n] for s0, n in blk.k_rows], axis=1)
    ot = jnp.dot(vt, jnp.concatenate(ps, axis=1), preferred_element_type=F32)
    outs, lses = [], []
    for h in range(HEADS_PER_STEP):
        cols = slice(h * Q_BLOCK, (h + 1) * Q_BLOCK)
        l = ot[LANES:LANES + 1, cols]
        outs.append(ot[h * HEAD_DIM:(h + 1) * HEAD_DIM, cols] * (1.0 / l))
        lses.append(ms[h] + jnp.log2(l))
    o = jnp.concatenate(outs, axis=0).T
    lse = jnp.concatenate([jnp.broadcast_to(x, (HEAD_DIM, Q_BLOCK)) for x in lses], axis=0).T
    if prev is not None:
        o_prev, lse_prev = (_rows(ref, (), blk.q_rows) for ref in prev)
        top = jnp.maximum(lse, lse_prev)
        e, e_prev = jnp.exp2(lse - top), jnp.exp2(lse_prev - top)
        den = e + e_prev
        o = (e * o + e_prev * o_prev) * (1.0 / den)
        lse = top + jnp.log2(den)
    at = 0
    for start, n in blk.out_rows:
        dst = pl.ds(start, n) if blk.out_stride == 1 else pl.ds(start, n, stride=blk.out_stride)
        if sink is None:
            a_ref[0, dst, :] = o[at:at + n].astype(a_ref.dtype)
        else:
            sink[0][dst, :] = o[at:at + n]
            sink[1][dst, :] = lse[at:at + n]
        at += n


def _attn_kernel(q_ref, k_ref, v_ref, a_ref, tmp_scr, bias_scr, *scrs, seq):
    n_cfg = len(DILATED_CONFIGS)
    q_scrs, k_scrs, vt_scrs = scrs[0:n_cfg], scrs[n_cfg:2 * n_cfg], scrs[2 * n_cfg:3 * n_cfg]
    hand = scrs[3 * n_cfg:5 * n_cfg - 2]
    hand = [hand[2 * i:2 * i + 2] for i in range(n_cfg - 1)]
    s_scrs = scrs[5 * n_cfg - 2:]
    blocks, tiles = _attn_plan(seq)
    order = _cfg_order()

    @pl.when(pl.program_id(1) == 0)
    def _():
        first_head = pl.program_id(0) * HEADS_PER_STEP
        for t, (kind, dil) in enumerate(tiles):
            bias_scr[t] = _bias_tile(kind, dil, first_head)

    first = lax.broadcasted_iota(jnp.int32, (1, LANES), 1) < HEAD_DIM

    def put(c, dst, q, k, v):
        q_scrs[c][0, dst, :] = jnp.where(first, q, 0.0).astype(BF16)
        q_scrs[c][1, dst, :] = jnp.where(first, 0.0, q).astype(BF16)
        k_scrs[c][0, dst, :] = k.astype(BF16)
        for i in range(v.shape[0] // LANES):
            at = dst.start + i * LANES
            vt_scrs[c][0, 0:LANES, at:at + LANES] = v[i * LANES:(i + 1) * LANES].T.astype(BF16)

    refs = (q_ref, k_ref, v_ref)
    for c in sorted(order, key=lambda c: (DILATED_CONFIGS[c][1] == 1, DILATED_CONFIGS[c][1])):
        dil = DILATED_CONFIGS[c][1]
        vt_scrs[c][0, LANES:LANES + BF16_ROWS, :] = _ones_rows(seq)
        cls = seq // dil
        if dil == 1:
            put(c, slice(0, seq), *(ref[0] for ref in refs))
        elif dil == 4:
            for r in range(dil):
                dst = slice(r * cls, (r + 1) * cls)
                vals = [ref[0, pl.ds(r, cls, stride=dil), :] for ref in refs]
                for t, val in enumerate(vals):
                    tmp_scr[t, dst, :] = val
                put(c, dst, *vals)
        else:
            sub = dil // 4
            for r in range(dil):
                src = pl.ds((r % 4) * (seq // 4) + r // 4, cls, stride=sub)
                put(c, slice(r * cls, (r + 1) * cls), *(tmp_scr[t, src, :] for t in range(len(refs))))

    zero = jnp.minimum(pl.program_id(1), 0)
    for t in range(len(blocks) + SCORE_LAG):
        if t < len(blocks):
            blk = blocks[t]
            _score_block(blk, zero, q_scrs[blk.cfg], k_scrs[blk.cfg], bias_scr, s_scrs[t % SCORE_SLOTS])
        if t >= SCORE_LAG:
            u = t - SCORE_LAG
            blk = blocks[u]
            n = order.index(blk.cfg)
            _value_block(blk, zero, vt_scrs[blk.cfg], s_scrs[u % SCORE_SLOTS],
                         hand[n - 1] if n > 0 else None, hand[n] if n + 1 < n_cfg else None, a_ref)


def _attn(q, k, v):
    b, s, w = q.shape
    n_cfg = len(DILATED_CONFIGS)
    assert [d for _, d in DILATED_CONFIGS] == [1, 4, 16] and s % (16 * Q_BLOCK) == 0
    n_tiles = len(_attn_plan(s)[1])
    blk = pl.BlockSpec((1, s, LANES), lambda j, i: (i, 0, j))
    return pl.pallas_call(
        functools.partial(_attn_kernel, seq=s),
        grid=(w // LANES, b),
        in_specs=[blk, blk, blk],
        out_specs=blk,
        out_shape=jax.ShapeDtypeStruct((b, s, w), BF16),
        scratch_shapes=[
            pltpu.VMEM((3, s, LANES), F32),
            pltpu.VMEM((n_tiles, KEY_WINDOW, HEADS_PER_STEP * Q_BLOCK), F32),
        ] + [pltpu.VMEM((HEADS_PER_STEP, s, LANES), BF16)] * n_cfg
        + [pltpu.VMEM((1, s, LANES), BF16)] * n_cfg
        + [pltpu.VMEM((1, LANES + BF16_ROWS, s), BF16)] * n_cfg
        + [pltpu.VMEM((s, LANES), F32)] * (2 * (n_cfg - 1))
        + [pltpu.VMEM((1, KEY_WINDOW, HEADS_PER_STEP * Q_BLOCK), F32)] * SCORE_SLOTS,
        compiler_params=pltpu.CompilerParams(
            dimension_semantics=("arbitrary", "arbitrary"), vmem_limit_bytes=VMEM_LIMIT_BYTES),
        name="attn",
    )(q, k, v)


def _mix_kernel(a_ref, act_ref, km_ref, vtm_ref, x_ref, ws_ref, bs_ref, wo_ref, gf_ref, o_ref):
    tm = x_ref.shape[1]
    lo = 0
    cols = []
    for width in ACT_SPLITS:
        cols.append(slice(lo, lo + width))
        lo += width
    sa, gu, vn, qm, sm = (act_ref[0, :, c] for c in cols)

    ga = sa * a_ref[0]

    group = lax.broadcasted_iota(jnp.int32, (SGU_CHUNK, SGU_WIDTH), 1) // SGU_GROUP
    mixed = []
    for c in range(tm // SGU_CHUNK):
        vc = vn[c * SGU_CHUNK:(c + 1) * SGU_CHUNK]
        stacked = jnp.concatenate([jnp.where(group == g, vc, jnp.zeros_like(vc)) for g in range(N_SGU_GROUPS)],
                                  axis=0)
        mixed.append(jnp.dot(ws_ref[...], stacked, preferred_element_type=F32) + bs_ref[...])
    gb = (gu.astype(F32) * jnp.concatenate(mixed, axis=0)).astype(BF16)

    mhead = lax.broadcasted_iota(jnp.int32, (tm, MEM_WIDTH), 1) // MEM_HEAD_DIM
    q4 = jnp.concatenate([jnp.where(mhead == h, qm, jnp.zeros_like(qm)) for h in range(N_MEM_HEADS)], axis=0)
    s = lax.dot_general(km_ref[0], q4, (((1,), (1,)), ((), ())), preferred_element_type=F32)
    p = jnp.exp2(s - jnp.max(s, axis=0, keepdims=True)).astype(BF16)
    ot = jnp.dot(vtm_ref[0], p, preferred_element_type=F32)
    mo = []
    for h in range(N_MEM_HEADS):
        c = slice(h * tm, (h + 1) * tm)
        mo.append(ot[h * MEM_HEAD_DIM:(h + 1) * MEM_HEAD_DIM, c] * (1.0 / ot[MEM_WIDTH:MEM_WIDTH + 1, c]))
    mo = jnp.concatenate(mo, axis=0).T
    gm = (sm.astype(F32) * mo).astype(BF16)

    e1 = ATTN_WIDTH
    e2 = e1 + SGU_WIDTH
    y = jnp.dot(ga, wo_ref[0:e1, :], preferred_element_type=F32)
    y = y + jnp.dot(gb, wo_ref[e1:e2, :], preferred_element_type=F32)
    y = y + jnp.dot(gm, wo_ref[e2:, :], preferred_element_type=F32)
    o_ref[0] = _rms(x_ref[0] + y, gf_ref[...])


def _mix(a, act, k_mem, vt_mem, x, w_s, b_tile, w_out, g_final):
    b, s, d = x.shape
    tile = lambda w: pl.BlockSpec((1, ROW_TILE, w), lambda i, j: (i, j, 0))
    per_batch = lambda arr: pl.BlockSpec((1,) + arr.shape[1:], lambda i, j: (i, 0, 0))
    fixed = lambda arr: pl.BlockSpec(arr.shape, lambda i, j: (0, 0))
    return pl.pallas_call(
        _mix_kernel,
        grid=(b, s // ROW_TILE),
        in_specs=[tile(a.shape[2]), tile(act.shape[2]), per_batch(k_mem), per_batch(vt_mem), tile(d),
                  fixed(w_s), fixed(b_tile), fixed(w_out), fixed(g_final)],
        out_specs=tile(d),
        out_shape=jax.ShapeDtypeStruct((b, s, d), x.dtype),
        compiler_params=pltpu.CompilerParams(
            dimension_semantics=("arbitrary", "arbitrary"), vmem_limit_bytes=VMEM_LIMIT_BYTES),
        name="mix",
    )(a, act, k_mem, vt_mem, x, w_s, b_tile, w_out, g_final)


def kernel(x, mem, g_norm, w_in, w_sgu_spatial, b_sgu_spatial, g_sgu_v, g_mem, w_mem_kv, w_out, g_final):
    assert g_norm.shape[0] == 1, "the final norm is fused into the single layer's last kernel"
    b, s, d = x.shape
    k_mem, vt_mem = _memkv(mem, g_mem[0][None, :], w_mem_kv[0].astype(BF16))
    q, k, v, act = _inproj(x.reshape(b * s, d), g_norm[0][None, :], g_sgu_v[0][None, :], w_in[0].astype(BF16))
    a = _attn(q.reshape(b, s, -1), k.reshape(b, s, -1), v.reshape(b, s, -1))
    w_s = jnp.concatenate(list(w_sgu_spatial[0].astype(BF16)), axis=1)
    b_tile = jnp.repeat(b_sgu_spatial[0].T, SGU_GROUP, axis=1)
    return _mix(a, act.reshape(b, s, -1), k_mem, vt_mem, x, w_s, b_tile, w_out[0].astype(BF16), g_final[None, :])
```

```python
import functools
import math
from typing import NamedTuple

import jax
import jax.numpy as jnp
from jax import lax
from jax.experimental import pallas as pl
from jax.experimental.pallas import tpu as pltpu

F32 = jnp.float32
BF16 = jnp.bfloat16

EPS = 1e-6
NEG_INF = -1e30
LOG2_E = math.log2(math.e)

HEAD_DIM = 64
N_ATTN_HEADS = 8
ATTN_WIDTH = HEAD_DIM * N_ATTN_HEADS
DILATED_CONFIGS = ((128, 1), (512, 4), (2048, 16))
RADIUS = 64
SGU_WIDTH = 256
N_SGU_GROUPS = 4
SGU_GROUP = SGU_WIDTH // N_SGU_GROUPS
SGU_CHUNK = 128
MEM_WIDTH = 256
N_MEM_HEADS = 4
MEM_HEAD_DIM = MEM_WIDTH // N_MEM_HEADS
ACT_SPLITS = (ATTN_WIDTH, SGU_WIDTH, SGU_WIDTH, MEM_WIDTH, MEM_WIDTH)
ACT_WIDTH = sum(ACT_SPLITS)

LANES = 128
BF16_ROWS = 16
MXU_COLS = 256
Q_BLOCK = 128
KEY_WINDOW = 2 * Q_BLOCK
HEADS_PER_STEP = LANES // HEAD_DIM
SCORE_LAG = 4
SCORE_SLOTS = 2 * SCORE_LAG
ROW_TILE = 512
VMEM_LIMIT_BYTES = 56 * 1024 * 1024


def _rms(x, g):
    return x * lax.rsqrt(jnp.mean(x * x, axis=-1, keepdims=True) + EPS) * g


def _silu(x):
    return x / (1.0 + jnp.exp(-x))


def _ones_rows(cols):
    first = lax.broadcasted_iota(jnp.int32, (BF16_ROWS, cols), 0) == 0
    return jnp.where(first, 1.0, 0.0).astype(BF16)


def _memkv_kernel(mem_ref, g_ref, w_ref, k_ref, vt_ref):
    h = _rms(mem_ref[0], g_ref[...]).astype(BF16)
    kv = jnp.dot(h, w_ref[...], preferred_element_type=F32)
    k_ref[0] = kv[:, 0:MEM_WIDTH].astype(BF16)
    vt_ref[0, 0:MEM_WIDTH, :] = kv[:, MEM_WIDTH:].T.astype(BF16)
    vt_ref[0, MEM_WIDTH:, :] = _ones_rows(kv.shape[0])


def _memkv(mem, g_mem, w_kv):
    b, m, d = mem.shape
    n = w_kv.shape[1]
    return pl.pallas_call(
        _memkv_kernel,
        grid=(b,),
        in_specs=[
            pl.BlockSpec((1, m, d), lambda i: (i, 0, 0)),
            pl.BlockSpec((1, d), lambda i: (0, 0)),
            pl.BlockSpec((d, n), lambda i: (0, 0)),
        ],
        out_specs=[
            pl.BlockSpec((1, m, MEM_WIDTH), lambda i: (i, 0, 0)),
            pl.BlockSpec((1, MEM_WIDTH + BF16_ROWS, m), lambda i: (i, 0, 0)),
        ],
        out_shape=[
            jax.ShapeDtypeStruct((b, m, MEM_WIDTH), BF16),
            jax.ShapeDtypeStruct((b, MEM_WIDTH + BF16_ROWS, m), BF16),
        ],
        compiler_params=pltpu.CompilerParams(
            dimension_semantics=("arbitrary",), vmem_limit_bytes=VMEM_LIMIT_BYTES),
        name="memkv",
    )(mem, g_mem, w_kv)


def _emit_orders(val, outs, lane_sets, order_scr):
    rows = val.shape[0]

    def write(ref, r, x):
        for lo, mask in lane_sets:
            y = x if mask is None else jnp.where(mask, x, 0.0)
            ref[0, r, :, lo:lo + LANES] = y.astype(BF16)

    by_dil = sorted(range(len(DILATED_CONFIGS)), key=lambda c: DILATED_CONFIGS[c][1])
    prev = 1
    for level, c in enumerate(by_dil):
        dil = DILATED_CONFIGS[c][1]
        if dil == 1:
            write(outs[c], 0, val)
            order_scr[0] = val
            continue
        step, cls_prev, cls = dil // prev, rows // prev, rows // dil
        for r in range(dil):
            x = order_scr[level - 1, pl.ds((r % prev) * cls_prev + r // prev, cls, stride=step), :]
            write(outs[c], r, x)
            if level + 1 < len(by_dil):
                order_scr[level, r * cls:(r + 1) * cls, :] = x
        prev = dil


def _inproj_kernel(x_ref, g_ref, gv_ref, w_ref, *refs):
    n_cfg = len(DILATED_CONFIGS)
    q_refs, k_refs, v_refs = refs[0:n_cfg], refs[n_cfg:2 * n_cfg], refs[2 * n_cfg:3 * n_cfg]
    act_ref = refs[3 * n_cfg]
    order_scrs = refs[3 * n_cfg + 1:]
    h = _rms(x_ref[0], g_ref[...]).astype(BF16)
    at = [0]

    def proj(width):
        lo = at[0]
        at[0] = lo + width
        return jnp.dot(h, w_ref[:, lo:lo + width], preferred_element_type=F32)

    first = lax.broadcasted_iota(jnp.int32, (1, LANES), 1) < HEAD_DIM
    scrs = iter(order_scrs)

    def emit(val, piece, out_refs, copies):
        for i in range(MXU_COLS // LANES):
            s = piece * (MXU_COLS // LANES) + i
            if copies == 1:
                lane_sets = [(s * LANES, None)]
            else:
                lane_sets = [((copies * s + h) * LANES, first if h == 0 else ~first) for h in range(copies)]
            _emit_orders(val[:, i * LANES:(i + 1) * LANES], out_refs, lane_sets, next(scrs))

    pending = None
    for out_refs, scale, copies in ((q_refs, HEAD_DIM ** -0.5 * LOG2_E, HEADS_PER_STEP), (k_refs, None, 1),
                                    (v_refs, None, 1)):
        for piece in range(ATTN_WIDTH // MXU_COLS):
            val = proj(MXU_COLS)
            if scale is not None:
                val = val * scale
            if pending is not None:
                emit(*pending)
            pending = (val, piece, out_refs, copies)
    za = proj(ATTN_WIDTH)
    emit(*pending)
    ub = proj(SGU_WIDTH)
    vb = proj(SGU_WIDTH)
    zb = proj(SGU_WIDTH)
    qm = proj(MEM_WIDTH)
    zm = proj(MEM_WIDTH)
    acts = (_silu(za), _silu(zb) * jax.nn.gelu(ub), _rms(jax.nn.gelu(vb), gv_ref[...]),
            qm * (MEM_HEAD_DIM ** -0.5 * LOG2_E), _silu(zm))
    lo = 0
    for width, val in zip(ACT_SPLITS, acts):
        act_ref[0, :, lo:lo + width] = val.astype(BF16)
        lo += width


def _inproj(x, g_norm, g_v, w_in):
    b, s, d = x.shape
    cols = w_in.shape[1]
    assert cols == 4 * ATTN_WIDTH + 3 * SGU_WIDTH + 2 * MEM_WIDTH
    tile = lambda w: pl.BlockSpec((1, ROW_TILE, w), lambda i, j: (i, j, 0))
    fixed = lambda i, j: (0, 0)
    dils = [dil for _, dil in DILATED_CONFIGS]
    assert all(ROW_TILE % (dil * BF16_ROWS) == 0 for dil in dils)
    ordered = lambda w: [pl.BlockSpec((1, dil, ROW_TILE // dil, w), lambda i, j: (i, 0, j, 0)) for dil in dils]
    shaped = lambda w: [jax.ShapeDtypeStruct((b, dil, s // dil, w), BF16) for dil in dils]
    widths = (HEADS_PER_STEP * ATTN_WIDTH, ATTN_WIDTH, ATTN_WIDTH)
    outs = pl.pallas_call(
        _inproj_kernel,
        grid=(b, s // ROW_TILE),
        in_specs=[
            tile(d),
            pl.BlockSpec((1, d), fixed),
            pl.BlockSpec((1, SGU_WIDTH), fixed),
            pl.BlockSpec((d, cols), fixed),
        ],
        out_specs=[spec for w in widths for spec in ordered(w)] + [tile(ACT_WIDTH)],
        out_shape=[shape for w in widths for shape in shaped(w)] + [jax.ShapeDtypeStruct((b, s, ACT_WIDTH), BF16)],
        scratch_shapes=[pltpu.VMEM((len(dils) - 1, ROW_TILE, LANES), F32)] * (3 * ATTN_WIDTH // LANES),
        compiler_params=pltpu.CompilerParams(
            dimension_semantics=("arbitrary", "arbitrary"), vmem_limit_bytes=VMEM_LIMIT_BYTES),
        name="inproj",
    )(x, g_norm, g_v, w_in)
    n = len(dils)
    q, k, v = ([o.reshape(b, s, -1) for o in outs[i * n:(i + 1) * n]] for i in range(3))
    return q, k, v, outs[3 * n]


SHIFT, EDGE, WHOLE = "shift", "edge", "whole"


def _bias_tile(kind, dilation, first_head):
    shape = (KEY_WINDOW, HEADS_PER_STEP * Q_BLOCK)
    key = lax.broadcasted_iota(jnp.int32, shape, 0)
    col = lax.broadcasted_iota(jnp.int32, shape, 1)
    second = col >= Q_BLOCK
    qi = jnp.where(second, col - Q_BLOCK, col)
    if kind == SHIFT:
        rel = jnp.abs(qi + RADIUS - key)
        valid = rel <= RADIUS
    elif kind == WHOLE:
        rel = jnp.abs(qi - key)
        valid = rel <= RADIUS
    else:
        upper = key >= Q_BLOCK
        rel = jnp.abs(qi - jnp.where(upper, key - Q_BLOCK, key))
        valid = (rel <= RADIUS) & (upper == (qi >= RADIUS))
    head = (first_head + jnp.where(second, 1, 0)).astype(F32)
    slope = jnp.exp2(-8.0 * (head + 1.0) / N_ATTN_HEADS)
    dist = (rel * dilation).astype(F32)
    return jnp.where(valid, -slope * dist * LOG2_E, NEG_INF)


class _Block(NamedTuple):
    cfg: int
    tile: int
    q_rows: tuple
    k_rows: tuple
    out_rows: tuple
    out_stride: int


def _cfg_order():
    return sorted(range(len(DILATED_CONFIGS)), key=lambda c: -DILATED_CONFIGS[c][1])


def _attn_plan(seq):
    blocks, tiles = [], []
    order = _cfg_order()
    for n, c in enumerate(order):
        dil = DILATED_CONFIGS[c][1]
        nxt = DILATED_CONFIGS[order[n + 1]][1] if n + 1 < len(order) else 1
        step = dil // nxt
        cls = seq // dil
        if cls == Q_BLOCK:
            tiles.append((WHOLE, dil))
            whole = len(tiles) - 1
        else:
            tiles.append((SHIFT, dil))
            tiles.append((EDGE, dil))
            shift, edge = len(tiles) - 2, len(tiles) - 1
        for r in range(dil):
            base = r * cls
            sink = (r % nxt) * (seq // nxt) + r // nxt
            if cls == Q_BLOCK:
                blocks.append(_Block(c, whole, ((base, Q_BLOCK),), ((base, Q_BLOCK),), ((sink, Q_BLOCK),), step))
                continue
            last = cls - RADIUS
            blocks.append(_Block(c, edge, ((base, RADIUS), (base + last, RADIUS)),
                                 ((base, Q_BLOCK), (base + cls - Q_BLOCK, Q_BLOCK)),
                                 ((sink, RADIUS), (sink + step * last, RADIUS)), step))
            for j in range(cls // Q_BLOCK - 1):
                u = RADIUS + j * Q_BLOCK
                blocks.append(_Block(c, shift, ((base + u, Q_BLOCK),), ((base + j * Q_BLOCK, KEY_WINDOW),),
                                     ((sink + step * u, Q_BLOCK),), step))
    return blocks, tiles


def _rows(ref, lead, ranges):
    parts = [ref[lead + (pl.ds(s, n), slice(None))] for s, n in ranges]
    return parts[0] if len(parts) == 1 else jnp.concatenate(parts, axis=0)


def _score_block(blk, q_ref, k_ref, bias_scr, s_scr):
    q2 = jnp.concatenate([jnp.concatenate([q_ref[0, pl.ds(s0, n), h * LANES:(h + 1) * LANES]
                                           for s0, n in blk.q_rows], axis=0)
                          for h in range(HEADS_PER_STEP)], axis=0)
    k_win = _rows(k_ref, (0,), blk.k_rows)
    window = k_win.shape[0]
    s = lax.dot_general(k_win, q2, (((1,), (1,)), ((), ())), preferred_element_type=F32)
    s_scr[0, 0:window, :] = s + bias_scr[blk.tile, 0:window, :]


def _value_block(blk, zero, vt_scr, s_scr, prev, sink, a_ref):
    window = sum(n for _, n in blk.k_rows)
    ms, ps = [], []
    for h in range(HEADS_PER_STEP):
        cols = slice(h * Q_BLOCK, (h + 1) * Q_BLOCK)
        s = s_scr[zero, 0:window, cols]
        m = jnp.max(s, axis=0, keepdims=True)
        ms.append(m)
        ps.append(jnp.exp2(s - m).astype(BF16))
    vt = jnp.concatenate([vt_scr[zero, :, s0:s0 + n] for s0, n in blk.k_rows], axis=1)
    ot = jnp.dot(vt, jnp.concatenate(ps, axis=1), preferred_element_type=F32)
    outs, lses = [], []
    for h in range(HEADS_PER_STEP):
        cols = slice(h * Q_BLOCK, (h + 1) * Q_BLOCK)
        l = ot[LANES:LANES + 1, cols]
        outs.append(ot[h * HEAD_DIM:(h + 1) * HEAD_DIM, cols] * (1.0 / l))
        lses.append(ms[h] + jnp.log2(l))
    o = jnp.concatenate(outs, axis=0).T
    lse = jnp.concatenate([jnp.broadcast_to(x, (HEAD_DIM, Q_BLOCK)) for x in lses], axis=0).T
    if prev is not None:
        o_prev, lse_prev = (_rows(ref, (), blk.q_rows) for ref in prev)
        top = jnp.maximum(lse, lse_prev)
        e, e_prev = jnp.exp2(lse - top), jnp.exp2(lse_prev - top)
        den = e + e_prev
        o = (e * o + e_prev * o_prev) * (1.0 / den)
        lse = top + jnp.log2(den)
    at = 0
    for start, n in blk.out_rows:
        dst = pl.ds(start, n) if blk.out_stride == 1 else pl.ds(start, n, stride=blk.out_stride)
        if sink is None:
            a_ref[0, dst, :] = o[at:at + n].astype(a_ref.dtype)
        else:
            sink[0][dst, :] = o[at:at + n]
            sink[1][dst, :] = lse[at:at + n]
        at += n


def _attn_kernel(*refs, seq):
    n_cfg = len(DILATED_CONFIGS)
    q_refs, k_refs, v_refs = refs[0:n_cfg], refs[n_cfg:2 * n_cfg], refs[2 * n_cfg:3 * n_cfg]
    a_ref, bias_scr = refs[3 * n_cfg:3 * n_cfg + 2]
    scrs = refs[3 * n_cfg + 2:]
    vt_scrs = scrs[0:n_cfg]
    hand = [scrs[n_cfg + 2 * i:n_cfg + 2 * i + 2] for i in range(n_cfg - 1)]
    s_scrs = scrs[3 * n_cfg - 2:]
    blocks, tiles = _attn_plan(seq)
    order = _cfg_order()

    @pl.when(pl.program_id(1) == 0)
    def _():
        first_head = pl.program_id(0) * HEADS_PER_STEP
        for t, (kind, dil) in enumerate(tiles):
            bias_scr[t] = _bias_tile(kind, dil, first_head)

    for c in order:
        vt_scrs[c][0, LANES:LANES + BF16_ROWS, :] = _ones_rows(seq)
        for i in range(seq // LANES):
            chunk = slice(i * LANES, (i + 1) * LANES)
            vt_scrs[c][0, 0:LANES, chunk] = v_refs[c][0, chunk, :].T

    zero = jnp.minimum(pl.program_id(1), 0)
    for t in range(len(blocks) + SCORE_LAG):
        if t < len(blocks):
            blk = blocks[t]
            _score_block(blk, q_refs[blk.cfg], k_refs[blk.cfg], bias_scr, s_scrs[t % SCORE_SLOTS])
        if t >= SCORE_LAG:
            u = t - SCORE_LAG
            blk = blocks[u]
            n = order.index(blk.cfg)
            _value_block(blk, zero, vt_scrs[blk.cfg], s_scrs[u % SCORE_SLOTS],
                         hand[n - 1] if n > 0 else None, hand[n] if n + 1 < n_cfg else None, a_ref)


def _attn(q, k, v):
    b, s, w = k[0].shape
    n_cfg = len(DILATED_CONFIGS)
    assert all(s % (dil * Q_BLOCK) == 0 for _, dil in DILATED_CONFIGS)
    n_tiles = len(_attn_plan(s)[1])
    blk = lambda lanes: pl.BlockSpec((1, s, lanes), lambda j, i: (i, 0, j))
    return pl.pallas_call(
        functools.partial(_attn_kernel, seq=s),
        grid=(w // LANES, b),
        in_specs=[blk(HEADS_PER_STEP * LANES)] * n_cfg + [blk(LANES)] * (2 * n_cfg),
        out_specs=blk(LANES),
        out_shape=jax.ShapeDtypeStruct((b, s, w), BF16),
        scratch_shapes=[
            pltpu.VMEM((n_tiles, KEY_WINDOW, HEADS_PER_STEP * Q_BLOCK), F32),
        ] + [pltpu.VMEM((1, LANES + BF16_ROWS, s), BF16)] * n_cfg
        + [pltpu.VMEM((s, LANES), F32)] * (2 * (n_cfg - 1))
        + [pltpu.VMEM((1, KEY_WINDOW, HEADS_PER_STEP * Q_BLOCK), F32)] * SCORE_SLOTS,
        compiler_params=pltpu.CompilerParams(
            dimension_semantics=("arbitrary", "arbitrary"), vmem_limit_bytes=VMEM_LIMIT_BYTES),
        name="attn",
    )(*q, *k, *v)


def _mix_kernel(a_ref, act_ref, km_ref, vtm_ref, x_ref, ws_ref, bs_ref, wo_ref, gf_ref, o_ref):
    tm = x_ref.shape[1]
    lo = 0
    cols = []
    for width in ACT_SPLITS:
        cols.append(slice(lo, lo + width))
        lo += width
    sa, gu, vn, qm, sm = (act_ref[0, :, c] for c in cols)

    ga = sa * a_ref[0]

    group = lax.broadcasted_iota(jnp.int32, (SGU_CHUNK, SGU_WIDTH), 1) // SGU_GROUP
    mixed = []
    for c in range(tm // SGU_CHUNK):
        vc = vn[c * SGU_CHUNK:(c + 1) * SGU_CHUNK]
        stacked = jnp.concatenate([jnp.where(group == g, vc, jnp.zeros_like(vc)) for g in range(N_SGU_GROUPS)],
                                  axis=0)
        mixed.append(jnp.dot(ws_ref[...], stacked, preferred_element_type=F32) + bs_ref[...])
    gb = (gu.astype(F32) * jnp.concatenate(mixed, axis=0)).astype(BF16)

    mhead = lax.broadcasted_iota(jnp.int32, (tm, MEM_WIDTH), 1) // MEM_HEAD_DIM
    q4 = jnp.concatenate([jnp.where(mhead == h, qm, jnp.zeros_like(qm)) for h in range(N_MEM_HEADS)], axis=0)
    s = lax.dot_general(km_ref[0], q4, (((1,), (1,)), ((), ())), preferred_element_type=F32)
    p = jnp.exp2(s - jnp.max(s, axis=0, keepdims=True)).astype(BF16)
    ot = jnp.dot(vtm_ref[0], p, preferred_element_type=F32)
    mo = []
    for h in range(N_MEM_HEADS):
        c = slice(h * tm, (h + 1) * tm)
        mo.append(ot[h * MEM_HEAD_DIM:(h + 1) * MEM_HEAD_DIM, c] * (1.0 / ot[MEM_WIDTH:MEM_WIDTH + 1, c]))
    mo = jnp.concatenate(mo, axis=0).T
    gm = (sm.astype(F32) * mo).astype(BF16)

    e1 = ATTN_WIDTH
    e2 = e1 + SGU_WIDTH
    y = jnp.dot(ga, wo_ref[0:e1, :], preferred_element_type=F32)
    y = y + jnp.dot(gb, wo_ref[e1:e2, :], preferred_element_type=F32)
    y = y + jnp.dot(gm, wo_ref[e2:, :], preferred_element_type=F32)
    o_ref[0] = _rms(x_ref[0] + y, gf_ref[...])


def _mix(a, act, k_mem, vt_mem, x, w_s, b_tile, w_out, g_final):
    b, s, d = x.shape
    tile = lambda w: pl.BlockSpec((1, ROW_TILE, w), lambda i, j: (i, j, 0))
    per_batch = lambda arr: pl.BlockSpec((1,) + arr.shape[1:], lambda i, j: (i, 0, 0))
    fixed = lambda arr: pl.BlockSpec(arr.shape, lambda i, j: (0, 0))
    return pl.pallas_call(
        _mix_kernel,
        grid=(b, s // ROW_TILE),
        in_specs=[tile(a.shape[2]), tile(act.shape[2]), per_batch(k_mem), per_batch(vt_mem), tile(d),
                  fixed(w_s), fixed(b_tile), fixed(w_out), fixed(g_final)],
        out_specs=tile(d),
        out_shape=jax.ShapeDtypeStruct((b, s, d), x.dtype),
        compiler_params=pltpu.CompilerParams(
            dimension_semantics=("arbitrary", "arbitrary"), vmem_limit_bytes=VMEM_LIMIT_BYTES),
        name="mix",
    )(a, act, k_mem, vt_mem, x, w_s, b_tile, w_out, g_final)


def kernel(x, mem, g_norm, w_in, w_sgu_spatial, b_sgu_spatial, g_sgu_v, g_mem, w_mem_kv, w_out, g_final):
    assert g_norm.shape[0] == 1, "the final norm is fused into the single layer's last kernel"
    b, s, d = x.shape
    k_mem, vt_mem = _memkv(mem, g_mem[0][None, :], w_mem_kv[0].astype(BF16))
    q, k, v, act = _inproj(x, g_norm[0][None, :], g_sgu_v[0][None, :], w_in[0].astype(BF16))
    a = _attn(q, k, v)
    w_s = jnp.concatenate(list(w_sgu_spatial[0].astype(BF16)), axis=1)
    b_tile = jnp.repeat(b_sgu_spatial[0].T, SGU_GROUP, axis=1)
    return _mix(a, act, k_mem, vt_mem, x, w_s, b_tile, w_out[0].astype(BF16), g_final[None, :])
```

```python
import functools
import math
from typing import NamedTuple

import jax
import jax.numpy as jnp
from jax import lax
from jax.experimental import pallas as pl
from jax.experimental.pallas import tpu as pltpu

F32 = jnp.float32
BF16 = jnp.bfloat16

EPS = 1e-6
NEG_INF = -1e30
LOG2_E = math.log2(math.e)

HEAD_DIM = 64
N_ATTN_HEADS = 8
ATTN_WIDTH = HEAD_DIM * N_ATTN_HEADS
DILATED_CONFIGS = ((128, 1), (512, 4), (2048, 16))
RADIUS = 64
SGU_WIDTH = 256
N_SGU_GROUPS = 4
SGU_GROUP = SGU_WIDTH // N_SGU_GROUPS
SGU_CHUNK = 128
MEM_WIDTH = 256
N_MEM_HEADS = 4
MEM_HEAD_DIM = MEM_WIDTH // N_MEM_HEADS
ACT_SPLITS = (ATTN_WIDTH, SGU_WIDTH, SGU_WIDTH, MEM_WIDTH, MEM_WIDTH)
ACT_WIDTH = sum(ACT_SPLITS)

LANES = 128
BF16_ROWS = 16
MXU_COLS = 256
Q_BLOCK = 128
KEY_WINDOW = 2 * Q_BLOCK
HEADS_PER_STEP = LANES // HEAD_DIM
SCORE_LAG = 4
SCORE_SLOTS = 2 * SCORE_LAG
ROW_TILE = 512
VMEM_LIMIT_BYTES = 56 * 1024 * 1024


def _rms(x, g):
    return x * lax.rsqrt(jnp.mean(x * x, axis=-1, keepdims=True) + EPS) * g


def _silu(x):
    return x / (1.0 + jnp.exp(-x))


def _ones_rows(cols):
    first = lax.broadcasted_iota(jnp.int32, (BF16_ROWS, cols), 0) == 0
    return jnp.where(first, 1.0, 0.0).astype(BF16)


def _memkv_kernel(mem_ref, g_ref, w_ref, k_ref, vt_ref):
    h = _rms(mem_ref[0], g_ref[...]).astype(BF16)
    kv = jnp.dot(h, w_ref[...], preferred_element_type=F32)
    k_ref[0] = kv[:, 0:MEM_WIDTH].astype(BF16)
    vt_ref[0, 0:MEM_WIDTH, :] = kv[:, MEM_WIDTH:].T.astype(BF16)
    vt_ref[0, MEM_WIDTH:, :] = _ones_rows(kv.shape[0])


def _memkv(mem, g_mem, w_kv):
    b, m, d = mem.shape
    n = w_kv.shape[1]
    return pl.pallas_call(
        _memkv_kernel,
        grid=(b,),
        in_specs=[
            pl.BlockSpec((1, m, d), lambda i: (i, 0, 0)),
            pl.BlockSpec((1, d), lambda i: (0, 0)),
            pl.BlockSpec((d, n), lambda i: (0, 0)),
        ],
        out_specs=[
            pl.BlockSpec((1, m, MEM_WIDTH), lambda i: (i, 0, 0)),
            pl.BlockSpec((1, MEM_WIDTH + BF16_ROWS, m), lambda i: (i, 0, 0)),
        ],
        out_shape=[
            jax.ShapeDtypeStruct((b, m, MEM_WIDTH), BF16),
            jax.ShapeDtypeStruct((b, MEM_WIDTH + BF16_ROWS, m), BF16),
        ],
        compiler_params=pltpu.CompilerParams(
            dimension_semantics=("arbitrary",), vmem_limit_bytes=VMEM_LIMIT_BYTES),
        name="memkv",
    )(mem, g_mem, w_kv)


def _emit_orders(val, outs, lo, per_head, order_scrs):
    rows = val.shape[0]
    n_slabs = MXU_COLS // LANES
    first = lax.broadcasted_iota(jnp.int32, (1, LANES), 1) < HEAD_DIM

    def write(ref, r, slabs):
        if per_head:
            for i, x in enumerate(slabs):
                y = jnp.concatenate([jnp.where(first, x, 0.0), jnp.where(first, 0.0, x)], axis=1)
                at = lo + i * HEADS_PER_STEP * LANES
                ref[r, :, at:at + HEADS_PER_STEP * LANES] = y.astype(BF16)
        else:
            ref[r, :, lo:lo + MXU_COLS] = jnp.concatenate(slabs, axis=1).astype(BF16)

    by_dil = sorted(range(len(DILATED_CONFIGS)), key=lambda c: DILATED_CONFIGS[c][1])
    prev = 1
    for level, c in enumerate(by_dil):
        dil = DILATED_CONFIGS[c][1]
        if dil == 1:
            slabs = [val[:, i * LANES:(i + 1) * LANES] for i in range(n_slabs)]
            write(outs[c], 0, slabs)
            for scr, x in zip(order_scrs, slabs):
                scr[0] = x
            continue
        step, cls_prev, cls = dil // prev, rows // prev, rows // dil
        for r in range(dil):
            slabs = [scr[level - 1, pl.ds((r % prev) * cls_prev + r // prev, cls, stride=step), :]
                     for scr in order_scrs]
            write(outs[c], r, slabs)
            if level + 1 < len(by_dil):
                for scr, x in zip(order_scrs, slabs):
                    scr[level, r * cls:(r + 1) * cls, :] = x
        prev = dil


def _inproj_kernel(x_ref, g_ref, gv_ref, w_ref, *refs):
    n_cfg = len(DILATED_CONFIGS)
    q_refs, k_refs, v_refs = refs[0:n_cfg], refs[n_cfg:2 * n_cfg], refs[2 * n_cfg:3 * n_cfg]
    act_ref = refs[3 * n_cfg]
    order_scrs = refs[3 * n_cfg + 1:]
    h = _rms(x_ref[0], g_ref[...]).astype(BF16)
    at = [0]

    def proj(width):
        lo = at[0]
        at[0] = lo + width
        return jnp.dot(h, w_ref[:, lo:lo + width], preferred_element_type=F32)

    n_slabs = MXU_COLS // LANES
    at_scr = 0
    for out_refs, scale, per_head in ((q_refs, HEAD_DIM ** -0.5 * LOG2_E, True), (k_refs, None, False),
                                      (v_refs, None, False)):
        for piece in range(ATTN_WIDTH // MXU_COLS):
            val = proj(MXU_COLS)
            if scale is not None:
                val = val * scale
            lo = piece * MXU_COLS * (HEADS_PER_STEP if per_head else 1)
            _emit_orders(val, out_refs, lo, per_head, order_scrs[at_scr:at_scr + n_slabs])
            at_scr += n_slabs
    za = proj(ATTN_WIDTH)
    ub = proj(SGU_WIDTH)
    vb = proj(SGU_WIDTH)
    zb = proj(SGU_WIDTH)
    qm = proj(MEM_WIDTH)
    zm = proj(MEM_WIDTH)
    acts = (_silu(za), _silu(zb) * jax.nn.gelu(ub), _rms(jax.nn.gelu(vb), gv_ref[...]),
            qm * (MEM_HEAD_DIM ** -0.5 * LOG2_E), _silu(zm))
    lo = 0
    for width, val in zip(ACT_SPLITS, acts):
        act_ref[0, :, lo:lo + width] = val.astype(BF16)
        lo += width


def _inproj(x, g_norm, g_v, w_in):
    b, s, d = x.shape
    cols = w_in.shape[1]
    assert cols == 4 * ATTN_WIDTH + 3 * SGU_WIDTH + 2 * MEM_WIDTH
    tile = lambda w: pl.BlockSpec((1, ROW_TILE, w), lambda i, j: (i, j, 0))
    fixed = lambda i, j: (0, 0)
    dils = [dil for _, dil in DILATED_CONFIGS]
    assert all(ROW_TILE % (dil * BF16_ROWS) == 0 for dil in dils)
    ordered = lambda w: [pl.BlockSpec((dil, ROW_TILE // dil, w), lambda i, j: (i, j, 0)) for dil in dils]
    shaped = lambda w: [jax.ShapeDtypeStruct((b * dil, s // dil, w), BF16) for dil in dils]
    widths = (HEADS_PER_STEP * ATTN_WIDTH, ATTN_WIDTH, ATTN_WIDTH)
    outs = pl.pallas_call(
        _inproj_kernel,
        grid=(b, s // ROW_TILE),
        in_specs=[
            tile(d),
            pl.BlockSpec((1, d), fixed),
            pl.BlockSpec((1, SGU_WIDTH), fixed),
            pl.BlockSpec((d, cols), fixed),
        ],
        out_specs=[spec for w in widths for spec in ordered(w)] + [tile(ACT_WIDTH)],
        out_shape=[shape for w in widths for shape in shaped(w)] + [jax.ShapeDtypeStruct((b, s, ACT_WIDTH), BF16)],
        scratch_shapes=[pltpu.VMEM((len(dils) - 1, ROW_TILE, LANES), F32)] * (3 * ATTN_WIDTH // LANES),
        compiler_params=pltpu.CompilerParams(
            dimension_semantics=("arbitrary", "arbitrary"), vmem_limit_bytes=VMEM_LIMIT_BYTES),
        name="inproj",
    )(x, g_norm, g_v, w_in)
    n = len(dils)
    q, k, v = ([o.reshape(b, s, -1) for o in outs[i * n:(i + 1) * n]] for i in range(3))
    return q, k, v, outs[3 * n]


SHIFT, EDGE, WHOLE = "shift", "edge", "whole"


def _bias_tile(kind, dilation, first_head):
    shape = (KEY_WINDOW, HEADS_PER_STEP * Q_BLOCK)
    key = lax.broadcasted_iota(jnp.int32, shape, 0)
    col = lax.broadcasted_iota(jnp.int32, shape, 1)
    second = col >= Q_BLOCK
    qi = jnp.where(second, col - Q_BLOCK, col)
    if kind == SHIFT:
        rel = jnp.abs(qi + RADIUS - key)
        valid = rel <= RADIUS
    elif kind == WHOLE:
        rel = jnp.abs(qi - key)
        valid = rel <= RADIUS
    else:
        upper = key >= Q_BLOCK
        rel = jnp.abs(qi - jnp.where(upper, key - Q_BLOCK, key))
        valid = (rel <= RADIUS) & (upper == (qi >= RADIUS))
    head = (first_head + jnp.where(second, 1, 0)).astype(F32)
    slope = jnp.exp2(-8.0 * (head + 1.0) / N_ATTN_HEADS)
    dist = (rel * dilation).astype(F32)
    return jnp.where(valid, -slope * dist * LOG2_E, NEG_INF)


class _Block(NamedTuple):
    cfg: int
    tile: int
    q_rows: tuple
    k_rows: tuple
    out_rows: tuple
    out_stride: int


def _cfg_order():
    return sorted(range(len(DILATED_CONFIGS)), key=lambda c: -DILATED_CONFIGS[c][1])


def _attn_plan(seq):
    blocks, tiles = [], []
    order = _cfg_order()
    for n, c in enumerate(order):
        dil = DILATED_CONFIGS[c][1]
        nxt = DILATED_CONFIGS[order[n + 1]][1] if n + 1 < len(order) else 1
        step = dil // nxt
        cls = seq // dil
        if cls == Q_BLOCK:
            tiles.append((WHOLE, dil))
            whole = len(tiles) - 1
        else:
            tiles.append((SHIFT, dil))
            tiles.append((EDGE, dil))
            shift, edge = len(tiles) - 2, len(tiles) - 1
        for r in range(dil):
            base = r * cls
            sink = (r % nxt) * (seq // nxt) + r // nxt
            if cls == Q_BLOCK:
                blocks.append(_Block(c, whole, ((base, Q_BLOCK),), ((base, Q_BLOCK),), ((sink, Q_BLOCK),), step))
                continue
            last = cls - RADIUS
            blocks.append(_Block(c, edge, ((base, RADIUS), (base + last, RADIUS)),
                                 ((base, Q_BLOCK), (base + cls - Q_BLOCK, Q_BLOCK)),
                                 ((sink, RADIUS), (sink + step * last, RADIUS)), step))
            for j in range(cls // Q_BLOCK - 1):
                u = RADIUS + j * Q_BLOCK
                blocks.append(_Block(c, shift, ((base + u, Q_BLOCK),), ((base + j * Q_BLOCK, KEY_WINDOW),),
                                     ((sink + step * u, Q_BLOCK),), step))
    return blocks, tiles


def _rows(ref, lead, ranges):
    parts = [ref[lead + (pl.ds(s, n), slice(None))] for s, n in ranges]
    return parts[0] if len(parts) == 1 else jnp.concatenate(parts, axis=0)


def _score_block(blk, q_ref, k_ref, bias_scr, s_scr):
    q2 = jnp.concatenate([jnp.concatenate([q_ref[0, pl.ds(s0, n), h * LANES:(h + 1) * LANES]
                                           for s0, n in blk.q_rows], axis=0)
                          for h in range(HEADS_PER_STEP)], axis=0)
    k_win = _rows(k_ref, (0,), blk.k_rows)
    window = k_win.shape[0]
    s = lax.dot_general(k_win, q2, (((1,), (1,)), ((), ())), preferred_element_type=F32)
    s_scr[0, 0:window, :] = s + bias_scr[blk.tile, 0:window, :]


def _value_block(blk, zero, vt_scr, s_scr, prev, sink, a_ref):
    window = sum(n for _, n in blk.k_rows)
    ms, ps = [], []
    for h in range(HEADS_PER_STEP):
        cols = slice(h * Q_BLOCK, (h + 1) * Q_BLOCK)
        s = s_scr[zero, 0:window, cols]
        m = jnp.max(s, axis=0, keepdims=True)
        ms.append(m)
        ps.append(jnp.exp2(s - m).astype(BF16))
    vt = jnp.concatenate([vt_scr[zero, :, s0:s0 + n] for s0, n in blk.k_rows], axis=1)
    ot = jnp.dot(vt, jnp.concatenate(ps, axis=1), preferred_element_type=F32)
    outs, lses = [], []
    for h in range(HEADS_PER_STEP):
        cols = slice(h * Q_BLOCK, (h + 1) * Q_BLOCK)
        l = ot[LANES:LANES + 1, cols]
        outs.append(ot[h * HEAD_DIM:(h + 1) * HEAD_DIM, cols] * (1.0 / l))
        lses.append(ms[h] + jnp.log2(l))
    o = jnp.concatenate(outs, axis=0).T
    lse = jnp.concatenate([jnp.broadcast_to(x, (HEAD_DIM, Q_BLOCK)) for x in lses], axis=0).T
    if prev is not None:
        o_prev, lse_prev = (_rows(ref, (), blk.q_rows) for ref in prev)
        top = jnp.maximum(lse, lse_prev)
        e, e_prev = jnp.exp2(lse - top), jnp.exp2(lse_prev - top)
        den = e + e_prev
        o = (e * o + e_prev * o_prev) * (1.0 / den)
        lse = top + jnp.log2(den)
    at = 0
    for start, n in blk.out_rows:
        dst = pl.ds(start, n) if blk.out_stride == 1 else pl.ds(start, n, stride=blk.out_stride)
        if sink is None:
            a_ref[0, dst, :] = o[at:at + n].astype(a_ref.dtype)
        else:
            sink[0][dst, :] = o[at:at + n]
            sink[1][dst, :] = lse[at:at + n]
        at += n


def _attn_kernel(*refs, seq):
    n_cfg = len(DILATED_CONFIGS)
    q_refs, k_refs, v_refs = refs[0:n_cfg], refs[n_cfg:2 * n_cfg], refs[2 * n_cfg:3 * n_cfg]
    a_ref, bias_scr = refs[3 * n_cfg:3 * n_cfg + 2]
    scrs = refs[3 * n_cfg + 2:]
    vt_scrs = scrs[0:n_cfg]
    hand = [scrs[n_cfg + 2 * i:n_cfg + 2 * i + 2] for i in range(n_cfg - 1)]
    s_scrs = scrs[3 * n_cfg - 2:]
    blocks, tiles = _attn_plan(seq)
    order = _cfg_order()

    @pl.when(pl.program_id(1) == 0)
    def _():
        first_head = pl.program_id(0) * HEADS_PER_STEP
        for t, (kind, dil) in enumerate(tiles):
            bias_scr[t] = _bias_tile(kind, dil, first_head)

    for c in order:
        vt_scrs[c][0, LANES:LANES + BF16_ROWS, :] = _ones_rows(seq)
        for i in range(seq // LANES):
            chunk = slice(i * LANES, (i + 1) * LANES)
            vt_scrs[c][0, 0:LANES, chunk] = v_refs[c][0, chunk, :].T

    zero = jnp.minimum(pl.program_id(1), 0)
    for t in range(len(blocks) + SCORE_LAG):
        if t < len(blocks):
            blk = blocks[t]
            _score_block(blk, q_refs[blk.cfg], k_refs[blk.cfg], bias_scr, s_scrs[t % SCORE_SLOTS])
        if t >= SCORE_LAG:
            u = t - SCORE_LAG
            blk = blocks[u]
            n = order.index(blk.cfg)
            _value_block(blk, zero, vt_scrs[blk.cfg], s_scrs[u % SCORE_SLOTS],
                         hand[n - 1] if n > 0 else None, hand[n] if n + 1 < n_cfg else None, a_ref)


def _attn(q, k, v):
    b, s, w = k[0].shape
    n_cfg = len(DILATED_CONFIGS)
    assert all(s % (dil * Q_BLOCK) == 0 for _, dil in DILATED_CONFIGS)
    n_tiles = len(_attn_plan(s)[1])
    blk = lambda lanes: pl.BlockSpec((1, s, lanes), lambda j, i: (i, 0, j))
    return pl.pallas_call(
        functools.partial(_attn_kernel, seq=s),
        grid=(w // LANES, b),
        in_specs=[blk(HEADS_PER_STEP * LANES)] * n_cfg + [blk(LANES)] * (2 * n_cfg),
        out_specs=blk(LANES),
        out_shape=jax.ShapeDtypeStruct((b, s, w), BF16),
        scratch_shapes=[
            pltpu.VMEM((n_tiles, KEY_WINDOW, HEADS_PER_STEP * Q_BLOCK), F32),
        ] + [pltpu.VMEM((1, LANES + BF16_ROWS, s), BF16)] * n_cfg
        + [pltpu.VMEM((s, LANES), F32)] * (2 * (n_cfg - 1))
        + [pltpu.VMEM((1, KEY_WINDOW, HEADS_PER_STEP * Q_BLOCK), F32)] * SCORE_SLOTS,
        compiler_params=pltpu.CompilerParams(
            dimension_semantics=("arbitrary", "arbitrary"), vmem_limit_bytes=VMEM_LIMIT_BYTES),
        name="attn",
    )(*q, *k, *v)


def _mix_kernel(a_ref, act_ref, km_ref, vtm_ref, x_ref, ws_ref, bs_ref, wo_ref, gf_ref, o_ref):
    tm = x_ref.shape[1]
    lo = 0
    cols = []
    for width in ACT_SPLITS:
        cols.append(slice(lo, lo + width))
        lo += width
    sa, gu, vn, qm, sm = (act_ref[0, :, c] for c in cols)

    ga = sa * a_ref[0]

    group = lax.broadcasted_iota(jnp.int32, (SGU_CHUNK, SGU_WIDTH), 1) // SGU_GROUP
    mixed = []
    for c in range(tm // SGU_CHUNK):
        vc = vn[c * SGU_CHUNK:(c + 1) * SGU_CHUNK]
        stacked = jnp.concatenate([jnp.where(group == g, vc, jnp.zeros_like(vc)) for g in range(N_SGU_GROUPS)],
                                  axis=0)
        mixed.append(jnp.dot(ws_ref[...], stacked, preferred_element_type=F32) + bs_ref[...])
    gb = (gu.astype(F32) * jnp.concatenate(mixed, axis=0)).astype(BF16)

    mhead = lax.broadcasted_iota(jnp.int32, (tm, MEM_WIDTH), 1) // MEM_HEAD_DIM
    q4 = jnp.concatenate([jnp.where(mhead == h, qm, jnp.zeros_like(qm)) for h in range(N_MEM_HEADS)], axis=0)
    s = lax.dot_general(km_ref[0], q4, (((1,), (1,)), ((), ())), preferred_element_type=F32)
    p = jnp.exp2(s - jnp.max(s, axis=0, keepdims=True)).astype(BF16)
    ot = jnp.dot(vtm_ref[0], p, preferred_element_type=F32)
    mo = []
    for h in range(N_MEM_HEADS):
        c = slice(h * tm, (h + 1) * tm)
        mo.append(ot[h * MEM_HEAD_DIM:(h + 1) * MEM_HEAD_DIM, c] * (1.0 / ot[MEM_WIDTH:MEM_WIDTH + 1, c]))
    mo = jnp.concatenate(mo, axis=0).T
    gm = (sm.astype(F32) * mo).astype(BF16)

    e1 = ATTN_WIDTH
    e2 = e1 + SGU_WIDTH
    y = jnp.dot(ga, wo_ref[0:e1, :], preferred_element_type=F32)
    y = y + jnp.dot(gb, wo_ref[e1:e2, :], preferred_element_type=F32)
    y = y + jnp.dot(gm, wo_ref[e2:, :], preferred_element_type=F32)
    o_ref[0] = _rms(x_ref[0] + y, gf_ref[...])


def _mix(a, act, k_mem, vt_mem, x, w_s, b_tile, w_out, g_final):
    b, s, d = x.shape
    tile = lambda w: pl.BlockSpec((1, ROW_TILE, w), lambda i, j: (i, j, 0))
    per_batch = lambda arr: pl.BlockSpec((1,) + arr.shape[1:], lambda i, j: (i, 0, 0))
    fixed = lambda arr: pl.BlockSpec(arr.shape, lambda i, j: (0, 0))
    return pl.pallas_call(
        _mix_kernel,
        grid=(b, s // ROW_TILE),
        in_specs=[tile(a.shape[2]), tile(act.shape[2]), per_batch(k_mem), per_batch(vt_mem), tile(d),
                  fixed(w_s), fixed(b_tile), fixed(w_out), fixed(g_final)],
        out_specs=tile(d),
        out_shape=jax.ShapeDtypeStruct((b, s, d), x.dtype),
        compiler_params=pltpu.CompilerParams(
            dimension_semantics=("arbitrary", "arbitrary"), vmem_limit_bytes=VMEM_LIMIT_BYTES),
        name="mix",
    )(a, act, k_mem, vt_mem, x, w_s, b_tile, w_out, g_final)


def kernel(x, mem, g_norm, w_in, w_sgu_spatial, b_sgu_spatial, g_sgu_v, g_mem, w_mem_kv, w_out, g_final):
    assert g_norm.shape[0] == 1, "the final norm is fused into the single layer's last kernel"
    b, s, d = x.shape
    k_mem, vt_mem = _memkv(mem, g_mem[0][None, :], w_mem_kv[0].astype(BF16))
    q, k, v, act = _inproj(x, g_norm[0][None, :], g_sgu_v[0][None, :], w_in[0].astype(BF16))
    a = _attn(q, k, v)
    w_s = jnp.concatenate(list(w_sgu_spatial[0].astype(BF16)), axis=1)
    b_tile = jnp.repeat(b_sgu_spatial[0].T, SGU_GROUP, axis=1)
    return _mix(a, act, k_mem, vt_mem, x, w_s, b_tile, w_out[0].astype(BF16), g_final[None, :])
```

```python
import functools
import math
from typing import NamedTuple

import jax
import jax.numpy as jnp
from jax import lax
from jax.experimental import pallas as pl
from jax.experimental.pallas import tpu as pltpu

F32 = jnp.float32
BF16 = jnp.bfloat16

EPS = 1e-6
NEG_INF = -1e30
LOG2_E = math.log2(math.e)

HEAD_DIM = 64
N_ATTN_HEADS = 8
ATTN_WIDTH = HEAD_DIM * N_ATTN_HEADS
DILATED_CONFIGS = ((128, 1), (512, 4), (2048, 16))
RADIUS = 64
SGU_WIDTH = 256
N_SGU_GROUPS = 4
SGU_GROUP = SGU_WIDTH // N_SGU_GROUPS
SGU_CHUNK = 128
MEM_WIDTH = 256
N_MEM_HEADS = 4
MEM_HEAD_DIM = MEM_WIDTH // N_MEM_HEADS
ACT_SPLITS = (ATTN_WIDTH, SGU_WIDTH, SGU_WIDTH, MEM_WIDTH, MEM_WIDTH)
ACT_WIDTH = sum(ACT_SPLITS)

LANES = 128
BF16_ROWS = 16
MXU_COLS = 256
Q_BLOCK = 128
KEY_WINDOW = 2 * Q_BLOCK
HEADS_PER_STEP = LANES // HEAD_DIM
SCORE_LAG = 4
SCORE_SLOTS = 2 * SCORE_LAG
ROW_TILE = 512
VMEM_LIMIT_BYTES = 56 * 1024 * 1024


def _rms(x, g):
    return x * lax.rsqrt(jnp.mean(x * x, axis=-1, keepdims=True) + EPS) * g


def _silu(x):
    return x / (1.0 + jnp.exp(-x))


def _ones_rows(cols):
    first = lax.broadcasted_iota(jnp.int32, (BF16_ROWS, cols), 0) == 0
    return jnp.where(first, 1.0, 0.0).astype(BF16)


def _memkv_kernel(mem_ref, g_ref, w_ref, k_ref, vt_ref):
    h = _rms(mem_ref[0], g_ref[...]).astype(BF16)
    kv = jnp.dot(h, w_ref[...], preferred_element_type=F32)
    k_ref[0] = kv[:, 0:MEM_WIDTH].astype(BF16)
    vt_ref[0, 0:MEM_WIDTH, :] = kv[:, MEM_WIDTH:].T.astype(BF16)
    vt_ref[0, MEM_WIDTH:, :] = _ones_rows(kv.shape[0])


def _memkv(mem, g_mem, w_kv):
    b, m, d = mem.shape
    n = w_kv.shape[1]
    return pl.pallas_call(
        _memkv_kernel,
        grid=(b,),
        in_specs=[
            pl.BlockSpec((1, m, d), lambda i: (i, 0, 0)),
            pl.BlockSpec((1, d), lambda i: (0, 0)),
            pl.BlockSpec((d, n), lambda i: (0, 0)),
        ],
        out_specs=[
            pl.BlockSpec((1, m, MEM_WIDTH), lambda i: (i, 0, 0)),
            pl.BlockSpec((1, MEM_WIDTH + BF16_ROWS, m), lambda i: (i, 0, 0)),
        ],
        out_shape=[
            jax.ShapeDtypeStruct((b, m, MEM_WIDTH), BF16),
            jax.ShapeDtypeStruct((b, MEM_WIDTH + BF16_ROWS, m), BF16),
        ],
        compiler_params=pltpu.CompilerParams(
            dimension_semantics=("arbitrary",), vmem_limit_bytes=VMEM_LIMIT_BYTES),
        name="memkv",
    )(mem, g_mem, w_kv)


def _emit_orders(val, outs, lo, order_scrs):
    rows = val.shape[0]
    n_slabs = MXU_COLS // LANES

    def write(ref, r, slabs):
        ref[r, :, lo:lo + MXU_COLS] = jnp.concatenate(slabs, axis=1).astype(BF16)

    by_dil = sorted(range(len(DILATED_CONFIGS)), key=lambda c: DILATED_CONFIGS[c][1])
    prev = 1
    for level, c in enumerate(by_dil):
        dil = DILATED_CONFIGS[c][1]
        if dil == 1:
            slabs = [val[:, i * LANES:(i + 1) * LANES] for i in range(n_slabs)]
            write(outs[c], 0, slabs)
            for scr, x in zip(order_scrs, slabs):
                scr[0] = x
            continue
        step, cls_prev, cls = dil // prev, rows // prev, rows // dil
        for r in range(dil):
            slabs = [scr[level - 1, pl.ds((r % prev) * cls_prev + r // prev, cls, stride=step), :]
                     for scr in order_scrs]
            write(outs[c], r, slabs)
            if level + 1 < len(by_dil):
                for scr, x in zip(order_scrs, slabs):
                    scr[level, r * cls:(r + 1) * cls, :] = x
        prev = dil


def _inproj_kernel(x_ref, g_ref, gv_ref, w_ref, *refs):
    n_cfg = len(DILATED_CONFIGS)
    q_refs, k_refs, v_refs = refs[0:n_cfg], refs[n_cfg:2 * n_cfg], refs[2 * n_cfg:3 * n_cfg]
    act_ref = refs[3 * n_cfg]
    order_scrs = refs[3 * n_cfg + 1:]
    h = _rms(x_ref[0], g_ref[...]).astype(BF16)
    at = [0]

    def proj(width):
        lo = at[0]
        at[0] = lo + width
        return jnp.dot(h, w_ref[:, lo:lo + width], preferred_element_type=F32)

    n_slabs = MXU_COLS // LANES
    at_scr = 0
    for out_refs, scale in ((q_refs, HEAD_DIM ** -0.5 * LOG2_E), (k_refs, None), (v_refs, None)):
        for piece in range(ATTN_WIDTH // MXU_COLS):
            val = proj(MXU_COLS)
            if scale is not None:
                val = val * scale
            _emit_orders(val, out_refs, piece * MXU_COLS, order_scrs[at_scr:at_scr + n_slabs])
            at_scr += n_slabs
    za = proj(ATTN_WIDTH)
    ub = proj(SGU_WIDTH)
    vb = proj(SGU_WIDTH)
    zb = proj(SGU_WIDTH)
    qm = proj(MEM_WIDTH)
    zm = proj(MEM_WIDTH)
    acts = (_silu(za), _silu(zb) * jax.nn.gelu(ub), _rms(jax.nn.gelu(vb), gv_ref[...]),
            qm * (MEM_HEAD_DIM ** -0.5 * LOG2_E), _silu(zm))
    lo = 0
    for width, val in zip(ACT_SPLITS, acts):
        act_ref[0, :, lo:lo + width] = val.astype(BF16)
        lo += width


def _inproj(x, g_norm, g_v, w_in):
    b, s, d = x.shape
    cols = w_in.shape[1]
    assert cols == 4 * ATTN_WIDTH + 3 * SGU_WIDTH + 2 * MEM_WIDTH
    tile = lambda w: pl.BlockSpec((1, ROW_TILE, w), lambda i, j: (i, j, 0))
    fixed = lambda i, j: (0, 0)
    dils = [dil for _, dil in DILATED_CONFIGS]
    assert all(ROW_TILE % (dil * BF16_ROWS) == 0 for dil in dils)
    ordered = lambda w: [pl.BlockSpec((dil, ROW_TILE // dil, w), lambda i, j: (i, j, 0)) for dil in dils]
    shaped = lambda w: [jax.ShapeDtypeStruct((b * dil, s // dil, w), BF16) for dil in dils]
    widths = (ATTN_WIDTH, ATTN_WIDTH, ATTN_WIDTH)
    outs = pl.pallas_call(
        _inproj_kernel,
        grid=(b, s // ROW_TILE),
        in_specs=[
            tile(d),
            pl.BlockSpec((1, d), fixed),
            pl.BlockSpec((1, SGU_WIDTH), fixed),
            pl.BlockSpec((d, cols), fixed),
        ],
        out_specs=[spec for w in widths for spec in ordered(w)] + [tile(ACT_WIDTH)],
        out_shape=[shape for w in widths for shape in shaped(w)] + [jax.ShapeDtypeStruct((b, s, ACT_WIDTH), BF16)],
        scratch_shapes=[pltpu.VMEM((len(dils) - 1, ROW_TILE, LANES), F32)] * (3 * ATTN_WIDTH // LANES),
        compiler_params=pltpu.CompilerParams(
            dimension_semantics=("arbitrary", "arbitrary"), vmem_limit_bytes=VMEM_LIMIT_BYTES),
        name="inproj",
    )(x, g_norm, g_v, w_in)
    n = len(dils)
    q, k, v = ([o.reshape(b, s, -1) for o in outs[i * n:(i + 1) * n]] for i in range(3))
    return q, k, v, outs[3 * n]


SHIFT, EDGE, WHOLE = "shift", "edge", "whole"


def _bias_tile(kind, dilation, first_head):
    shape = (KEY_WINDOW, HEADS_PER_STEP * Q_BLOCK)
    key = lax.broadcasted_iota(jnp.int32, shape, 0)
    col = lax.broadcasted_iota(jnp.int32, shape, 1)
    second = col >= Q_BLOCK
    qi = jnp.where(second, col - Q_BLOCK, col)
    if kind == SHIFT:
        rel = jnp.abs(qi + RADIUS - key)
        valid = rel <= RADIUS
    elif kind == WHOLE:
        rel = jnp.abs(qi - key)
        valid = rel <= RADIUS
    else:
        upper = key >= Q_BLOCK
        rel = jnp.abs(qi - jnp.where(upper, key - Q_BLOCK, key))
        valid = (rel <= RADIUS) & (upper == (qi >= RADIUS))
    head = (first_head + jnp.where(second, 1, 0)).astype(F32)
    slope = jnp.exp2(-8.0 * (head + 1.0) / N_ATTN_HEADS)
    dist = (rel * dilation).astype(F32)
    return jnp.where(valid, -slope * dist * LOG2_E, NEG_INF)


class _Block(NamedTuple):
    cfg: int
    tile: int
    q_rows: tuple
    k_rows: tuple
    out_rows: tuple
    out_stride: int


def _cfg_order():
    return sorted(range(len(DILATED_CONFIGS)), key=lambda c: -DILATED_CONFIGS[c][1])


def _attn_plan(seq):
    blocks, tiles = [], []
    order = _cfg_order()
    for n, c in enumerate(order):
        dil = DILATED_CONFIGS[c][1]
        nxt = DILATED_CONFIGS[order[n + 1]][1] if n + 1 < len(order) else 1
        step = dil // nxt
        cls = seq // dil
        if cls == Q_BLOCK:
            tiles.append((WHOLE, dil))
            whole = len(tiles) - 1
        else:
            tiles.append((SHIFT, dil))
            tiles.append((EDGE, dil))
            shift, edge = len(tiles) - 2, len(tiles) - 1
        for r in range(dil):
            base = r * cls
            sink = (r % nxt) * (seq // nxt) + r // nxt
            if cls == Q_BLOCK:
                blocks.append(_Block(c, whole, ((base, Q_BLOCK),), ((base, Q_BLOCK),), ((sink, Q_BLOCK),), step))
                continue
            last = cls - RADIUS
            blocks.append(_Block(c, edge, ((base, RADIUS), (base + last, RADIUS)),
                                 ((base, Q_BLOCK), (base + cls - Q_BLOCK, Q_BLOCK)),
                                 ((sink, RADIUS), (sink + step * last, RADIUS)), step))
            for j in range(cls // Q_BLOCK - 1):
                u = RADIUS + j * Q_BLOCK
                blocks.append(_Block(c, shift, ((base + u, Q_BLOCK),), ((base + j * Q_BLOCK, KEY_WINDOW),),
                                     ((sink + step * u, Q_BLOCK),), step))
    return blocks, tiles


def _rows(ref, lead, ranges):
    parts = [ref[lead + (pl.ds(s, n), slice(None))] for s, n in ranges]
    return parts[0] if len(parts) == 1 else jnp.concatenate(parts, axis=0)


def _score_block(blk, q_ref, k_ref, bias_scr, s_scr):
    q = _rows(q_ref, (0,), blk.q_rows)
    first = lax.broadcasted_iota(jnp.int32, (1, LANES), 1) < HEAD_DIM
    zeros = jnp.zeros_like(q)
    q2 = jnp.concatenate([jnp.where(first, q, zeros), jnp.where(first, zeros, q)], axis=0)
    k_win = _rows(k_ref, (0,), blk.k_rows)
    window = k_win.shape[0]
    s = lax.dot_general(k_win, q2, (((1,), (1,)), ((), ())), preferred_element_type=F32)
    s_scr[0, 0:window, :] = s + bias_scr[blk.tile, 0:window, :]


def _value_block(blk, zero, vt_scr, s_scr, prev, sink, a_ref):
    window = sum(n for _, n in blk.k_rows)
    ms, ps = [], []
    for h in range(HEADS_PER_STEP):
        cols = slice(h * Q_BLOCK, (h + 1) * Q_BLOCK)
        s = s_scr[zero, 0:window, cols]
        m = jnp.max(s, axis=0, keepdims=True)
        ms.append(m)
        ps.append(jnp.exp2(s - m).astype(BF16))
    vt = jnp.concatenate([vt_scr[zero, :, s0:s0 + n] for s0, n in blk.k_rows], axis=1)
    ot = jnp.dot(vt, jnp.concatenate(ps, axis=1), preferred_element_type=F32)
    outs, lses = [], []
    for h in range(HEADS_PER_STEP):
        cols = slice(h * Q_BLOCK, (h + 1) * Q_BLOCK)
        l = ot[LANES:LANES + 1, cols]
        outs.append(ot[h * HEAD_DIM:(h + 1) * HEAD_DIM, cols] * (1.0 / l))
        lses.append(ms[h] + jnp.log2(l))
    o = jnp.concatenate(outs, axis=0).T
    lse = jnp.concatenate([jnp.broadcast_to(x, (HEAD_DIM, Q_BLOCK)) for x in lses], axis=0).T
    if prev is not None:
        o_prev, lse_prev = (_rows(ref, (), blk.q_rows) for ref in prev)
        top = jnp.maximum(lse, lse_prev)
        e, e_prev = jnp.exp2(lse - top), jnp.exp2(lse_prev - top)
        den = e + e_prev
        o = (e * o + e_prev * o_prev) * (1.0 / den)
        lse = top + jnp.log2(den)
    at = 0
    for start, n in blk.out_rows:
        dst = pl.ds(start, n) if blk.out_stride == 1 else pl.ds(start, n, stride=blk.out_stride)
        if sink is None:
            a_ref[0, dst, :] = o[at:at + n].astype(a_ref.dtype)
        else:
            sink[0][dst, :] = o[at:at + n]
            sink[1][dst, :] = lse[at:at + n]
        at += n


def _attn_kernel(*refs, seq):
    n_cfg = len(DILATED_CONFIGS)
    q_refs, k_refs, v_refs = refs[0:n_cfg], refs[n_cfg:2 * n_cfg], refs[2 * n_cfg:3 * n_cfg]
    a_ref, bias_scr = refs[3 * n_cfg:3 * n_cfg + 2]
    scrs = refs[3 * n_cfg + 2:]
    vt_scrs = scrs[0:n_cfg]
    hand = [scrs[n_cfg + 2 * i:n_cfg + 2 * i + 2] for i in range(n_cfg - 1)]
    s_scrs = scrs[3 * n_cfg - 2:]
    blocks, tiles = _attn_plan(seq)
    order = _cfg_order()

    @pl.when(pl.program_id(1) == 0)
    def _():
        first_head = pl.program_id(0) * HEADS_PER_STEP
        for t, (kind, dil) in enumerate(tiles):
            bias_scr[t] = _bias_tile(kind, dil, first_head)

    for c in order:
        vt_scrs[c][0, LANES:LANES + BF16_ROWS, :] = _ones_rows(seq)
        for i in range(seq // LANES):
            chunk = slice(i * LANES, (i + 1) * LANES)
            vt_scrs[c][0, 0:LANES, chunk] = v_refs[c][0, chunk, :].T

    zero = jnp.minimum(pl.program_id(1), 0)
    for t in range(len(blocks) + SCORE_LAG):
        if t < len(blocks):
            blk = blocks[t]
            _score_block(blk, q_refs[blk.cfg], k_refs[blk.cfg], bias_scr, s_scrs[t % SCORE_SLOTS])
        if t >= SCORE_LAG:
            u = t - SCORE_LAG
            blk = blocks[u]
            n = order.index(blk.cfg)
            _value_block(blk, zero, vt_scrs[blk.cfg], s_scrs[u % SCORE_SLOTS],
                         hand[n - 1] if n > 0 else None, hand[n] if n + 1 < n_cfg else None, a_ref)


def _attn(q, k, v):
    b, s, w = k[0].shape
    n_cfg = len(DILATED_CONFIGS)
    assert all(s % (dil * Q_BLOCK) == 0 for _, dil in DILATED_CONFIGS)
    n_tiles = len(_attn_plan(s)[1])
    blk = lambda lanes: pl.BlockSpec((1, s, lanes), lambda j, i: (i, 0, j))
    return pl.pallas_call(
        functools.partial(_attn_kernel, seq=s),
        grid=(w // LANES, b),
        in_specs=[blk(LANES)] * (3 * n_cfg),
        out_specs=blk(LANES),
        out_shape=jax.ShapeDtypeStruct((b, s, w), BF16),
        scratch_shapes=[
            pltpu.VMEM((n_tiles, KEY_WINDOW, HEADS_PER_STEP * Q_BLOCK), F32),
        ] + [pltpu.VMEM((1, LANES + BF16_ROWS, s), BF16)] * n_cfg
        + [pltpu.VMEM((s, LANES), F32)] * (2 * (n_cfg - 1))
        + [pltpu.VMEM((1, KEY_WINDOW, HEADS_PER_STEP * Q_BLOCK), F32)] * SCORE_SLOTS,
        compiler_params=pltpu.CompilerParams(
            dimension_semantics=("arbitrary", "arbitrary"), vmem_limit_bytes=VMEM_LIMIT_BYTES),
        name="attn",
    )(*q, *k, *v)


def _mix_kernel(a_ref, act_ref, km_ref, vtm_ref, x_ref, ws_ref, bs_ref, wo_ref, gf_ref, o_ref):
    tm = x_ref.shape[1]
    lo = 0
    cols = []
    for width in ACT_SPLITS:
        cols.append(slice(lo, lo + width))
        lo += width
    sa, gu, vn, qm, sm = (act_ref[0, :, c] for c in cols)

    ga = sa * a_ref[0]

    group = lax.broadcasted_iota(jnp.int32, (SGU_CHUNK, SGU_WIDTH), 1) // SGU_GROUP
    mixed = []
    for c in range(tm // SGU_CHUNK):
        vc = vn[c * SGU_CHUNK:(c + 1) * SGU_CHUNK]
        stacked = jnp.concatenate([jnp.where(group == g, vc, jnp.zeros_like(vc)) for g in range(N_SGU_GROUPS)],
                                  axis=0)
        mixed.append(jnp.dot(ws_ref[...], stacked, preferred_element_type=F32) + bs_ref[...])
    gb = (gu.astype(F32) * jnp.concatenate(mixed, axis=0)).astype(BF16)

    mhead = lax.broadcasted_iota(jnp.int32, (tm, MEM_WIDTH), 1) // MEM_HEAD_DIM
    q4 = jnp.concatenate([jnp.where(mhead == h, qm, jnp.zeros_like(qm)) for h in range(N_MEM_HEADS)], axis=0)
    s = lax.dot_general(km_ref[0], q4, (((1,), (1,)), ((), ())), preferred_element_type=F32)
    p = jnp.exp2(s - jnp.max(s, axis=0, keepdims=True)).astype(BF16)
    ones = vtm_ref[0, MEM_WIDTH:, :]
    mo = []
    for h in range(N_MEM_HEADS):
        vt = jnp.concatenate([vtm_ref[0, h * MEM_HEAD_DIM:(h + 1) * MEM_HEAD_DIM, :], ones], axis=0)
        ot = jnp.dot(vt, p[:, h * tm:(h + 1) * tm], preferred_element_type=F32)
        mo.append(ot[0:MEM_HEAD_DIM] * (1.0 / ot[MEM_HEAD_DIM:MEM_HEAD_DIM + 1]))
    mo = jnp.concatenate(mo, axis=0).T
    gm = (sm.astype(F32) * mo).astype(BF16)

    e1 = ATTN_WIDTH
    e2 = e1 + SGU_WIDTH
    y = jnp.dot(ga, wo_ref[0:e1, :], preferred_element_type=F32)
    y = y + jnp.dot(gb, wo_ref[e1:e2, :], preferred_element_type=F32)
    y = y + jnp.dot(gm, wo_ref[e2:, :], preferred_element_type=F32)
    o_ref[0] = _rms(x_ref[0] + y, gf_ref[...])


def _mix(a, act, k_mem, vt_mem, x, w_s, b_tile, w_out, g_final):
    b, s, d = x.shape
    tile = lambda w: pl.BlockSpec((1, ROW_TILE, w), lambda i, j: (i, j, 0))
    per_batch = lambda arr: pl.BlockSpec((1,) + arr.shape[1:], lambda i, j: (i, 0, 0))
    fixed = lambda arr: pl.BlockSpec(arr.shape, lambda i, j: (0, 0))
    return pl.pallas_call(
        _mix_kernel,
        grid=(b, s // ROW_TILE),
        in_specs=[tile(a.shape[2]), tile(act.shape[2]), per_batch(k_mem), per_batch(vt_mem), tile(d),
                  fixed(w_s), fixed(b_tile), fixed(w_out), fixed(g_final)],
        out_specs=tile(d),
        out_shape=jax.ShapeDtypeStruct((b, s, d), x.dtype),
        compiler_params=pltpu.CompilerParams(
            dimension_semantics=("arbitrary", "arbitrary"), vmem_limit_bytes=VMEM_LIMIT_BYTES),
        name="mix",
    )(a, act, k_mem, vt_mem, x, w_s, b_tile, w_out, g_final)


def kernel(x, mem, g_norm, w_in, w_sgu_spatial, b_sgu_spatial, g_sgu_v, g_mem, w_mem_kv, w_out, g_final):
    assert g_norm.shape[0] == 1, "the final norm is fused into the single layer's last kernel"
    b, s, d = x.shape
    k_mem, vt_mem = _memkv(mem, g_mem[0][None, :], w_mem_kv[0].astype(BF16))
    q, k, v, act = _inproj(x, g_norm[0][None, :], g_sgu_v[0][None, :], w_in[0].astype(BF16))
    a = _attn(q, k, v)
    w_s = jnp.concatenate(list(w_sgu_spatial[0].astype(BF16)), axis=1)
    b_tile = jnp.repeat(b_sgu_spatial[0].T, SGU_GROUP, axis=1)
    return _mix(a, act, k_mem, vt_mem, x, w_s, b_tile, w_out[0].astype(BF16), g_final[None, :])
```

```python
import functools
import math
from typing import NamedTuple

import jax
import jax.numpy as jnp
from jax import lax
from jax.experimental import pallas as pl
from jax.experimental.pallas import tpu as pltpu

F32 = jnp.float32
BF16 = jnp.bfloat16

EPS = 1e-6
NEG_INF = -1e30
LOG2_E = math.log2(math.e)

HEAD_DIM = 64
N_ATTN_HEADS = 8
ATTN_WIDTH = HEAD_DIM * N_ATTN_HEADS
DILATED_CONFIGS = ((128, 1), (512, 4), (2048, 16))
RADIUS = 64
SGU_WIDTH = 256
N_SGU_GROUPS = 4
SGU_GROUP = SGU_WIDTH // N_SGU_GROUPS
SGU_CHUNK = 128
MEM_WIDTH = 256
N_MEM_HEADS = 4
MEM_HEAD_DIM = MEM_WIDTH // N_MEM_HEADS
ACT_SPLITS = (ATTN_WIDTH, SGU_WIDTH, SGU_WIDTH, MEM_WIDTH, MEM_WIDTH)
ACT_WIDTH = sum(ACT_SPLITS)

LANES = 128
BF16_ROWS = 16
MXU_COLS = 256
Q_BLOCK = 128
KEY_WINDOW = 2 * Q_BLOCK
HEADS_PER_STEP = LANES // HEAD_DIM
SCORE_LAG = 4
SCORE_SLOTS = 2 * SCORE_LAG
ROW_TILE = 512
MIX_TILE = 1024
VMEM_LIMIT_BYTES = 56 * 1024 * 1024


def _rms(x, g):
    return x * lax.rsqrt(jnp.mean(x * x, axis=-1, keepdims=True) + EPS) * g


def _silu(x):
    return x / (1.0 + jnp.exp(-x))


def _ones_rows(cols):
    first = lax.broadcasted_iota(jnp.int32, (BF16_ROWS, cols), 0) == 0
    return jnp.where(first, 1.0, 0.0).astype(BF16)


def _memkv_kernel(mem_ref, g_ref, w_ref, k_ref, vt_ref):
    h = _rms(mem_ref[0], g_ref[...]).astype(BF16)
    kv = jnp.dot(h, w_ref[...], preferred_element_type=F32)
    k_ref[0] = kv[:, 0:MEM_WIDTH].astype(BF16)
    vt_ref[0, 0:MEM_WIDTH, :] = kv[:, MEM_WIDTH:].T.astype(BF16)
    vt_ref[0, MEM_WIDTH:, :] = _ones_rows(kv.shape[0])


def _memkv(mem, g_mem, w_kv):
    b, m, d = mem.shape
    n = w_kv.shape[1]
    return pl.pallas_call(
        _memkv_kernel,
        grid=(b,),
        in_specs=[
            pl.BlockSpec((1, m, d), lambda i: (i, 0, 0)),
            pl.BlockSpec((1, d), lambda i: (0, 0)),
            pl.BlockSpec((d, n), lambda i: (0, 0)),
        ],
        out_specs=[
            pl.BlockSpec((1, m, MEM_WIDTH), lambda i: (i, 0, 0)),
            pl.BlockSpec((1, MEM_WIDTH + BF16_ROWS, m), lambda i: (i, 0, 0)),
        ],
        out_shape=[
            jax.ShapeDtypeStruct((b, m, MEM_WIDTH), BF16),
            jax.ShapeDtypeStruct((b, MEM_WIDTH + BF16_ROWS, m), BF16),
        ],
        compiler_params=pltpu.CompilerParams(
            dimension_semantics=("arbitrary",), vmem_limit_bytes=VMEM_LIMIT_BYTES),
        name="memkv",
    )(mem, g_mem, w_kv)


def _emit_orders(val, outs, lo, order_scrs):
    rows = val.shape[0]
    n_slabs = MXU_COLS // LANES

    def write(ref, r, slabs):
        ref[r, :, lo:lo + MXU_COLS] = jnp.concatenate(slabs, axis=1).astype(BF16)

    by_dil = sorted(range(len(DILATED_CONFIGS)), key=lambda c: DILATED_CONFIGS[c][1])
    prev = 1
    for level, c in enumerate(by_dil):
        dil = DILATED_CONFIGS[c][1]
        if dil == 1:
            slabs = [val[:, i * LANES:(i + 1) * LANES] for i in range(n_slabs)]
            write(outs[c], 0, slabs)
            for scr, x in zip(order_scrs, slabs):
                scr[0] = x
            continue
        step, cls_prev, cls = dil // prev, rows // prev, rows // dil
        for r in range(dil):
            slabs = [scr[level - 1, pl.ds((r % prev) * cls_prev + r // prev, cls, stride=step), :]
                     for scr in order_scrs]
            write(outs[c], r, slabs)
            if level + 1 < len(by_dil):
                for scr, x in zip(order_scrs, slabs):
                    scr[level, r * cls:(r + 1) * cls, :] = x
        prev = dil


def _inproj_kernel(x_ref, g_ref, gv_ref, w_ref, *refs):
    n_cfg = len(DILATED_CONFIGS)
    q_refs, k_refs, v_refs = refs[0:n_cfg], refs[n_cfg:2 * n_cfg], refs[2 * n_cfg:3 * n_cfg]
    act_ref = refs[3 * n_cfg]
    order_scrs = refs[3 * n_cfg + 1:]
    h = _rms(x_ref[0], g_ref[...]).astype(BF16)
    at = [0]

    def proj(width):
        lo = at[0]
        at[0] = lo + width
        return jnp.dot(h, w_ref[:, lo:lo + width], preferred_element_type=F32)

    n_slabs = MXU_COLS // LANES
    at_scr = 0
    for out_refs, scale in ((q_refs, HEAD_DIM ** -0.5 * LOG2_E), (k_refs, None), (v_refs, None)):
        for piece in range(ATTN_WIDTH // MXU_COLS):
            val = proj(MXU_COLS)
            if scale is not None:
                val = val * scale
            _emit_orders(val, out_refs, piece * MXU_COLS, order_scrs[at_scr:at_scr + n_slabs])
            at_scr += n_slabs
    za = proj(ATTN_WIDTH)
    ub = proj(SGU_WIDTH)
    vb = proj(SGU_WIDTH)
    zb = proj(SGU_WIDTH)
    qm = proj(MEM_WIDTH)
    zm = proj(MEM_WIDTH)
    acts = (_silu(za), _silu(zb) * jax.nn.gelu(ub), _rms(jax.nn.gelu(vb), gv_ref[...]),
            qm * (MEM_HEAD_DIM ** -0.5 * LOG2_E), _silu(zm))
    lo = 0
    for width, val in zip(ACT_SPLITS, acts):
        act_ref[0, :, lo:lo + width] = val.astype(BF16)
        lo += width


def _inproj(x, g_norm, g_v, w_in):
    b, s, d = x.shape
    cols = w_in.shape[1]
    assert cols == 4 * ATTN_WIDTH + 3 * SGU_WIDTH + 2 * MEM_WIDTH
    tile = lambda w: pl.BlockSpec((1, ROW_TILE, w), lambda i, j: (i, j, 0))
    fixed = lambda i, j: (0, 0)
    dils = [dil for _, dil in DILATED_CONFIGS]
    assert all(ROW_TILE % (dil * BF16_ROWS) == 0 for dil in dils)
    ordered = lambda w: [pl.BlockSpec((dil, ROW_TILE // dil, w), lambda i, j: (i, j, 0)) for dil in dils]
    shaped = lambda w: [jax.ShapeDtypeStruct((b * dil, s // dil, w), BF16) for dil in dils]
    widths = (ATTN_WIDTH, ATTN_WIDTH, ATTN_WIDTH)
    outs = pl.pallas_call(
        _inproj_kernel,
        grid=(b, s // ROW_TILE),
        in_specs=[
            tile(d),
            pl.BlockSpec((1, d), fixed),
            pl.BlockSpec((1, SGU_WIDTH), fixed),
            pl.BlockSpec((d, cols), fixed),
        ],
        out_specs=[spec for w in widths for spec in ordered(w)] + [tile(ACT_WIDTH)],
        out_shape=[shape for w in widths for shape in shaped(w)] + [jax.ShapeDtypeStruct((b, s, ACT_WIDTH), BF16)],
        scratch_shapes=[pltpu.VMEM((len(dils) - 1, ROW_TILE, LANES), F32)] * (3 * ATTN_WIDTH // LANES),
        compiler_params=pltpu.CompilerParams(
            dimension_semantics=("arbitrary", "arbitrary"), vmem_limit_bytes=VMEM_LIMIT_BYTES),
        name="inproj",
    )(x, g_norm, g_v, w_in)
    n = len(dils)
    q, k, v = ([o.reshape(b, s, -1) for o in outs[i * n:(i + 1) * n]] for i in range(3))
    return q, k, v, outs[3 * n]


SHIFT, EDGE, WHOLE = "shift", "edge", "whole"


def _bias_tile(kind, dilation, first_head):
    shape = (KEY_WINDOW, HEADS_PER_STEP * Q_BLOCK)
    key = lax.broadcasted_iota(jnp.int32, shape, 0)
    col = lax.broadcasted_iota(jnp.int32, shape, 1)
    second = col >= Q_BLOCK
    qi = jnp.where(second, col - Q_BLOCK, col)
    if kind == SHIFT:
        rel = jnp.abs(qi + RADIUS - key)
        valid = rel <= RADIUS
    elif kind == WHOLE:
        rel = jnp.abs(qi - key)
        valid = rel <= RADIUS
    else:
        upper = key >= Q_BLOCK
        rel = jnp.abs(qi - jnp.where(upper, key - Q_BLOCK, key))
        valid = (rel <= RADIUS) & (upper == (qi >= RADIUS))
    head = (first_head + jnp.where(second, 1, 0)).astype(F32)
    slope = jnp.exp2(-8.0 * (head + 1.0) / N_ATTN_HEADS)
    dist = (rel * dilation).astype(F32)
    return jnp.where(valid, -slope * dist * LOG2_E, NEG_INF)


class _Block(NamedTuple):
    cfg: int
    tile: int
    q_rows: tuple
    k_rows: tuple
    out_rows: tuple
    out_stride: int


def _cfg_order():
    return sorted(range(len(DILATED_CONFIGS)), key=lambda c: -DILATED_CONFIGS[c][1])


def _attn_plan(seq):
    blocks, tiles = [], []
    order = _cfg_order()
    for n, c in enumerate(order):
        dil = DILATED_CONFIGS[c][1]
        nxt = DILATED_CONFIGS[order[n + 1]][1] if n + 1 < len(order) else 1
        step = dil // nxt
        cls = seq // dil
        if cls == Q_BLOCK:
            tiles.append((WHOLE, dil))
            whole = len(tiles) - 1
        else:
            tiles.append((SHIFT, dil))
            tiles.append((EDGE, dil))
            shift, edge = len(tiles) - 2, len(tiles) - 1
        for r in range(dil):
            base = r * cls
            sink = (r % nxt) * (seq // nxt) + r // nxt
            if cls == Q_BLOCK:
                blocks.append(_Block(c, whole, ((base, Q_BLOCK),), ((base, Q_BLOCK),), ((sink, Q_BLOCK),), step))
                continue
            last = cls - RADIUS
            blocks.append(_Block(c, edge, ((base, RADIUS), (base + last, RADIUS)),
                                 ((base, Q_BLOCK), (base + cls - Q_BLOCK, Q_BLOCK)),
                                 ((sink, RADIUS), (sink + step * last, RADIUS)), step))
            for j in range(cls // Q_BLOCK - 1):
                u = RADIUS + j * Q_BLOCK
                blocks.append(_Block(c, shift, ((base + u, Q_BLOCK),), ((base + j * Q_BLOCK, KEY_WINDOW),),
                                     ((sink + step * u, Q_BLOCK),), step))
    return blocks, tiles


def _rows(ref, lead, ranges):
    parts = [ref[lead + (pl.ds(s, n), slice(None))] for s, n in ranges]
    return parts[0] if len(parts) == 1 else jnp.concatenate(parts, axis=0)


def _score_block(blk, q_ref, k_ref, bias_scr, s_scr):
    q = _rows(q_ref, (0,), blk.q_rows)
    first = lax.broadcasted_iota(jnp.int32, (1, LANES), 1) < HEAD_DIM
    zeros = jnp.zeros_like(q)
    q2 = jnp.concatenate([jnp.where(first, q, zeros), jnp.where(first, zeros, q)], axis=0)
    k_win = _rows(k_ref, (0,), blk.k_rows)
    window = k_win.shape[0]
    s = lax.dot_general(k_win, q2, (((1,), (1,)), ((), ())), preferred_element_type=F32)
    s_scr[0, 0:window, :] = s + bias_scr[blk.tile, 0:window, :]


def _value_block(blk, zero, vt_scr, s_scr, prev, sink, a_ref):
    window = sum(n for _, n in blk.k_rows)
    ms, ps = [], []
    for h in range(HEADS_PER_STEP):
        cols = slice(h * Q_BLOCK, (h + 1) * Q_BLOCK)
        s = s_scr[zero, 0:window, cols]
        m = jnp.max(s, axis=0, keepdims=True)
        ms.append(m)
        ps.append(jnp.exp2(s - m).astype(BF16))
    vt = jnp.concatenate([vt_scr[zero, :, s0:s0 + n] for s0, n in blk.k_rows], axis=1)
    ot = jnp.dot(vt, jnp.concatenate(ps, axis=1), preferred_element_type=F32)
    outs, lses = [], []
    for h in range(HEADS_PER_STEP):
        cols = slice(h * Q_BLOCK, (h + 1) * Q_BLOCK)
        l = ot[LANES:LANES + 1, cols]
        outs.append(ot[h * HEAD_DIM:(h + 1) * HEAD_DIM, cols] * (1.0 / l))
        lses.append(ms[h] + jnp.log2(l))
    o = jnp.concatenate(outs, axis=0).T
    lse = jnp.concatenate([jnp.broadcast_to(x, (HEAD_DIM, Q_BLOCK)) for x in lses], axis=0).T
    if prev is not None:
        o_prev, lse_prev = (_rows(ref, (), blk.q_rows) for ref in prev)
        top = jnp.maximum(lse, lse_prev)
        e, e_prev = jnp.exp2(lse - top), jnp.exp2(lse_prev - top)
        den = e + e_prev
        o = (e * o + e_prev * o_prev) * (1.0 / den)
        lse = top + jnp.log2(den)
    at = 0
    for start, n in blk.out_rows:
        dst = pl.ds(start, n) if blk.out_stride == 1 else pl.ds(start, n, stride=blk.out_stride)
        if sink is None:
            a_ref[0, dst, :] = o[at:at + n].astype(a_ref.dtype)
        else:
            sink[0][dst, :] = o[at:at + n]
            sink[1][dst, :] = lse[at:at + n]
        at += n


def _attn_kernel(*refs, seq):
    n_cfg = len(DILATED_CONFIGS)
    q_refs, k_refs, v_refs = refs[0:n_cfg], refs[n_cfg:2 * n_cfg], refs[2 * n_cfg:3 * n_cfg]
    a_ref, bias_scr = refs[3 * n_cfg:3 * n_cfg + 2]
    scrs = refs[3 * n_cfg + 2:]
    vt_scrs = scrs[0:n_cfg]
    hand = [scrs[n_cfg + 2 * i:n_cfg + 2 * i + 2] for i in range(n_cfg - 1)]
    s_scrs = scrs[3 * n_cfg - 2:]
    blocks, tiles = _attn_plan(seq)
    order = _cfg_order()

    @pl.when(pl.program_id(1) == 0)
    def _():
        first_head = pl.program_id(0) * HEADS_PER_STEP
        for t, (kind, dil) in enumerate(tiles):
            bias_scr[t] = _bias_tile(kind, dil, first_head)

    for c in order:
        vt_scrs[c][0, LANES:LANES + BF16_ROWS, :] = _ones_rows(seq)
        for i in range(seq // LANES):
            chunk = slice(i * LANES, (i + 1) * LANES)
            vt_scrs[c][0, 0:LANES, chunk] = v_refs[c][0, chunk, :].T

    zero = jnp.minimum(pl.program_id(1), 0)
    for t in range(len(blocks) + SCORE_LAG):
        if t < len(blocks):
            blk = blocks[t]
            _score_block(blk, q_refs[blk.cfg], k_refs[blk.cfg], bias_scr, s_scrs[t % SCORE_SLOTS])
        if t >= SCORE_LAG:
            u = t - SCORE_LAG
            blk = blocks[u]
            n = order.index(blk.cfg)
            _value_block(blk, zero, vt_scrs[blk.cfg], s_scrs[u % SCORE_SLOTS],
                         hand[n - 1] if n > 0 else None, hand[n] if n + 1 < n_cfg else None, a_ref)


def _attn(q, k, v):
    b, s, w = k[0].shape
    n_cfg = len(DILATED_CONFIGS)
    assert all(s % (dil * Q_BLOCK) == 0 for _, dil in DILATED_CONFIGS)
    n_tiles = len(_attn_plan(s)[1])
    blk = lambda lanes: pl.BlockSpec((1, s, lanes), lambda j, i: (i, 0, j))
    return pl.pallas_call(
        functools.partial(_attn_kernel, seq=s),
        grid=(w // LANES, b),
        in_specs=[blk(LANES)] * (3 * n_cfg),
        out_specs=blk(LANES),
        out_shape=jax.ShapeDtypeStruct((b, s, w), BF16),
        scratch_shapes=[
            pltpu.VMEM((n_tiles, KEY_WINDOW, HEADS_PER_STEP * Q_BLOCK), F32),
        ] + [pltpu.VMEM((1, LANES + BF16_ROWS, s), BF16)] * n_cfg
        + [pltpu.VMEM((s, LANES), F32)] * (2 * (n_cfg - 1))
        + [pltpu.VMEM((1, KEY_WINDOW, HEADS_PER_STEP * Q_BLOCK), F32)] * SCORE_SLOTS,
        compiler_params=pltpu.CompilerParams(
            dimension_semantics=("arbitrary", "arbitrary"), vmem_limit_bytes=VMEM_LIMIT_BYTES),
        name="attn",
    )(*q, *k, *v)


def _mix_kernel(a_ref, act_ref, km_ref, vtm_ref, x_ref, ws_ref, bs_ref, wo_ref, gf_ref, o_ref):
    rows = pl.ds(0, x_ref.shape[1])
    gated = _mix_branches(rows, a_ref, act_ref, km_ref, vtm_ref, ws_ref, bs_ref)
    _mix_project(rows, gated, x_ref, wo_ref, gf_ref, o_ref)


def _mix_branches(rows, a_ref, act_ref, km_ref, vtm_ref, ws_ref, bs_ref):
    tm = rows.size
    lo = 0
    cols = []
    for width in ACT_SPLITS:
        cols.append(slice(lo, lo + width))
        lo += width
    sa, gu, vn, qm, sm = (act_ref[0, rows, c] for c in cols)

    ga = sa * a_ref[0, rows, :]

    group = lax.broadcasted_iota(jnp.int32, (SGU_CHUNK, SGU_WIDTH), 1) // SGU_GROUP
    mixed = []
    for c in range(tm // SGU_CHUNK):
        vc = vn[c * SGU_CHUNK:(c + 1) * SGU_CHUNK]
        stacked = jnp.concatenate([jnp.where(group == g, vc, jnp.zeros_like(vc)) for g in range(N_SGU_GROUPS)],
                                  axis=0)
        mixed.append(jnp.dot(ws_ref[...], stacked, preferred_element_type=F32) + bs_ref[...])
    gb = (gu.astype(F32) * jnp.concatenate(mixed, axis=0)).astype(BF16)

    mhead = lax.broadcasted_iota(jnp.int32, (tm, MEM_WIDTH), 1) // MEM_HEAD_DIM
    q4 = jnp.concatenate([jnp.where(mhead == h, qm, jnp.zeros_like(qm)) for h in range(N_MEM_HEADS)], axis=0)
    s = lax.dot_general(km_ref[0], q4, (((1,), (1,)), ((), ())), preferred_element_type=F32)
    p = jnp.exp2(s - jnp.max(s, axis=0, keepdims=True)).astype(BF16)
    ones = vtm_ref[0, MEM_WIDTH:, :]
    mo = []
    for h in range(N_MEM_HEADS):
        vt = jnp.concatenate([vtm_ref[0, h * MEM_HEAD_DIM:(h + 1) * MEM_HEAD_DIM, :], ones], axis=0)
        ot = jnp.dot(vt, p[:, h * tm:(h + 1) * tm], preferred_element_type=F32)
        mo.append(ot[0:MEM_HEAD_DIM] * (1.0 / ot[MEM_HEAD_DIM:MEM_HEAD_DIM + 1]))
    mo = jnp.concatenate(mo, axis=0).T
    gm = (sm.astype(F32) * mo).astype(BF16)
    return ga, gb, gm


def _mix_project(rows, gated, x_ref, wo_ref, gf_ref, o_ref):
    ga, gb, gm = gated
    e1 = ATTN_WIDTH
    e2 = e1 + SGU_WIDTH
    y = jnp.dot(ga, wo_ref[0:e1, :], preferred_element_type=F32)
    y = y + jnp.dot(gb, wo_ref[e1:e2, :], preferred_element_type=F32)
    y = y + jnp.dot(gm, wo_ref[e2:, :], preferred_element_type=F32)
    o_ref[0, rows, :] = _rms(x_ref[0, rows, :] + y, gf_ref[...])


def _mix(a, act, k_mem, vt_mem, x, w_s, b_tile, w_out, g_final):
    b, s, d = x.shape
    tile = lambda w: pl.BlockSpec((1, MIX_TILE, w), lambda i, j: (i, j, 0))
    per_batch = lambda arr: pl.BlockSpec((1,) + arr.shape[1:], lambda i, j: (i, 0, 0))
    fixed = lambda arr: pl.BlockSpec(arr.shape, lambda i, j: (0, 0))
    return pl.pallas_call(
        _mix_kernel,
        grid=(b, s // MIX_TILE),
        in_specs=[tile(a.shape[2]), tile(act.shape[2]), per_batch(k_mem), per_batch(vt_mem), tile(d),
                  fixed(w_s), fixed(b_tile), fixed(w_out), fixed(g_final)],
        out_specs=tile(d),
        out_shape=jax.ShapeDtypeStruct((b, s, d), x.dtype),
        compiler_params=pltpu.CompilerParams(
            dimension_semantics=("arbitrary", "arbitrary"), vmem_limit_bytes=VMEM_LIMIT_BYTES),
        name="mix",
    )(a, act, k_mem, vt_mem, x, w_s, b_tile, w_out, g_final)


def kernel(x, mem, g_norm, w_in, w_sgu_spatial, b_sgu_spatial, g_sgu_v, g_mem, w_mem_kv, w_out, g_final):
    assert g_norm.shape[0] == 1, "the final norm is fused into the single layer's last kernel"
    b, s, d = x.shape
    k_mem, vt_mem = _memkv(mem, g_mem[0][None, :], w_mem_kv[0].astype(BF16))
    q, k, v, act = _inproj(x, g_norm[0][None, :], g_sgu_v[0][None, :], w_in[0].astype(BF16))
    a = _attn(q, k, v)
    w_s = jnp.concatenate(list(w_sgu_spatial[0].astype(BF16)), axis=1)
    b_tile = jnp.repeat(b_sgu_spatial[0].T, SGU_GROUP, axis=1)
    return _mix(a, act, k_mem, vt_mem, x, w_s, b_tile, w_out[0].astype(BF16), g_final[None, :])
```

```python
import functools
import math
from typing import NamedTuple

import jax
import jax.numpy as jnp
from jax import lax
from jax.experimental import pallas as pl
from jax.experimental.pallas import tpu as pltpu

F32 = jnp.float32
BF16 = jnp.bfloat16

EPS = 1e-6
NEG_INF = -1e30
LOG2_E = math.log2(math.e)

HEAD_DIM = 64
N_ATTN_HEADS = 8
ATTN_WIDTH = HEAD_DIM * N_ATTN_HEADS
DILATED_CONFIGS = ((128, 1), (512, 4), (2048, 16))
RADIUS = 64
SGU_WIDTH = 256
N_SGU_GROUPS = 4
SGU_GROUP = SGU_WIDTH // N_SGU_GROUPS
SGU_CHUNK = 128
MEM_WIDTH = 256
N_MEM_HEADS = 4
MEM_HEAD_DIM = MEM_WIDTH // N_MEM_HEADS
ACT_SPLITS = (ATTN_WIDTH, SGU_WIDTH, SGU_WIDTH, MEM_WIDTH, MEM_WIDTH)
ACT_WIDTH = sum(ACT_SPLITS)

LANES = 128
BF16_ROWS = 16
MXU_COLS = 256
Q_BLOCK = 128
KEY_WINDOW = 2 * Q_BLOCK
HEADS_PER_STEP = LANES // HEAD_DIM
SCORE_LAG = 4
SCORE_SLOTS = 2 * SCORE_LAG
ROW_TILE = 1024
ORDER_BUFFERS = 4
MIX_TILE = 1024
VMEM_LIMIT_BYTES = 56 * 1024 * 1024


def _rms(x, g):
    return x * lax.rsqrt(jnp.mean(x * x, axis=-1, keepdims=True) + EPS) * g


def _silu(x):
    return x / (1.0 + jnp.exp(-x))


def _ones_rows(cols):
    first = lax.broadcasted_iota(jnp.int32, (BF16_ROWS, cols), 0) == 0
    return jnp.where(first, 1.0, 0.0).astype(BF16)


def _memkv_kernel(mem_ref, g_ref, w_ref, k_ref, vt_ref):
    h = _rms(mem_ref[0], g_ref[...]).astype(BF16)
    kv = jnp.dot(h, w_ref[...], preferred_element_type=F32)
    k_ref[0] = kv[:, 0:MEM_WIDTH].astype(BF16)
    vt_ref[0, 0:MEM_WIDTH, :] = kv[:, MEM_WIDTH:].T.astype(BF16)
    vt_ref[0, MEM_WIDTH:, :] = _ones_rows(kv.shape[0])


def _memkv(mem, g_mem, w_kv):
    b, m, d = mem.shape
    n = w_kv.shape[1]
    return pl.pallas_call(
        _memkv_kernel,
        grid=(b,),
        in_specs=[
            pl.BlockSpec((1, m, d), lambda i: (i, 0, 0)),
            pl.BlockSpec((1, d), lambda i: (0, 0)),
            pl.BlockSpec((d, n), lambda i: (0, 0)),
        ],
        out_specs=[
            pl.BlockSpec((1, m, MEM_WIDTH), lambda i: (i, 0, 0)),
            pl.BlockSpec((1, MEM_WIDTH + BF16_ROWS, m), lambda i: (i, 0, 0)),
        ],
        out_shape=[
            jax.ShapeDtypeStruct((b, m, MEM_WIDTH), BF16),
            jax.ShapeDtypeStruct((b, MEM_WIDTH + BF16_ROWS, m), BF16),
        ],
        compiler_params=pltpu.CompilerParams(
            dimension_semantics=("arbitrary",), vmem_limit_bytes=VMEM_LIMIT_BYTES),
        name="memkv",
    )(mem, g_mem, w_kv)


def _emit_orders(val, outs, lo, order_scrs):
    rows = val.shape[0]
    n_slabs = MXU_COLS // LANES

    def write(ref, r, slabs):
        ref[r, :, lo:lo + MXU_COLS] = jnp.concatenate(slabs, axis=1).astype(BF16)

    by_dil = sorted(range(len(DILATED_CONFIGS)), key=lambda c: DILATED_CONFIGS[c][1])
    prev = 1
    for level, c in enumerate(by_dil):
        dil = DILATED_CONFIGS[c][1]
        if dil == 1:
            slabs = [val[:, i * LANES:(i + 1) * LANES] for i in range(n_slabs)]
            write(outs[c], 0, slabs)
            for scr, x in zip(order_scrs, slabs):
                scr[0] = x
            continue
        step, cls_prev, cls = dil // prev, rows // prev, rows // dil
        for r in range(dil):
            slabs = [scr[level - 1, pl.ds((r % prev) * cls_prev + r // prev, cls, stride=step), :]
                     for scr in order_scrs]
            write(outs[c], r, slabs)
            if level + 1 < len(by_dil):
                for scr, x in zip(order_scrs, slabs):
                    scr[level, r * cls:(r + 1) * cls, :] = x
        prev = dil


def _inproj_kernel(x_ref, g_ref, gv_ref, w_ref, *refs):
    n_cfg = len(DILATED_CONFIGS)
    q_refs, k_refs, v_refs = refs[0:n_cfg], refs[n_cfg:2 * n_cfg], refs[2 * n_cfg:3 * n_cfg]
    act_ref = refs[3 * n_cfg]
    order_scrs = refs[3 * n_cfg + 1:]
    h = _rms(x_ref[0], g_ref[...]).astype(BF16)
    at = [0]

    def proj(width):
        lo = at[0]
        at[0] = lo + width
        return jnp.dot(h, w_ref[:, lo:lo + width], preferred_element_type=F32)

    n_slabs = MXU_COLS // LANES
    at_scr = 0
    for out_refs, scale in ((q_refs, HEAD_DIM ** -0.5 * LOG2_E), (k_refs, None), (v_refs, None)):
        for piece in range(ATTN_WIDTH // MXU_COLS):
            val = proj(MXU_COLS)
            if scale is not None:
                val = val * scale
            _emit_orders(val, out_refs, piece * MXU_COLS, order_scrs[at_scr:at_scr + n_slabs])
            at_scr = (at_scr + n_slabs) % len(order_scrs)
    za = proj(ATTN_WIDTH)
    ub = proj(SGU_WIDTH)
    vb = proj(SGU_WIDTH)
    zb = proj(SGU_WIDTH)
    qm = proj(MEM_WIDTH)
    zm = proj(MEM_WIDTH)
    acts = (_silu(za), _silu(zb) * jax.nn.gelu(ub), _rms(jax.nn.gelu(vb), gv_ref[...]),
            qm * (MEM_HEAD_DIM ** -0.5 * LOG2_E), _silu(zm))
    lo = 0
    for width, val in zip(ACT_SPLITS, acts):
        act_ref[0, :, lo:lo + width] = val.astype(BF16)
        lo += width


def _inproj(x, g_norm, g_v, w_in):
    b, s, d = x.shape
    cols = w_in.shape[1]
    assert cols == 4 * ATTN_WIDTH + 3 * SGU_WIDTH + 2 * MEM_WIDTH
    tile = lambda w: pl.BlockSpec((1, ROW_TILE, w), lambda i, j: (i, j, 0))
    fixed = lambda i, j: (0, 0)
    dils = [dil for _, dil in DILATED_CONFIGS]
    assert all(ROW_TILE % (dil * BF16_ROWS) == 0 for dil in dils)
    ordered = lambda w: [pl.BlockSpec((dil, ROW_TILE // dil, w), lambda i, j: (i, j, 0)) for dil in dils]
    shaped = lambda w: [jax.ShapeDtypeStruct((b * dil, s // dil, w), BF16) for dil in dils]
    widths = (ATTN_WIDTH, ATTN_WIDTH, ATTN_WIDTH)
    outs = pl.pallas_call(
        _inproj_kernel,
        grid=(b, s // ROW_TILE),
        in_specs=[
            tile(d),
            pl.BlockSpec((1, d), fixed),
            pl.BlockSpec((1, SGU_WIDTH), fixed),
            pl.BlockSpec((d, cols), fixed, pipeline_mode=pl.Buffered(1)),
        ],
        out_specs=[spec for w in widths for spec in ordered(w)] + [tile(ACT_WIDTH)],
        out_shape=[shape for w in widths for shape in shaped(w)] + [jax.ShapeDtypeStruct((b, s, ACT_WIDTH), BF16)],
        scratch_shapes=[pltpu.VMEM((len(dils) - 1, ROW_TILE, LANES), F32)] * ORDER_BUFFERS,
        compiler_params=pltpu.CompilerParams(
            dimension_semantics=("arbitrary", "arbitrary"), vmem_limit_bytes=VMEM_LIMIT_BYTES),
        name="inproj",
    )(x, g_norm, g_v, w_in)
    n = len(dils)
    q, k, v = ([o.reshape(b, s, -1) for o in outs[i * n:(i + 1) * n]] for i in range(3))
    return q, k, v, outs[3 * n]


SHIFT, EDGE, WHOLE = "shift", "edge", "whole"


def _bias_tile(kind, dilation, first_head):
    shape = (KEY_WINDOW, HEADS_PER_STEP * Q_BLOCK)
    key = lax.broadcasted_iota(jnp.int32, shape, 0)
    col = lax.broadcasted_iota(jnp.int32, shape, 1)
    second = col >= Q_BLOCK
    qi = jnp.where(second, col - Q_BLOCK, col)
    if kind == SHIFT:
        rel = jnp.abs(qi + RADIUS - key)
        valid = rel <= RADIUS
    elif kind == WHOLE:
        rel = jnp.abs(qi - key)
        valid = rel <= RADIUS
    else:
        upper = key >= Q_BLOCK
        rel = jnp.abs(qi - jnp.where(upper, key - Q_BLOCK, key))
        valid = (rel <= RADIUS) & (upper == (qi >= RADIUS))
    head = (first_head + jnp.where(second, 1, 0)).astype(F32)
    slope = jnp.exp2(-8.0 * (head + 1.0) / N_ATTN_HEADS)
    dist = (rel * dilation).astype(F32)
    return jnp.where(valid, -slope * dist * LOG2_E, NEG_INF)


class _Block(NamedTuple):
    cfg: int
    tile: int
    q_rows: tuple
    k_rows: tuple
    out_rows: tuple
    out_stride: int


def _cfg_order():
    return sorted(range(len(DILATED_CONFIGS)), key=lambda c: -DILATED_CONFIGS[c][1])


def _attn_plan(seq):
    blocks, tiles = [], []
    order = _cfg_order()
    for n, c in enumerate(order):
        dil = DILATED_CONFIGS[c][1]
        nxt = DILATED_CONFIGS[order[n + 1]][1] if n + 1 < len(order) else 1
        step = dil // nxt
        cls = seq // dil
        if cls == Q_BLOCK:
            tiles.append((WHOLE, dil))
            whole = len(tiles) - 1
        else:
            tiles.append((SHIFT, dil))
            tiles.append((EDGE, dil))
            shift, edge = len(tiles) - 2, len(tiles) - 1
        for r in range(dil):
            base = r * cls
            sink = (r % nxt) * (seq // nxt) + r // nxt
            if cls == Q_BLOCK:
                blocks.append(_Block(c, whole, ((base, Q_BLOCK),), ((base, Q_BLOCK),), ((sink, Q_BLOCK),), step))
                continue
            last = cls - RADIUS
            blocks.append(_Block(c, edge, ((base, RADIUS), (base + last, RADIUS)),
                                 ((base, Q_BLOCK), (base + cls - Q_BLOCK, Q_BLOCK)),
                                 ((sink, RADIUS), (sink + step * last, RADIUS)), step))
            for j in range(cls // Q_BLOCK - 1):
                u = RADIUS + j * Q_BLOCK
                blocks.append(_Block(c, shift, ((base + u, Q_BLOCK),), ((base + j * Q_BLOCK, KEY_WINDOW),),
                                     ((sink + step * u, Q_BLOCK),), step))
    return blocks, tiles


def _rows(ref, lead, ranges):
    parts = [ref[lead + (pl.ds(s, n), slice(None))] for s, n in ranges]
    return parts[0] if len(parts) == 1 else jnp.concatenate(parts, axis=0)


def _score_block(blk, q_ref, k_ref, bias_scr, s_scr):
    q = _rows(q_ref, (0,), blk.q_rows)
    first = lax.broadcasted_iota(jnp.int32, (1, LANES), 1) < HEAD_DIM
    zeros = jnp.zeros_like(q)
    q2 = jnp.concatenate([jnp.where(first, q, zeros), jnp.where(first, zeros, q)], axis=0)
    k_win = _rows(k_ref, (0,), blk.k_rows)
    window = k_win.shape[0]
    s = lax.dot_general(k_win, q2, (((1,), (1,)), ((), ())), preferred_element_type=F32)
    s_scr[0, 0:window, :] = s + bias_scr[blk.tile, 0:window, :]


def _value_block(blk, zero, vt_scr, s_scr, prev, sink, a_ref):
    window = sum(n for _, n in blk.k_rows)
    ms, ps = [], []
    for h in range(HEADS_PER_STEP):
        cols = slice(h * Q_BLOCK, (h + 1) * Q_BLOCK)
        s = s_scr[zero, 0:window, cols]
        m = jnp.max(s, axis=0, keepdims=True)
        ms.append(m)
        ps.append(jnp.exp2(s - m).astype(BF16))
    vt = jnp.concatenate([vt_scr[zero, :, s0:s0 + n] for s0, n in blk.k_rows], axis=1)
    ot = jnp.dot(vt, jnp.concatenate(ps, axis=1), preferred_element_type=F32)
    outs, lses = [], []
    for h in range(HEADS_PER_STEP):
        cols = slice(h * Q_BLOCK, (h + 1) * Q_BLOCK)
        l = ot[LANES:LANES + 1, cols]
        outs.append(ot[h * HEAD_DIM:(h + 1) * HEAD_DIM, cols] * (1.0 / l))
        lses.append(ms[h] + jnp.log2(l))
    o = jnp.concatenate(outs, axis=0).T
    lse = jnp.concatenate([jnp.broadcast_to(x, (HEAD_DIM, Q_BLOCK)) for x in lses], axis=0).T
    if prev is not None:
        o_prev, lse_prev = (_rows(ref, (), blk.q_rows) for ref in prev)
        top = jnp.maximum(lse, lse_prev)
        e, e_prev = jnp.exp2(lse - top), jnp.exp2(lse_prev - top)
        den = e + e_prev
        o = (e * o + e_prev * o_prev) * (1.0 / den)
        lse = top + jnp.log2(den)
    at = 0
    for start, n in blk.out_rows:
        dst = pl.ds(start, n) if blk.out_stride == 1 else pl.ds(start, n, stride=blk.out_stride)
        if sink is None:
            a_ref[0, dst, :] = o[at:at + n].astype(a_ref.dtype)
        else:
            sink[0][dst, :] = o[at:at + n]
            sink[1][dst, :] = lse[at:at + n]
        at += n


def _attn_kernel(*refs, seq):
    n_cfg = len(DILATED_CONFIGS)
    q_refs, k_refs, v_refs = refs[0:n_cfg], refs[n_cfg:2 * n_cfg], refs[2 * n_cfg:3 * n_cfg]
    a_ref, bias_scr = refs[3 * n_cfg:3 * n_cfg + 2]
    scrs = refs[3 * n_cfg + 2:]
    vt_scrs = scrs[0:n_cfg]
    hand = [scrs[n_cfg + 2 * i:n_cfg + 2 * i + 2] for i in range(n_cfg - 1)]
    s_scrs = scrs[3 * n_cfg - 2:]
    blocks, tiles = _attn_plan(seq)
    order = _cfg_order()

    @pl.when(pl.program_id(1) == 0)
    def _():
        first_head = pl.program_id(0) * HEADS_PER_STEP
        for t, (kind, dil) in enumerate(tiles):
            bias_scr[t] = _bias_tile(kind, dil, first_head)

    for c in order:
        vt_scrs[c][0, LANES:LANES + BF16_ROWS, :] = _ones_rows(seq)
        for i in range(seq // LANES):
            chunk = slice(i * LANES, (i + 1) * LANES)
            vt_scrs[c][0, 0:LANES, chunk] = v_refs[c][0, chunk, :].T

    zero = jnp.minimum(pl.program_id(1), 0)
    for t in range(len(blocks) + SCORE_LAG):
        if t < len(blocks):
            blk = blocks[t]
            _score_block(blk, q_refs[blk.cfg], k_refs[blk.cfg], bias_scr, s_scrs[t % SCORE_SLOTS])
        if t >= SCORE_LAG:
            u = t - SCORE_LAG
            blk = blocks[u]
            n = order.index(blk.cfg)
            _value_block(blk, zero, vt_scrs[blk.cfg], s_scrs[u % SCORE_SLOTS],
                         hand[n - 1] if n > 0 else None, hand[n] if n + 1 < n_cfg else None, a_ref)


def _attn(q, k, v):
    b, s, w = k[0].shape
    n_cfg = len(DILATED_CONFIGS)
    assert all(s % (dil * Q_BLOCK) == 0 for _, dil in DILATED_CONFIGS)
    n_tiles = len(_attn_plan(s)[1])
    blk = lambda lanes: pl.BlockSpec((1, s, lanes), lambda j, i: (i, 0, j))
    return pl.pallas_call(
        functools.partial(_attn_kernel, seq=s),
        grid=(w // LANES, b),
        in_specs=[blk(LANES)] * (3 * n_cfg),
        out_specs=blk(LANES),
        out_shape=jax.ShapeDtypeStruct((b, s, w), BF16),
        scratch_shapes=[
            pltpu.VMEM((n_tiles, KEY_WINDOW, HEADS_PER_STEP * Q_BLOCK), F32),
        ] + [pltpu.VMEM((1, LANES + BF16_ROWS, s), BF16)] * n_cfg
        + [pltpu.VMEM((s, LANES), F32)] * (2 * (n_cfg - 1))
        + [pltpu.VMEM((1, KEY_WINDOW, HEADS_PER_STEP * Q_BLOCK), F32)] * SCORE_SLOTS,
        compiler_params=pltpu.CompilerParams(
            dimension_semantics=("arbitrary", "arbitrary"), vmem_limit_bytes=VMEM_LIMIT_BYTES),
        name="attn",
    )(*q, *k, *v)


def _mix_kernel(a_ref, act_ref, km_ref, vtm_ref, x_ref, ws_ref, bs_ref, wo_ref, gf_ref, o_ref):
    rows = pl.ds(0, x_ref.shape[1])
    gated = _mix_branches(rows, a_ref, act_ref, km_ref, vtm_ref, ws_ref, bs_ref)
    _mix_project(rows, gated, x_ref, wo_ref, gf_ref, o_ref)


def _mix_branches(rows, a_ref, act_ref, km_ref, vtm_ref, ws_ref, bs_ref):
    tm = rows.size
    lo = 0
    cols = []
    for width in ACT_SPLITS:
        cols.append(slice(lo, lo + width))
        lo += width
    sa, gu, vn, qm, sm = (act_ref[0, rows, c] for c in cols)

    ga = sa * a_ref[0, rows, :]

    group = lax.broadcasted_iota(jnp.int32, (SGU_CHUNK, SGU_WIDTH), 1) // SGU_GROUP
    mixed = []
    for c in range(tm // SGU_CHUNK):
        vc = vn[c * SGU_CHUNK:(c + 1) * SGU_CHUNK]
        stacked = jnp.concatenate([jnp.where(group == g, vc, jnp.zeros_like(vc)) for g in range(N_SGU_GROUPS)],
                                  axis=0)
        mixed.append(jnp.dot(ws_ref[...], stacked, preferred_element_type=F32) + bs_ref[...])
    gb = (gu.astype(F32) * jnp.concatenate(mixed, axis=0)).astype(BF16)

    mhead = lax.broadcasted_iota(jnp.int32, (tm, MEM_WIDTH), 1) // MEM_HEAD_DIM
    q4 = jnp.concatenate([jnp.where(mhead == h, qm, jnp.zeros_like(qm)) for h in range(N_MEM_HEADS)], axis=0)
    s = lax.dot_general(km_ref[0], q4, (((1,), (1,)), ((), ())), preferred_element_type=F32)
    p = jnp.exp2(s - jnp.max(s, axis=0, keepdims=True)).astype(BF16)
    ones = vtm_ref[0, MEM_WIDTH:, :]
    mo = []
    for h in range(N_MEM_HEADS):
        vt = jnp.concatenate([vtm_ref[0, h * MEM_HEAD_DIM:(h + 1) * MEM_HEAD_DIM, :], ones], axis=0)
        ot = jnp.dot(vt, p[:, h * tm:(h + 1) * tm], preferred_element_type=F32)
        mo.append(ot[0:MEM_HEAD_DIM] * (1.0 / ot[MEM_HEAD_DIM:MEM_HEAD_DIM + 1]))
    mo = jnp.concatenate(mo, axis=0).T
    gm = (sm.astype(F32) * mo).astype(BF16)
    return ga, gb, gm


def _mix_project(rows, gated, x_ref, wo_ref, gf_ref, o_ref):
    ga, gb, gm = gated
    e1 = ATTN_WIDTH
    e2 = e1 + SGU_WIDTH
    y = jnp.dot(ga, wo_ref[0:e1, :], preferred_element_type=F32)
    y = y + jnp.dot(gb, wo_ref[e1:e2, :], preferred_element_type=F32)
    y = y + jnp.dot(gm, wo_ref[e2:, :], preferred_element_type=F32)
    o_ref[0, rows, :] = _rms(x_ref[0, rows, :] + y, gf_ref[...])


def _mix(a, act, k_mem, vt_mem, x, w_s, b_tile, w_out, g_final):
    b, s, d = x.shape
    tile = lambda w: pl.BlockSpec((1, MIX_TILE, w), lambda i, j: (i, j, 0))
    per_batch = lambda arr: pl.BlockSpec((1,) + arr.shape[1:], lambda i, j: (i, 0, 0))
    fixed = lambda arr: pl.BlockSpec(arr.shape, lambda i, j: (0, 0))
    return pl.pallas_call(
        _mix_kernel,
        grid=(b, s // MIX_TILE),
        in_specs=[tile(a.shape[2]), tile(act.shape[2]), per_batch(k_mem), per_batch(vt_mem), tile(d),
                  fixed(w_s), fixed(b_tile), fixed(w_out), fixed(g_final)],
        out_specs=tile(d),
        out_shape=jax.ShapeDtypeStruct((b, s, d), x.dtype),
        compiler_params=pltpu.CompilerParams(
            dimension_semantics=("arbitrary", "arbitrary"), vmem_limit_bytes=VMEM_LIMIT_BYTES),
        name="mix",
    )(a, act, k_mem, vt_mem, x, w_s, b_tile, w_out, g_final)


def kernel(x, mem, g_norm, w_in, w_sgu_spatial, b_sgu_spatial, g_sgu_v, g_mem, w_mem_kv, w_out, g_final):
    assert g_norm.shape[0] == 1, "the final norm is fused into the single layer's last kernel"
    b, s, d = x.shape
    k_mem, vt_mem = _memkv(mem, g_mem[0][None, :], w_mem_kv[0].astype(BF16))
    q, k, v, act = _inproj(x, g_norm[0][None, :], g_sgu_v[0][None, :], w_in[0].astype(BF16))
    a = _attn(q, k, v)
    w_s = jnp.concatenate(list(w_sgu_spatial[0].astype(BF16)), axis=1)
    b_tile = jnp.repeat(b_sgu_spatial[0].T, SGU_GROUP, axis=1)
    return _mix(a, act, k_mem, vt_mem, x, w_s, b_tile, w_out[0].astype(BF16), g_final[None, :])
```

```python
import functools
import math
from typing import NamedTuple

import jax
import jax.numpy as jnp
from jax import lax
from jax.experimental import pallas as pl
from jax.experimental.pallas import tpu as pltpu

F32 = jnp.float32
BF16 = jnp.bfloat16

EPS = 1e-6
NEG_INF = -1e30
LOG2_E = math.log2(math.e)

HEAD_DIM = 64
N_ATTN_HEADS = 8
ATTN_WIDTH = HEAD_DIM * N_ATTN_HEADS
DILATED_CONFIGS = ((128, 1), (512, 4), (2048, 16))
RADIUS = 64
SGU_WIDTH = 256
N_SGU_GROUPS = 4
SGU_GROUP = SGU_WIDTH // N_SGU_GROUPS
SGU_CHUNK = 128
MEM_WIDTH = 256
N_MEM_HEADS = 4
MEM_HEAD_DIM = MEM_WIDTH // N_MEM_HEADS
ACT_SPLITS = (ATTN_WIDTH, SGU_WIDTH, SGU_WIDTH, MEM_WIDTH, MEM_WIDTH)
ACT_WIDTH = sum(ACT_SPLITS)

LANES = 128
BF16_ROWS = 16
MXU_COLS = 256
Q_BLOCK = 128
KEY_WINDOW = 2 * Q_BLOCK
HEADS_PER_STEP = LANES // HEAD_DIM
SLABS_PER_STEP = 2
SCORE_LAG = 4
SCORE_SLOTS = 2 * SCORE_LAG
ROW_TILE = 1024
ORDER_BUFFERS = 4
MIX_TILE = 1024
VMEM_LIMIT_BYTES = 56 * 1024 * 1024


def _rms(x, g):
    return x * lax.rsqrt(jnp.mean(x * x, axis=-1, keepdims=True) + EPS) * g


def _silu(x):
    return x / (1.0 + jnp.exp(-x))


def _ones_rows(cols):
    first = lax.broadcasted_iota(jnp.int32, (BF16_ROWS, cols), 0) == 0
    return jnp.where(first, 1.0, 0.0).astype(BF16)


def _memkv_kernel(mem_ref, g_ref, w_ref, k_ref, vt_ref):
    h = _rms(mem_ref[0], g_ref[...]).astype(BF16)
    kv = jnp.dot(h, w_ref[...], preferred_element_type=F32)
    k_ref[0] = kv[:, 0:MEM_WIDTH].astype(BF16)
    vt_ref[0, 0:MEM_WIDTH, :] = kv[:, MEM_WIDTH:].T.astype(BF16)
    vt_ref[0, MEM_WIDTH:, :] = _ones_rows(kv.shape[0])


def _memkv(mem, g_mem, w_kv):
    b, m, d = mem.shape
    n = w_kv.shape[1]
    return pl.pallas_call(
        _memkv_kernel,
        grid=(b,),
        in_specs=[
            pl.BlockSpec((1, m, d), lambda i: (i, 0, 0)),
            pl.BlockSpec((1, d), lambda i: (0, 0)),
            pl.BlockSpec((d, n), lambda i: (0, 0)),
        ],
        out_specs=[
            pl.BlockSpec((1, m, MEM_WIDTH), lambda i: (i, 0, 0)),
            pl.BlockSpec((1, MEM_WIDTH + BF16_ROWS, m), lambda i: (i, 0, 0)),
        ],
        out_shape=[
            jax.ShapeDtypeStruct((b, m, MEM_WIDTH), BF16),
            jax.ShapeDtypeStruct((b, MEM_WIDTH + BF16_ROWS, m), BF16),
        ],
        compiler_params=pltpu.CompilerParams(
            dimension_semantics=("arbitrary",), vmem_limit_bytes=VMEM_LIMIT_BYTES),
        name="memkv",
    )(mem, g_mem, w_kv)


def _emit_orders(val, outs, lo, order_scrs):
    rows = val.shape[0]
    n_slabs = MXU_COLS // LANES

    def write(ref, r, slabs):
        ref[r, :, lo:lo + MXU_COLS] = jnp.concatenate(slabs, axis=1).astype(BF16)

    by_dil = sorted(range(len(DILATED_CONFIGS)), key=lambda c: DILATED_CONFIGS[c][1])
    prev = 1
    for level, c in enumerate(by_dil):
        dil = DILATED_CONFIGS[c][1]
        if dil == 1:
            slabs = [val[:, i * LANES:(i + 1) * LANES] for i in range(n_slabs)]
            write(outs[c], 0, slabs)
            for scr, x in zip(order_scrs, slabs):
                scr[0] = x
            continue
        step, cls_prev, cls = dil // prev, rows // prev, rows // dil
        for r in range(dil):
            slabs = [scr[level - 1, pl.ds((r % prev) * cls_prev + r // prev, cls, stride=step), :]
                     for scr in order_scrs]
            write(outs[c], r, slabs)
            if level + 1 < len(by_dil):
                for scr, x in zip(order_scrs, slabs):
                    scr[level, r * cls:(r + 1) * cls, :] = x
        prev = dil


def _inproj_kernel(x_ref, g_ref, gv_ref, w_ref, *refs):
    n_cfg = len(DILATED_CONFIGS)
    q_refs, k_refs, v_refs = refs[0:n_cfg], refs[n_cfg:2 * n_cfg], refs[2 * n_cfg:3 * n_cfg]
    act_ref = refs[3 * n_cfg]
    order_scrs = refs[3 * n_cfg + 1:]
    h = _rms(x_ref[0], g_ref[...]).astype(BF16)
    at = [0]

    def proj(width):
        lo = at[0]
        at[0] = lo + width
        return jnp.dot(h, w_ref[:, lo:lo + width], preferred_element_type=F32)

    n_slabs = MXU_COLS // LANES
    at_scr = 0
    for out_refs, scale in ((q_refs, HEAD_DIM ** -0.5 * LOG2_E), (k_refs, None), (v_refs, None)):
        for piece in range(ATTN_WIDTH // MXU_COLS):
            val = proj(MXU_COLS)
            if scale is not None:
                val = val * scale
            _emit_orders(val, out_refs, piece * MXU_COLS, order_scrs[at_scr:at_scr + n_slabs])
            at_scr = (at_scr + n_slabs) % len(order_scrs)
    za = proj(ATTN_WIDTH)
    ub = proj(SGU_WIDTH)
    vb = proj(SGU_WIDTH)
    zb = proj(SGU_WIDTH)
    qm = proj(MEM_WIDTH)
    zm = proj(MEM_WIDTH)
    acts = (_silu(za), _silu(zb) * jax.nn.gelu(ub), _rms(jax.nn.gelu(vb), gv_ref[...]),
            qm * (MEM_HEAD_DIM ** -0.5 * LOG2_E), _silu(zm))
    lo = 0
    for width, val in zip(ACT_SPLITS, acts):
        act_ref[0, :, lo:lo + width] = val.astype(BF16)
        lo += width


def _inproj(x, g_norm, g_v, w_in):
    b, s, d = x.shape
    cols = w_in.shape[1]
    assert cols == 4 * ATTN_WIDTH + 3 * SGU_WIDTH + 2 * MEM_WIDTH
    tile = lambda w: pl.BlockSpec((1, ROW_TILE, w), lambda i, j: (i, j, 0))
    fixed = lambda i, j: (0, 0)
    dils = [dil for _, dil in DILATED_CONFIGS]
    assert all(ROW_TILE % (dil * BF16_ROWS) == 0 for dil in dils)
    ordered = lambda w: [pl.BlockSpec((dil, ROW_TILE // dil, w), lambda i, j: (i, j, 0)) for dil in dils]
    shaped = lambda w: [jax.ShapeDtypeStruct((b * dil, s // dil, w), BF16) for dil in dils]
    widths = (ATTN_WIDTH, ATTN_WIDTH, ATTN_WIDTH)
    outs = pl.pallas_call(
        _inproj_kernel,
        grid=(b, s // ROW_TILE),
        in_specs=[
            tile(d),
            pl.BlockSpec((1, d), fixed),
            pl.BlockSpec((1, SGU_WIDTH), fixed),
            pl.BlockSpec((d, cols), fixed, pipeline_mode=pl.Buffered(1)),
        ],
        out_specs=[spec for w in widths for spec in ordered(w)] + [tile(ACT_WIDTH)],
        out_shape=[shape for w in widths for shape in shaped(w)] + [jax.ShapeDtypeStruct((b, s, ACT_WIDTH), BF16)],
        scratch_shapes=[pltpu.VMEM((len(dils) - 1, ROW_TILE, LANES), F32)] * ORDER_BUFFERS,
        compiler_params=pltpu.CompilerParams(
            dimension_semantics=("arbitrary", "arbitrary"), vmem_limit_bytes=VMEM_LIMIT_BYTES),
        name="inproj",
    )(x, g_norm, g_v, w_in)
    n = len(dils)
    q, k, v = ([o.reshape(b, s, -1) for o in outs[i * n:(i + 1) * n]] for i in range(3))
    return q, k, v, outs[3 * n]


SHIFT, EDGE, WHOLE = "shift", "edge", "whole"


def _bias_tile(kind, dilation, first_head):
    shape = (KEY_WINDOW, HEADS_PER_STEP * Q_BLOCK)
    key = lax.broadcasted_iota(jnp.int32, shape, 0)
    col = lax.broadcasted_iota(jnp.int32, shape, 1)
    second = col >= Q_BLOCK
    qi = jnp.where(second, col - Q_BLOCK, col)
    if kind == SHIFT:
        rel = jnp.abs(qi + RADIUS - key)
        valid = rel <= RADIUS
    elif kind == WHOLE:
        rel = jnp.abs(qi - key)
        valid = rel <= RADIUS
    else:
        upper = key >= Q_BLOCK
        rel = jnp.abs(qi - jnp.where(upper, key - Q_BLOCK, key))
        valid = (rel <= RADIUS) & (upper == (qi >= RADIUS))
    head = (first_head + jnp.where(second, 1, 0)).astype(F32)
    slope = jnp.exp2(-8.0 * (head + 1.0) / N_ATTN_HEADS)
    dist = (rel * dilation).astype(F32)
    return jnp.where(valid, -slope * dist * LOG2_E, NEG_INF)


class _Block(NamedTuple):
    cfg: int
    tile: int
    q_rows: tuple
    k_rows: tuple
    out_rows: tuple
    out_stride: int


def _cfg_order():
    return sorted(range(len(DILATED_CONFIGS)), key=lambda c: -DILATED_CONFIGS[c][1])


def _attn_plan(seq):
    blocks, tiles = [], []
    order = _cfg_order()
    for n, c in enumerate(order):
        dil = DILATED_CONFIGS[c][1]
        nxt = DILATED_CONFIGS[order[n + 1]][1] if n + 1 < len(order) else 1
        step = dil // nxt
        cls = seq // dil
        if cls == Q_BLOCK:
            tiles.append((WHOLE, dil))
            whole = len(tiles) - 1
        else:
            tiles.append((SHIFT, dil))
            tiles.append((EDGE, dil))
            shift, edge = len(tiles) - 2, len(tiles) - 1
        for r in range(dil):
            base = r * cls
            sink = (r % nxt) * (seq // nxt) + r // nxt
            if cls == Q_BLOCK:
                blocks.append(_Block(c, whole, ((base, Q_BLOCK),), ((base, Q_BLOCK),), ((sink, Q_BLOCK),), step))
                continue
            last = cls - RADIUS
            blocks.append(_Block(c, edge, ((base, RADIUS), (base + last, RADIUS)),
                                 ((base, Q_BLOCK), (base + cls - Q_BLOCK, Q_BLOCK)),
                                 ((sink, RADIUS), (sink + step * last, RADIUS)), step))
            for j in range(cls // Q_BLOCK - 1):
                u = RADIUS + j * Q_BLOCK
                blocks.append(_Block(c, shift, ((base + u, Q_BLOCK),), ((base + j * Q_BLOCK, KEY_WINDOW),),
                                     ((sink + step * u, Q_BLOCK),), step))
    return blocks, tiles


def _rows(ref, lead, ranges, lanes=slice(None)):
    parts = [ref[lead + (pl.ds(s, n), lanes)] for s, n in ranges]
    return parts[0] if len(parts) == 1 else jnp.concatenate(parts, axis=0)


def _score_block(blk, lanes, q_ref, k_ref, bias, s_scr):
    q = _rows(q_ref, (0,), blk.q_rows, lanes)
    first = lax.broadcasted_iota(jnp.int32, (1, LANES), 1) < HEAD_DIM
    zeros = jnp.zeros_like(q)
    q2 = jnp.concatenate([jnp.where(first, q, zeros), jnp.where(first, zeros, q)], axis=0)
    k_win = _rows(k_ref, (0,), blk.k_rows, lanes)
    window = k_win.shape[0]
    s = lax.dot_general(k_win, q2, (((1,), (1,)), ((), ())), preferred_element_type=F32)
    s_scr[0, 0:window, :] = s + bias[0:window, :]


def _value_block(blk, lanes, zero, vt_scr, s_scr, prev, sink, a_ref):
    window = sum(n for _, n in blk.k_rows)
    ms, ps = [], []
    for h in range(HEADS_PER_STEP):
        cols = slice(h * Q_BLOCK, (h + 1) * Q_BLOCK)
        s = s_scr[zero, 0:window, cols]
        m = jnp.max(s, axis=0, keepdims=True)
        ms.append(m)
        ps.append(jnp.exp2(s - m).astype(BF16))
    vt = jnp.concatenate([vt_scr[zero, :, s0:s0 + n] for s0, n in blk.k_rows], axis=1)
    ot = jnp.dot(vt, jnp.concatenate(ps, axis=1), preferred_element_type=F32)
    outs, lses = [], []
    for h in range(HEADS_PER_STEP):
        cols = slice(h * Q_BLOCK, (h + 1) * Q_BLOCK)
        l = ot[LANES:LANES + 1, cols]
        outs.append(ot[h * HEAD_DIM:(h + 1) * HEAD_DIM, cols] * (1.0 / l))
        lses.append(ms[h] + jnp.log2(l))
    o = jnp.concatenate(outs, axis=0).T
    lse = jnp.concatenate([jnp.broadcast_to(x, (HEAD_DIM, Q_BLOCK)) for x in lses], axis=0).T
    if prev is not None:
        o_prev, lse_prev = (_rows(ref, (), blk.q_rows) for ref in prev)
        top = jnp.maximum(lse, lse_prev)
        e, e_prev = jnp.exp2(lse - top), jnp.exp2(lse_prev - top)
        den = e + e_prev
        o = (e * o + e_prev * o_prev) * (1.0 / den)
        lse = top + jnp.log2(den)
    at = 0
    for start, n in blk.out_rows:
        dst = pl.ds(start, n) if blk.out_stride == 1 else pl.ds(start, n, stride=blk.out_stride)
        if sink is None:
            a_ref[0, dst, lanes] = o[at:at + n].astype(a_ref.dtype)
        else:
            sink[0][dst, :] = o[at:at + n]
            sink[1][dst, :] = lse[at:at + n]
        at += n


def _attn_kernel(*refs, seq):
    n_cfg = len(DILATED_CONFIGS)
    q_refs, k_refs, v_refs = refs[0:n_cfg], refs[n_cfg:2 * n_cfg], refs[2 * n_cfg:3 * n_cfg]
    a_ref, bias_scr = refs[3 * n_cfg:3 * n_cfg + 2]
    scrs = refs[3 * n_cfg + 2:]
    n_vt, n_hand = SLABS_PER_STEP * n_cfg, SLABS_PER_STEP * (n_cfg - 1)
    vt_scrs = [scrs[p * n_cfg:(p + 1) * n_cfg] for p in range(SLABS_PER_STEP)]
    hand = [[scrs[n_vt + 2 * (p * (n_cfg - 1) + i):n_vt + 2 * (p * (n_cfg - 1) + i) + 2] for i in range(n_cfg - 1)]
            for p in range(SLABS_PER_STEP)]
    s_scrs = scrs[n_vt + 2 * n_hand:]
    blocks, tiles = _attn_plan(seq)
    order = _cfg_order()
    slabs = [slice(p * LANES, (p + 1) * LANES) for p in range(SLABS_PER_STEP)]

    @pl.when(pl.program_id(1) == 0)
    def _():
        for p in range(SLABS_PER_STEP):
            first_head = (pl.program_id(0) * SLABS_PER_STEP + p) * HEADS_PER_STEP
            for t, (kind, dil) in enumerate(tiles):
                bias_scr[p * len(tiles) + t] = _bias_tile(kind, dil, first_head)

    for p in range(SLABS_PER_STEP):
        for c in order:
            vt_scrs[p][c][0, LANES:LANES + BF16_ROWS, :] = _ones_rows(seq)
            for i in range(seq // LANES):
                chunk = slice(i * LANES, (i + 1) * LANES)
                vt_scrs[p][c][0, 0:LANES, chunk] = v_refs[c][0, chunk, slabs[p]].T

    zero = jnp.minimum(pl.program_id(1), 0)
    work = [(p, blk) for p in range(SLABS_PER_STEP) for blk in blocks]
    for t in range(len(work) + SCORE_LAG):
        if t < len(work):
            p, blk = work[t]
            _score_block(blk, slabs[p], q_refs[blk.cfg], k_refs[blk.cfg], bias_scr.at[p * len(tiles) + blk.tile],
                         s_scrs[t % SCORE_SLOTS])
        if t >= SCORE_LAG:
            u = t - SCORE_LAG
            p, blk = work[u]
            n = order.index(blk.cfg)
            _value_block(blk, slabs[p], zero, vt_scrs[p][blk.cfg], s_scrs[u % SCORE_SLOTS],
                         hand[p][n - 1] if n > 0 else None, hand[p][n] if n + 1 < n_cfg else None, a_ref)


def _attn(q, k, v):
    b, s, w = k[0].shape
    n_cfg = len(DILATED_CONFIGS)
    assert all(s % (dil * Q_BLOCK) == 0 for _, dil in DILATED_CONFIGS)
    n_tiles = len(_attn_plan(s)[1])
    blk = pl.BlockSpec((1, s, SLABS_PER_STEP * LANES), lambda j, i: (i, 0, j))
    return pl.pallas_call(
        functools.partial(_attn_kernel, seq=s),
        grid=(w // (SLABS_PER_STEP * LANES), b),
        in_specs=[blk] * (3 * n_cfg),
        out_specs=blk,
        out_shape=jax.ShapeDtypeStruct((b, s, w), BF16),
        scratch_shapes=[
            pltpu.VMEM((SLABS_PER_STEP * n_tiles, KEY_WINDOW, HEADS_PER_STEP * Q_BLOCK), F32),
        ] + [pltpu.VMEM((1, LANES + BF16_ROWS, s), BF16)] * (SLABS_PER_STEP * n_cfg)
        + [pltpu.VMEM((s, LANES), F32)] * (2 * SLABS_PER_STEP * (n_cfg - 1))
        + [pltpu.VMEM((1, KEY_WINDOW, HEADS_PER_STEP * Q_BLOCK), F32)] * SCORE_SLOTS,
        compiler_params=pltpu.CompilerParams(
            dimension_semantics=("arbitrary", "arbitrary"), vmem_limit_bytes=VMEM_LIMIT_BYTES),
        name="attn",
    )(*q, *k, *v)


def _mix_kernel(a_ref, act_ref, km_ref, vtm_ref, x_ref, ws_ref, bs_ref, wo_ref, gf_ref, o_ref):
    rows = pl.ds(0, x_ref.shape[1])
    gated = _mix_branches(rows, a_ref, act_ref, km_ref, vtm_ref, ws_ref, bs_ref)
    _mix_project(rows, gated, x_ref, wo_ref, gf_ref, o_ref)


def _mix_branches(rows, a_ref, act_ref, km_ref, vtm_ref, ws_ref, bs_ref):
    tm = rows.size
    lo = 0
    cols = []
    for width in ACT_SPLITS:
        cols.append(slice(lo, lo + width))
        lo += width
    sa, gu, vn, qm, sm = (act_ref[0, rows, c] for c in cols)

    ga = sa * a_ref[0, rows, :]

    group = lax.broadcasted_iota(jnp.int32, (SGU_CHUNK, SGU_WIDTH), 1) // SGU_GROUP
    mixed = []
    for c in range(tm // SGU_CHUNK):
        vc = vn[c * SGU_CHUNK:(c + 1) * SGU_CHUNK]
        stacked = jnp.concatenate([jnp.where(group == g, vc, jnp.zeros_like(vc)) for g in range(N_SGU_GROUPS)],
                                  axis=0)
        mixed.append(jnp.dot(ws_ref[...], stacked, preferred_element_type=F32) + bs_ref[...])
    gb = (gu.astype(F32) * jnp.concatenate(mixed, axis=0)).astype(BF16)

    mhead = lax.broadcasted_iota(jnp.int32, (tm, MEM_WIDTH), 1) // MEM_HEAD_DIM
    q4 = jnp.concatenate([jnp.where(mhead == h, qm, jnp.zeros_like(qm)) for h in range(N_MEM_HEADS)], axis=0)
    s = lax.dot_general(km_ref[0], q4, (((1,), (1,)), ((), ())), preferred_element_type=F32)
    p = jnp.exp2(s - jnp.max(s, axis=0, keepdims=True)).astype(BF16)
    ones = vtm_ref[0, MEM_WIDTH:, :]
    mo = []
    for h in range(N_MEM_HEADS):
        vt = jnp.concatenate([vtm_ref[0, h * MEM_HEAD_DIM:(h + 1) * MEM_HEAD_DIM, :], ones], axis=0)
        ot = jnp.dot(vt, p[:, h * tm:(h + 1) * tm], preferred_element_type=F32)
        mo.append(ot[0:MEM_HEAD_DIM] * (1.0 / ot[MEM_HEAD_DIM:MEM_HEAD_DIM + 1]))
    mo = jnp.concatenate(mo, axis=0).T
    gm = (sm.astype(F32) * mo).astype(BF16)
    return ga, gb, gm


def _mix_project(rows, gated, x_ref, wo_ref, gf_ref, o_ref):
    ga, gb, gm = gated
    e1 = ATTN_WIDTH
    e2 = e1 + SGU_WIDTH
    y = jnp.dot(ga, wo_ref[0:e1, :], preferred_element_type=F32)
    y = y + jnp.dot(gb, wo_ref[e1:e2, :], preferred_element_type=F32)
    y = y + jnp.dot(gm, wo_ref[e2:, :], preferred_element_type=F32)
    o_ref[0, rows, :] = _rms(x_ref[0, rows, :] + y, gf_ref[...])


def _mix(a, act, k_mem, vt_mem, x, w_s, b_tile, w_out, g_final):
    b, s, d = x.shape
    tile = lambda w: pl.BlockSpec((1, MIX_TILE, w), lambda i, j: (i, j, 0))
    per_batch = lambda arr: pl.BlockSpec((1,) + arr.shape[1:], lambda i, j: (i, 0, 0))
    fixed = lambda arr: pl.BlockSpec(arr.shape, lambda i, j: (0, 0))
    return pl.pallas_call(
        _mix_kernel,
        grid=(b, s // MIX_TILE),
        in_specs=[tile(a.shape[2]), tile(act.shape[2]), per_batch(k_mem), per_batch(vt_mem), tile(d),
                  fixed(w_s), fixed(b_tile), fixed(w_out), fixed(g_final)],
        out_specs=tile(d),
        out_shape=jax.ShapeDtypeStruct((b, s, d), x.dtype),
        compiler_params=pltpu.CompilerParams(
            dimension_semantics=("arbitrary", "arbitrary"), vmem_limit_bytes=VMEM_LIMIT_BYTES),
        name="mix",
    )(a, act, k_mem, vt_mem, x, w_s, b_tile, w_out, g_final)


def kernel(x, mem, g_norm, w_in, w_sgu_spatial, b_sgu_spatial, g_sgu_v, g_mem, w_mem_kv, w_out, g_final):
    assert g_norm.shape[0] == 1, "the final norm is fused into the single layer's last kernel"
    b, s, d = x.shape
    k_mem, vt_mem = _memkv(mem, g_mem[0][None, :], w_mem_kv[0].astype(BF16))
    q, k, v, act = _inproj(x, g_norm[0][None, :], g_sgu_v[0][None, :], w_in[0].astype(BF16))
    a = _attn(q, k, v)
    w_s = jnp.concatenate(list(w_sgu_spatial[0].astype(BF16)), axis=1)
    b_tile = jnp.repeat(b_sgu_spatial[0].T, SGU_GROUP, axis=1)
    return _mix(a, act, k_mem, vt_mem, x, w_s, b_tile, w_out[0].astype(BF16), g_final[None, :])
```

```python
import functools
import math
from typing import NamedTuple

import jax
import jax.numpy as jnp
from jax import lax
from jax.experimental import pallas as pl
from jax.experimental.pallas import tpu as pltpu

F32 = jnp.float32
BF16 = jnp.bfloat16

EPS = 1e-6
NEG_INF = -1e30
LOG2_E = math.log2(math.e)

HEAD_DIM = 64
N_ATTN_HEADS = 8
ATTN_WIDTH = HEAD_DIM * N_ATTN_HEADS
DILATED_CONFIGS = ((128, 1), (512, 4), (2048, 16))
RADIUS = 64
SGU_WIDTH = 256
N_SGU_GROUPS = 4
SGU_GROUP = SGU_WIDTH // N_SGU_GROUPS
SGU_CHUNK = 128
MEM_WIDTH = 256
N_MEM_HEADS = 4
MEM_HEAD_DIM = MEM_WIDTH // N_MEM_HEADS
ACT_SPLITS = (ATTN_WIDTH, SGU_WIDTH, SGU_WIDTH, MEM_WIDTH, MEM_WIDTH)
ACT_WIDTH = sum(ACT_SPLITS)

LANES = 128
BF16_ROWS = 16
MXU_COLS = 256
Q_BLOCK = 128
KEY_WINDOW = 2 * Q_BLOCK
HEADS_PER_STEP = LANES // HEAD_DIM
SLABS_PER_STEP = 1
SCORE_LAG = 4
SCORE_SLOTS = 2 * SCORE_LAG
ROW_TILE = 1024
ORDER_BUFFERS = 4
MIX_TILE = 1024
VMEM_LIMIT_BYTES = 56 * 1024 * 1024


def _rms(x, g):
    return x * lax.rsqrt(jnp.mean(x * x, axis=-1, keepdims=True) + EPS) * g


def _silu(x):
    return x / (1.0 + jnp.exp(-x))


def _ones_rows(cols):
    first = lax.broadcasted_iota(jnp.int32, (BF16_ROWS, cols), 0) == 0
    return jnp.where(first, 1.0, 0.0).astype(BF16)


def _memkv_kernel(mem_ref, g_ref, w_ref, k_ref, vt_ref):
    h = _rms(mem_ref[0], g_ref[...]).astype(BF16)
    kv = jnp.dot(h, w_ref[...], preferred_element_type=F32)
    k_ref[0] = kv[:, 0:MEM_WIDTH].astype(BF16)
    vt_ref[0, 0:MEM_WIDTH, :] = kv[:, MEM_WIDTH:].T.astype(BF16)
    vt_ref[0, MEM_WIDTH:, :] = _ones_rows(kv.shape[0])


def _memkv(mem, g_mem, w_kv):
    b, m, d = mem.shape
    n = w_kv.shape[1]
    return pl.pallas_call(
        _memkv_kernel,
        grid=(b,),
        in_specs=[
            pl.BlockSpec((1, m, d), lambda i: (i, 0, 0)),
            pl.BlockSpec((1, d), lambda i: (0, 0)),
            pl.BlockSpec((d, n), lambda i: (0, 0)),
        ],
        out_specs=[
            pl.BlockSpec((1, m, MEM_WIDTH), lambda i: (i, 0, 0)),
            pl.BlockSpec((1, MEM_WIDTH + BF16_ROWS, m), lambda i: (i, 0, 0)),
        ],
        out_shape=[
            jax.ShapeDtypeStruct((b, m, MEM_WIDTH), BF16),
            jax.ShapeDtypeStruct((b, MEM_WIDTH + BF16_ROWS, m), BF16),
        ],
        compiler_params=pltpu.CompilerParams(
            dimension_semantics=("arbitrary",), vmem_limit_bytes=VMEM_LIMIT_BYTES),
        name="memkv",
    )(mem, g_mem, w_kv)


def _emit_orders(val, outs, lo, order_scrs):
    rows = val.shape[0]
    n_slabs = MXU_COLS // LANES

    def write(ref, r, slabs):
        ref[r, :, lo:lo + MXU_COLS] = jnp.concatenate(slabs, axis=1).astype(BF16)

    by_dil = sorted(range(len(DILATED_CONFIGS)), key=lambda c: DILATED_CONFIGS[c][1])
    prev = 1
    for level, c in enumerate(by_dil):
        dil = DILATED_CONFIGS[c][1]
        if dil == 1:
            slabs = [val[:, i * LANES:(i + 1) * LANES] for i in range(n_slabs)]
            write(outs[c], 0, slabs)
            for scr, x in zip(order_scrs, slabs):
                scr[0] = x
            continue
        step, cls_prev, cls = dil // prev, rows // prev, rows // dil
        for r in range(dil):
            slabs = [scr[level - 1, pl.ds((r % prev) * cls_prev + r // prev, cls, stride=step), :]
                     for scr in order_scrs]
            write(outs[c], r, slabs)
            if level + 1 < len(by_dil):
                for scr, x in zip(order_scrs, slabs):
                    scr[level, r * cls:(r + 1) * cls, :] = x
        prev = dil


def _inproj_kernel(x_ref, g_ref, gv_ref, w_ref, *refs):
    n_cfg = len(DILATED_CONFIGS)
    q_refs, k_refs, v_refs = refs[0:n_cfg], refs[n_cfg:2 * n_cfg], refs[2 * n_cfg:3 * n_cfg]
    act_ref = refs[3 * n_cfg]
    order_scrs = refs[3 * n_cfg + 1:]
    h = _rms(x_ref[0], g_ref[...]).astype(BF16)
    at = [0]

    def proj(width):
        lo = at[0]
        at[0] = lo + width
        return jnp.dot(h, w_ref[:, lo:lo + width], preferred_element_type=F32)

    n_slabs = MXU_COLS // LANES
    at_scr = 0
    for out_refs, scale in ((q_refs, HEAD_DIM ** -0.5 * LOG2_E), (k_refs, None), (v_refs, None)):
        for piece in range(ATTN_WIDTH // MXU_COLS):
            val = proj(MXU_COLS)
            if scale is not None:
                val = val * scale
            _emit_orders(val, out_refs, piece * MXU_COLS, order_scrs[at_scr:at_scr + n_slabs])
            at_scr = (at_scr + n_slabs) % len(order_scrs)
    za = proj(ATTN_WIDTH)
    ub = proj(SGU_WIDTH)
    vb = proj(SGU_WIDTH)
    zb = proj(SGU_WIDTH)
    qm = proj(MEM_WIDTH)
    zm = proj(MEM_WIDTH)
    acts = (_silu(za), _silu(zb) * jax.nn.gelu(ub), _rms(jax.nn.gelu(vb), gv_ref[...]),
            qm * (MEM_HEAD_DIM ** -0.5 * LOG2_E), _silu(zm))
    lo = 0
    for width, val in zip(ACT_SPLITS, acts):
        act_ref[0, :, lo:lo + width] = val.astype(BF16)
        lo += width


def _inproj(x, g_norm, g_v, w_in):
    b, s, d = x.shape
    cols = w_in.shape[1]
    assert cols == 4 * ATTN_WIDTH + 3 * SGU_WIDTH + 2 * MEM_WIDTH
    tile = lambda w: pl.BlockSpec((1, ROW_TILE, w), lambda i, j: (i, j, 0))
    fixed = lambda i, j: (0, 0)
    dils = [dil for _, dil in DILATED_CONFIGS]
    assert all(ROW_TILE % (dil * BF16_ROWS) == 0 for dil in dils)
    ordered = lambda w: [pl.BlockSpec((dil, ROW_TILE // dil, w), lambda i, j: (i, j, 0)) for dil in dils]
    shaped = lambda w: [jax.ShapeDtypeStruct((b * dil, s // dil, w), BF16) for dil in dils]
    widths = (ATTN_WIDTH, ATTN_WIDTH, ATTN_WIDTH)
    outs = pl.pallas_call(
        _inproj_kernel,
        grid=(b, s // ROW_TILE),
        in_specs=[
            tile(d),
            pl.BlockSpec((1, d), fixed),
            pl.BlockSpec((1, SGU_WIDTH), fixed),
            pl.BlockSpec((d, cols), fixed, pipeline_mode=pl.Buffered(1)),
        ],
        out_specs=[spec for w in widths for spec in ordered(w)] + [tile(ACT_WIDTH)],
        out_shape=[shape for w in widths for shape in shaped(w)] + [jax.ShapeDtypeStruct((b, s, ACT_WIDTH), BF16)],
        scratch_shapes=[pltpu.VMEM((len(dils) - 1, ROW_TILE, LANES), F32)] * ORDER_BUFFERS,
        compiler_params=pltpu.CompilerParams(
            dimension_semantics=("arbitrary", "arbitrary"), vmem_limit_bytes=VMEM_LIMIT_BYTES),
        name="inproj",
    )(x, g_norm, g_v, w_in)
    n = len(dils)
    q, k, v = ([o.reshape(b, s, -1) for o in outs[i * n:(i + 1) * n]] for i in range(3))
    return q, k, v, outs[3 * n]


SHIFT, EDGE, WHOLE = "shift", "edge", "whole"


def _bias_tile(kind, dilation, first_head):
    shape = (KEY_WINDOW, HEADS_PER_STEP * Q_BLOCK)
    key = lax.broadcasted_iota(jnp.int32, shape, 0)
    col = lax.broadcasted_iota(jnp.int32, shape, 1)
    second = col >= Q_BLOCK
    qi = jnp.where(second, col - Q_BLOCK, col)
    if kind == SHIFT:
        rel = jnp.abs(qi + RADIUS - key)
        valid = rel <= RADIUS
    elif kind == WHOLE:
        rel = jnp.abs(qi - key)
        valid = rel <= RADIUS
    else:
        upper = key >= Q_BLOCK
        rel = jnp.abs(qi - jnp.where(upper, key - Q_BLOCK, key))
        valid = (rel <= RADIUS) & (upper == (qi >= RADIUS))
    head = (first_head + jnp.where(second, 1, 0)).astype(F32)
    slope = jnp.exp2(-8.0 * (head + 1.0) / N_ATTN_HEADS)
    dist = (rel * dilation).astype(F32)
    return jnp.where(valid, -slope * dist * LOG2_E, NEG_INF)


class _Block(NamedTuple):
    cfg: int
    tile: int
    q_rows: tuple
    k_rows: tuple
    out_rows: tuple
    out_stride: int


def _cfg_order():
    return sorted(range(len(DILATED_CONFIGS)), key=lambda c: -DILATED_CONFIGS[c][1])


def _attn_plan(seq):
    blocks, tiles = [], []
    order = _cfg_order()
    for n, c in enumerate(order):
        dil = DILATED_CONFIGS[c][1]
        nxt = DILATED_CONFIGS[order[n + 1]][1] if n + 1 < len(order) else 1
        step = dil // nxt
        cls = seq // dil
        if cls == Q_BLOCK:
            tiles.append((WHOLE, dil))
            whole = len(tiles) - 1
        else:
            tiles.append((SHIFT, dil))
            tiles.append((EDGE, dil))
            shift, edge = len(tiles) - 2, len(tiles) - 1
        for r in range(dil):
            base = r * cls
            sink = (r % nxt) * (seq // nxt) + r // nxt
            if cls == Q_BLOCK:
                blocks.append(_Block(c, whole, ((base, Q_BLOCK),), ((base, Q_BLOCK),), ((sink, Q_BLOCK),), step))
                continue
            last = cls - RADIUS
            blocks.append(_Block(c, edge, ((base, RADIUS), (base + last, RADIUS)),
                                 ((base, Q_BLOCK), (base + cls - Q_BLOCK, Q_BLOCK)),
                                 ((sink, RADIUS), (sink + step * last, RADIUS)), step))
            for j in range(cls // Q_BLOCK - 1):
                u = RADIUS + j * Q_BLOCK
                blocks.append(_Block(c, shift, ((base + u, Q_BLOCK),), ((base + j * Q_BLOCK, KEY_WINDOW),),
                                     ((sink + step * u, Q_BLOCK),), step))
    return blocks, tiles


def _rows(ref, lead, ranges, lanes=slice(None)):
    parts = [ref[lead + (pl.ds(s, n), lanes)] for s, n in ranges]
    return parts[0] if len(parts) == 1 else jnp.concatenate(parts, axis=0)


def _score_block(blk, lanes, q_ref, k_ref, bias, s_scr):
    q = _rows(q_ref, (0,), blk.q_rows, lanes)
    first = lax.broadcasted_iota(jnp.int32, (1, LANES), 1) < HEAD_DIM
    zeros = jnp.zeros_like(q)
    q2 = jnp.concatenate([jnp.where(first, q, zeros), jnp.where(first, zeros, q)], axis=0)
    k_win = _rows(k_ref, (0,), blk.k_rows, lanes)
    window = k_win.shape[0]
    s = lax.dot_general(k_win, q2, (((1,), (1,)), ((), ())), preferred_element_type=F32)
    s_scr[0, 0:window, :] = (s + bias[0:window, :]).astype(s_scr.dtype)


def _value_block(blk, lanes, zero, vt_scr, s_scr, prev, sink, a_ref):
    window = sum(n for _, n in blk.k_rows)
    ms, ps = [], []
    for h in range(HEADS_PER_STEP):
        cols = slice(h * Q_BLOCK, (h + 1) * Q_BLOCK)
        s = s_scr[zero, 0:window, cols]
        m = jnp.max(s, axis=0, keepdims=True)
        ms.append(m)
        ps.append(jnp.exp2(s - m).astype(BF16))
        ms[-1] = m.astype(F32)
    vt = jnp.concatenate([vt_scr[zero, :, s0:s0 + n] for s0, n in blk.k_rows], axis=1)
    ot = jnp.dot(vt, jnp.concatenate(ps, axis=1), preferred_element_type=F32)
    outs, lses = [], []
    for h in range(HEADS_PER_STEP):
        cols = slice(h * Q_BLOCK, (h + 1) * Q_BLOCK)
        l = ot[LANES:LANES + 1, cols]
        outs.append(ot[h * HEAD_DIM:(h + 1) * HEAD_DIM, cols] * (1.0 / l))
        lses.append(ms[h] + jnp.log2(l))
    o = jnp.concatenate(outs, axis=0).T
    lse = jnp.concatenate([jnp.broadcast_to(x, (HEAD_DIM, Q_BLOCK)) for x in lses], axis=0).T
    if prev is not None:
        o_prev, lse_prev = (_rows(ref, (), blk.q_rows) for ref in prev)
        top = jnp.maximum(lse, lse_prev)
        e, e_prev = jnp.exp2(lse - top), jnp.exp2(lse_prev - top)
        den = e + e_prev
        o = (e * o + e_prev * o_prev) * (1.0 / den)
        lse = top + jnp.log2(den)
    at = 0
    for start, n in blk.out_rows:
        dst = pl.ds(start, n) if blk.out_stride == 1 else pl.ds(start, n, stride=blk.out_stride)
        if sink is None:
            a_ref[0, dst, lanes] = o[at:at + n].astype(a_ref.dtype)
        else:
            sink[0][dst, :] = o[at:at + n]
            sink[1][dst, :] = lse[at:at + n]
        at += n


def _attn_kernel(*refs, seq):
    n_cfg = len(DILATED_CONFIGS)
    q_refs, k_refs, v_refs = refs[0:n_cfg], refs[n_cfg:2 * n_cfg], refs[2 * n_cfg:3 * n_cfg]
    a_ref, bias_scr = refs[3 * n_cfg:3 * n_cfg + 2]
    scrs = refs[3 * n_cfg + 2:]
    n_vt, n_hand = SLABS_PER_STEP * n_cfg, SLABS_PER_STEP * (n_cfg - 1)
    vt_scrs = [scrs[p * n_cfg:(p + 1) * n_cfg] for p in range(SLABS_PER_STEP)]
    hand = [[scrs[n_vt + 2 * (p * (n_cfg - 1) + i):n_vt + 2 * (p * (n_cfg - 1) + i) + 2] for i in range(n_cfg - 1)]
            for p in range(SLABS_PER_STEP)]
    s_scrs = scrs[n_vt + 2 * n_hand:]
    blocks, tiles = _attn_plan(seq)
    order = _cfg_order()
    slabs = [slice(p * LANES, (p + 1) * LANES) for p in range(SLABS_PER_STEP)]

    @pl.when(pl.program_id(1) == 0)
    def _():
        for p in range(SLABS_PER_STEP):
            first_head = (pl.program_id(0) * SLABS_PER_STEP + p) * HEADS_PER_STEP
            for t, (kind, dil) in enumerate(tiles):
                bias_scr[p * len(tiles) + t] = _bias_tile(kind, dil, first_head)

    for p in range(SLABS_PER_STEP):
        for c in order:
            vt_scrs[p][c][0, LANES:LANES + BF16_ROWS, :] = _ones_rows(seq)
            for i in range(seq // LANES):
                chunk = slice(i * LANES, (i + 1) * LANES)
                vt_scrs[p][c][0, 0:LANES, chunk] = v_refs[c][0, chunk, slabs[p]].T

    zero = jnp.minimum(pl.program_id(1), 0)
    work = [(p, blk) for p in range(SLABS_PER_STEP) for blk in blocks]
    for t in range(len(work) + SCORE_LAG):
        if t < len(work):
            p, blk = work[t]
            _score_block(blk, slabs[p], q_refs[blk.cfg], k_refs[blk.cfg], bias_scr.at[p * len(tiles) + blk.tile],
                         s_scrs[t % SCORE_SLOTS])
        if t >= SCORE_LAG:
            u = t - SCORE_LAG
            p, blk = work[u]
            n = order.index(blk.cfg)
            _value_block(blk, slabs[p], zero, vt_scrs[p][blk.cfg], s_scrs[u % SCORE_SLOTS],
                         hand[p][n - 1] if n > 0 else None, hand[p][n] if n + 1 < n_cfg else None, a_ref)


def _attn(q, k, v):
    b, s, w = k[0].shape
    n_cfg = len(DILATED_CONFIGS)
    assert all(s % (dil * Q_BLOCK) == 0 for _, dil in DILATED_CONFIGS)
    n_tiles = len(_attn_plan(s)[1])
    blk = pl.BlockSpec((1, s, SLABS_PER_STEP * LANES), lambda j, i: (i, 0, j))
    return pl.pallas_call(
        functools.partial(_attn_kernel, seq=s),
        grid=(w // (SLABS_PER_STEP * LANES), b),
        in_specs=[blk] * (3 * n_cfg),
        out_specs=blk,
        out_shape=jax.ShapeDtypeStruct((b, s, w), BF16),
        scratch_shapes=[
            pltpu.VMEM((SLABS_PER_STEP * n_tiles, KEY_WINDOW, HEADS_PER_STEP * Q_BLOCK), F32),
        ] + [pltpu.VMEM((1, LANES + BF16_ROWS, s), BF16)] * (SLABS_PER_STEP * n_cfg)
        + [pltpu.VMEM((s, LANES), F32)] * (2 * SLABS_PER_STEP * (n_cfg - 1))
        + [pltpu.VMEM((1, KEY_WINDOW, HEADS_PER_STEP * Q_BLOCK), BF16)] * SCORE_SLOTS,
        compiler_params=pltpu.CompilerParams(
            dimension_semantics=("arbitrary", "arbitrary"), vmem_limit_bytes=VMEM_LIMIT_BYTES),
        name="attn",
    )(*q, *k, *v)


def _mix_kernel(a_ref, act_ref, km_ref, vtm_ref, x_ref, ws_ref, bs_ref, wo_ref, gf_ref, o_ref):
    rows = pl.ds(0, x_ref.shape[1])
    gated = _mix_branches(rows, a_ref, act_ref, km_ref, vtm_ref, ws_ref, bs_ref)
    _mix_project(rows, gated, x_ref, wo_ref, gf_ref, o_ref)


def _mix_branches(rows, a_ref, act_ref, km_ref, vtm_ref, ws_ref, bs_ref):
    tm = rows.size
    lo = 0
    cols = []
    for width in ACT_SPLITS:
        cols.append(slice(lo, lo + width))
        lo += width
    sa, gu, vn, qm, sm = (act_ref[0, rows, c] for c in cols)

    ga = sa * a_ref[0, rows, :]

    group = lax.broadcasted_iota(jnp.int32, (SGU_CHUNK, SGU_WIDTH), 1) // SGU_GROUP
    mixed = []
    for c in range(tm // SGU_CHUNK):
        vc = vn[c * SGU_CHUNK:(c + 1) * SGU_CHUNK]
        stacked = jnp.concatenate([jnp.where(group == g, vc, jnp.zeros_like(vc)) for g in range(N_SGU_GROUPS)],
                                  axis=0)
        mixed.append(jnp.dot(ws_ref[...], stacked, preferred_element_type=F32) + bs_ref[...])
    gb = (gu.astype(F32) * jnp.concatenate(mixed, axis=0)).astype(BF16)

    mhead = lax.broadcasted_iota(jnp.int32, (tm, MEM_WIDTH), 1) // MEM_HEAD_DIM
    q4 = jnp.concatenate([jnp.where(mhead == h, qm, jnp.zeros_like(qm)) for h in range(N_MEM_HEADS)], axis=0)
    s = lax.dot_general(km_ref[0], q4, (((1,), (1,)), ((), ())), preferred_element_type=F32)
    p = jnp.exp2(s - jnp.max(s, axis=0, keepdims=True)).astype(BF16)
    ones = vtm_ref[0, MEM_WIDTH:, :]
    mo = []
    for h in range(N_MEM_HEADS):
        vt = jnp.concatenate([vtm_ref[0, h * MEM_HEAD_DIM:(h + 1) * MEM_HEAD_DIM, :], ones], axis=0)
        ot = jnp.dot(vt, p[:, h * tm:(h + 1) * tm], preferred_element_type=F32)
        mo.append(ot[0:MEM_HEAD_DIM] * (1.0 / ot[MEM_HEAD_DIM:MEM_HEAD_DIM + 1]))
    mo = jnp.concatenate(mo, axis=0).T
    gm = (sm.astype(F32) * mo).astype(BF16)
    return ga, gb, gm


def _mix_project(rows, gated, x_ref, wo_ref, gf_ref, o_ref):
    ga, gb, gm = gated
    e1 = ATTN_WIDTH
    e2 = e1 + SGU_WIDTH
    y = jnp.dot(ga, wo_ref[0:e1, :], preferred_element_type=F32)
    y = y + jnp.dot(gb, wo_ref[e1:e2, :], preferred_element_type=F32)
    y = y + jnp.dot(gm, wo_ref[e2:, :], preferred_element_type=F32)
    o_ref[0, rows, :] = _rms(x_ref[0, rows, :] + y, gf_ref[...])


def _mix(a, act, k_mem, vt_mem, x, w_s, b_tile, w_out, g_final):
    b, s, d = x.shape
    tile = lambda w: pl.BlockSpec((1, MIX_TILE, w), lambda i, j: (i, j, 0))
    per_batch = lambda arr: pl.BlockSpec((1,) + arr.shape[1:], lambda i, j: (i, 0, 0))
    fixed = lambda arr: pl.BlockSpec(arr.shape, lambda i, j: (0, 0))
    return pl.pallas_call(
        _mix_kernel,
        grid=(b, s // MIX_TILE),
        in_specs=[tile(a.shape[2]), tile(act.shape[2]), per_batch(k_mem), per_batch(vt_mem), tile(d),
                  fixed(w_s), fixed(b_tile), fixed(w_out), fixed(g_final)],
        out_specs=tile(d),
        out_shape=jax.ShapeDtypeStruct((b, s, d), x.dtype),
        compiler_params=pltpu.CompilerParams(
            dimension_semantics=("arbitrary", "arbitrary"), vmem_limit_bytes=VMEM_LIMIT_BYTES),
        name="mix",
    )(a, act, k_mem, vt_mem, x, w_s, b_tile, w_out, g_final)


def kernel(x, mem, g_norm, w_in, w_sgu_spatial, b_sgu_spatial, g_sgu_v, g_mem, w_mem_kv, w_out, g_final):
    assert g_norm.shape[0] == 1, "the final norm is fused into the single layer's last kernel"
    b, s, d = x.shape
    k_mem, vt_mem = _memkv(mem, g_mem[0][None, :], w_mem_kv[0].astype(BF16))
    q, k, v, act = _inproj(x, g_norm[0][None, :], g_sgu_v[0][None, :], w_in[0].astype(BF16))
    a = _attn(q, k, v)
    w_s = jnp.concatenate(list(w_sgu_spatial[0].astype(BF16)), axis=1)
    b_tile = jnp.repeat(b_sgu_spatial[0].T, SGU_GROUP, axis=1)
    return _mix(a, act, k_mem, vt_mem, x, w_s, b_tile, w_out[0].astype(BF16), g_final[None, :])
```

```python
import functools
import math
from typing import NamedTuple

import jax
import jax.numpy as jnp
from jax import lax
from jax.experimental import pallas as pl
from jax.experimental.pallas import tpu as pltpu

F32 = jnp.float32
BF16 = jnp.bfloat16

EPS = 1e-6
NEG_INF = -1e30
LOG2_E = math.log2(math.e)

HEAD_DIM = 64
N_ATTN_HEADS = 8
ATTN_WIDTH = HEAD_DIM * N_ATTN_HEADS
DILATED_CONFIGS = ((128, 1), (512, 4), (2048, 16))
RADIUS = 64
SGU_WIDTH = 256
N_SGU_GROUPS = 4
SGU_GROUP = SGU_WIDTH // N_SGU_GROUPS
SGU_CHUNK = 128
MEM_WIDTH = 256
N_MEM_HEADS = 4
MEM_HEAD_DIM = MEM_WIDTH // N_MEM_HEADS
ACT_SPLITS = (ATTN_WIDTH, SGU_WIDTH, SGU_WIDTH, MEM_WIDTH, MEM_WIDTH)
ACT_WIDTH = sum(ACT_SPLITS)

LANES = 128
BF16_ROWS = 16
MXU_COLS = 256
Q_BLOCK = 128
KEY_WINDOW = 2 * Q_BLOCK
HEADS_PER_STEP = LANES // HEAD_DIM
SLABS_PER_STEP = 1
SCORE_LAG = 4
SCORE_SLOTS = 2 * SCORE_LAG
ROW_TILE = 1024
ORDER_BUFFERS = 4
MIX_TILE = 1024
VMEM_LIMIT_BYTES = 56 * 1024 * 1024


def _rms(x, g):
    return x * lax.rsqrt(jnp.mean(x * x, axis=-1, keepdims=True) + EPS) * g


def _silu(x):
    half = 0.5 * x
    return half + half * jnp.tanh(half)


def _ones_rows(cols):
    first = lax.broadcasted_iota(jnp.int32, (BF16_ROWS, cols), 0) == 0
    return jnp.where(first, 1.0, 0.0).astype(BF16)


def _memkv_kernel(mem_ref, g_ref, w_ref, k_ref, vt_ref):
    h = _rms(mem_ref[0], g_ref[...]).astype(BF16)
    kv = jnp.dot(h, w_ref[...], preferred_element_type=F32)
    k_ref[0] = kv[:, 0:MEM_WIDTH].astype(BF16)
    vt_ref[0, 0:MEM_WIDTH, :] = kv[:, MEM_WIDTH:].T.astype(BF16)
    vt_ref[0, MEM_WIDTH:, :] = _ones_rows(kv.shape[0])


def _memkv(mem, g_mem, w_kv):
    b, m, d = mem.shape
    n = w_kv.shape[1]
    return pl.pallas_call(
        _memkv_kernel,
        grid=(b,),
        in_specs=[
            pl.BlockSpec((1, m, d), lambda i: (i, 0, 0)),
            pl.BlockSpec((1, d), lambda i: (0, 0)),
            pl.BlockSpec((d, n), lambda i: (0, 0)),
        ],
        out_specs=[
            pl.BlockSpec((1, m, MEM_WIDTH), lambda i: (i, 0, 0)),
            pl.BlockSpec((1, MEM_WIDTH + BF16_ROWS, m), lambda i: (i, 0, 0)),
        ],
        out_shape=[
            jax.ShapeDtypeStruct((b, m, MEM_WIDTH), BF16),
            jax.ShapeDtypeStruct((b, MEM_WIDTH + BF16_ROWS, m), BF16),
        ],
        compiler_params=pltpu.CompilerParams(
            dimension_semantics=("arbitrary",), vmem_limit_bytes=VMEM_LIMIT_BYTES),
        name="memkv",
    )(mem, g_mem, w_kv)


def _emit_orders(val, outs, lo, order_scrs):
    rows = val.shape[0]
    n_slabs = MXU_COLS // LANES

    def write(ref, r, slabs):
        ref[r, :, lo:lo + MXU_COLS] = jnp.concatenate(slabs, axis=1).astype(BF16)

    by_dil = sorted(range(len(DILATED_CONFIGS)), key=lambda c: DILATED_CONFIGS[c][1])
    prev = 1
    for level, c in enumerate(by_dil):
        dil = DILATED_CONFIGS[c][1]
        if dil == 1:
            slabs = [val[:, i * LANES:(i + 1) * LANES] for i in range(n_slabs)]
            write(outs[c], 0, slabs)
            for scr, x in zip(order_scrs, slabs):
                scr[0] = x
            continue
        step, cls_prev, cls = dil // prev, rows // prev, rows // dil
        for r in range(dil):
            slabs = [scr[level - 1, pl.ds((r % prev) * cls_prev + r // prev, cls, stride=step), :]
                     for scr in order_scrs]
            write(outs[c], r, slabs)
            if level + 1 < len(by_dil):
                for scr, x in zip(order_scrs, slabs):
                    scr[level, r * cls:(r + 1) * cls, :] = x
        prev = dil


def _inproj_kernel(x_ref, g_ref, gv_ref, w_ref, *refs):
    n_cfg = len(DILATED_CONFIGS)
    q_refs, k_refs, v_refs = refs[0:n_cfg], refs[n_cfg:2 * n_cfg], refs[2 * n_cfg:3 * n_cfg]
    act_ref = refs[3 * n_cfg]
    order_scrs = refs[3 * n_cfg + 1:]
    h = _rms(x_ref[0], g_ref[...]).astype(BF16)

    def proj(lo, width):
        return jnp.dot(h, w_ref[:, lo:lo + width], preferred_element_type=F32)

    n_slabs = MXU_COLS // LANES
    at_scr = 0
    for t, (out_refs, scale) in enumerate(((q_refs, HEAD_DIM ** -0.5 * LOG2_E), (k_refs, None), (v_refs, None))):
        for piece in range(ATTN_WIDTH // MXU_COLS):
            val = proj(t * ATTN_WIDTH + piece * MXU_COLS, MXU_COLS)
            if scale is not None:
                val = val * scale
            _emit_orders(val, out_refs, piece * MXU_COLS, order_scrs[at_scr:at_scr + n_slabs])
            at_scr = (at_scr + n_slabs) % len(order_scrs)
    lo = 3 * ATTN_WIDTH
    raw = []
    for width in (ATTN_WIDTH, SGU_WIDTH, SGU_WIDTH, SGU_WIDTH, MEM_WIDTH, MEM_WIDTH):
        raw.append(proj(lo, width))
        lo += width
    za, ub, vb, zb, qm, zm = raw
    acts = (_silu(za), _silu(zb) * jax.nn.gelu(ub), _rms(jax.nn.gelu(vb), gv_ref[...]),
            qm * (MEM_HEAD_DIM ** -0.5 * LOG2_E), _silu(zm))
    lo = 0
    for width, val in zip(ACT_SPLITS, acts):
        act_ref[0, :, lo:lo + width] = val.astype(BF16)
        lo += width


def _inproj(x, g_norm, g_v, w_in):
    b, s, d = x.shape
    cols = w_in.shape[1]
    assert cols == 4 * ATTN_WIDTH + 3 * SGU_WIDTH + 2 * MEM_WIDTH
    tile = lambda w: pl.BlockSpec((1, ROW_TILE, w), lambda i, j: (i, j, 0))
    fixed = lambda i, j: (0, 0)
    dils = [dil for _, dil in DILATED_CONFIGS]
    assert all(ROW_TILE % (dil * BF16_ROWS) == 0 for dil in dils)
    ordered = lambda w: [pl.BlockSpec((dil, ROW_TILE // dil, w), lambda i, j: (i, j, 0)) for dil in dils]
    shaped = lambda w: [jax.ShapeDtypeStruct((b * dil, s // dil, w), BF16) for dil in dils]
    widths = (ATTN_WIDTH, ATTN_WIDTH, ATTN_WIDTH)
    outs = pl.pallas_call(
        _inproj_kernel,
        grid=(b, s // ROW_TILE),
        in_specs=[
            tile(d),
            pl.BlockSpec((1, d), fixed),
            pl.BlockSpec((1, SGU_WIDTH), fixed),
            pl.BlockSpec((d, cols), fixed, pipeline_mode=pl.Buffered(1)),
        ],
        out_specs=[spec for w in widths for spec in ordered(w)] + [tile(ACT_WIDTH)],
        out_shape=[shape for w in widths for shape in shaped(w)] + [jax.ShapeDtypeStruct((b, s, ACT_WIDTH), BF16)],
        scratch_shapes=[pltpu.VMEM((len(dils) - 1, ROW_TILE, LANES), F32)] * ORDER_BUFFERS,
        compiler_params=pltpu.CompilerParams(
            dimension_semantics=("arbitrary", "arbitrary"), vmem_limit_bytes=VMEM_LIMIT_BYTES),
        name="inproj",
    )(x, g_norm, g_v, w_in)
    n = len(dils)
    q, k, v = ([o.reshape(b, s, -1) for o in outs[i * n:(i + 1) * n]] for i in range(3))
    return q, k, v, outs[3 * n]


SHIFT, EDGE, WHOLE = "shift", "edge", "whole"


def _bias_tile(kind, dilation, first_head):
    shape = (KEY_WINDOW, HEADS_PER_STEP * Q_BLOCK)
    key = lax.broadcasted_iota(jnp.int32, shape, 0)
    col = lax.broadcasted_iota(jnp.int32, shape, 1)
    second = col >= Q_BLOCK
    qi = jnp.where(second, col - Q_BLOCK, col)
    if kind == SHIFT:
        rel = jnp.abs(qi + RADIUS - key)
        valid = rel <= RADIUS
    elif kind == WHOLE:
        rel = jnp.abs(qi - key)
        valid = rel <= RADIUS
    else:
        upper = key >= Q_BLOCK
        rel = jnp.abs(qi - jnp.where(upper, key - Q_BLOCK, key))
        valid = (rel <= RADIUS) & (upper == (qi >= RADIUS))
    head = (first_head + jnp.where(second, 1, 0)).astype(F32)
    slope = jnp.exp2(-8.0 * (head + 1.0) / N_ATTN_HEADS)
    dist = (rel * dilation).astype(F32)
    return jnp.where(valid, -slope * dist * LOG2_E, NEG_INF)


class _Block(NamedTuple):
    cfg: int
    tile: int
    q_rows: tuple
    k_rows: tuple
    out_rows: tuple
    out_stride: int


def _cfg_order():
    return sorted(range(len(DILATED_CONFIGS)), key=lambda c: -DILATED_CONFIGS[c][1])


def _attn_plan(seq):
    blocks, tiles = [], []
    order = _cfg_order()
    for n, c in enumerate(order):
        dil = DILATED_CONFIGS[c][1]
        nxt = DILATED_CONFIGS[order[n + 1]][1] if n + 1 < len(order) else 1
        step = dil // nxt
        cls = seq // dil
        if cls == Q_BLOCK:
            tiles.append((WHOLE, dil))
            whole = len(tiles) - 1
        else:
            tiles.append((SHIFT, dil))
            tiles.append((EDGE, dil))
            shift, edge = len(tiles) - 2, len(tiles) - 1
        for r in range(dil):
            base = r * cls
            sink = (r % nxt) * (seq // nxt) + r // nxt
            if cls == Q_BLOCK:
                blocks.append(_Block(c, whole, ((base, Q_BLOCK),), ((base, Q_BLOCK),), ((sink, Q_BLOCK),), step))
                continue
            last = cls - RADIUS
            blocks.append(_Block(c, edge, ((base, RADIUS), (base + last, RADIUS)),
                                 ((base, Q_BLOCK), (base + cls - Q_BLOCK, Q_BLOCK)),
                                 ((sink, RADIUS), (sink + step * last, RADIUS)), step))
            for j in range(cls // Q_BLOCK - 1):
                u = RADIUS + j * Q_BLOCK
                blocks.append(_Block(c, shift, ((base + u, Q_BLOCK),), ((base + j * Q_BLOCK, KEY_WINDOW),),
                                     ((sink + step * u, Q_BLOCK),), step))
    return blocks, tiles


def _rows(ref, lead, ranges, lanes=slice(None)):
    parts = [ref[lead + (pl.ds(s, n), lanes)] for s, n in ranges]
    return parts[0] if len(parts) == 1 else jnp.concatenate(parts, axis=0)


def _score_block(blk, lanes, q_ref, k_ref, bias, s_scr):
    q = _rows(q_ref, (0,), blk.q_rows, lanes)
    first = lax.broadcasted_iota(jnp.int32, (1, LANES), 1) < HEAD_DIM
    zeros = jnp.zeros_like(q)
    q2 = jnp.concatenate([jnp.where(first, q, zeros), jnp.where(first, zeros, q)], axis=0)
    k_win = _rows(k_ref, (0,), blk.k_rows, lanes)
    window = k_win.shape[0]
    s = lax.dot_general(k_win, q2, (((1,), (1,)), ((), ())), preferred_element_type=F32)
    s_scr[0, 0:window, :] = s + bias[0:window, :]


def _value_block(blk, lanes, zero, vt_scr, s_scr, prev, sink, a_ref):
    window = sum(n for _, n in blk.k_rows)
    ms, ps = [], []
    for h in range(HEADS_PER_STEP):
        cols = slice(h * Q_BLOCK, (h + 1) * Q_BLOCK)
        s = s_scr[zero, 0:window, cols]
        m = jnp.max(s, axis=0, keepdims=True)
        ms.append(m)
        ps.append(jnp.exp2(s - m).astype(BF16))
    vt = jnp.concatenate([vt_scr[zero, :, s0:s0 + n] for s0, n in blk.k_rows], axis=1)
    ot = jnp.dot(vt, jnp.concatenate(ps, axis=1), preferred_element_type=F32)
    outs, lses = [], []
    for h in range(HEADS_PER_STEP):
        cols = slice(h * Q_BLOCK, (h + 1) * Q_BLOCK)
        l = ot[LANES:LANES + 1, cols]
        outs.append(ot[h * HEAD_DIM:(h + 1) * HEAD_DIM, cols] * (1.0 / l))
        lses.append(ms[h] + jnp.log2(l))
    o = jnp.concatenate(outs, axis=0).T
    lse = jnp.concatenate([jnp.broadcast_to(x, (HEAD_DIM, Q_BLOCK)) for x in lses], axis=0).T
    if prev is not None:
        o_prev, lse_prev = (_rows(ref, (), blk.q_rows) for ref in prev)
        if sink is None:
            w = 1.0 / (1.0 + jnp.exp2(lse_prev - lse))
            o = o_prev + w * (o - o_prev)
        else:
            top = jnp.maximum(lse, lse_prev)
            e, e_prev = jnp.exp2(lse - top), jnp.exp2(lse_prev - top)
            den = e + e_prev
            o = (e * o + e_prev * o_prev) * (1.0 / den)
            lse = top + jnp.log2(den)
    at = 0
    for start, n in blk.out_rows:
        dst = pl.ds(start, n) if blk.out_stride == 1 else pl.ds(start, n, stride=blk.out_stride)
        if sink is None:
            a_ref[0, dst, lanes] = o[at:at + n].astype(a_ref.dtype)
        else:
            sink[0][dst, :] = o[at:at + n]
            sink[1][dst, :] = lse[at:at + n]
        at += n


def _attn_kernel(*refs, seq):
    n_cfg = len(DILATED_CONFIGS)
    q_refs, k_refs, v_refs = refs[0:n_cfg], refs[n_cfg:2 * n_cfg], refs[2 * n_cfg:3 * n_cfg]
    a_ref, bias_scr = refs[3 * n_cfg:3 * n_cfg + 2]
    scrs = refs[3 * n_cfg + 2:]
    n_vt, n_hand = SLABS_PER_STEP * n_cfg, SLABS_PER_STEP * (n_cfg - 1)
    vt_scrs = [scrs[p * n_cfg:(p + 1) * n_cfg] for p in range(SLABS_PER_STEP)]
    hand = [[scrs[n_vt + 2 * (p * (n_cfg - 1) + i):n_vt + 2 * (p * (n_cfg - 1) + i) + 2] for i in range(n_cfg - 1)]
            for p in range(SLABS_PER_STEP)]
    s_scrs = scrs[n_vt + 2 * n_hand:]
    blocks, tiles = _attn_plan(seq)
    order = _cfg_order()
    slabs = [slice(p * LANES, (p + 1) * LANES) for p in range(SLABS_PER_STEP)]

    @pl.when(pl.program_id(1) == 0)
    def _():
        for p in range(SLABS_PER_STEP):
            first_head = (pl.program_id(0) * SLABS_PER_STEP + p) * HEADS_PER_STEP
            for t, (kind, dil) in enumerate(tiles):
                bias_scr[p * len(tiles) + t] = _bias_tile(kind, dil, first_head)

    for p in range(SLABS_PER_STEP):
        for c in order:
            vt_scrs[p][c][0, LANES:LANES + BF16_ROWS, :] = _ones_rows(seq)
            for i in range(seq // LANES):
                chunk = slice(i * LANES, (i + 1) * LANES)
                vt_scrs[p][c][0, 0:LANES, chunk] = v_refs[c][0, chunk, slabs[p]].T

    zero = jnp.minimum(pl.program_id(1), 0)
    work = [(p, blk) for p in range(SLABS_PER_STEP) for blk in blocks]
    for t in range(len(work) + SCORE_LAG):
        if t < len(work):
            p, blk = work[t]
            _score_block(blk, slabs[p], q_refs[blk.cfg], k_refs[blk.cfg], bias_scr.at[p * len(tiles) + blk.tile],
                         s_scrs[t % SCORE_SLOTS])
        if t >= SCORE_LAG:
            u = t - SCORE_LAG
            p, blk = work[u]
            n = order.index(blk.cfg)
            _value_block(blk, slabs[p], zero, vt_scrs[p][blk.cfg], s_scrs[u % SCORE_SLOTS],
                         hand[p][n - 1] if n > 0 else None, hand[p][n] if n + 1 < n_cfg else None, a_ref)


def _attn(q, k, v):
    b, s, w = k[0].shape
    n_cfg = len(DILATED_CONFIGS)
    assert all(s % (dil * Q_BLOCK) == 0 for _, dil in DILATED_CONFIGS)
    n_tiles = len(_attn_plan(s)[1])
    blk = pl.BlockSpec((1, s, SLABS_PER_STEP * LANES), lambda j, i: (i, 0, j))
    return pl.pallas_call(
        functools.partial(_attn_kernel, seq=s),
        grid=(w // (SLABS_PER_STEP * LANES), b),
        in_specs=[blk] * (3 * n_cfg),
        out_specs=blk,
        out_shape=jax.ShapeDtypeStruct((b, s, w), BF16),
        scratch_shapes=[
            pltpu.VMEM((SLABS_PER_STEP * n_tiles, KEY_WINDOW, HEADS_PER_STEP * Q_BLOCK), F32),
        ] + [pltpu.VMEM((1, LANES + BF16_ROWS, s), BF16)] * (SLABS_PER_STEP * n_cfg)
        + [pltpu.VMEM((s, LANES), F32)] * (2 * SLABS_PER_STEP * (n_cfg - 1))
        + [pltpu.VMEM((1, KEY_WINDOW, HEADS_PER_STEP * Q_BLOCK), F32)] * SCORE_SLOTS,
        compiler_params=pltpu.CompilerParams(
            dimension_semantics=("arbitrary", "arbitrary"), vmem_limit_bytes=VMEM_LIMIT_BYTES),
        name="attn",
    )(*q, *k, *v)


def _mix_kernel(a_ref, act_ref, km_ref, vtm_ref, x_ref, ws_ref, bs_ref, wo_ref, gf_ref, o_ref):
    rows = pl.ds(0, x_ref.shape[1])
    gated = _mix_branches(rows, a_ref, act_ref, km_ref, vtm_ref, ws_ref, bs_ref)
    _mix_project(rows, gated, x_ref, wo_ref, gf_ref, o_ref)


def _mix_branches(rows, a_ref, act_ref, km_ref, vtm_ref, ws_ref, bs_ref):
    tm = rows.size
    lo = 0
    cols = []
    for width in ACT_SPLITS:
        cols.append(slice(lo, lo + width))
        lo += width
    sa, gu, vn, qm, sm = (act_ref[0, rows, c] for c in cols)

    ga = sa * a_ref[0, rows, :]

    group = lax.broadcasted_iota(jnp.int32, (SGU_CHUNK, SGU_WIDTH), 1) // SGU_GROUP
    mixed = []
    for c in range(tm // SGU_CHUNK):
        vc = vn[c * SGU_CHUNK:(c + 1) * SGU_CHUNK]
        stacked = jnp.concatenate([jnp.where(group == g, vc, jnp.zeros_like(vc)) for g in range(N_SGU_GROUPS)],
                                  axis=0)
        mixed.append(jnp.dot(ws_ref[...], stacked, preferred_element_type=F32) + bs_ref[...])
    gb = (gu.astype(F32) * jnp.concatenate(mixed, axis=0)).astype(BF16)

    mhead = lax.broadcasted_iota(jnp.int32, (tm, MEM_WIDTH), 1) // MEM_HEAD_DIM
    q4 = jnp.concatenate([jnp.where(mhead == h, qm, jnp.zeros_like(qm)) for h in range(N_MEM_HEADS)], axis=0)
    s = lax.dot_general(km_ref[0], q4, (((1,), (1,)), ((), ())), preferred_element_type=F32)
    p = jnp.exp2(s - jnp.max(s, axis=0, keepdims=True)).astype(BF16)
    ones = vtm_ref[0, MEM_WIDTH:, :]
    mo = []
    for h in range(N_MEM_HEADS):
        vt = jnp.concatenate([vtm_ref[0, h * MEM_HEAD_DIM:(h + 1) * MEM_HEAD_DIM, :], ones], axis=0)
        ot = jnp.dot(vt, p[:, h * tm:(h + 1) * tm], preferred_element_type=F32)
        mo.append(ot[0:MEM_HEAD_DIM] * (1.0 / ot[MEM_HEAD_DIM:MEM_HEAD_DIM + 1]))
    mo = jnp.concatenate(mo, axis=0).T
    gm = (sm.astype(F32) * mo).astype(BF16)
    return ga, gb, gm


def _mix_project(rows, gated, x_ref, wo_ref, gf_ref, o_ref):
    ga, gb, gm = gated
    e1 = ATTN_WIDTH
    e2 = e1 + SGU_WIDTH
    y = jnp.dot(ga, wo_ref[0:e1, :], preferred_element_type=F32)
    y = y + jnp.dot(gb, wo_ref[e1:e2, :], preferred_element_type=F32)
    y = y + jnp.dot(gm, wo_ref[e2:, :], preferred_element_type=F32)
    o_ref[0, rows, :] = _rms(x_ref[0, rows, :] + y, gf_ref[...])


def _mix(a, act, k_mem, vt_mem, x, w_s, b_tile, w_out, g_final):
    b, s, d = x.shape
    tile = lambda w: pl.BlockSpec((1, MIX_TILE, w), lambda i, j: (i, j, 0))
    per_batch = lambda arr: pl.BlockSpec((1,) + arr.shape[1:], lambda i, j: (i, 0, 0))
    fixed = lambda arr: pl.BlockSpec(arr.shape, lambda i, j: (0, 0))
    return pl.pallas_call(
        _mix_kernel,
        grid=(b, s // MIX_TILE),
        in_specs=[tile(a.shape[2]), tile(act.shape[2]), per_batch(k_mem), per_batch(vt_mem), tile(d),
                  fixed(w_s), fixed(b_tile), fixed(w_out), fixed(g_final)],
        out_specs=tile(d),
        out_shape=jax.ShapeDtypeStruct((b, s, d), x.dtype),
        compiler_params=pltpu.CompilerParams(
            dimension_semantics=("arbitrary", "arbitrary"), vmem_limit_bytes=VMEM_LIMIT_BYTES),
        name="mix",
    )(a, act, k_mem, vt_mem, x, w_s, b_tile, w_out, g_final)


def kernel(x, mem, g_norm, w_in, w_sgu_spatial, b_sgu_spatial, g_sgu_v, g_mem, w_mem_kv, w_out, g_final):
    assert g_norm.shape[0] == 1, "the final norm is fused into the single layer's last kernel"
    b, s, d = x.shape
    k_mem, vt_mem = _memkv(mem, g_mem[0][None, :], w_mem_kv[0].astype(BF16))
    q, k, v, act = _inproj(x, g_norm[0][None, :], g_sgu_v[0][None, :], w_in[0].astype(BF16))
    a = _attn(q, k, v)
    w_s = jnp.concatenate(list(w_sgu_spatial[0].astype(BF16)), axis=1)
    b_tile = jnp.repeat(b_sgu_spatial[0].T, SGU_GROUP, axis=1)
    return _mix(a, act, k_mem, vt_mem, x, w_s, b_tile, w_out[0].astype(BF16), g_final[None, :])
```

```python
import functools
import math
from typing import NamedTuple

import jax
import jax.numpy as jnp
from jax import lax
from jax.experimental import pallas as pl
from jax.experimental.pallas import tpu as pltpu

F32 = jnp.float32
BF16 = jnp.bfloat16

EPS = 1e-6
NEG_INF = -1e30
LOG2_E = math.log2(math.e)

HEAD_DIM = 64
N_ATTN_HEADS = 8
ATTN_WIDTH = HEAD_DIM * N_ATTN_HEADS
DILATED_CONFIGS = ((128, 1), (512, 4), (2048, 16))
RADIUS = 64
SGU_WIDTH = 256
N_SGU_GROUPS = 4
SGU_GROUP = SGU_WIDTH // N_SGU_GROUPS
SGU_CHUNK = 128
MEM_WIDTH = 256
N_MEM_HEADS = 4
MEM_HEAD_DIM = MEM_WIDTH // N_MEM_HEADS
ACT_SPLITS = (ATTN_WIDTH, SGU_WIDTH, SGU_WIDTH, MEM_WIDTH, MEM_WIDTH)
ACT_WIDTH = sum(ACT_SPLITS)

LANES = 128
BF16_ROWS = 16
MXU_COLS = 256
Q_BLOCK = 128
KEY_WINDOW = 2 * Q_BLOCK
HEADS_PER_STEP = LANES // HEAD_DIM
SLABS_PER_STEP = 1
SCORE_LAG = 4
SCORE_SLOTS = 2 * SCORE_LAG
ROW_TILE = 1024
ORDER_BUFFERS = 4
ACT_ROWS = 256
MIX_TILE = 1024
PROJECT_ROWS = 256
VMEM_LIMIT_BYTES = 56 * 1024 * 1024


def _rms(x, g):
    return x * lax.rsqrt(jnp.mean(x * x, axis=-1, keepdims=True) + EPS) * g


def _silu(x):
    half = 0.5 * x
    return half + half * jnp.tanh(half)


def _ones_rows(cols):
    first = lax.broadcasted_iota(jnp.int32, (BF16_ROWS, cols), 0) == 0
    return jnp.where(first, 1.0, 0.0).astype(BF16)


def _memkv_kernel(mem_ref, g_ref, w_ref, k_ref, vt_ref):
    h = _rms(mem_ref[0], g_ref[...]).astype(BF16)
    kv = jnp.dot(h, w_ref[...], preferred_element_type=F32)
    k_ref[0] = kv[:, 0:MEM_WIDTH].astype(BF16)
    vt_ref[0, 0:MEM_WIDTH, :] = kv[:, MEM_WIDTH:].T.astype(BF16)
    vt_ref[0, MEM_WIDTH:, :] = _ones_rows(kv.shape[0])


def _memkv(mem, g_mem, w_kv):
    b, m, d = mem.shape
    n = w_kv.shape[1]
    return pl.pallas_call(
        _memkv_kernel,
        grid=(b,),
        in_specs=[
            pl.BlockSpec((1, m, d), lambda i: (i, 0, 0)),
            pl.BlockSpec((1, d), lambda i: (0, 0)),
            pl.BlockSpec((d, n), lambda i: (0, 0)),
        ],
        out_specs=[
            pl.BlockSpec((1, m, MEM_WIDTH), lambda i: (i, 0, 0)),
            pl.BlockSpec((1, MEM_WIDTH + BF16_ROWS, m), lambda i: (i, 0, 0)),
        ],
        out_shape=[
            jax.ShapeDtypeStruct((b, m, MEM_WIDTH), BF16),
            jax.ShapeDtypeStruct((b, MEM_WIDTH + BF16_ROWS, m), BF16),
        ],
        compiler_params=pltpu.CompilerParams(
            dimension_semantics=("arbitrary",), vmem_limit_bytes=VMEM_LIMIT_BYTES),
        name="memkv",
    )(mem, g_mem, w_kv)


def _emit_orders(val, outs, lo, order_scrs):
    rows = val.shape[0]
    n_slabs = MXU_COLS // LANES

    def write(ref, r, slabs):
        ref[r, :, lo:lo + MXU_COLS] = jnp.concatenate(slabs, axis=1).astype(BF16)

    by_dil = sorted(range(len(DILATED_CONFIGS)), key=lambda c: DILATED_CONFIGS[c][1])
    prev = 1
    for level, c in enumerate(by_dil):
        dil = DILATED_CONFIGS[c][1]
        if dil == 1:
            slabs = [val[:, i * LANES:(i + 1) * LANES] for i in range(n_slabs)]
            write(outs[c], 0, slabs)
            for scr, x in zip(order_scrs, slabs):
                scr[0] = x
            continue
        step, cls_prev, cls = dil // prev, rows // prev, rows // dil
        for r in range(dil):
            slabs = [scr[level - 1, pl.ds((r % prev) * cls_prev + r // prev, cls, stride=step), :]
                     for scr in order_scrs]
            write(outs[c], r, slabs)
            if level + 1 < len(by_dil):
                for scr, x in zip(order_scrs, slabs):
                    scr[level, r * cls:(r + 1) * cls, :] = x
        prev = dil


def _inproj_kernel(x_ref, g_ref, gv_ref, w_ref, *refs):
    n_cfg = len(DILATED_CONFIGS)
    q_refs, k_refs, v_refs = refs[0:n_cfg], refs[n_cfg:2 * n_cfg], refs[2 * n_cfg:3 * n_cfg]
    act_ref = refs[3 * n_cfg]
    order_scrs = refs[3 * n_cfg + 1:]
    h = _rms(x_ref[0], g_ref[...]).astype(BF16)

    def proj(lo, width):
        return jnp.dot(h, w_ref[:, lo:lo + width], preferred_element_type=F32)

    n_slabs = MXU_COLS // LANES
    at_scr = 0
    for t, (out_refs, scale) in enumerate(((q_refs, HEAD_DIM ** -0.5 * LOG2_E), (k_refs, None), (v_refs, None))):
        for piece in range(ATTN_WIDTH // MXU_COLS):
            val = proj(t * ATTN_WIDTH + piece * MXU_COLS, MXU_COLS)
            if scale is not None:
                val = val * scale
            _emit_orders(val, out_refs, piece * MXU_COLS, order_scrs[at_scr:at_scr + n_slabs])
            at_scr = (at_scr + n_slabs) % len(order_scrs)
    for r0 in range(0, h.shape[0], ACT_ROWS):
        hp = h[r0:r0 + ACT_ROWS]
        lo = 3 * ATTN_WIDTH
        raw = []
        for width in (ATTN_WIDTH, SGU_WIDTH, SGU_WIDTH, SGU_WIDTH, MEM_WIDTH, MEM_WIDTH):
            raw.append(jnp.dot(hp, w_ref[:, lo:lo + width], preferred_element_type=F32))
            lo += width
        za, ub, vb, zb, qm, zm = raw
        acts = (_silu(za), _silu(zb) * jax.nn.gelu(ub), _rms(jax.nn.gelu(vb), gv_ref[...]),
                qm * (MEM_HEAD_DIM ** -0.5 * LOG2_E), _silu(zm))
        lo = 0
        for width, val in zip(ACT_SPLITS, acts):
            act_ref[0, r0:r0 + ACT_ROWS, lo:lo + width] = val.astype(BF16)
            lo += width


def _inproj(x, g_norm, g_v, w_in):
    b, s, d = x.shape
    cols = w_in.shape[1]
    assert cols == 4 * ATTN_WIDTH + 3 * SGU_WIDTH + 2 * MEM_WIDTH
    tile = lambda w: pl.BlockSpec((1, ROW_TILE, w), lambda i, j: (i, j, 0))
    fixed = lambda i, j: (0, 0)
    dils = [dil for _, dil in DILATED_CONFIGS]
    assert all(ROW_TILE % (dil * BF16_ROWS) == 0 for dil in dils)
    ordered = lambda w: [pl.BlockSpec((dil, ROW_TILE // dil, w), lambda i, j: (i, j, 0)) for dil in dils]
    shaped = lambda w: [jax.ShapeDtypeStruct((b * dil, s // dil, w), BF16) for dil in dils]
    widths = (ATTN_WIDTH, ATTN_WIDTH, ATTN_WIDTH)
    outs = pl.pallas_call(
        _inproj_kernel,
        grid=(b, s // ROW_TILE),
        in_specs=[
            tile(d),
            pl.BlockSpec((1, d), fixed),
            pl.BlockSpec((1, SGU_WIDTH), fixed),
            pl.BlockSpec((d, cols), fixed, pipeline_mode=pl.Buffered(1)),
        ],
        out_specs=[spec for w in widths for spec in ordered(w)] + [tile(ACT_WIDTH)],
        out_shape=[shape for w in widths for shape in shaped(w)] + [jax.ShapeDtypeStruct((b, s, ACT_WIDTH), BF16)],
        scratch_shapes=[pltpu.VMEM((len(dils) - 1, ROW_TILE, LANES), F32)] * ORDER_BUFFERS,
        compiler_params=pltpu.CompilerParams(
            dimension_semantics=("arbitrary", "arbitrary"), vmem_limit_bytes=VMEM_LIMIT_BYTES),
        name="inproj",
    )(x, g_norm, g_v, w_in)
    n = len(dils)
    q, k, v = ([o.reshape(b, s, -1) for o in outs[i * n:(i + 1) * n]] for i in range(3))
    return q, k, v, outs[3 * n]


SHIFT, EDGE, WHOLE = "shift", "edge", "whole"


def _bias_tile(kind, dilation, first_head):
    shape = (KEY_WINDOW, HEADS_PER_STEP * Q_BLOCK)
    key = lax.broadcasted_iota(jnp.int32, shape, 0)
    col = lax.broadcasted_iota(jnp.int32, shape, 1)
    second = col >= Q_BLOCK
    qi = jnp.where(second, col - Q_BLOCK, col)
    if kind == SHIFT:
        rel = jnp.abs(qi + RADIUS - key)
        valid = rel <= RADIUS
    elif kind == WHOLE:
        rel = jnp.abs(qi - key)
        valid = rel <= RADIUS
    else:
        upper = key >= Q_BLOCK
        rel = jnp.abs(qi - jnp.where(upper, key - Q_BLOCK, key))
        valid = (rel <= RADIUS) & (upper == (qi >= RADIUS))
    head = (first_head + jnp.where(second, 1, 0)).astype(F32)
    slope = jnp.exp2(-8.0 * (head + 1.0) / N_ATTN_HEADS)
    dist = (rel * dilation).astype(F32)
    return jnp.where(valid, -slope * dist * LOG2_E, NEG_INF)


class _Block(NamedTuple):
    cfg: int
    tile: int
    q_rows: tuple
    k_rows: tuple
    out_rows: tuple
    out_stride: int


def _cfg_order():
    return sorted(range(len(DILATED_CONFIGS)), key=lambda c: -DILATED_CONFIGS[c][1])


def _attn_plan(seq):
    blocks, tiles = [], []
    order = _cfg_order()
    for n, c in enumerate(order):
        dil = DILATED_CONFIGS[c][1]
        nxt = DILATED_CONFIGS[order[n + 1]][1] if n + 1 < len(order) else 1
        step = dil // nxt
        cls = seq // dil
        if cls == Q_BLOCK:
            tiles.append((WHOLE, dil))
            whole = len(tiles) - 1
        else:
            tiles.append((SHIFT, dil))
            tiles.append((EDGE, dil))
            shift, edge = len(tiles) - 2, len(tiles) - 1
        for r in range(dil):
            base = r * cls
            sink = (r % nxt) * (seq // nxt) + r // nxt
            if cls == Q_BLOCK:
                blocks.append(_Block(c, whole, ((base, Q_BLOCK),), ((base, Q_BLOCK),), ((sink, Q_BLOCK),), step))
                continue
            last = cls - RADIUS
            blocks.append(_Block(c, edge, ((base, RADIUS), (base + last, RADIUS)),
                                 ((base, Q_BLOCK), (base + cls - Q_BLOCK, Q_BLOCK)),
                                 ((sink, RADIUS), (sink + step * last, RADIUS)), step))
            for j in range(cls // Q_BLOCK - 1):
                u = RADIUS + j * Q_BLOCK
                blocks.append(_Block(c, shift, ((base + u, Q_BLOCK),), ((base + j * Q_BLOCK, KEY_WINDOW),),
                                     ((sink + step * u, Q_BLOCK),), step))
    return blocks, tiles


def _rows(ref, lead, ranges, lanes=slice(None)):
    parts = [ref[lead + (pl.ds(s, n), lanes)] for s, n in ranges]
    return parts[0] if len(parts) == 1 else jnp.concatenate(parts, axis=0)


def _score_block(blk, lanes, q_ref, k_ref, bias, s_scr):
    q = _rows(q_ref, (0,), blk.q_rows, lanes)
    first = lax.broadcasted_iota(jnp.int32, (1, LANES), 1) < HEAD_DIM
    zeros = jnp.zeros_like(q)
    q2 = jnp.concatenate([jnp.where(first, q, zeros), jnp.where(first, zeros, q)], axis=0)
    k_win = _rows(k_ref, (0,), blk.k_rows, lanes)
    window = k_win.shape[0]
    s = lax.dot_general(k_win, q2, (((1,), (1,)), ((), ())), preferred_element_type=F32)
    s_scr[0, 0:window, :] = s + bias[0:window, :]


def _value_block(blk, lanes, zero, vt_scr, s_scr, prev, sink, a_ref):
    window = sum(n for _, n in blk.k_rows)
    ms, ps = [], []
    for h in range(HEADS_PER_STEP):
        cols = slice(h * Q_BLOCK, (h + 1) * Q_BLOCK)
        s = s_scr[zero, 0:window, cols]
        m = jnp.max(s, axis=0, keepdims=True)
        ms.append(m)
        ps.append(jnp.exp2(s - m).astype(BF16))
    vt = jnp.concatenate([vt_scr[zero, :, s0:s0 + n] for s0, n in blk.k_rows], axis=1)
    ot = jnp.dot(vt, jnp.concatenate(ps, axis=1), preferred_element_type=F32)
    outs, lses = [], []
    for h in range(HEADS_PER_STEP):
        cols = slice(h * Q_BLOCK, (h + 1) * Q_BLOCK)
        l = ot[LANES:LANES + 1, cols]
        outs.append(ot[h * HEAD_DIM:(h + 1) * HEAD_DIM, cols] * (1.0 / l))
        lses.append(ms[h] + jnp.log2(l))
    o = jnp.concatenate(outs, axis=0).T
    lse = jnp.concatenate([jnp.broadcast_to(x, (HEAD_DIM, Q_BLOCK)) for x in lses], axis=0).T
    if prev is not None:
        o_prev, lse_prev = (_rows(ref, (), blk.q_rows) for ref in prev)
        if sink is None:
            w = 1.0 / (1.0 + jnp.exp2(lse_prev - lse))
            o = o_prev + w * (o - o_prev)
        else:
            top = jnp.maximum(lse, lse_prev)
            e, e_prev = jnp.exp2(lse - top), jnp.exp2(lse_prev - top)
            den = e + e_prev
            o = (e * o + e_prev * o_prev) * (1.0 / den)
            lse = top + jnp.log2(den)
    at = 0
    for start, n in blk.out_rows:
        dst = pl.ds(start, n) if blk.out_stride == 1 else pl.ds(start, n, stride=blk.out_stride)
        if sink is None:
            a_ref[0, dst, lanes] = o[at:at + n].astype(a_ref.dtype)
        else:
            sink[0][dst, :] = o[at:at + n]
            sink[1][dst, :] = lse[at:at + n]
        at += n


def _attn_kernel(*refs, seq):
    n_cfg = len(DILATED_CONFIGS)
    q_refs, k_refs, v_refs = refs[0:n_cfg], refs[n_cfg:2 * n_cfg], refs[2 * n_cfg:3 * n_cfg]
    a_ref, bias_scr = refs[3 * n_cfg:3 * n_cfg + 2]
    scrs = refs[3 * n_cfg + 2:]
    n_vt, n_hand = SLABS_PER_STEP * n_cfg, SLABS_PER_STEP * (n_cfg - 1)
    vt_scrs = [scrs[p * n_cfg:(p + 1) * n_cfg] for p in range(SLABS_PER_STEP)]
    hand = [[scrs[n_vt + 2 * (p * (n_cfg - 1) + i):n_vt + 2 * (p * (n_cfg - 1) + i) + 2] for i in range(n_cfg - 1)]
            for p in range(SLABS_PER_STEP)]
    s_scrs = scrs[n_vt + 2 * n_hand:]
    blocks, tiles = _attn_plan(seq)
    order = _cfg_order()
    slabs = [slice(p * LANES, (p + 1) * LANES) for p in range(SLABS_PER_STEP)]

    @pl.when(pl.program_id(1) == 0)
    def _():
        for p in range(SLABS_PER_STEP):
            first_head = (pl.program_id(0) * SLABS_PER_STEP + p) * HEADS_PER_STEP
            for t, (kind, dil) in enumerate(tiles):
                bias_scr[p * len(tiles) + t] = _bias_tile(kind, dil, first_head)

    for p in range(SLABS_PER_STEP):
        for c in order:
            vt_scrs[p][c][0, LANES:LANES + BF16_ROWS, :] = _ones_rows(seq)
            for i in range(seq // LANES):
                chunk = slice(i * LANES, (i + 1) * LANES)
                vt_scrs[p][c][0, 0:LANES, chunk] = v_refs[c][0, chunk, slabs[p]].T

    zero = jnp.minimum(pl.program_id(1), 0)
    work = [(p, blk) for p in range(SLABS_PER_STEP) for blk in blocks]
    for t in range(len(work) + SCORE_LAG):
        if t < len(work):
            p, blk = work[t]
            _score_block(blk, slabs[p], q_refs[blk.cfg], k_refs[blk.cfg], bias_scr.at[p * len(tiles) + blk.tile],
                         s_scrs[t % SCORE_SLOTS])
        if t >= SCORE_LAG:
            u = t - SCORE_LAG
            p, blk = work[u]
            n = order.index(blk.cfg)
            _value_block(blk, slabs[p], zero, vt_scrs[p][blk.cfg], s_scrs[u % SCORE_SLOTS],
                         hand[p][n - 1] if n > 0 else None, hand[p][n] if n + 1 < n_cfg else None, a_ref)


def _attn(q, k, v):
    b, s, w = k[0].shape
    n_cfg = len(DILATED_CONFIGS)
    assert all(s % (dil * Q_BLOCK) == 0 for _, dil in DILATED_CONFIGS)
    n_tiles = len(_attn_plan(s)[1])
    blk = pl.BlockSpec((1, s, SLABS_PER_STEP * LANES), lambda j, i: (i, 0, j))
    return pl.pallas_call(
        functools.partial(_attn_kernel, seq=s),
        grid=(w // (SLABS_PER_STEP * LANES), b),
        in_specs=[blk] * (3 * n_cfg),
        out_specs=blk,
        out_shape=jax.ShapeDtypeStruct((b, s, w), BF16),
        scratch_shapes=[
            pltpu.VMEM((SLABS_PER_STEP * n_tiles, KEY_WINDOW, HEADS_PER_STEP * Q_BLOCK), F32),
        ] + [pltpu.VMEM((1, LANES + BF16_ROWS, s), BF16)] * (SLABS_PER_STEP * n_cfg)
        + [pltpu.VMEM((s, LANES), F32)] * (2 * SLABS_PER_STEP * (n_cfg - 1))
        + [pltpu.VMEM((1, KEY_WINDOW, HEADS_PER_STEP * Q_BLOCK), F32)] * SCORE_SLOTS,
        compiler_params=pltpu.CompilerParams(
            dimension_semantics=("arbitrary", "arbitrary"), vmem_limit_bytes=VMEM_LIMIT_BYTES),
        name="attn",
    )(*q, *k, *v)


def _mix_kernel(a_ref, act_ref, km_ref, vtm_ref, x_ref, ws_ref, bs_ref, wo_ref, gf_ref, o_ref):
    tm = x_ref.shape[1]
    gated = _mix_branches(pl.ds(0, tm), a_ref, act_ref, km_ref, vtm_ref, ws_ref, bs_ref)
    for lo in range(0, tm, PROJECT_ROWS):
        _mix_project(pl.ds(lo, PROJECT_ROWS), [g[lo:lo + PROJECT_ROWS] for g in gated], x_ref, wo_ref, gf_ref, o_ref)


def _mix_branches(rows, a_ref, act_ref, km_ref, vtm_ref, ws_ref, bs_ref):
    tm = rows.size
    lo = 0
    cols = []
    for width in ACT_SPLITS:
        cols.append(slice(lo, lo + width))
        lo += width
    sa, gu, vn, qm, sm = (act_ref[0, rows, c] for c in cols)

    ga = sa * a_ref[0, rows, :]

    group = lax.broadcasted_iota(jnp.int32, (SGU_CHUNK, SGU_WIDTH), 1) // SGU_GROUP
    mixed = []
    for c in range(tm // SGU_CHUNK):
        vc = vn[c * SGU_CHUNK:(c + 1) * SGU_CHUNK]
        stacked = jnp.concatenate([jnp.where(group == g, vc, jnp.zeros_like(vc)) for g in range(N_SGU_GROUPS)],
                                  axis=0)
        mixed.append(jnp.dot(ws_ref[...], stacked, preferred_element_type=F32) + bs_ref[...])
    gb = (gu.astype(F32) * jnp.concatenate(mixed, axis=0)).astype(BF16)

    mhead = lax.broadcasted_iota(jnp.int32, (tm, MEM_WIDTH), 1) // MEM_HEAD_DIM
    q4 = jnp.concatenate([jnp.where(mhead == h, qm, jnp.zeros_like(qm)) for h in range(N_MEM_HEADS)], axis=0)
    s = lax.dot_general(km_ref[0], q4, (((1,), (1,)), ((), ())), preferred_element_type=F32)
    p = jnp.exp2(s - jnp.max(s, axis=0, keepdims=True)).astype(BF16)
    ones = vtm_ref[0, MEM_WIDTH:, :]
    mo = []
    for h in range(N_MEM_HEADS):
        vt = jnp.concatenate([vtm_ref[0, h * MEM_HEAD_DIM:(h + 1) * MEM_HEAD_DIM, :], ones], axis=0)
        ot = jnp.dot(vt, p[:, h * tm:(h + 1) * tm], preferred_element_type=F32)
        mo.append(ot[0:MEM_HEAD_DIM] * (1.0 / ot[MEM_HEAD_DIM:MEM_HEAD_DIM + 1]))
    mo = jnp.concatenate(mo, axis=0).T
    gm = (sm.astype(F32) * mo).astype(BF16)
    return ga, gb, gm


def _mix_project(rows, gated, x_ref, wo_ref, gf_ref, o_ref):
    ga, gb, gm = gated
    e1 = ATTN_WIDTH
    e2 = e1 + SGU_WIDTH
    y = jnp.dot(ga, wo_ref[0:e1, :], preferred_element_type=F32)
    y = y + jnp.dot(gb, wo_ref[e1:e2, :], preferred_element_type=F32)
    y = y + jnp.dot(gm, wo_ref[e2:, :], preferred_element_type=F32)
    o_ref[0, rows, :] = _rms(x_ref[0, rows, :] + y, gf_ref[...])


def _mix(a, act, k_mem, vt_mem, x, w_s, b_tile, w_out, g_final):
    b, s, d = x.shape
    tile = lambda w: pl.BlockSpec((1, MIX_TILE, w), lambda i, j: (i, j, 0))
    per_batch = lambda arr: pl.BlockSpec((1,) + arr.shape[1:], lambda i, j: (i, 0, 0))
    fixed = lambda arr: pl.BlockSpec(arr.shape, lambda i, j: (0, 0))
    return pl.pallas_call(
        _mix_kernel,
        grid=(b, s // MIX_TILE),
        in_specs=[tile(a.shape[2]), tile(act.shape[2]), per_batch(k_mem), per_batch(vt_mem), tile(d),
                  fixed(w_s), fixed(b_tile), fixed(w_out), fixed(g_final)],
        out_specs=tile(d),
        out_shape=jax.ShapeDtypeStruct((b, s, d), x.dtype),
        compiler_params=pltpu.CompilerParams(
            dimension_semantics=("arbitrary", "arbitrary"), vmem_limit_bytes=VMEM_LIMIT_BYTES),
        name="mix",
    )(a, act, k_mem, vt_mem, x, w_s, b_tile, w_out, g_final)


def kernel(x, mem, g_norm, w_in, w_sgu_spatial, b_sgu_spatial, g_sgu_v, g_mem, w_mem_kv, w_out, g_final):
    assert g_norm.shape[0] == 1, "the final norm is fused into the single layer's last kernel"
    b, s, d = x.shape
    k_mem, vt_mem = _memkv(mem, g_mem[0][None, :], w_mem_kv[0].astype(BF16))
    q, k, v, act = _inproj(x, g_norm[0][None, :], g_sgu_v[0][None, :], w_in[0].astype(BF16))
    a = _attn(q, k, v)
    w_s = jnp.concatenate(list(w_sgu_spatial[0].astype(BF16)), axis=1)
    b_tile = jnp.repeat(b_sgu_spatial[0].T, SGU_GROUP, axis=1)
    return _mix(a, act, k_mem, vt_mem, x, w_s, b_tile, w_out[0].astype(BF16), g_final[None, :])
```

```python
import functools
import math
from typing import NamedTuple

import jax
import jax.numpy as jnp
from jax import lax
from jax.experimental import pallas as pl
from jax.experimental.pallas import tpu as pltpu

F32 = jnp.float32
BF16 = jnp.bfloat16

EPS = 1e-6
NEG_INF = -1e30
LOG2_E = math.log2(math.e)

HEAD_DIM = 64
N_ATTN_HEADS = 8
ATTN_WIDTH = HEAD_DIM * N_ATTN_HEADS
DILATED_CONFIGS = ((128, 1), (512, 4), (2048, 16))
RADIUS = 64
SGU_WIDTH = 256
N_SGU_GROUPS = 4
SGU_GROUP = SGU_WIDTH // N_SGU_GROUPS
SGU_CHUNK = 128
MEM_WIDTH = 256
N_MEM_HEADS = 4
MEM_HEAD_DIM = MEM_WIDTH // N_MEM_HEADS
ACT_SPLITS = (ATTN_WIDTH, SGU_WIDTH, SGU_WIDTH, MEM_WIDTH, MEM_WIDTH)
ACT_WIDTH = sum(ACT_SPLITS)

LANES = 128
BF16_ROWS = 16
MXU_COLS = 256
Q_BLOCK = 128
KEY_WINDOW = 2 * Q_BLOCK
HEADS_PER_SLAB = LANES // HEAD_DIM
SLABS_PER_STEP = 1
SCORE_LAG = 4
SCORE_SLOTS = 2 * SCORE_LAG
ROW_TILE = 1024
ORDER_BUFFERS = 4
EMIT_ROWS = 1024
ACT_ROWS = 256
MIX_TILE = 1024
PROJECT_ROWS = 256
VMEM_LIMIT_BYTES = 56 * 1024 * 1024


def _rms(x, g):
    return x * lax.rsqrt(jnp.mean(x * x, axis=-1, keepdims=True) + EPS) * g


def _silu(x):
    half = 0.5 * x
    return half + half * jnp.tanh(half)


def _ones_rows(cols):
    first = lax.broadcasted_iota(jnp.int32, (BF16_ROWS, cols), 0) == 0
    return jnp.where(first, 1.0, 0.0).astype(BF16)


def _emit_orders(val, row0, outs, lo, order_scrs):
    rows = val.shape[0]
    n_slabs = MXU_COLS // LANES

    def write(ref, r, slabs):
        n = slabs[0].shape[0]
        at = row0 * n // rows
        ref[r, at:at + n, lo:lo + MXU_COLS] = jnp.concatenate(slabs, axis=1).astype(BF16)

    by_dil = sorted(range(len(DILATED_CONFIGS)), key=lambda c: DILATED_CONFIGS[c][1])
    prev = 1
    for level, c in enumerate(by_dil):
        dil = DILATED_CONFIGS[c][1]
        if dil == 1:
            slabs = [val[:, i * LANES:(i + 1) * LANES] for i in range(n_slabs)]
            write(outs[c], 0, slabs)
            for scr, x in zip(order_scrs, slabs):
                scr[0] = x
            continue
        step, cls_prev, cls = dil // prev, rows // prev, rows // dil
        for r in range(dil):
            slabs = [scr[level - 1, pl.ds((r % prev) * cls_prev + r // prev, cls, stride=step), :]
                     for scr in order_scrs]
            write(outs[c], r, slabs)
            if level + 1 < len(by_dil):
                for scr, x in zip(order_scrs, slabs):
                    scr[level, r * cls:(r + 1) * cls, :] = x
        prev = dil


def _inproj_kernel(x_ref, g_ref, gv_ref, w_ref, *refs):
    n_cfg = len(DILATED_CONFIGS)
    q_refs, k_refs, v_refs = refs[0:n_cfg], refs[n_cfg:2 * n_cfg], refs[2 * n_cfg:3 * n_cfg]
    act_ref = refs[3 * n_cfg]
    order_scrs = refs[3 * n_cfg + 1:]
    h = _rms(x_ref[0], g_ref[...]).astype(BF16)

    n_slabs = MXU_COLS // LANES
    at_scr = 0
    for t, (out_refs, scale) in enumerate(((q_refs, HEAD_DIM ** -0.5 * LOG2_E), (k_refs, None), (v_refs, None))):
        for piece in range(ATTN_WIDTH // MXU_COLS):
            cols = slice(t * ATTN_WIDTH + piece * MXU_COLS, t * ATTN_WIDTH + (piece + 1) * MXU_COLS)
            for r0 in range(0, h.shape[0], EMIT_ROWS):
                val = jnp.dot(h[r0:r0 + EMIT_ROWS], w_ref[:, cols], preferred_element_type=F32)
                if scale is not None:
                    val = val * scale
                _emit_orders(val, r0, out_refs, piece * MXU_COLS, order_scrs[at_scr:at_scr + n_slabs])
                at_scr = (at_scr + n_slabs) % len(order_scrs)
    for r0 in range(0, h.shape[0], ACT_ROWS):
        hp = h[r0:r0 + ACT_ROWS]
        lo = 3 * ATTN_WIDTH
        raw = []
        for width in (ATTN_WIDTH, SGU_WIDTH, SGU_WIDTH, SGU_WIDTH, MEM_WIDTH, MEM_WIDTH):
            raw.append(jnp.dot(hp, w_ref[:, lo:lo + width], preferred_element_type=F32))
            lo += width
        za, ub, vb, zb, qm, zm = raw
        acts = (_silu(za), _silu(zb) * jax.nn.gelu(ub), _rms(jax.nn.gelu(vb), gv_ref[...]),
                qm * (MEM_HEAD_DIM ** -0.5 * LOG2_E), _silu(zm))
        lo = 0
        for width, val in zip(ACT_SPLITS, acts):
            act_ref[0, r0:r0 + ACT_ROWS, lo:lo + width] = val.astype(BF16)
            lo += width


def _inproj(x, g_norm, g_v, w_in):
    b, s, d = x.shape
    cols = w_in.shape[1]
    assert cols == 4 * ATTN_WIDTH + 3 * SGU_WIDTH + 2 * MEM_WIDTH
    tile = lambda w: pl.BlockSpec((1, ROW_TILE, w), lambda i, j: (i, j, 0))
    fixed = lambda i, j: (0, 0)
    dils = [dil for _, dil in DILATED_CONFIGS]
    assert all(EMIT_ROWS % (dil * BF16_ROWS) == 0 for dil in dils) and ROW_TILE % EMIT_ROWS == 0
    ordered = lambda w: [pl.BlockSpec((dil, ROW_TILE // dil, w), lambda i, j: (i, j, 0)) for dil in dils]
    shaped = lambda w: [jax.ShapeDtypeStruct((b * dil, s // dil, w), BF16) for dil in dils]
    widths = (ATTN_WIDTH, ATTN_WIDTH, ATTN_WIDTH)
    outs = pl.pallas_call(
        _inproj_kernel,
        grid=(b, s // ROW_TILE),
        in_specs=[
            tile(d),
            pl.BlockSpec((1, d), fixed),
            pl.BlockSpec((1, SGU_WIDTH), fixed),
            pl.BlockSpec((d, cols), fixed, pipeline_mode=pl.Buffered(1)),
        ],
        out_specs=[spec for w in widths for spec in ordered(w)] + [tile(ACT_WIDTH)],
        out_shape=[shape for w in widths for shape in shaped(w)] + [jax.ShapeDtypeStruct((b, s, ACT_WIDTH), BF16)],
        scratch_shapes=[pltpu.VMEM((len(dils) - 1, EMIT_ROWS, LANES), F32)] * ORDER_BUFFERS,
        compiler_params=pltpu.CompilerParams(
            dimension_semantics=("arbitrary", "arbitrary"), vmem_limit_bytes=VMEM_LIMIT_BYTES),
        name="inproj",
    )(x, g_norm, g_v, w_in)
    n = len(dils)
    q, k, v = ([o.reshape(b, s, -1) for o in outs[i * n:(i + 1) * n]] for i in range(3))
    return q, k, v, outs[3 * n]


SHIFT, EDGE, WHOLE = "shift", "edge", "whole"


def _bias_tile(kind, dilation, first_head):
    shape = (KEY_WINDOW, HEADS_PER_SLAB * Q_BLOCK)
    key = lax.broadcasted_iota(jnp.int32, shape, 0)
    col = lax.broadcasted_iota(jnp.int32, shape, 1)
    second = col >= Q_BLOCK
    qi = jnp.where(second, col - Q_BLOCK, col)
    if kind == SHIFT:
        rel = jnp.abs(qi + RADIUS - key)
        valid = rel <= RADIUS
    elif kind == WHOLE:
        rel = jnp.abs(qi - key)
        valid = rel <= RADIUS
    else:
        upper = key >= Q_BLOCK
        rel = jnp.abs(qi - jnp.where(upper, key - Q_BLOCK, key))
        valid = (rel <= RADIUS) & (upper == (qi >= RADIUS))
    head = (first_head + jnp.where(second, 1, 0)).astype(F32)
    slope = jnp.exp2(-8.0 * (head + 1.0) / N_ATTN_HEADS)
    dist = (rel * dilation).astype(F32)
    return jnp.where(valid, -slope * dist * LOG2_E, NEG_INF)


class _Block(NamedTuple):
    cfg: int
    tile: int
    q_rows: tuple
    k_rows: tuple
    out_rows: tuple
    out_stride: int


def _cfg_order():
    return sorted(range(len(DILATED_CONFIGS)), key=lambda c: -DILATED_CONFIGS[c][1])


def _attn_plan(seq):
    blocks, tiles = [], []
    order = _cfg_order()
    for n, c in enumerate(order):
        dil = DILATED_CONFIGS[c][1]
        nxt = DILATED_CONFIGS[order[n + 1]][1] if n + 1 < len(order) else 1
        step = dil // nxt
        cls = seq // dil
        if cls == Q_BLOCK:
            tiles.append((WHOLE, dil))
            whole = len(tiles) - 1
        else:
            tiles.append((SHIFT, dil))
            tiles.append((EDGE, dil))
            shift, edge = len(tiles) - 2, len(tiles) - 1
        for r in range(dil):
            base = r * cls
            sink = (r % nxt) * (seq // nxt) + r // nxt
            if cls == Q_BLOCK:
                blocks.append(_Block(c, whole, ((base, Q_BLOCK),), ((base, Q_BLOCK),), ((sink, Q_BLOCK),), step))
                continue
            last = cls - RADIUS
            blocks.append(_Block(c, edge, ((base, RADIUS), (base + last, RADIUS)),
                                 ((base, Q_BLOCK), (base + cls - Q_BLOCK, Q_BLOCK)),
                                 ((sink, RADIUS), (sink + step * last, RADIUS)), step))
            for j in range(cls // Q_BLOCK - 1):
                u = RADIUS + j * Q_BLOCK
                blocks.append(_Block(c, shift, ((base + u, Q_BLOCK),), ((base + j * Q_BLOCK, KEY_WINDOW),),
                                     ((sink + step * u, Q_BLOCK),), step))
    return blocks, tiles


def _rows(ref, lead, ranges, lanes=slice(None)):
    parts = [ref[lead + (pl.ds(s, n), lanes)] for s, n in ranges]
    return parts[0] if len(parts) == 1 else jnp.concatenate(parts, axis=0)


def _score_block(blk, lanes, q_ref, k_ref, bias, s_scr):
    q = _rows(q_ref, (0,), blk.q_rows, lanes)
    first = lax.broadcasted_iota(jnp.int32, (1, LANES), 1) < HEAD_DIM
    zeros = jnp.zeros_like(q)
    q2 = jnp.concatenate([jnp.where(first, q, zeros), jnp.where(first, zeros, q)], axis=0)
    k_win = _rows(k_ref, (0,), blk.k_rows, lanes)
    window = k_win.shape[0]
    s = lax.dot_general(k_win, q2, (((1,), (1,)), ((), ())), preferred_element_type=F32)
    s_scr[0, 0:window, :] = s + bias[0:window, :]


def _value_block(blk, lanes, zero, vt_scr, s_scr, prev, sink, a_ref):
    window = sum(n for _, n in blk.k_rows)
    ms, ps = [], []
    for h in range(HEADS_PER_SLAB):
        cols = slice(h * Q_BLOCK, (h + 1) * Q_BLOCK)
        s = s_scr[zero, 0:window, cols]
        m = jnp.max(s, axis=0, keepdims=True)
        ms.append(m)
        ps.append(jnp.exp2(s - m).astype(BF16))
    vt = jnp.concatenate([vt_scr[zero, :, s0:s0 + n] for s0, n in blk.k_rows], axis=1)
    ot = jnp.dot(vt, jnp.concatenate(ps, axis=1), preferred_element_type=F32)
    outs, lses = [], []
    for h in range(HEADS_PER_SLAB):
        cols = slice(h * Q_BLOCK, (h + 1) * Q_BLOCK)
        l = ot[LANES:LANES + 1, cols]
        outs.append(ot[h * HEAD_DIM:(h + 1) * HEAD_DIM, cols] * (1.0 / l))
        lses.append(ms[h] + jnp.log2(l))
    o = jnp.concatenate(outs, axis=0).T
    lse = jnp.concatenate([jnp.broadcast_to(x, (HEAD_DIM, Q_BLOCK)) for x in lses], axis=0).T
    if prev is not None:
        o_prev, lse_prev = (_rows(ref, (), blk.q_rows) for ref in prev)
        if sink is None:
            w = 1.0 / (1.0 + jnp.exp2(lse_prev - lse))
            o = o_prev + w * (o - o_prev)
        else:
            top = jnp.maximum(lse, lse_prev)
            e, e_prev = jnp.exp2(lse - top), jnp.exp2(lse_prev - top)
            den = e + e_prev
            o = (e * o + e_prev * o_prev) * (1.0 / den)
            lse = top + jnp.log2(den)
    at = 0
    for start, n in blk.out_rows:
        dst = pl.ds(start, n) if blk.out_stride == 1 else pl.ds(start, n, stride=blk.out_stride)
        if sink is None:
            a_ref[0, dst, lanes] = o[at:at + n].astype(a_ref.dtype)
        else:
            sink[0][dst, :] = o[at:at + n]
            sink[1][dst, :] = lse[at:at + n]
        at += n


def _attn_kernel(*refs, seq):
    n_cfg = len(DILATED_CONFIGS)
    q_refs, k_refs, v_refs = refs[0:n_cfg], refs[n_cfg:2 * n_cfg], refs[2 * n_cfg:3 * n_cfg]
    a_ref, bias_scr = refs[3 * n_cfg:3 * n_cfg + 2]
    scrs = refs[3 * n_cfg + 2:]
    n_vt, n_hand = SLABS_PER_STEP * n_cfg, SLABS_PER_STEP * (n_cfg - 1)
    vt_scrs = [scrs[p * n_cfg:(p + 1) * n_cfg] for p in range(SLABS_PER_STEP)]
    hand = [[scrs[n_vt + 2 * (p * (n_cfg - 1) + i):n_vt + 2 * (p * (n_cfg - 1) + i) + 2] for i in range(n_cfg - 1)]
            for p in range(SLABS_PER_STEP)]
    s_scrs = scrs[n_vt + 2 * n_hand:]
    blocks, tiles = _attn_plan(seq)
    order = _cfg_order()
    slabs = [slice(p * LANES, (p + 1) * LANES) for p in range(SLABS_PER_STEP)]

    @pl.when(pl.program_id(1) == 0)
    def _():
        for p in range(SLABS_PER_STEP):
            first_head = (pl.program_id(0) * SLABS_PER_STEP + p) * HEADS_PER_SLAB
            for t, (kind, dil) in enumerate(tiles):
                bias_scr[p * len(tiles) + t] = _bias_tile(kind, dil, first_head)

    for p in range(SLABS_PER_STEP):
        for c in order:
            vt_scrs[p][c][0, LANES:LANES + BF16_ROWS, :] = _ones_rows(seq)
            for i in range(seq // LANES):
                chunk = slice(i * LANES, (i + 1) * LANES)
                vt_scrs[p][c][0, 0:LANES, chunk] = v_refs[c][0, chunk, slabs[p]].T

    zero = jnp.minimum(pl.program_id(1), 0)
    work = [(p, blk) for p in range(SLABS_PER_STEP) for blk in blocks]
    for t in range(len(work) + SCORE_LAG):
        if t < len(work):
            p, blk = work[t]
            _score_block(blk, slabs[p], q_refs[blk.cfg], k_refs[blk.cfg], bias_scr.at[p * len(tiles) + blk.tile],
                         s_scrs[t % SCORE_SLOTS])
        if t >= SCORE_LAG:
            u = t - SCORE_LAG
            p, blk = work[u]
            n = order.index(blk.cfg)
            _value_block(blk, slabs[p], zero, vt_scrs[p][blk.cfg], s_scrs[u % SCORE_SLOTS],
                         hand[p][n - 1] if n > 0 else None, hand[p][n] if n + 1 < n_cfg else None, a_ref)


def _attn(q, k, v):
    b, s, w = k[0].shape
    n_cfg = len(DILATED_CONFIGS)
    assert all(s % (dil * Q_BLOCK) == 0 for _, dil in DILATED_CONFIGS)
    n_tiles = len(_attn_plan(s)[1])
    blk = pl.BlockSpec((1, s, SLABS_PER_STEP * LANES), lambda j, i: (i, 0, j))
    return pl.pallas_call(
        functools.partial(_attn_kernel, seq=s),
        grid=(w // (SLABS_PER_STEP * LANES), b),
        in_specs=[blk] * (3 * n_cfg),
        out_specs=blk,
        out_shape=jax.ShapeDtypeStruct((b, s, w), BF16),
        scratch_shapes=[
            pltpu.VMEM((SLABS_PER_STEP * n_tiles, KEY_WINDOW, HEADS_PER_SLAB * Q_BLOCK), F32),
        ] + [pltpu.VMEM((1, LANES + BF16_ROWS, s), BF16)] * (SLABS_PER_STEP * n_cfg)
        + [pltpu.VMEM((s, LANES), F32)] * (2 * SLABS_PER_STEP * (n_cfg - 1))
        + [pltpu.VMEM((1, KEY_WINDOW, HEADS_PER_SLAB * Q_BLOCK), F32)] * SCORE_SLOTS,
        compiler_params=pltpu.CompilerParams(
            dimension_semantics=("arbitrary", "arbitrary"), vmem_limit_bytes=VMEM_LIMIT_BYTES),
        name="attn",
    )(*q, *k, *v)


def _mix_kernel(a_ref, act_ref, mem_ref, gm_ref, wkv_ref, x_ref, ws_ref, bs_ref, wo_ref, gf_ref, o_ref,
                km_ref, vtm_ref):
    @pl.when(pl.program_id(1) == 0)
    def _():
        hm = _rms(mem_ref[0], gm_ref[...]).astype(BF16)
        kv = jnp.dot(hm, wkv_ref[...], preferred_element_type=F32)
        km_ref[...] = kv[:, 0:MEM_WIDTH].astype(BF16)
        vtm_ref[0:MEM_WIDTH, :] = kv[:, MEM_WIDTH:].T.astype(BF16)
        vtm_ref[MEM_WIDTH:, :] = _ones_rows(kv.shape[0])

    tm = x_ref.shape[1]
    gated = _mix_branches(pl.ds(0, tm), a_ref, act_ref, km_ref, vtm_ref, ws_ref, bs_ref)
    for lo in range(0, tm, PROJECT_ROWS):
        _mix_project(pl.ds(lo, PROJECT_ROWS), [g[lo:lo + PROJECT_ROWS] for g in gated], x_ref, wo_ref, gf_ref, o_ref)


def _mix_branches(rows, a_ref, act_ref, km_ref, vtm_ref, ws_ref, bs_ref):
    tm = rows.size
    lo = 0
    cols = []
    for width in ACT_SPLITS:
        cols.append(slice(lo, lo + width))
        lo += width
    sa, gu, vn, qm, sm = (act_ref[0, rows, c] for c in cols)

    ga = sa * a_ref[0, rows, :]

    group = lax.broadcasted_iota(jnp.int32, (SGU_CHUNK, SGU_WIDTH), 1) // SGU_GROUP
    mixed = []
    for c in range(tm // SGU_CHUNK):
        vc = vn[c * SGU_CHUNK:(c + 1) * SGU_CHUNK]
        stacked = jnp.concatenate([jnp.where(group == g, vc, jnp.zeros_like(vc)) for g in range(N_SGU_GROUPS)],
                                  axis=0)
        mixed.append(jnp.dot(ws_ref[...], stacked, preferred_element_type=F32) + bs_ref[...])
    gb = (gu.astype(F32) * jnp.concatenate(mixed, axis=0)).astype(BF16)

    mhead = lax.broadcasted_iota(jnp.int32, (tm, MEM_WIDTH), 1) // MEM_HEAD_DIM
    q4 = jnp.concatenate([jnp.where(mhead == h, qm, jnp.zeros_like(qm)) for h in range(N_MEM_HEADS)], axis=0)
    s = lax.dot_general(km_ref[...], q4, (((1,), (1,)), ((), ())), preferred_element_type=F32)
    p = jnp.exp2(s - jnp.max(s, axis=0, keepdims=True)).astype(BF16)
    ones = vtm_ref[MEM_WIDTH:, :]
    mo = []
    for h in range(N_MEM_HEADS):
        vt = jnp.concatenate([vtm_ref[h * MEM_HEAD_DIM:(h + 1) * MEM_HEAD_DIM, :], ones], axis=0)
        ot = jnp.dot(vt, p[:, h * tm:(h + 1) * tm], preferred_element_type=F32)
        mo.append(ot[0:MEM_HEAD_DIM] * (1.0 / ot[MEM_HEAD_DIM:MEM_HEAD_DIM + 1]))
    mo = jnp.concatenate(mo, axis=0).T
    gm = (sm.astype(F32) * mo).astype(BF16)
    return ga, gb, gm


def _mix_project(rows, gated, x_ref, wo_ref, gf_ref, o_ref):
    ga, gb, gm = gated
    e1 = ATTN_WIDTH
    e2 = e1 + SGU_WIDTH
    y = jnp.dot(ga, wo_ref[0:e1, :], preferred_element_type=F32)
    y = y + jnp.dot(gb, wo_ref[e1:e2, :], preferred_element_type=F32)
    y = y + jnp.dot(gm, wo_ref[e2:, :], preferred_element_type=F32)
    o_ref[0, rows, :] = _rms(x_ref[0, rows, :] + y, gf_ref[...])


def _mix(a, act, mem, g_mem, w_kv, x, w_s, b_tile, w_out, g_final):
    b, s, d = x.shape
    m = mem.shape[1]
    tile = lambda w: pl.BlockSpec((1, MIX_TILE, w), lambda i, j: (i, j, 0))
    fixed = lambda arr: pl.BlockSpec(arr.shape, lambda i, j: (0, 0))
    return pl.pallas_call(
        _mix_kernel,
        grid=(b, s // MIX_TILE),
        in_specs=[tile(a.shape[2]), tile(act.shape[2]), pl.BlockSpec((1, m, d), lambda i, j: (i, 0, 0)),
                  fixed(g_mem), fixed(w_kv), tile(d), fixed(w_s), fixed(b_tile), fixed(w_out), fixed(g_final)],
        out_specs=tile(d),
        out_shape=jax.ShapeDtypeStruct((b, s, d), x.dtype),
        scratch_shapes=[
            pltpu.VMEM((m, MEM_WIDTH), BF16),
            pltpu.VMEM((MEM_WIDTH + BF16_ROWS, m), BF16),
        ],
        compiler_params=pltpu.CompilerParams(
            dimension_semantics=("arbitrary", "arbitrary"), vmem_limit_bytes=VMEM_LIMIT_BYTES),
        name="mix",
    )(a, act, mem, g_mem, w_kv, x, w_s, b_tile, w_out, g_final)


def kernel(x, mem, g_norm, w_in, w_sgu_spatial, b_sgu_spatial, g_sgu_v, g_mem, w_mem_kv, w_out, g_final):
    assert g_norm.shape[0] == 1, "the final norm is fused into the single layer's last kernel"
    b, s, d = x.shape
    q, k, v, act = _inproj(x, g_norm[0][None, :], g_sgu_v[0][None, :], w_in[0].astype(BF16))
    a = _attn(q, k, v)
    w_s = jnp.concatenate(list(w_sgu_spatial[0].astype(BF16)), axis=1)
    b_tile = jnp.repeat(b_sgu_spatial[0].T, SGU_GROUP, axis=1)
    return _mix(a, act, mem, g_mem[0][None, :], w_mem_kv[0].astype(BF16), x, w_s, b_tile, w_out[0].astype(BF16),
                g_final[None, :])
```

```python
import functools
import math
from typing import NamedTuple

import jax
import jax.numpy as jnp
from jax import lax
from jax.experimental import pallas as pl
from jax.experimental.pallas import tpu as pltpu

F32 = jnp.float32
BF16 = jnp.bfloat16

EPS = 1e-6
NEG_INF = -1e30
LOG2_E = math.log2(math.e)

HEAD_DIM = 64
N_ATTN_HEADS = 8
ATTN_WIDTH = HEAD_DIM * N_ATTN_HEADS
DILATED_CONFIGS = ((128, 1), (512, 4), (2048, 16))
RADIUS = 64
SGU_WIDTH = 256
N_SGU_GROUPS = 4
SGU_GROUP = SGU_WIDTH // N_SGU_GROUPS
SGU_CHUNK = 128
MEM_WIDTH = 256
N_MEM_HEADS = 4
MEM_HEAD_DIM = MEM_WIDTH // N_MEM_HEADS
ACT_SPLITS = (ATTN_WIDTH, SGU_WIDTH, SGU_WIDTH, MEM_WIDTH, MEM_WIDTH)
ACT_WIDTH = sum(ACT_SPLITS)

LANES = 128
BF16_ROWS = 16
MXU_COLS = 256
Q_BLOCK = 128
KEY_WINDOW = 2 * Q_BLOCK
HEADS_PER_SLAB = LANES // HEAD_DIM
SLABS_PER_STEP = 1
SCORE_LAG = 4
SCORE_SLOTS = 2 * SCORE_LAG
ROW_TILE = 1024
ORDER_BUFFERS = 6
EMIT_ROWS = 1024
ACT_ROWS = 256
MIX_TILE = 1024
PROJECT_ROWS = 256
VMEM_LIMIT_BYTES = 56 * 1024 * 1024


def _rms(x, g):
    return x * lax.rsqrt(jnp.mean(x * x, axis=-1, keepdims=True) + EPS) * g


def _silu(x):
    half = 0.5 * x
    return half + half * jnp.tanh(half)


def _ones_rows(cols):
    first = lax.broadcasted_iota(jnp.int32, (BF16_ROWS, cols), 0) == 0
    return jnp.where(first, 1.0, 0.0).astype(BF16)


def _emit_orders(val, row0, outs, lo, order_scrs):
    rows = val.shape[0]
    n_slabs = MXU_COLS // LANES

    def write(ref, r, slabs):
        n = slabs[0].shape[0]
        at = row0 * n // rows
        ref[r, at:at + n, lo:lo + MXU_COLS] = jnp.concatenate(slabs, axis=1).astype(BF16)

    by_dil = sorted(range(len(DILATED_CONFIGS)), key=lambda c: DILATED_CONFIGS[c][1])
    prev = 1
    for level, c in enumerate(by_dil):
        dil = DILATED_CONFIGS[c][1]
        if dil == 1:
            slabs = [val[:, i * LANES:(i + 1) * LANES] for i in range(n_slabs)]
            write(outs[c], 0, slabs)
            for scr, x in zip(order_scrs, slabs):
                scr[0] = x
            continue
        step, cls_prev, cls = dil // prev, rows // prev, rows // dil
        for r in range(dil):
            slabs = [scr[level - 1, pl.ds((r % prev) * cls_prev + r // prev, cls, stride=step), :]
                     for scr in order_scrs]
            write(outs[c], r, slabs)
            if level + 1 < len(by_dil):
                for scr, x in zip(order_scrs, slabs):
                    scr[level, r * cls:(r + 1) * cls, :] = x
        prev = dil


def _inproj_kernel(x_ref, g_ref, gv_ref, w_ref, *refs):
    n_cfg = len(DILATED_CONFIGS)
    q_refs, k_refs, v_refs = refs[0:n_cfg], refs[n_cfg:2 * n_cfg], refs[2 * n_cfg:3 * n_cfg]
    act_ref = refs[3 * n_cfg]
    order_scrs = refs[3 * n_cfg + 1:]
    h = _rms(x_ref[0], g_ref[...]).astype(BF16)

    n_slabs = MXU_COLS // LANES
    at_scr = 0
    for t, (out_refs, scale) in enumerate(((q_refs, HEAD_DIM ** -0.5 * LOG2_E), (k_refs, None), (v_refs, None))):
        for piece in range(ATTN_WIDTH // MXU_COLS):
            cols = slice(t * ATTN_WIDTH + piece * MXU_COLS, t * ATTN_WIDTH + (piece + 1) * MXU_COLS)
            for r0 in range(0, h.shape[0], EMIT_ROWS):
                val = jnp.dot(h[r0:r0 + EMIT_ROWS], w_ref[:, cols], preferred_element_type=F32)
                if scale is not None:
                    val = val * scale
                _emit_orders(val, r0, out_refs, piece * MXU_COLS, order_scrs[at_scr:at_scr + n_slabs])
                at_scr = (at_scr + n_slabs) % len(order_scrs)
    for r0 in range(0, h.shape[0], ACT_ROWS):
        hp = h[r0:r0 + ACT_ROWS]
        lo = 3 * ATTN_WIDTH
        raw = []
        for width in (ATTN_WIDTH, SGU_WIDTH, SGU_WIDTH, SGU_WIDTH, MEM_WIDTH, MEM_WIDTH):
            raw.append(jnp.dot(hp, w_ref[:, lo:lo + width], preferred_element_type=F32))
            lo += width
        za, ub, vb, zb, qm, zm = raw
        acts = (_silu(za), _silu(zb) * jax.nn.gelu(ub), _rms(jax.nn.gelu(vb), gv_ref[...]),
                qm * (MEM_HEAD_DIM ** -0.5 * LOG2_E), _silu(zm))
        lo = 0
        for width, val in zip(ACT_SPLITS, acts):
            act_ref[0, r0:r0 + ACT_ROWS, lo:lo + width] = val.astype(BF16)
            lo += width


def _inproj(x, g_norm, g_v, w_in):
    b, s, d = x.shape
    cols = w_in.shape[1]
    assert cols == 4 * ATTN_WIDTH + 3 * SGU_WIDTH + 2 * MEM_WIDTH
    tile = lambda w: pl.BlockSpec((1, ROW_TILE, w), lambda i, j: (i, j, 0))
    fixed = lambda i, j: (0, 0)
    dils = [dil for _, dil in DILATED_CONFIGS]
    assert all(EMIT_ROWS % (dil * BF16_ROWS) == 0 for dil in dils) and ROW_TILE % EMIT_ROWS == 0
    ordered = lambda w: [pl.BlockSpec((dil, ROW_TILE // dil, w), lambda i, j: (i, j, 0)) for dil in dils]
    shaped = lambda w: [jax.ShapeDtypeStruct((b * dil, s // dil, w), BF16) for dil in dils]
    widths = (ATTN_WIDTH, ATTN_WIDTH, ATTN_WIDTH)
    outs = pl.pallas_call(
        _inproj_kernel,
        grid=(b, s // ROW_TILE),
        in_specs=[
            tile(d),
            pl.BlockSpec((1, d), fixed),
            pl.BlockSpec((1, SGU_WIDTH), fixed),
            pl.BlockSpec((d, cols), fixed, pipeline_mode=pl.Buffered(1)),
        ],
        out_specs=[spec for w in widths for spec in ordered(w)] + [tile(ACT_WIDTH)],
        out_shape=[shape for w in widths for shape in shaped(w)] + [jax.ShapeDtypeStruct((b, s, ACT_WIDTH), BF16)],
        scratch_shapes=[pltpu.VMEM((len(dils) - 1, EMIT_ROWS, LANES), F32)] * ORDER_BUFFERS,
        compiler_params=pltpu.CompilerParams(
            dimension_semantics=("arbitrary", "arbitrary"), vmem_limit_bytes=VMEM_LIMIT_BYTES),
        name="inproj",
    )(x, g_norm, g_v, w_in)
    n = len(dils)
    q, k, v = ([o.reshape(b, s, -1) for o in outs[i * n:(i + 1) * n]] for i in range(3))
    return q, k, v, outs[3 * n]


SHIFT, EDGE, WHOLE = "shift", "edge", "whole"


def _bias_tile(kind, dilation, first_head):
    shape = (KEY_WINDOW, HEADS_PER_SLAB * Q_BLOCK)
    key = lax.broadcasted_iota(jnp.int32, shape, 0)
    col = lax.broadcasted_iota(jnp.int32, shape, 1)
    second = col >= Q_BLOCK
    qi = jnp.where(second, col - Q_BLOCK, col)
    if kind == SHIFT:
        rel = jnp.abs(qi + RADIUS - key)
        valid = rel <= RADIUS
    elif kind == WHOLE:
        rel = jnp.abs(qi - key)
        valid = rel <= RADIUS
    else:
        upper = key >= Q_BLOCK
        rel = jnp.abs(qi - jnp.where(upper, key - Q_BLOCK, key))
        valid = (rel <= RADIUS) & (upper == (qi >= RADIUS))
    head = (first_head + jnp.where(second, 1, 0)).astype(F32)
    slope = jnp.exp2(-8.0 * (head + 1.0) / N_ATTN_HEADS)
    dist = (rel * dilation).astype(F32)
    return jnp.where(valid, -slope * dist * LOG2_E, NEG_INF)


class _Block(NamedTuple):
    cfg: int
    tile: int
    q_rows: tuple
    k_rows: tuple
    out_rows: tuple
    out_stride: int


def _cfg_order():
    return sorted(range(len(DILATED_CONFIGS)), key=lambda c: -DILATED_CONFIGS[c][1])


def _attn_plan(seq):
    blocks, tiles = [], []
    order = _cfg_order()
    for n, c in enumerate(order):
        dil = DILATED_CONFIGS[c][1]
        nxt = DILATED_CONFIGS[order[n + 1]][1] if n + 1 < len(order) else 1
        step = dil // nxt
        cls = seq // dil
        if cls == Q_BLOCK:
            tiles.append((WHOLE, dil))
            whole = len(tiles) - 1
        else:
            tiles.append((SHIFT, dil))
            tiles.append((EDGE, dil))
            shift, edge = len(tiles) - 2, len(tiles) - 1
        for r in range(dil):
            base = r * cls
            sink = (r % nxt) * (seq // nxt) + r // nxt
            if cls == Q_BLOCK:
                blocks.append(_Block(c, whole, ((base, Q_BLOCK),), ((base, Q_BLOCK),), ((sink, Q_BLOCK),), step))
                continue
            last = cls - RADIUS
            blocks.append(_Block(c, edge, ((base, RADIUS), (base + last, RADIUS)),
                                 ((base, Q_BLOCK), (base + cls - Q_BLOCK, Q_BLOCK)),
                                 ((sink, RADIUS), (sink + step * last, RADIUS)), step))
            for j in range(cls // Q_BLOCK - 1):
                u = RADIUS + j * Q_BLOCK
                blocks.append(_Block(c, shift, ((base + u, Q_BLOCK),), ((base + j * Q_BLOCK, KEY_WINDOW),),
                                     ((sink + step * u, Q_BLOCK),), step))
    return blocks, tiles


def _rows(ref, lead, ranges, lanes=slice(None)):
    parts = [ref[lead + (pl.ds(s, n), lanes)] for s, n in ranges]
    return parts[0] if len(parts) == 1 else jnp.concatenate(parts, axis=0)


def _score_block(blk, lanes, q_ref, k_ref, bias, s_scr):
    q = _rows(q_ref, (0,), blk.q_rows, lanes)
    first = lax.broadcasted_iota(jnp.int32, (1, LANES), 1) < HEAD_DIM
    zeros = jnp.zeros_like(q)
    q2 = jnp.concatenate([jnp.where(first, q, zeros), jnp.where(first, zeros, q)], axis=0)
    k_win = _rows(k_ref, (0,), blk.k_rows, lanes)
    window = k_win.shape[0]
    s = lax.dot_general(k_win, q2, (((1,), (1,)), ((), ())), preferred_element_type=F32)
    s_scr[0, 0:window, :] = s + bias[0:window, :]


def _value_block(blk, lanes, zero, vt_scr, s_scr, prev, sink, a_ref):
    window = sum(n for _, n in blk.k_rows)
    ms, ps = [], []
    for h in range(HEADS_PER_SLAB):
        cols = slice(h * Q_BLOCK, (h + 1) * Q_BLOCK)
        s = s_scr[zero, 0:window, cols]
        m = jnp.max(s, axis=0, keepdims=True)
        ms.append(m)
        ps.append(jnp.exp2(s - m).astype(BF16))
    vt = jnp.concatenate([vt_scr[zero, :, s0:s0 + n] for s0, n in blk.k_rows], axis=1)
    ot = jnp.dot(vt, jnp.concatenate(ps, axis=1), preferred_element_type=F32)
    outs, lses = [], []
    for h in range(HEADS_PER_SLAB):
        cols = slice(h * Q_BLOCK, (h + 1) * Q_BLOCK)
        l = ot[LANES:LANES + 1, cols]
        outs.append(ot[h * HEAD_DIM:(h + 1) * HEAD_DIM, cols] * (1.0 / l))
        lses.append(ms[h] + jnp.log2(l))
    o = jnp.concatenate(outs, axis=0).T
    lse = jnp.concatenate([jnp.broadcast_to(x, (HEAD_DIM, Q_BLOCK)) for x in lses], axis=0).T
    if prev is not None:
        o_prev, lse_prev = (_rows(ref, (), blk.q_rows) for ref in prev)
        if sink is None:
            w = 1.0 / (1.0 + jnp.exp2(lse_prev - lse))
            o = o_prev + w * (o - o_prev)
        else:
            top = jnp.maximum(lse, lse_prev)
            e, e_prev = jnp.exp2(lse - top), jnp.exp2(lse_prev - top)
            den = e + e_prev
            o = (e * o + e_prev * o_prev) * (1.0 / den)
            lse = top + jnp.log2(den)
    at = 0
    for start, n in blk.out_rows:
        dst = pl.ds(start, n) if blk.out_stride == 1 else pl.ds(start, n, stride=blk.out_stride)
        if sink is None:
            a_ref[0, dst, lanes] = o[at:at + n].astype(a_ref.dtype)
        else:
            sink[0][dst, :] = o[at:at + n]
            sink[1][dst, :] = lse[at:at + n]
        at += n


def _attn_kernel(*refs, seq):
    n_cfg = len(DILATED_CONFIGS)
    q_refs, k_refs, v_refs = refs[0:n_cfg], refs[n_cfg:2 * n_cfg], refs[2 * n_cfg:3 * n_cfg]
    a_ref, bias_scr = refs[3 * n_cfg:3 * n_cfg + 2]
    scrs = refs[3 * n_cfg + 2:]
    n_vt, n_hand = SLABS_PER_STEP * n_cfg, SLABS_PER_STEP * (n_cfg - 1)
    vt_scrs = [scrs[p * n_cfg:(p + 1) * n_cfg] for p in range(SLABS_PER_STEP)]
    hand = [[scrs[n_vt + 2 * (p * (n_cfg - 1) + i):n_vt + 2 * (p * (n_cfg - 1) + i) + 2] for i in range(n_cfg - 1)]
            for p in range(SLABS_PER_STEP)]
    s_scrs = scrs[n_vt + 2 * n_hand:]
    blocks, tiles = _attn_plan(seq)
    order = _cfg_order()
    slabs = [slice(p * LANES, (p + 1) * LANES) for p in range(SLABS_PER_STEP)]

    @pl.when(pl.program_id(1) == 0)
    def _():
        for p in range(SLABS_PER_STEP):
            first_head = (pl.program_id(0) * SLABS_PER_STEP + p) * HEADS_PER_SLAB
            for t, (kind, dil) in enumerate(tiles):
                bias_scr[p * len(tiles) + t] = _bias_tile(kind, dil, first_head)

    for p in range(SLABS_PER_STEP):
        for c in order:
            vt_scrs[p][c][0, LANES:LANES + BF16_ROWS, :] = _ones_rows(seq)
            for i in range(seq // LANES):
                chunk = slice(i * LANES, (i + 1) * LANES)
                vt_scrs[p][c][0, 0:LANES, chunk] = v_refs[c][0, chunk, slabs[p]].T

    zero = jnp.minimum(pl.program_id(1), 0)
    work = [(p, blk) for p in range(SLABS_PER_STEP) for blk in blocks]
    for t in range(len(work) + SCORE_LAG):
        if t < len(work):
            p, blk = work[t]
            _score_block(blk, slabs[p], q_refs[blk.cfg], k_refs[blk.cfg], bias_scr.at[p * len(tiles) + blk.tile],
                         s_scrs[t % SCORE_SLOTS])
        if t >= SCORE_LAG:
            u = t - SCORE_LAG
            p, blk = work[u]
            n = order.index(blk.cfg)
            _value_block(blk, slabs[p], zero, vt_scrs[p][blk.cfg], s_scrs[u % SCORE_SLOTS],
                         hand[p][n - 1] if n > 0 else None, hand[p][n] if n + 1 < n_cfg else None, a_ref)


def _attn(q, k, v):
    b, s, w = k[0].shape
    n_cfg = len(DILATED_CONFIGS)
    assert all(s % (dil * Q_BLOCK) == 0 for _, dil in DILATED_CONFIGS)
    n_tiles = len(_attn_plan(s)[1])
    blk = pl.BlockSpec((1, s, SLABS_PER_STEP * LANES), lambda j, i: (i, 0, j))
    return pl.pallas_call(
        functools.partial(_attn_kernel, seq=s),
        grid=(w // (SLABS_PER_STEP * LANES), b),
        in_specs=[blk] * (3 * n_cfg),
        out_specs=blk,
        out_shape=jax.ShapeDtypeStruct((b, s, w), BF16),
        scratch_shapes=[
            pltpu.VMEM((SLABS_PER_STEP * n_tiles, KEY_WINDOW, HEADS_PER_SLAB * Q_BLOCK), F32),
        ] + [pltpu.VMEM((1, LANES + BF16_ROWS, s), BF16)] * (SLABS_PER_STEP * n_cfg)
        + [pltpu.VMEM((s, LANES), F32)] * (2 * SLABS_PER_STEP * (n_cfg - 1))
        + [pltpu.VMEM((1, KEY_WINDOW, HEADS_PER_SLAB * Q_BLOCK), F32)] * SCORE_SLOTS,
        compiler_params=pltpu.CompilerParams(
            dimension_semantics=("arbitrary", "arbitrary"), vmem_limit_bytes=VMEM_LIMIT_BYTES),
        name="attn",
    )(*q, *k, *v)


def _mix_kernel(a_ref, act_ref, mem_ref, gm_ref, wkv_ref, x_ref, ws_ref, bs_ref, wo_ref, gf_ref, o_ref,
                km_ref, vtm_ref):
    @pl.when(pl.program_id(1) == 0)
    def _():
        hm = _rms(mem_ref[0], gm_ref[...]).astype(BF16)
        kv = jnp.dot(hm, wkv_ref[...], preferred_element_type=F32)
        km_ref[...] = kv[:, 0:MEM_WIDTH].astype(BF16)
        vtm_ref[0:MEM_WIDTH, :] = kv[:, MEM_WIDTH:].T.astype(BF16)
        vtm_ref[MEM_WIDTH:, :] = _ones_rows(kv.shape[0])

    tm = x_ref.shape[1]
    gated = _mix_branches(pl.ds(0, tm), a_ref, act_ref, km_ref, vtm_ref, ws_ref, bs_ref)
    for lo in range(0, tm, PROJECT_ROWS):
        _mix_project(pl.ds(lo, PROJECT_ROWS), [g[lo:lo + PROJECT_ROWS] for g in gated], x_ref, wo_ref, gf_ref, o_ref)


def _mix_branches(rows, a_ref, act_ref, km_ref, vtm_ref, ws_ref, bs_ref):
    tm = rows.size
    lo = 0
    cols = []
    for width in ACT_SPLITS:
        cols.append(slice(lo, lo + width))
        lo += width
    sa, gu, vn, qm, sm = (act_ref[0, rows, c] for c in cols)

    ga = sa * a_ref[0, rows, :]

    group = lax.broadcasted_iota(jnp.int32, (SGU_CHUNK, SGU_WIDTH), 1) // SGU_GROUP
    mixed = []
    for c in range(tm // SGU_CHUNK):
        vc = vn[c * SGU_CHUNK:(c + 1) * SGU_CHUNK]
        stacked = jnp.concatenate([jnp.where(group == g, vc, jnp.zeros_like(vc)) for g in range(N_SGU_GROUPS)],
                                  axis=0)
        mixed.append(jnp.dot(ws_ref[...], stacked, preferred_element_type=F32) + bs_ref[...])
    gb = (gu.astype(F32) * jnp.concatenate(mixed, axis=0)).astype(BF16)

    mhead = lax.broadcasted_iota(jnp.int32, (tm, MEM_WIDTH), 1) // MEM_HEAD_DIM
    q4 = jnp.concatenate([jnp.where(mhead == h, qm, jnp.zeros_like(qm)) for h in range(N_MEM_HEADS)], axis=0)
    s = lax.dot_general(km_ref[...], q4, (((1,), (1,)), ((), ())), preferred_element_type=F32)
    p = jnp.exp2(s - jnp.max(s, axis=0, keepdims=True)).astype(BF16)
    ones = vtm_ref[MEM_WIDTH:, :]
    mo = []
    for h in range(N_MEM_HEADS):
        vt = jnp.concatenate([vtm_ref[h * MEM_HEAD_DIM:(h + 1) * MEM_HEAD_DIM, :], ones], axis=0)
        ot = jnp.dot(vt, p[:, h * tm:(h + 1) * tm], preferred_element_type=F32)
        mo.append(ot[0:MEM_HEAD_DIM] * (1.0 / ot[MEM_HEAD_DIM:MEM_HEAD_DIM + 1]))
    mo = jnp.concatenate(mo, axis=0).T
    gm = (sm.astype(F32) * mo).astype(BF16)
    return ga, gb, gm


def _mix_project(rows, gated, x_ref, wo_ref, gf_ref, o_ref):
    ga, gb, gm = gated
    e1 = ATTN_WIDTH
    e2 = e1 + SGU_WIDTH
    y = jnp.dot(ga, wo_ref[0:e1, :], preferred_element_type=F32)
    y = y + jnp.dot(gb, wo_ref[e1:e2, :], preferred_element_type=F32)
    y = y + jnp.dot(gm, wo_ref[e2:, :], preferred_element_type=F32)
    o_ref[0, rows, :] = _rms(x_ref[0, rows, :] + y, gf_ref[...])


def _mix(a, act, mem, g_mem, w_kv, x, w_s, b_tile, w_out, g_final):
    b, s, d = x.shape
    m = mem.shape[1]
    tile = lambda w: pl.BlockSpec((1, MIX_TILE, w), lambda i, j: (i, j, 0))
    fixed = lambda arr: pl.BlockSpec(arr.shape, lambda i, j: (0, 0))
    return pl.pallas_call(
        _mix_kernel,
        grid=(b, s // MIX_TILE),
        in_specs=[tile(a.shape[2]), tile(act.shape[2]), pl.BlockSpec((1, m, d), lambda i, j: (i, 0, 0)),
                  fixed(g_mem), fixed(w_kv), tile(d), fixed(w_s), fixed(b_tile), fixed(w_out), fixed(g_final)],
        out_specs=tile(d),
        out_shape=jax.ShapeDtypeStruct((b, s, d), x.dtype),
        scratch_shapes=[
            pltpu.VMEM((m, MEM_WIDTH), BF16),
            pltpu.VMEM((MEM_WIDTH + BF16_ROWS, m), BF16),
        ],
        compiler_params=pltpu.CompilerParams(
            dimension_semantics=("arbitrary", "arbitrary"), vmem_limit_bytes=VMEM_LIMIT_BYTES),
        name="mix",
    )(a, act, mem, g_mem, w_kv, x, w_s, b_tile, w_out, g_final)


def kernel(x, mem, g_norm, w_in, w_sgu_spatial, b_sgu_spatial, g_sgu_v, g_mem, w_mem_kv, w_out, g_final):
    assert g_norm.shape[0] == 1, "the final norm is fused into the single layer's last kernel"
    b, s, d = x.shape
    q, k, v, act = _inproj(x, g_norm[0][None, :], g_sgu_v[0][None, :], w_in[0].astype(BF16))
    a = _attn(q, k, v)
    w_s = jnp.concatenate(list(w_sgu_spatial[0].astype(BF16)), axis=1)
    b_tile = jnp.repeat(b_sgu_spatial[0].T, SGU_GROUP, axis=1)
    return _mix(a, act, mem, g_mem[0][None, :], w_mem_kv[0].astype(BF16), x, w_s, b_tile, w_out[0].astype(BF16),
                g_final[None, :])
```

```python
import functools
import math
from typing import NamedTuple

import jax
import jax.numpy as jnp
from jax import lax
from jax.experimental import pallas as pl
from jax.experimental.pallas import tpu as pltpu

F32 = jnp.float32
BF16 = jnp.bfloat16

EPS = 1e-6
NEG_INF = -1e30
LOG2_E = math.log2(math.e)

HEAD_DIM = 64
N_ATTN_HEADS = 8
ATTN_WIDTH = HEAD_DIM * N_ATTN_HEADS
DILATED_CONFIGS = ((128, 1), (512, 4), (2048, 16))
RADIUS = 64
SGU_WIDTH = 256
N_SGU_GROUPS = 4
SGU_GROUP = SGU_WIDTH // N_SGU_GROUPS
SGU_CHUNK = 128
MEM_WIDTH = 256
N_MEM_HEADS = 4
MEM_HEAD_DIM = MEM_WIDTH // N_MEM_HEADS
ACT_SPLITS = (ATTN_WIDTH, SGU_WIDTH, SGU_WIDTH, MEM_WIDTH, MEM_WIDTH)
ACT_WIDTH = sum(ACT_SPLITS)

LANES = 128
BF16_ROWS = 16
MXU_COLS = 256
Q_BLOCK = 128
KEY_WINDOW = 2 * Q_BLOCK
HEADS_PER_SLAB = LANES // HEAD_DIM
SLABS_PER_STEP = 1
SCORE_LAG = 4
SCORE_SLOTS = 2 * SCORE_LAG
ROW_TILE = 1024
ORDER_BUFFERS = 6
EMIT_ROWS = 1024
ACT_ROWS = 256
MIX_TILE = 1024
PROJECT_ROWS = 256
VMEM_LIMIT_BYTES = 56 * 1024 * 1024


def _rms(x, g):
    return x * lax.rsqrt(jnp.mean(x * x, axis=-1, keepdims=True) + EPS) * g


def _silu(x):
    half = 0.5 * x
    return half + half * jnp.tanh(half)


def _ones_rows(cols):
    first = lax.broadcasted_iota(jnp.int32, (BF16_ROWS, cols), 0) == 0
    return jnp.where(first, 1.0, 0.0).astype(BF16)


def _emit_orders(val, row0, outs, lo, order_scrs):
    rows = val.shape[0]
    n_slabs = MXU_COLS // LANES

    def write(ref, r, slabs):
        n = slabs[0].shape[0]
        at = row0 * n // rows
        ref[r, at:at + n, lo:lo + MXU_COLS] = jnp.concatenate(slabs, axis=1).astype(BF16)

    by_dil = sorted(range(len(DILATED_CONFIGS)), key=lambda c: DILATED_CONFIGS[c][1])
    prev = 1
    for level, c in enumerate(by_dil):
        dil = DILATED_CONFIGS[c][1]
        if dil == 1:
            slabs = [val[:, i * LANES:(i + 1) * LANES] for i in range(n_slabs)]
            write(outs[c], 0, slabs)
            for scr, x in zip(order_scrs, slabs):
                scr[0] = x
            continue
        step, cls_prev, cls = dil // prev, rows // prev, rows // dil
        for r in range(dil):
            slabs = [scr[level - 1, pl.ds((r % prev) * cls_prev + r // prev, cls, stride=step), :]
                     for scr in order_scrs]
            write(outs[c], r, slabs)
            if level + 1 < len(by_dil):
                for scr, x in zip(order_scrs, slabs):
                    scr[level, r * cls:(r + 1) * cls, :] = x
        prev = dil


def _inproj_kernel(x_ref, g_ref, gv_ref, w_ref, *refs):
    n_cfg = len(DILATED_CONFIGS)
    q_refs, k_refs, v_refs = refs[0:n_cfg], refs[n_cfg:2 * n_cfg], refs[2 * n_cfg:3 * n_cfg]
    act_ref = refs[3 * n_cfg]
    order_scrs = refs[3 * n_cfg + 1:]
    h = _rms(x_ref[0], g_ref[...]).astype(BF16)

    n_slabs = MXU_COLS // LANES
    at_scr = 0
    for t, (out_refs, scale) in enumerate(((q_refs, HEAD_DIM ** -0.5 * LOG2_E), (k_refs, None), (v_refs, None))):
        for piece in range(ATTN_WIDTH // MXU_COLS):
            cols = slice(t * ATTN_WIDTH + piece * MXU_COLS, t * ATTN_WIDTH + (piece + 1) * MXU_COLS)
            for r0 in range(0, h.shape[0], EMIT_ROWS):
                val = jnp.dot(h[r0:r0 + EMIT_ROWS], w_ref[:, cols], preferred_element_type=F32)
                if scale is not None:
                    val = val * scale
                _emit_orders(val, r0, out_refs, piece * MXU_COLS, order_scrs[at_scr:at_scr + n_slabs])
                at_scr = (at_scr + n_slabs) % len(order_scrs)
    for r0 in range(0, h.shape[0], ACT_ROWS):
        hp = h[r0:r0 + ACT_ROWS]
        lo = 3 * ATTN_WIDTH
        raw = []
        for width in (ATTN_WIDTH, SGU_WIDTH, SGU_WIDTH, SGU_WIDTH, MEM_WIDTH, MEM_WIDTH):
            raw.append(jnp.dot(hp, w_ref[:, lo:lo + width], preferred_element_type=F32))
            lo += width
        za, ub, vb, zb, qm, zm = raw
        acts = (_silu(za), _silu(zb) * jax.nn.gelu(ub), _rms(jax.nn.gelu(vb), gv_ref[...]),
                qm * (MEM_HEAD_DIM ** -0.5 * LOG2_E), _silu(zm))
        lo = 0
        for width, val in zip(ACT_SPLITS, acts):
            act_ref[0, r0:r0 + ACT_ROWS, lo:lo + width] = val.astype(BF16)
            lo += width


def _inproj(x, g_norm, g_v, w_in):
    b, s, d = x.shape
    cols = w_in.shape[1]
    assert cols == 4 * ATTN_WIDTH + 3 * SGU_WIDTH + 2 * MEM_WIDTH
    tile = lambda w: pl.BlockSpec((1, ROW_TILE, w), lambda i, j: (i, j, 0))
    fixed = lambda i, j: (0, 0)
    dils = [dil for _, dil in DILATED_CONFIGS]
    assert all(EMIT_ROWS % (dil * BF16_ROWS) == 0 for dil in dils) and ROW_TILE % EMIT_ROWS == 0
    ordered = lambda w: [pl.BlockSpec((dil, ROW_TILE // dil, w), lambda i, j: (i, j, 0)) for dil in dils]
    shaped = lambda w: [jax.ShapeDtypeStruct((b * dil, s // dil, w), BF16) for dil in dils]
    widths = (ATTN_WIDTH, ATTN_WIDTH, ATTN_WIDTH)
    outs = pl.pallas_call(
        _inproj_kernel,
        grid=(b, s // ROW_TILE),
        in_specs=[
            tile(d),
            pl.BlockSpec((1, d), fixed),
            pl.BlockSpec((1, SGU_WIDTH), fixed),
            pl.BlockSpec((d, cols), fixed, pipeline_mode=pl.Buffered(1)),
        ],
        out_specs=[spec for w in widths for spec in ordered(w)] + [tile(ACT_WIDTH)],
        out_shape=[shape for w in widths for shape in shaped(w)] + [jax.ShapeDtypeStruct((b, s, ACT_WIDTH), BF16)],
        scratch_shapes=[pltpu.VMEM((len(dils) - 1, EMIT_ROWS, LANES), F32)] * ORDER_BUFFERS,
        compiler_params=pltpu.CompilerParams(
            dimension_semantics=("arbitrary", "arbitrary"), vmem_limit_bytes=VMEM_LIMIT_BYTES),
        name="inproj",
    )(x, g_norm, g_v, w_in)
    n = len(dils)
    q, k, v = ([o.reshape(b, s, -1) for o in outs[i * n:(i + 1) * n]] for i in range(3))
    return q, k, v, outs[3 * n]


SHIFT, EDGE, WHOLE = "shift", "edge", "whole"


def _bias_tile(kind, dilation, first_head):
    shape = (KEY_WINDOW, HEADS_PER_SLAB * Q_BLOCK)
    key = lax.broadcasted_iota(jnp.int32, shape, 0)
    col = lax.broadcasted_iota(jnp.int32, shape, 1)
    second = col >= Q_BLOCK
    qi = jnp.where(second, col - Q_BLOCK, col)
    if kind == SHIFT:
        rel = jnp.abs(qi + RADIUS - key)
        valid = rel <= RADIUS
    elif kind == WHOLE:
        rel = jnp.abs(qi - key)
        valid = rel <= RADIUS
    else:
        upper = key >= Q_BLOCK
        rel = jnp.abs(qi - jnp.where(upper, key - Q_BLOCK, key))
        valid = (rel <= RADIUS) & (upper == (qi >= RADIUS))
    head = (first_head + jnp.where(second, 1, 0)).astype(F32)
    slope = jnp.exp2(-8.0 * (head + 1.0) / N_ATTN_HEADS)
    dist = (rel * dilation).astype(F32)
    return jnp.where(valid, -slope * dist * LOG2_E, NEG_INF)


class _Block(NamedTuple):
    cfg: int
    tile: int
    q_rows: tuple
    k_rows: tuple
    out_rows: tuple
    out_stride: int


def _cfg_order():
    return sorted(range(len(DILATED_CONFIGS)), key=lambda c: -DILATED_CONFIGS[c][1])


def _attn_plan(seq):
    blocks, tiles = [], []
    order = _cfg_order()
    for n, c in enumerate(order):
        dil = DILATED_CONFIGS[c][1]
        nxt = DILATED_CONFIGS[order[n + 1]][1] if n + 1 < len(order) else 1
        step = dil // nxt
        cls = seq // dil
        if cls == Q_BLOCK:
            tiles.append((WHOLE, dil))
            whole = len(tiles) - 1
        else:
            tiles.append((SHIFT, dil))
            tiles.append((EDGE, dil))
            shift, edge = len(tiles) - 2, len(tiles) - 1
        for r in range(dil):
            base = r * cls
            sink = (r % nxt) * (seq // nxt) + r // nxt
            if cls == Q_BLOCK:
                blocks.append(_Block(c, whole, ((base, Q_BLOCK),), ((base, Q_BLOCK),), ((sink, Q_BLOCK),), step))
                continue
            last = cls - RADIUS
            blocks.append(_Block(c, edge, ((base, RADIUS), (base + last, RADIUS)),
                                 ((base, Q_BLOCK), (base + cls - Q_BLOCK, Q_BLOCK)),
                                 ((sink, RADIUS), (sink + step * last, RADIUS)), step))
            for j in range(cls // Q_BLOCK - 1):
                u = RADIUS + j * Q_BLOCK
                blocks.append(_Block(c, shift, ((base + u, Q_BLOCK),), ((base + j * Q_BLOCK, KEY_WINDOW),),
                                     ((sink + step * u, Q_BLOCK),), step))
    return blocks, tiles


def _rows(ref, lead, ranges, lanes=slice(None)):
    parts = [ref[lead + (pl.ds(s, n), lanes)] for s, n in ranges]
    return parts[0] if len(parts) == 1 else jnp.concatenate(parts, axis=0)


def _score_block(blk, lanes, q_ref, k_ref, bias, s_scr):
    q = _rows(q_ref, (0,), blk.q_rows, lanes)
    first = lax.broadcasted_iota(jnp.int32, (1, LANES), 1) < HEAD_DIM
    zeros = jnp.zeros_like(q)
    q2 = jnp.concatenate([jnp.where(first, q, zeros), jnp.where(first, zeros, q)], axis=0)
    k_win = _rows(k_ref, (0,), blk.k_rows, lanes)
    window = k_win.shape[0]
    s = lax.dot_general(k_win, q2, (((1,), (1,)), ((), ())), preferred_element_type=F32)
    s_scr[0, 0:window, :] = s + bias[0:window, :]


def _value_block(blk, lanes, zero, vt_scr, s_scr, prev, sink, a_ref):
    window = sum(n for _, n in blk.k_rows)
    ms, ps = [], []
    for h in range(HEADS_PER_SLAB):
        cols = slice(h * Q_BLOCK, (h + 1) * Q_BLOCK)
        s = s_scr[zero, 0:window, cols]
        m = jnp.max(s, axis=0, keepdims=True)
        ms.append(m)
        ps.append(jnp.exp2(s - m).astype(BF16))
    vt = jnp.concatenate([vt_scr[zero, :, s0:s0 + n] for s0, n in blk.k_rows], axis=1)
    ot = jnp.dot(vt, jnp.concatenate(ps, axis=1), preferred_element_type=F32)
    if prev is not None:
        o_prev, lse_prev = (_rows(ref, (), blk.q_rows) for ref in prev)
    last = prev is not None and sink is None
    if last:
        lse_back = lse_prev.T
    outs, lses, keeps = [], [], []
    for h in range(HEADS_PER_SLAB):
        cols = slice(h * Q_BLOCK, (h + 1) * Q_BLOCK)
        l = ot[LANES:LANES + 1, cols]
        lses.append(ms[h] + jnp.log2(l))
        scale = 1.0 / l
        if last:
            w = 1.0 / (1.0 + jnp.exp2(lse_back[h * HEAD_DIM:h * HEAD_DIM + 1] - lses[h]))
            scale = w * scale
            keeps.append(1.0 - w)
        outs.append(ot[h * HEAD_DIM:(h + 1) * HEAD_DIM, cols] * scale)

    def onto_rows(rows):
        return jnp.concatenate([jnp.broadcast_to(x, (HEAD_DIM, Q_BLOCK)) for x in rows], axis=0).T

    o = jnp.concatenate(outs, axis=0).T
    if last:
        o = o + onto_rows(keeps) * o_prev
    else:
        lse = onto_rows(lses)
        if prev is not None:
            top = jnp.maximum(lse, lse_prev)
            e, e_prev = jnp.exp2(lse - top), jnp.exp2(lse_prev - top)
            den = e + e_prev
            o = (e * o + e_prev * o_prev) * (1.0 / den)
            lse = top + jnp.log2(den)
    at = 0
    for start, n in blk.out_rows:
        dst = pl.ds(start, n) if blk.out_stride == 1 else pl.ds(start, n, stride=blk.out_stride)
        if sink is None:
            a_ref[0, dst, lanes] = o[at:at + n].astype(a_ref.dtype)
        else:
            sink[0][dst, :] = o[at:at + n]
            sink[1][dst, :] = lse[at:at + n]
        at += n


def _attn_kernel(*refs, seq):
    n_cfg = len(DILATED_CONFIGS)
    q_refs, k_refs, v_refs = refs[0:n_cfg], refs[n_cfg:2 * n_cfg], refs[2 * n_cfg:3 * n_cfg]
    a_ref, bias_scr = refs[3 * n_cfg:3 * n_cfg + 2]
    scrs = refs[3 * n_cfg + 2:]
    n_vt, n_hand = SLABS_PER_STEP * n_cfg, SLABS_PER_STEP * (n_cfg - 1)
    vt_scrs = [scrs[p * n_cfg:(p + 1) * n_cfg] for p in range(SLABS_PER_STEP)]
    hand = [[scrs[n_vt + 2 * (p * (n_cfg - 1) + i):n_vt + 2 * (p * (n_cfg - 1) + i) + 2] for i in range(n_cfg - 1)]
            for p in range(SLABS_PER_STEP)]
    s_scrs = scrs[n_vt + 2 * n_hand:]
    blocks, tiles = _attn_plan(seq)
    order = _cfg_order()
    slabs = [slice(p * LANES, (p + 1) * LANES) for p in range(SLABS_PER_STEP)]

    @pl.when(pl.program_id(1) == 0)
    def _():
        for p in range(SLABS_PER_STEP):
            first_head = (pl.program_id(0) * SLABS_PER_STEP + p) * HEADS_PER_SLAB
            for t, (kind, dil) in enumerate(tiles):
                bias_scr[p * len(tiles) + t] = _bias_tile(kind, dil, first_head)

    for p in range(SLABS_PER_STEP):
        for c in order:
            vt_scrs[p][c][0, LANES:LANES + BF16_ROWS, :] = _ones_rows(seq)
            for i in range(seq // LANES):
                chunk = slice(i * LANES, (i + 1) * LANES)
                vt_scrs[p][c][0, 0:LANES, chunk] = v_refs[c][0, chunk, slabs[p]].T

    zero = jnp.minimum(pl.program_id(1), 0)
    work = [(p, blk) for p in range(SLABS_PER_STEP) for blk in blocks]
    for t in range(len(work) + SCORE_LAG):
        if t < len(work):
            p, blk = work[t]
            _score_block(blk, slabs[p], q_refs[blk.cfg], k_refs[blk.cfg], bias_scr.at[p * len(tiles) + blk.tile],
                         s_scrs[t % SCORE_SLOTS])
        if t >= SCORE_LAG:
            u = t - SCORE_LAG
            p, blk = work[u]
            n = order.index(blk.cfg)
            _value_block(blk, slabs[p], zero, vt_scrs[p][blk.cfg], s_scrs[u % SCORE_SLOTS],
                         hand[p][n - 1] if n > 0 else None, hand[p][n] if n + 1 < n_cfg else None, a_ref)


def _attn(q, k, v):
    b, s, w = k[0].shape
    n_cfg = len(DILATED_CONFIGS)
    assert all(s % (dil * Q_BLOCK) == 0 for _, dil in DILATED_CONFIGS)
    n_tiles = len(_attn_plan(s)[1])
    blk = pl.BlockSpec((1, s, SLABS_PER_STEP * LANES), lambda j, i: (i, 0, j))
    return pl.pallas_call(
        functools.partial(_attn_kernel, seq=s),
        grid=(w // (SLABS_PER_STEP * LANES), b),
        in_specs=[blk] * (3 * n_cfg),
        out_specs=blk,
        out_shape=jax.ShapeDtypeStruct((b, s, w), BF16),
        scratch_shapes=[
            pltpu.VMEM((SLABS_PER_STEP * n_tiles, KEY_WINDOW, HEADS_PER_SLAB * Q_BLOCK), F32),
        ] + [pltpu.VMEM((1, LANES + BF16_ROWS, s), BF16)] * (SLABS_PER_STEP * n_cfg)
        + [pltpu.VMEM((s, LANES), F32)] * (2 * SLABS_PER_STEP * (n_cfg - 1))
        + [pltpu.VMEM((1, KEY_WINDOW, HEADS_PER_SLAB * Q_BLOCK), F32)] * SCORE_SLOTS,
        compiler_params=pltpu.CompilerParams(
            dimension_semantics=("arbitrary", "arbitrary"), vmem_limit_bytes=VMEM_LIMIT_BYTES),
        name="attn",
    )(*q, *k, *v)


def _mix_kernel(a_ref, act_ref, mem_ref, gm_ref, wkv_ref, x_ref, ws_ref, bs_ref, wo_ref, gf_ref, o_ref,
                km_ref, vtm_ref):
    @pl.when(pl.program_id(1) == 0)
    def _():
        hm = _rms(mem_ref[0], gm_ref[...]).astype(BF16)
        kv = jnp.dot(hm, wkv_ref[...], preferred_element_type=F32)
        km_ref[...] = kv[:, 0:MEM_WIDTH].astype(BF16)
        vtm_ref[0:MEM_WIDTH, :] = kv[:, MEM_WIDTH:].T.astype(BF16)
        vtm_ref[MEM_WIDTH:, :] = _ones_rows(kv.shape[0])

    tm = x_ref.shape[1]
    gated = _mix_branches(pl.ds(0, tm), a_ref, act_ref, km_ref, vtm_ref, ws_ref, bs_ref)
    for lo in range(0, tm, PROJECT_ROWS):
        _mix_project(pl.ds(lo, PROJECT_ROWS), [g[lo:lo + PROJECT_ROWS] for g in gated], x_ref, wo_ref, gf_ref, o_ref)


def _mix_branches(rows, a_ref, act_ref, km_ref, vtm_ref, ws_ref, bs_ref):
    tm = rows.size
    lo = 0
    cols = []
    for width in ACT_SPLITS:
        cols.append(slice(lo, lo + width))
        lo += width
    sa, gu, vn, qm, sm = (act_ref[0, rows, c] for c in cols)

    ga = sa * a_ref[0, rows, :]

    group = lax.broadcasted_iota(jnp.int32, (SGU_CHUNK, SGU_WIDTH), 1) // SGU_GROUP
    mixed = []
    for c in range(tm // SGU_CHUNK):
        vc = vn[c * SGU_CHUNK:(c + 1) * SGU_CHUNK]
        stacked = jnp.concatenate([jnp.where(group == g, vc, jnp.zeros_like(vc)) for g in range(N_SGU_GROUPS)],
                                  axis=0)
        mixed.append(jnp.dot(ws_ref[...], stacked, preferred_element_type=F32) + bs_ref[...])
    gb = (gu.astype(F32) * jnp.concatenate(mixed, axis=0)).astype(BF16)

    mhead = lax.broadcasted_iota(jnp.int32, (tm, MEM_WIDTH), 1) // MEM_HEAD_DIM
    q4 = jnp.concatenate([jnp.where(mhead == h, qm, jnp.zeros_like(qm)) for h in range(N_MEM_HEADS)], axis=0)
    s = lax.dot_general(km_ref[...], q4, (((1,), (1,)), ((), ())), preferred_element_type=F32)
    p = jnp.exp2(s - jnp.max(s, axis=0, keepdims=True)).astype(BF16)
    ones = vtm_ref[MEM_WIDTH:, :]
    mo = []
    for h in range(N_MEM_HEADS):
        vt = jnp.concatenate([vtm_ref[h * MEM_HEAD_DIM:(h + 1) * MEM_HEAD_DIM, :], ones], axis=0)
        ot = jnp.dot(vt, p[:, h * tm:(h + 1) * tm], preferred_element_type=F32)
        mo.append(ot[0:MEM_HEAD_DIM] * (1.0 / ot[MEM_HEAD_DIM:MEM_HEAD_DIM + 1]))
    mo = jnp.concatenate(mo, axis=0).T
    gm = (sm.astype(F32) * mo).astype(BF16)
    return ga, gb, gm


def _mix_project(rows, gated, x_ref, wo_ref, gf_ref, o_ref):
    ga, gb, gm = gated
    e1 = ATTN_WIDTH
    e2 = e1 + SGU_WIDTH
    y = jnp.dot(ga, wo_ref[0:e1, :], preferred_element_type=F32)
    y = y + jnp.dot(gb, wo_ref[e1:e2, :], preferred_element_type=F32)
    y = y + jnp.dot(gm, wo_ref[e2:, :], preferred_element_type=F32)
    o_ref[0, rows, :] = _rms(x_ref[0, rows, :] + y, gf_ref[...])


def _mix(a, act, mem, g_mem, w_kv, x, w_s, b_tile, w_out, g_final):
    b, s, d = x.shape
    m = mem.shape[1]
    tile = lambda w: pl.BlockSpec((1, MIX_TILE, w), lambda i, j: (i, j, 0))
    fixed = lambda arr: pl.BlockSpec(arr.shape, lambda i, j: (0, 0))
    return pl.pallas_call(
        _mix_kernel,
        grid=(b, s // MIX_TILE),
        in_specs=[tile(a.shape[2]), tile(act.shape[2]), pl.BlockSpec((1, m, d), lambda i, j: (i, 0, 0)),
                  fixed(g_mem), fixed(w_kv), tile(d), fixed(w_s), fixed(b_tile), fixed(w_out), fixed(g_final)],
        out_specs=tile(d),
        out_shape=jax.ShapeDtypeStruct((b, s, d), x.dtype),
        scratch_shapes=[
            pltpu.VMEM((m, MEM_WIDTH), BF16),
            pltpu.VMEM((MEM_WIDTH + BF16_ROWS, m), BF16),
        ],
        compiler_params=pltpu.CompilerParams(
            dimension_semantics=("arbitrary", "arbitrary"), vmem_limit_bytes=VMEM_LIMIT_BYTES),
        name="mix",
    )(a, act, mem, g_mem, w_kv, x, w_s, b_tile, w_out, g_final)


def kernel(x, mem, g_norm, w_in, w_sgu_spatial, b_sgu_spatial, g_sgu_v, g_mem, w_mem_kv, w_out, g_final):
    assert g_norm.shape[0] == 1, "the final norm is fused into the single layer's last kernel"
    b, s, d = x.shape
    q, k, v, act = _inproj(x, g_norm[0][None, :], g_sgu_v[0][None, :], w_in[0].astype(BF16))
    a = _attn(q, k, v)
    w_s = jnp.concatenate(list(w_sgu_spatial[0].astype(BF16)), axis=1)
    b_tile = jnp.repeat(b_sgu_spatial[0].T, SGU_GROUP, axis=1)
    return _mix(a, act, mem, g_mem[0][None, :], w_mem_kv[0].astype(BF16), x, w_s, b_tile, w_out[0].astype(BF16),
                g_final[None, :])
```

```python
import functools
import math
from typing import NamedTuple

import jax
import jax.numpy as jnp
from jax import lax
from jax.experimental import pallas as pl
from jax.experimental.pallas import tpu as pltpu

F32 = jnp.float32
BF16 = jnp.bfloat16

EPS = 1e-6
NEG_INF = -1e30
LOG2_E = math.log2(math.e)

HEAD_DIM = 64
N_ATTN_HEADS = 8
ATTN_WIDTH = HEAD_DIM * N_ATTN_HEADS
DILATED_CONFIGS = ((128, 1), (512, 4), (2048, 16))
RADIUS = 64
SGU_WIDTH = 256
N_SGU_GROUPS = 4
SGU_GROUP = SGU_WIDTH // N_SGU_GROUPS
SGU_CHUNK = 128
MEM_WIDTH = 256
N_MEM_HEADS = 4
MEM_HEAD_DIM = MEM_WIDTH // N_MEM_HEADS
ACT_SPLITS = (ATTN_WIDTH, SGU_WIDTH, SGU_WIDTH, MEM_WIDTH, MEM_WIDTH)
ACT_WIDTH = sum(ACT_SPLITS)

LANES = 128
BF16_ROWS = 16
MXU_COLS = 256
Q_BLOCK = 128
KEY_WINDOW = 2 * Q_BLOCK
HEADS_PER_SLAB = LANES // HEAD_DIM
SLABS_PER_STEP = 2
SCORE_LAG = 8
SCORE_SLOTS = 2 * SCORE_LAG
ROW_TILE = 1024
ORDER_BUFFERS = 6
EMIT_ROWS = 1024
ACT_ROWS = 256
MIX_TILE = 1024
PROJECT_ROWS = 256
MEM_SCORE_LAG = 2
VMEM_LIMIT_BYTES = 56 * 1024 * 1024


def _rms(x, g):
    return x * lax.rsqrt(jnp.mean(x * x, axis=-1, keepdims=True) + EPS) * g


def _silu(x):
    half = 0.5 * x
    return half + half * jnp.tanh(half)


def _ones_rows(cols):
    first = lax.broadcasted_iota(jnp.int32, (BF16_ROWS, cols), 0) == 0
    return jnp.where(first, 1.0, 0.0).astype(BF16)


def _emit_orders(val, row0, outs, lo, order_scrs):
    rows = val.shape[0]
    n_slabs = MXU_COLS // LANES

    def write(ref, r, slabs):
        n = slabs[0].shape[0]
        at = row0 * n // rows
        ref[r, at:at + n, lo:lo + MXU_COLS] = jnp.concatenate(slabs, axis=1).astype(BF16)

    by_dil = sorted(range(len(DILATED_CONFIGS)), key=lambda c: DILATED_CONFIGS[c][1])
    prev = 1
    for level, c in enumerate(by_dil):
        dil = DILATED_CONFIGS[c][1]
        if dil == 1:
            slabs = [val[:, i * LANES:(i + 1) * LANES] for i in range(n_slabs)]
            write(outs[c], 0, slabs)
            for scr, x in zip(order_scrs, slabs):
                scr[0] = x
            continue
        step, cls_prev, cls = dil // prev, rows // prev, rows // dil
        for r in range(dil):
            slabs = [scr[level - 1, pl.ds((r % prev) * cls_prev + r // prev, cls, stride=step), :]
                     for scr in order_scrs]
            write(outs[c], r, slabs)
            if level + 1 < len(by_dil):
                for scr, x in zip(order_scrs, slabs):
                    scr[level, r * cls:(r + 1) * cls, :] = x
        prev = dil


def _inproj_kernel(x_ref, g_ref, gv_ref, w_ref, *refs):
    n_cfg = len(DILATED_CONFIGS)
    q_refs, k_refs, v_refs = refs[0:n_cfg], refs[n_cfg:2 * n_cfg], refs[2 * n_cfg:3 * n_cfg]
    act_ref = refs[3 * n_cfg]
    order_scrs = refs[3 * n_cfg + 1:]
    h = _rms(x_ref[0], g_ref[...]).astype(BF16)

    n_slabs = MXU_COLS // LANES
    at_scr = 0
    for t, (out_refs, scale) in enumerate(((q_refs, HEAD_DIM ** -0.5 * LOG2_E), (k_refs, None), (v_refs, None))):
        for piece in range(ATTN_WIDTH // MXU_COLS):
            cols = slice(t * ATTN_WIDTH + piece * MXU_COLS, t * ATTN_WIDTH + (piece + 1) * MXU_COLS)
            for r0 in range(0, h.shape[0], EMIT_ROWS):
                val = jnp.dot(h[r0:r0 + EMIT_ROWS], w_ref[:, cols], preferred_element_type=F32)
                if scale is not None:
                    val = val * scale
                _emit_orders(val, r0, out_refs, piece * MXU_COLS, order_scrs[at_scr:at_scr + n_slabs])
                at_scr = (at_scr + n_slabs) % len(order_scrs)
    for r0 in range(0, h.shape[0], ACT_ROWS):
        hp = h[r0:r0 + ACT_ROWS]
        lo = 3 * ATTN_WIDTH
        raw = []
        for width in (ATTN_WIDTH, SGU_WIDTH, SGU_WIDTH, SGU_WIDTH, MEM_WIDTH, MEM_WIDTH):
            raw.append(jnp.dot(hp, w_ref[:, lo:lo + width], preferred_element_type=F32))
            lo += width
        za, ub, vb, zb, qm, zm = raw
        acts = (_silu(za), _silu(zb) * jax.nn.gelu(ub), _rms(jax.nn.gelu(vb), gv_ref[...]),
                qm * (MEM_HEAD_DIM ** -0.5 * LOG2_E), _silu(zm))
        lo = 0
        for width, val in zip(ACT_SPLITS, acts):
            act_ref[0, r0:r0 + ACT_ROWS, lo:lo + width] = val.astype(BF16)
            lo += width


def _inproj(x, g_norm, g_v, w_in):
    b, s, d = x.shape
    cols = w_in.shape[1]
    assert cols == 4 * ATTN_WIDTH + 3 * SGU_WIDTH + 2 * MEM_WIDTH
    tile = lambda w: pl.BlockSpec((1, ROW_TILE, w), lambda i, j: (i, j, 0))
    fixed = lambda i, j: (0, 0)
    dils = [dil for _, dil in DILATED_CONFIGS]
    assert all(EMIT_ROWS % (dil * BF16_ROWS) == 0 for dil in dils) and ROW_TILE % EMIT_ROWS == 0
    ordered = lambda w: [pl.BlockSpec((dil, ROW_TILE // dil, w), lambda i, j: (i, j, 0)) for dil in dils]
    shaped = lambda w: [jax.ShapeDtypeStruct((b * dil, s // dil, w), BF16) for dil in dils]
    widths = (ATTN_WIDTH, ATTN_WIDTH, ATTN_WIDTH)
    outs = pl.pallas_call(
        _inproj_kernel,
        grid=(b, s // ROW_TILE),
        in_specs=[
            tile(d),
            pl.BlockSpec((1, d), fixed),
            pl.BlockSpec((1, SGU_WIDTH), fixed),
            pl.BlockSpec((d, cols), fixed, pipeline_mode=pl.Buffered(1)),
        ],
        out_specs=[spec for w in widths for spec in ordered(w)] + [tile(ACT_WIDTH)],
        out_shape=[shape for w in widths for shape in shaped(w)] + [jax.ShapeDtypeStruct((b, s, ACT_WIDTH), BF16)],
        scratch_shapes=[pltpu.VMEM((len(dils) - 1, EMIT_ROWS, LANES), F32)] * ORDER_BUFFERS,
        compiler_params=pltpu.CompilerParams(
            dimension_semantics=("arbitrary", "arbitrary"), vmem_limit_bytes=VMEM_LIMIT_BYTES),
        name="inproj",
    )(x, g_norm, g_v, w_in)
    n = len(dils)
    q, k, v = ([o.reshape(b, s, -1) for o in outs[i * n:(i + 1) * n]] for i in range(3))
    return q, k, v, outs[3 * n]


SHIFT, EDGE, WHOLE = "shift", "edge", "whole"


def _bias_tile(kind, dilation, first_head):
    shape = (KEY_WINDOW, HEADS_PER_SLAB * Q_BLOCK)
    key = lax.broadcasted_iota(jnp.int32, shape, 0)
    col = lax.broadcasted_iota(jnp.int32, shape, 1)
    second = col >= Q_BLOCK
    qi = jnp.where(second, col - Q_BLOCK, col)
    if kind == SHIFT:
        rel = jnp.abs(qi + RADIUS - key)
        valid = rel <= RADIUS
    elif kind == WHOLE:
        rel = jnp.abs(qi - key)
        valid = rel <= RADIUS
    else:
        upper = key >= Q_BLOCK
        rel = jnp.abs(qi - jnp.where(upper, key - Q_BLOCK, key))
        valid = (rel <= RADIUS) & (upper == (qi >= RADIUS))
    head = (first_head + jnp.where(second, 1, 0)).astype(F32)
    slope = jnp.exp2(-8.0 * (head + 1.0) / N_ATTN_HEADS)
    dist = (rel * dilation).astype(F32)
    return jnp.where(valid, -slope * dist * LOG2_E, NEG_INF)


class _Block(NamedTuple):
    cfg: int
    tile: int
    q_rows: tuple
    k_rows: tuple
    out_rows: tuple
    out_stride: int


def _cfg_order():
    return sorted(range(len(DILATED_CONFIGS)), key=lambda c: -DILATED_CONFIGS[c][1])


def _attn_plan(seq):
    blocks, tiles = [], []
    order = _cfg_order()
    for n, c in enumerate(order):
        dil = DILATED_CONFIGS[c][1]
        nxt = DILATED_CONFIGS[order[n + 1]][1] if n + 1 < len(order) else 1
        step = dil // nxt
        cls = seq // dil
        if cls == Q_BLOCK:
            tiles.append((WHOLE, dil))
            whole = len(tiles) - 1
        else:
            tiles.append((SHIFT, dil))
            tiles.append((EDGE, dil))
            shift, edge = len(tiles) - 2, len(tiles) - 1
        for r in range(dil):
            base = r * cls
            sink = (r % nxt) * (seq // nxt) + r // nxt
            if cls == Q_BLOCK:
                blocks.append(_Block(c, whole, ((base, Q_BLOCK),), ((base, Q_BLOCK),), ((sink, Q_BLOCK),), step))
                continue
            last = cls - RADIUS
            blocks.append(_Block(c, edge, ((base, RADIUS), (base + last, RADIUS)),
                                 ((base, Q_BLOCK), (base + cls - Q_BLOCK, Q_BLOCK)),
                                 ((sink, RADIUS), (sink + step * last, RADIUS)), step))
            for j in range(cls // Q_BLOCK - 1):
                u = RADIUS + j * Q_BLOCK
                blocks.append(_Block(c, shift, ((base + u, Q_BLOCK),), ((base + j * Q_BLOCK, KEY_WINDOW),),
                                     ((sink + step * u, Q_BLOCK),), step))
    return blocks, tiles


def _rows(ref, lead, ranges, lanes=slice(None)):
    parts = [ref[lead + (pl.ds(s, n), lanes)] for s, n in ranges]
    return parts[0] if len(parts) == 1 else jnp.concatenate(parts, axis=0)


def _score_block(blk, lanes, q_ref, k_ref, bias, s_scr):
    q = _rows(q_ref, (0,), blk.q_rows, lanes)
    first = lax.broadcasted_iota(jnp.int32, (1, LANES), 1) < HEAD_DIM
    zeros = jnp.zeros_like(q)
    q2 = jnp.concatenate([jnp.where(first, q, zeros), jnp.where(first, zeros, q)], axis=0)
    k_win = _rows(k_ref, (0,), blk.k_rows, lanes)
    window = k_win.shape[0]
    s = lax.dot_general(k_win, q2, (((1,), (1,)), ((), ())), preferred_element_type=F32)
    s_scr[0, 0:window, :] = s + bias[0:window, :]


def _value_block(blk, lanes, zero, vt_scr, s_scr, prev, sink, a_ref):
    window = sum(n for _, n in blk.k_rows)
    ms, ps = [], []
    for h in range(HEADS_PER_SLAB):
        cols = slice(h * Q_BLOCK, (h + 1) * Q_BLOCK)
        s = s_scr[zero, 0:window, cols]
        m = jnp.max(s, axis=0, keepdims=True)
        ms.append(m)
        ps.append(jnp.exp2(s - m).astype(BF16))
    vt = jnp.concatenate([vt_scr[zero, :, s0:s0 + n] for s0, n in blk.k_rows], axis=1)
    ot = jnp.dot(vt, jnp.concatenate(ps, axis=1), preferred_element_type=F32)
    if prev is not None:
        o_prev, lse_prev = (_rows(ref, (), blk.q_rows) for ref in prev)
    last = prev is not None and sink is None
    if last:
        lse_back = lse_prev.T
    outs, lses, keeps = [], [], []
    for h in range(HEADS_PER_SLAB):
        cols = slice(h * Q_BLOCK, (h + 1) * Q_BLOCK)
        l = ot[LANES:LANES + 1, cols]
        lses.append(ms[h] + jnp.log2(l))
        scale = 1.0 / l
        if last:
            w = 1.0 / (1.0 + jnp.exp2(lse_back[h * HEAD_DIM:h * HEAD_DIM + 1] - lses[h]))
            scale = w * scale
            keeps.append(1.0 - w)
        outs.append(ot[h * HEAD_DIM:(h + 1) * HEAD_DIM, cols] * scale)

    def onto_rows(rows):
        return jnp.concatenate([jnp.broadcast_to(x, (HEAD_DIM, Q_BLOCK)) for x in rows], axis=0).T

    o = jnp.concatenate(outs, axis=0).T
    if last:
        o = o + onto_rows(keeps) * o_prev
    else:
        lse = onto_rows(lses)
        if prev is not None:
            top = jnp.maximum(lse, lse_prev)
            e, e_prev = jnp.exp2(lse - top), jnp.exp2(lse_prev - top)
            den = e + e_prev
            o = (e * o + e_prev * o_prev) * (1.0 / den)
            lse = top + jnp.log2(den)
    at = 0
    for start, n in blk.out_rows:
        dst = pl.ds(start, n) if blk.out_stride == 1 else pl.ds(start, n, stride=blk.out_stride)
        if sink is None:
            a_ref[0, dst, lanes] = o[at:at + n].astype(a_ref.dtype)
        else:
            sink[0][dst, :] = o[at:at + n]
            sink[1][dst, :] = lse[at:at + n]
        at += n


def _attn_kernel(*refs, seq):
    n_cfg = len(DILATED_CONFIGS)
    q_refs, k_refs, v_refs = refs[0:n_cfg], refs[n_cfg:2 * n_cfg], refs[2 * n_cfg:3 * n_cfg]
    a_ref, bias_scr = refs[3 * n_cfg:3 * n_cfg + 2]
    scrs = refs[3 * n_cfg + 2:]
    n_vt, n_hand = SLABS_PER_STEP * n_cfg, SLABS_PER_STEP * (n_cfg - 1)
    vt_scrs = [scrs[p * n_cfg:(p + 1) * n_cfg] for p in range(SLABS_PER_STEP)]
    hand = [[scrs[n_vt + 2 * (p * (n_cfg - 1) + i):n_vt + 2 * (p * (n_cfg - 1) + i) + 2] for i in range(n_cfg - 1)]
            for p in range(SLABS_PER_STEP)]
    s_scrs = scrs[n_vt + 2 * n_hand:]
    blocks, tiles = _attn_plan(seq)
    order = _cfg_order()
    slabs = [slice(p * LANES, (p + 1) * LANES) for p in range(SLABS_PER_STEP)]

    @pl.when(pl.program_id(1) == 0)
    def _():
        for p in range(SLABS_PER_STEP):
            first_head = (pl.program_id(0) * SLABS_PER_STEP + p) * HEADS_PER_SLAB
            for t, (kind, dil) in enumerate(tiles):
                bias_scr[p * len(tiles) + t] = _bias_tile(kind, dil, first_head)

    for p in range(SLABS_PER_STEP):
        for c in order:
            vt_scrs[p][c][0, LANES:LANES + BF16_ROWS, :] = _ones_rows(seq)
            for i in range(seq // LANES):
                chunk = slice(i * LANES, (i + 1) * LANES)
                vt_scrs[p][c][0, 0:LANES, chunk] = v_refs[c][0, chunk, slabs[p]].T

    zero = jnp.minimum(pl.program_id(1), 0)
    work = [(p, blk) for blk in blocks for p in range(SLABS_PER_STEP)]
    for t in range(len(work) + SCORE_LAG):
        if t < len(work):
            p, blk = work[t]
            _score_block(blk, slabs[p], q_refs[blk.cfg], k_refs[blk.cfg], bias_scr.at[p * len(tiles) + blk.tile],
                         s_scrs[t % SCORE_SLOTS])
        if t >= SCORE_LAG:
            u = t - SCORE_LAG
            p, blk = work[u]
            n = order.index(blk.cfg)
            _value_block(blk, slabs[p], zero, vt_scrs[p][blk.cfg], s_scrs[u % SCORE_SLOTS],
                         hand[p][n - 1] if n > 0 else None, hand[p][n] if n + 1 < n_cfg else None, a_ref)


def _attn(q, k, v):
    b, s, w = k[0].shape
    n_cfg = len(DILATED_CONFIGS)
    assert all(s % (dil * Q_BLOCK) == 0 for _, dil in DILATED_CONFIGS)
    n_tiles = len(_attn_plan(s)[1])
    blk = pl.BlockSpec((1, s, SLABS_PER_STEP * LANES), lambda j, i: (i, 0, j))
    return pl.pallas_call(
        functools.partial(_attn_kernel, seq=s),
        grid=(w // (SLABS_PER_STEP * LANES), b),
        in_specs=[blk] * (3 * n_cfg),
        out_specs=blk,
        out_shape=jax.ShapeDtypeStruct((b, s, w), BF16),
        scratch_shapes=[
            pltpu.VMEM((SLABS_PER_STEP * n_tiles, KEY_WINDOW, HEADS_PER_SLAB * Q_BLOCK), F32),
        ] + [pltpu.VMEM((1, LANES + BF16_ROWS, s), BF16)] * (SLABS_PER_STEP * n_cfg)
        + [pltpu.VMEM((s, LANES), F32)] * (2 * SLABS_PER_STEP * (n_cfg - 1))
        + [pltpu.VMEM((1, KEY_WINDOW, HEADS_PER_SLAB * Q_BLOCK), F32)] * SCORE_SLOTS,
        compiler_params=pltpu.CompilerParams(
            dimension_semantics=("arbitrary", "arbitrary"), vmem_limit_bytes=VMEM_LIMIT_BYTES),
        name="attn",
    )(*q, *k, *v)


def _mix_kernel(a_ref, act_ref, mem_ref, gm_ref, wkv_ref, x_ref, ws_ref, bs_ref, wo_ref, gf_ref, o_ref,
                km_ref, vtm_ref, *s_scrs):
    @pl.when(pl.program_id(1) == 0)
    def _():
        hm = _rms(mem_ref[0], gm_ref[...]).astype(BF16)
        kv = jnp.dot(hm, wkv_ref[...], preferred_element_type=F32)
        km_ref[...] = kv[:, 0:MEM_WIDTH].astype(BF16)
        vtm_ref[0:MEM_WIDTH, :] = kv[:, MEM_WIDTH:].T.astype(BF16)
        vtm_ref[MEM_WIDTH:, :] = _ones_rows(kv.shape[0])

    tm = x_ref.shape[1]
    gated = _mix_branches(pl.ds(0, tm), a_ref, act_ref, km_ref, vtm_ref, ws_ref, bs_ref, s_scrs)
    for lo in range(0, tm, PROJECT_ROWS):
        _mix_project(pl.ds(lo, PROJECT_ROWS), [g[lo:lo + PROJECT_ROWS] for g in gated], x_ref, wo_ref, gf_ref, o_ref)


def _mix_branches(rows, a_ref, act_ref, km_ref, vtm_ref, ws_ref, bs_ref, s_scrs):
    tm = rows.size
    lo = 0
    cols = []
    for width in ACT_SPLITS:
        cols.append(slice(lo, lo + width))
        lo += width
    sa, gu, vn, qm, sm = (act_ref[0, rows, c] for c in cols)

    ga = sa * a_ref[0, rows, :]

    group = lax.broadcasted_iota(jnp.int32, (SGU_CHUNK, SGU_WIDTH), 1) // SGU_GROUP
    mixed = []
    for c in range(tm // SGU_CHUNK):
        vc = vn[c * SGU_CHUNK:(c + 1) * SGU_CHUNK]
        stacked = jnp.concatenate([jnp.where(group == g, vc, jnp.zeros_like(vc)) for g in range(N_SGU_GROUPS)],
                                  axis=0)
        mixed.append(jnp.dot(ws_ref[...], stacked, preferred_element_type=F32) + bs_ref[...])
    gb = (gu.astype(F32) * jnp.concatenate(mixed, axis=0)).astype(BF16)

    mhead = lax.broadcasted_iota(jnp.int32, (tm, MEM_WIDTH), 1) // MEM_HEAD_DIM
    q4 = jnp.concatenate([jnp.where(mhead == h, qm, jnp.zeros_like(qm)) for h in range(N_MEM_HEADS)], axis=0)
    zero = jnp.minimum(pl.program_id(1), 0)
    ps = []
    for c in range(len(s_scrs) + MEM_SCORE_LAG):
        if c < len(s_scrs):
            qc = q4[c * MXU_COLS:(c + 1) * MXU_COLS]
            s_scrs[c][0] = lax.dot_general(km_ref[...], qc, (((1,), (1,)), ((), ())), preferred_element_type=F32)
        if c >= MEM_SCORE_LAG:
            s = s_scrs[c - MEM_SCORE_LAG][zero]
            ps.append(jnp.exp2(s - jnp.max(s, axis=0, keepdims=True)).astype(BF16))
    p = jnp.concatenate(ps, axis=1)
    ones = vtm_ref[MEM_WIDTH:, :]
    mo = []
    for h in range(N_MEM_HEADS):
        vt = jnp.concatenate([vtm_ref[h * MEM_HEAD_DIM:(h + 1) * MEM_HEAD_DIM, :], ones], axis=0)
        ot = jnp.dot(vt, p[:, h * tm:(h + 1) * tm], preferred_element_type=F32)
        mo.append(ot[0:MEM_HEAD_DIM] * (1.0 / ot[MEM_HEAD_DIM:MEM_HEAD_DIM + 1]))
    mo = jnp.concatenate(mo, axis=0).T
    gm = (sm.astype(F32) * mo).astype(BF16)
    return ga, gb, gm


def _mix_project(rows, gated, x_ref, wo_ref, gf_ref, o_ref):
    ga, gb, gm = gated
    e1 = ATTN_WIDTH
    e2 = e1 + SGU_WIDTH
    y = jnp.dot(ga, wo_ref[0:e1, :], preferred_element_type=F32)
    y = y + jnp.dot(gb, wo_ref[e1:e2, :], preferred_element_type=F32)
    y = y + jnp.dot(gm, wo_ref[e2:, :], preferred_element_type=F32)
    o_ref[0, rows, :] = _rms(x_ref[0, rows, :] + y, gf_ref[...])


def _mix(a, act, mem, g_mem, w_kv, x, w_s, b_tile, w_out, g_final):
    b, s, d = x.shape
    m = mem.shape[1]
    tile = lambda w: pl.BlockSpec((1, MIX_TILE, w), lambda i, j: (i, j, 0))
    fixed = lambda arr: pl.BlockSpec(arr.shape, lambda i, j: (0, 0))
    return pl.pallas_call(
        _mix_kernel,
        grid=(b, s // MIX_TILE),
        in_specs=[tile(a.shape[2]), tile(act.shape[2]), pl.BlockSpec((1, m, d), lambda i, j: (i, 0, 0)),
                  fixed(g_mem), fixed(w_kv), tile(d), fixed(w_s), fixed(b_tile), fixed(w_out), fixed(g_final)],
        out_specs=tile(d),
        out_shape=jax.ShapeDtypeStruct((b, s, d), x.dtype),
        scratch_shapes=[
            pltpu.VMEM((m, MEM_WIDTH), BF16),
            pltpu.VMEM((MEM_WIDTH + BF16_ROWS, m), BF16),
        ] + [pltpu.VMEM((1, m, MXU_COLS), F32)] * (N_MEM_HEADS * MIX_TILE // MXU_COLS),
        compiler_params=pltpu.CompilerParams(
            dimension_semantics=("arbitrary", "arbitrary"), vmem_limit_bytes=VMEM_LIMIT_BYTES),
        name="mix",
    )(a, act, mem, g_mem, w_kv, x, w_s, b_tile, w_out, g_final)


def kernel(x, mem, g_norm, w_in, w_sgu_spatial, b_sgu_spatial, g_sgu_v, g_mem, w_mem_kv, w_out, g_final):
    assert g_norm.shape[0] == 1, "the final norm is fused into the single layer's last kernel"
    b, s, d = x.shape
    q, k, v, act = _inproj(x, g_norm[0][None, :], g_sgu_v[0][None, :], w_in[0].astype(BF16))
    a = _attn(q, k, v)
    w_s = jnp.concatenate(list(w_sgu_spatial[0].astype(BF16)), axis=1)
    b_tile = jnp.repeat(b_sgu_spatial[0].T, SGU_GROUP, axis=1)
    return _mix(a, act, mem, g_mem[0][None, :], w_mem_kv[0].astype(BF16), x, w_s, b_tile, w_out[0].astype(BF16),
                g_final[None, :])
```

```python
import functools
import math
from typing import NamedTuple

import jax
import jax.numpy as jnp
from jax import lax
from jax.experimental import pallas as pl
from jax.experimental.pallas import tpu as pltpu

F32 = jnp.float32
BF16 = jnp.bfloat16

EPS = 1e-6
NEG_INF = -1e30
LOG2_E = math.log2(math.e)

HEAD_DIM = 64
N_ATTN_HEADS = 8
ATTN_WIDTH = HEAD_DIM * N_ATTN_HEADS
DILATED_CONFIGS = ((128, 1), (512, 4), (2048, 16))
RADIUS = 64
SGU_WIDTH = 256
N_SGU_GROUPS = 4
SGU_GROUP = SGU_WIDTH // N_SGU_GROUPS
SGU_CHUNK = 128
MEM_WIDTH = 256
N_MEM_HEADS = 4
MEM_HEAD_DIM = MEM_WIDTH // N_MEM_HEADS
ACT_SPLITS = (ATTN_WIDTH, SGU_WIDTH, SGU_WIDTH, MEM_WIDTH, MEM_WIDTH)
ACT_WIDTH = sum(ACT_SPLITS)

LANES = 128
BF16_ROWS = 16
MXU_COLS = 256
Q_BLOCK = 128
KEY_WINDOW = 2 * Q_BLOCK
HEADS_PER_SLAB = LANES // HEAD_DIM
SLABS_PER_STEP = 2
SCORE_LAG = 4 * SLABS_PER_STEP
SCORE_SLOTS = 2 * SCORE_LAG
ROW_TILE = 1024
ORDER_BUFFERS = 6
EMIT_ROWS = 1024
ACT_ROWS = 256
MIX_TILE = 1024
PROJECT_ROWS = 256
VMEM_LIMIT_BYTES = 56 * 1024 * 1024


def _rms(x, g):
    return x * lax.rsqrt(jnp.mean(x * x, axis=-1, keepdims=True) + EPS) * g


def _silu(x):
    half = 0.5 * x
    return half + half * jnp.tanh(half)


def _ones_rows(cols):
    first = lax.broadcasted_iota(jnp.int32, (BF16_ROWS, cols), 0) == 0
    return jnp.where(first, 1.0, 0.0).astype(BF16)


def _emit_orders(val, row0, outs, lo, order_scrs):
    rows = val.shape[0]
    n_slabs = MXU_COLS // LANES

    def write(ref, r, slabs):
        n = slabs[0].shape[0]
        at = row0 * n // rows
        ref[r, at:at + n, lo:lo + MXU_COLS] = jnp.concatenate(slabs, axis=1).astype(BF16)

    by_dil = sorted(range(len(DILATED_CONFIGS)), key=lambda c: DILATED_CONFIGS[c][1])
    prev = 1
    for level, c in enumerate(by_dil):
        dil = DILATED_CONFIGS[c][1]
        if dil == 1:
            slabs = [val[:, i * LANES:(i + 1) * LANES] for i in range(n_slabs)]
            write(outs[c], 0, slabs)
            for scr, x in zip(order_scrs, slabs):
                scr[0] = x
            continue
        step, cls_prev, cls = dil // prev, rows // prev, rows // dil
        for r in range(dil):
            slabs = [scr[level - 1, pl.ds((r % prev) * cls_prev + r // prev, cls, stride=step), :]
                     for scr in order_scrs]
            write(outs[c], r, slabs)
            if level + 1 < len(by_dil):
                for scr, x in zip(order_scrs, slabs):
                    scr[level, r * cls:(r + 1) * cls, :] = x
        prev = dil


def _inproj_kernel(x_ref, g_ref, gv_ref, w_ref, *refs):
    n_cfg = len(DILATED_CONFIGS)
    q_refs, k_refs, v_refs = refs[0:n_cfg], refs[n_cfg:2 * n_cfg], refs[2 * n_cfg:3 * n_cfg]
    act_ref = refs[3 * n_cfg]
    order_scrs = refs[3 * n_cfg + 1:]
    h = _rms(x_ref[0], g_ref[...]).astype(BF16)

    n_slabs = MXU_COLS // LANES
    at_scr = 0
    for t, (out_refs, scale) in enumerate(((q_refs, HEAD_DIM ** -0.5 * LOG2_E), (k_refs, None), (v_refs, None))):
        for piece in range(ATTN_WIDTH // MXU_COLS):
            cols = slice(t * ATTN_WIDTH + piece * MXU_COLS, t * ATTN_WIDTH + (piece + 1) * MXU_COLS)
            for r0 in range(0, h.shape[0], EMIT_ROWS):
                val = jnp.dot(h[r0:r0 + EMIT_ROWS], w_ref[:, cols], preferred_element_type=F32)
                if scale is not None:
                    val = val * scale
                _emit_orders(val, r0, out_refs, piece * MXU_COLS, order_scrs[at_scr:at_scr + n_slabs])
                at_scr = (at_scr + n_slabs) % len(order_scrs)
    for r0 in range(0, h.shape[0], ACT_ROWS):
        hp = h[r0:r0 + ACT_ROWS]
        lo = 3 * ATTN_WIDTH
        raw = []
        for width in (ATTN_WIDTH, SGU_WIDTH, SGU_WIDTH, SGU_WIDTH, MEM_WIDTH, MEM_WIDTH):
            raw.append(jnp.dot(hp, w_ref[:, lo:lo + width], preferred_element_type=F32))
            lo += width
        za, ub, vb, zb, qm, zm = raw
        acts = (_silu(za), _silu(zb) * jax.nn.gelu(ub), _rms(jax.nn.gelu(vb), gv_ref[...]),
                qm * (MEM_HEAD_DIM ** -0.5 * LOG2_E), _silu(zm))
        lo = 0
        for width, val in zip(ACT_SPLITS, acts):
            act_ref[0, r0:r0 + ACT_ROWS, lo:lo + width] = val.astype(BF16)
            lo += width


def _inproj(x, g_norm, g_v, w_in):
    b, s, d = x.shape
    cols = w_in.shape[1]
    assert cols == 4 * ATTN_WIDTH + 3 * SGU_WIDTH + 2 * MEM_WIDTH
    tile = lambda w: pl.BlockSpec((1, ROW_TILE, w), lambda i, j: (i, j, 0))
    fixed = lambda i, j: (0, 0)
    dils = [dil for _, dil in DILATED_CONFIGS]
    assert all(EMIT_ROWS % (dil * BF16_ROWS) == 0 for dil in dils) and ROW_TILE % EMIT_ROWS == 0
    ordered = lambda w: [pl.BlockSpec((dil, ROW_TILE // dil, w), lambda i, j: (i, j, 0)) for dil in dils]
    shaped = lambda w: [jax.ShapeDtypeStruct((b * dil, s // dil, w), BF16) for dil in dils]
    widths = (ATTN_WIDTH, ATTN_WIDTH, ATTN_WIDTH)
    outs = pl.pallas_call(
        _inproj_kernel,
        grid=(b, s // ROW_TILE),
        in_specs=[
            tile(d),
            pl.BlockSpec((1, d), fixed),
            pl.BlockSpec((1, SGU_WIDTH), fixed),
            pl.BlockSpec((d, cols), fixed, pipeline_mode=pl.Buffered(1)),
        ],
        out_specs=[spec for w in widths for spec in ordered(w)] + [tile(ACT_WIDTH)],
        out_shape=[shape for w in widths for shape in shaped(w)] + [jax.ShapeDtypeStruct((b, s, ACT_WIDTH), BF16)],
        scratch_shapes=[pltpu.VMEM((len(dils) - 1, EMIT_ROWS, LANES), F32)] * ORDER_BUFFERS,
        compiler_params=pltpu.CompilerParams(
            dimension_semantics=("arbitrary", "arbitrary"), vmem_limit_bytes=VMEM_LIMIT_BYTES),
        name="inproj",
    )(x, g_norm, g_v, w_in)
    n = len(dils)
    q, k, v = ([o.reshape(b, s, -1) for o in outs[i * n:(i + 1) * n]] for i in range(3))
    return q, k, v, outs[3 * n]


SHIFT, EDGE, WHOLE = "shift", "edge", "whole"


def _bias_tile(kind, dilation, first_head):
    shape = (KEY_WINDOW, HEADS_PER_SLAB * Q_BLOCK)
    key = lax.broadcasted_iota(jnp.int32, shape, 0)
    col = lax.broadcasted_iota(jnp.int32, shape, 1)
    second = col >= Q_BLOCK
    qi = jnp.where(second, col - Q_BLOCK, col)
    if kind == SHIFT:
        rel = jnp.abs(qi + RADIUS - key)
        valid = rel <= RADIUS
    elif kind == WHOLE:
        rel = jnp.abs(qi - key)
        valid = rel <= RADIUS
    else:
        upper = key >= Q_BLOCK
        rel = jnp.abs(qi - jnp.where(upper, key - Q_BLOCK, key))
        valid = (rel <= RADIUS) & (upper == (qi >= RADIUS))
    head = (first_head + jnp.where(second, 1, 0)).astype(F32)
    slope = jnp.exp2(-8.0 * (head + 1.0) / N_ATTN_HEADS)
    dist = (rel * dilation).astype(F32)
    return jnp.where(valid, -slope * dist * LOG2_E, NEG_INF)


class _Block(NamedTuple):
    cfg: int
    tile: int
    q_rows: tuple
    k_rows: tuple
    out_rows: tuple
    out_stride: int


def _cfg_order():
    return sorted(range(len(DILATED_CONFIGS)), key=lambda c: -DILATED_CONFIGS[c][1])


def _attn_plan(seq):
    blocks, tiles = [], []
    order = _cfg_order()
    for n, c in enumerate(order):
        dil = DILATED_CONFIGS[c][1]
        nxt = DILATED_CONFIGS[order[n + 1]][1] if n + 1 < len(order) else 1
        step = dil // nxt
        cls = seq // dil
        if cls == Q_BLOCK:
            tiles.append((WHOLE, dil))
            whole = len(tiles) - 1
        else:
            tiles.append((SHIFT, dil))
            tiles.append((EDGE, dil))
            shift, edge = len(tiles) - 2, len(tiles) - 1
        for r in range(dil):
            base = r * cls
            sink = (r % nxt) * (seq // nxt) + r // nxt
            if cls == Q_BLOCK:
                blocks.append(_Block(c, whole, ((base, Q_BLOCK),), ((base, Q_BLOCK),), ((sink, Q_BLOCK),), step))
                continue
            last = cls - RADIUS
            blocks.append(_Block(c, edge, ((base, RADIUS), (base + last, RADIUS)),
                                 ((base, Q_BLOCK), (base + cls - Q_BLOCK, Q_BLOCK)),
                                 ((sink, RADIUS), (sink + step * last, RADIUS)), step))
            for j in range(cls // Q_BLOCK - 1):
                u = RADIUS + j * Q_BLOCK
                blocks.append(_Block(c, shift, ((base + u, Q_BLOCK),), ((base + j * Q_BLOCK, KEY_WINDOW),),
                                     ((sink + step * u, Q_BLOCK),), step))
    return blocks, tiles


def _rows(ref, lead, ranges, lanes=slice(None)):
    parts = [ref[lead + (pl.ds(s, n), lanes)] for s, n in ranges]
    return parts[0] if len(parts) == 1 else jnp.concatenate(parts, axis=0)


def _score_block(blk, lanes, q_ref, k_ref, bias, s_scr):
    q = _rows(q_ref, (0,), blk.q_rows, lanes)
    first = lax.broadcasted_iota(jnp.int32, (1, LANES), 1) < HEAD_DIM
    zeros = jnp.zeros_like(q)
    q2 = jnp.concatenate([jnp.where(first, q, zeros), jnp.where(first, zeros, q)], axis=0)
    k_win = _rows(k_ref, (0,), blk.k_rows, lanes)
    window = k_win.shape[0]
    s = lax.dot_general(k_win, q2, (((1,), (1,)), ((), ())), preferred_element_type=F32)
    s_scr[0, 0:window, :] = s + bias[0:window, :]


def _value_block(blk, lanes, zero, vt_scr, s_scr, prev, sink, a_ref):
    window = sum(n for _, n in blk.k_rows)
    ms, ps = [], []
    for h in range(HEADS_PER_SLAB):
        cols = slice(h * Q_BLOCK, (h + 1) * Q_BLOCK)
        s = s_scr[zero, 0:window, cols]
        m = jnp.max(s, axis=0, keepdims=True)
        ms.append(m)
        ps.append(jnp.exp2(s - m).astype(BF16))
    vt = jnp.concatenate([vt_scr[zero, :, s0:s0 + n] for s0, n in blk.k_rows], axis=1)
    ot = jnp.dot(vt, jnp.concatenate(ps, axis=1), preferred_element_type=F32)
    if prev is not None:
        o_prev, lse_prev = (_rows(ref, (), blk.q_rows) for ref in prev)
    last = prev is not None and sink is None
    if last:
        lse_back = lse_prev.T
    outs, lses, keeps = [], [], []
    for h in range(HEADS_PER_SLAB):
        cols = slice(h * Q_BLOCK, (h + 1) * Q_BLOCK)
        l = ot[LANES:LANES + 1, cols]
        lses.append(ms[h] + jnp.log2(l))
        scale = 1.0 / l
        if last:
            w = 1.0 / (1.0 + jnp.exp2(lse_back[h * HEAD_DIM:h * HEAD_DIM + 1] - lses[h]))
            scale = w * scale
            keeps.append(1.0 - w)
        outs.append(ot[h * HEAD_DIM:(h + 1) * HEAD_DIM, cols] * scale)

    def onto_rows(rows):
        return jnp.concatenate([jnp.broadcast_to(x, (HEAD_DIM, Q_BLOCK)) for x in rows], axis=0).T

    o = jnp.concatenate(outs, axis=0).T
    if last:
        o = o + onto_rows(keeps) * o_prev
    else:
        lse = onto_rows(lses)
        if prev is not None:
            top = jnp.maximum(lse, lse_prev)
            e, e_prev = jnp.exp2(lse - top), jnp.exp2(lse_prev - top)
            den = e + e_prev
            o = (e * o + e_prev * o_prev) * (1.0 / den)
            lse = top + jnp.log2(den)
    at = 0
    for start, n in blk.out_rows:
        dst = pl.ds(start, n) if blk.out_stride == 1 else pl.ds(start, n, stride=blk.out_stride)
        if sink is None:
            a_ref[0, dst, lanes] = o[at:at + n].astype(a_ref.dtype)
        else:
            sink[0][dst, :] = o[at:at + n]
            sink[1][dst, :] = lse[at:at + n]
        at += n


def _attn_kernel(*refs, seq):
    n_cfg = len(DILATED_CONFIGS)
    q_refs, k_refs, v_refs = refs[0:n_cfg], refs[n_cfg:2 * n_cfg], refs[2 * n_cfg:3 * n_cfg]
    a_ref, bias_scr = refs[3 * n_cfg:3 * n_cfg + 2]
    scrs = refs[3 * n_cfg + 2:]
    n_vt, n_hand = SLABS_PER_STEP * n_cfg, SLABS_PER_STEP * (n_cfg - 1)
    vt_scrs = [scrs[p * n_cfg:(p + 1) * n_cfg] for p in range(SLABS_PER_STEP)]
    hand = [[scrs[n_vt + 2 * (p * (n_cfg - 1) + i):n_vt + 2 * (p * (n_cfg - 1) + i) + 2] for i in range(n_cfg - 1)]
            for p in range(SLABS_PER_STEP)]
    s_scrs = scrs[n_vt + 2 * n_hand:]
    blocks, tiles = _attn_plan(seq)
    order = _cfg_order()
    slabs = [slice(p * LANES, (p + 1) * LANES) for p in range(SLABS_PER_STEP)]

    @pl.when(pl.program_id(1) == 0)
    def _():
        for p in range(SLABS_PER_STEP):
            first_head = (pl.program_id(0) * SLABS_PER_STEP + p) * HEADS_PER_SLAB
            for t, (kind, dil) in enumerate(tiles):
                bias_scr[p * len(tiles) + t] = _bias_tile(kind, dil, first_head)

    for p in range(SLABS_PER_STEP):
        for c in order:
            vt_scrs[p][c][0, LANES:LANES + BF16_ROWS, :] = _ones_rows(seq)
            for i in range(seq // LANES):
                chunk = slice(i * LANES, (i + 1) * LANES)
                vt_scrs[p][c][0, 0:LANES, chunk] = v_refs[c][0, chunk, slabs[p]].T

    zero = jnp.minimum(pl.program_id(1), 0)
    work = [(p, blk) for blk in blocks for p in range(SLABS_PER_STEP)]
    for t in range(len(work) + SCORE_LAG):
        if t < len(work):
            p, blk = work[t]
            _score_block(blk, slabs[p], q_refs[blk.cfg], k_refs[blk.cfg], bias_scr.at[p * len(tiles) + blk.tile],
                         s_scrs[t % SCORE_SLOTS])
        if t >= SCORE_LAG:
            u = t - SCORE_LAG
            p, blk = work[u]
            n = order.index(blk.cfg)
            _value_block(blk, slabs[p], zero, vt_scrs[p][blk.cfg], s_scrs[u % SCORE_SLOTS],
                         hand[p][n - 1] if n > 0 else None, hand[p][n] if n + 1 < n_cfg else None, a_ref)


def _attn(q, k, v):
    b, s, w = k[0].shape
    n_cfg = len(DILATED_CONFIGS)
    assert all(s % (dil * Q_BLOCK) == 0 for _, dil in DILATED_CONFIGS)
    n_tiles = len(_attn_plan(s)[1])
    blk = pl.BlockSpec((1, s, SLABS_PER_STEP * LANES), lambda j, i: (i, 0, j))
    return pl.pallas_call(
        functools.partial(_attn_kernel, seq=s),
        grid=(w // (SLABS_PER_STEP * LANES), b),
        in_specs=[blk] * (3 * n_cfg),
        out_specs=blk,
        out_shape=jax.ShapeDtypeStruct((b, s, w), BF16),
        scratch_shapes=[
            pltpu.VMEM((SLABS_PER_STEP * n_tiles, KEY_WINDOW, HEADS_PER_SLAB * Q_BLOCK), F32),
        ] + [pltpu.VMEM((1, LANES + BF16_ROWS, s), BF16)] * (SLABS_PER_STEP * n_cfg)
        + [pltpu.VMEM((s, LANES), F32)] * (2 * SLABS_PER_STEP * (n_cfg - 1))
        + [pltpu.VMEM((1, KEY_WINDOW, HEADS_PER_SLAB * Q_BLOCK), F32)] * SCORE_SLOTS,
        compiler_params=pltpu.CompilerParams(
            dimension_semantics=("arbitrary", "arbitrary"), vmem_limit_bytes=VMEM_LIMIT_BYTES),
        name="attn",
    )(*q, *k, *v)


def _mix_kernel(a_ref, act_ref, mem_ref, gm_ref, wkv_ref, x_ref, ws_ref, bs_ref, wo_ref, gf_ref, o_ref,
                km_ref, vtm_ref):
    @pl.when(pl.program_id(1) == 0)
    def _():
        hm = _rms(mem_ref[0], gm_ref[...]).astype(BF16)
        kv = jnp.dot(hm, wkv_ref[...], preferred_element_type=F32)
        km_ref[...] = kv[:, 0:MEM_WIDTH].astype(BF16)
        vtm_ref[0:MEM_WIDTH, :] = kv[:, MEM_WIDTH:].T.astype(BF16)
        vtm_ref[MEM_WIDTH:, :] = _ones_rows(kv.shape[0])

    tm = x_ref.shape[1]
    gated = _mix_branches(pl.ds(0, tm), a_ref, act_ref, km_ref, vtm_ref, ws_ref, bs_ref)
    for lo in range(0, tm, PROJECT_ROWS):
        _mix_project(pl.ds(lo, PROJECT_ROWS), [g[lo:lo + PROJECT_ROWS] for g in gated], x_ref, wo_ref, gf_ref, o_ref)


def _mix_branches(rows, a_ref, act_ref, km_ref, vtm_ref, ws_ref, bs_ref):
    tm = rows.size
    lo = 0
    cols = []
    for width in ACT_SPLITS:
        cols.append(slice(lo, lo + width))
        lo += width
    sa, gu, vn, qm, sm = (act_ref[0, rows, c] for c in cols)

    ga = sa * a_ref[0, rows, :]

    group = lax.broadcasted_iota(jnp.int32, (SGU_CHUNK, SGU_WIDTH), 1) // SGU_GROUP
    mixed = []
    for c in range(tm // SGU_CHUNK):
        vc = vn[c * SGU_CHUNK:(c + 1) * SGU_CHUNK]
        stacked = jnp.concatenate([jnp.where(group == g, vc, jnp.zeros_like(vc)) for g in range(N_SGU_GROUPS)],
                                  axis=0)
        mixed.append(jnp.dot(ws_ref[...], stacked, preferred_element_type=F32) + bs_ref[...])
    gb = (gu.astype(F32) * jnp.concatenate(mixed, axis=0)).astype(BF16)

    mhead = lax.broadcasted_iota(jnp.int32, (tm, MEM_WIDTH), 1) // MEM_HEAD_DIM
    q4 = jnp.concatenate([jnp.where(mhead == h, qm, jnp.zeros_like(qm)) for h in range(N_MEM_HEADS)], axis=0)
    s = lax.dot_general(km_ref[...], q4, (((1,), (1,)), ((), ())), preferred_element_type=F32)
    p = jnp.exp2(s - jnp.max(s, axis=0, keepdims=True)).astype(BF16)
    ones = vtm_ref[MEM_WIDTH:, :]
    mo = []
    for h in range(N_MEM_HEADS):
        vt = jnp.concatenate([vtm_ref[h * MEM_HEAD_DIM:(h + 1) * MEM_HEAD_DIM, :], ones], axis=0)
        ot = jnp.dot(vt, p[:, h * tm:(h + 1) * tm], preferred_element_type=F32)
        mo.append(ot[0:MEM_HEAD_DIM] * (1.0 / ot[MEM_HEAD_DIM:MEM_HEAD_DIM + 1]))
    mo = jnp.concatenate(mo, axis=0).T
    gm = (sm.astype(F32) * mo).astype(BF16)
    return ga, gb, gm


def _mix_project(rows, gated, x_ref, wo_ref, gf_ref, o_ref):
    ga, gb, gm = gated
    e1 = ATTN_WIDTH
    e2 = e1 + SGU_WIDTH
    y = jnp.dot(ga, wo_ref[0:e1, :], preferred_element_type=F32)
    y = y + jnp.dot(gb, wo_ref[e1:e2, :], preferred_element_type=F32)
    y = y + jnp.dot(gm, wo_ref[e2:, :], preferred_element_type=F32)
    o_ref[0, rows, :] = _rms(x_ref[0, rows, :] + y, gf_ref[...])


def _mix(a, act, mem, g_mem, w_kv, x, w_s, b_tile, w_out, g_final):
    b, s, d = x.shape
    m = mem.shape[1]
    tile = lambda w: pl.BlockSpec((1, MIX_TILE, w), lambda i, j: (i, j, 0))
    fixed = lambda arr: pl.BlockSpec(arr.shape, lambda i, j: (0, 0))
    return pl.pallas_call(
        _mix_kernel,
        grid=(b, s // MIX_TILE),
        in_specs=[tile(a.shape[2]), tile(act.shape[2]), pl.BlockSpec((1, m, d), lambda i, j: (i, 0, 0)),
                  fixed(g_mem), fixed(w_kv), tile(d), fixed(w_s), fixed(b_tile), fixed(w_out), fixed(g_final)],
        out_specs=tile(d),
        out_shape=jax.ShapeDtypeStruct((b, s, d), x.dtype),
        scratch_shapes=[
            pltpu.VMEM((m, MEM_WIDTH), BF16),
            pltpu.VMEM((MEM_WIDTH + BF16_ROWS, m), BF16),
        ],
        compiler_params=pltpu.CompilerParams(
            dimension_semantics=("arbitrary", "arbitrary"), vmem_limit_bytes=VMEM_LIMIT_BYTES),
        name="mix",
    )(a, act, mem, g_mem, w_kv, x, w_s, b_tile, w_out, g_final)


def kernel(x, mem, g_norm, w_in, w_sgu_spatial, b_sgu_spatial, g_sgu_v, g_mem, w_mem_kv, w_out, g_final):
    assert g_norm.shape[0] == 1, "the final norm is fused into the single layer's last kernel"
    b, s, d = x.shape
    q, k, v, act = _inproj(x, g_norm[0][None, :], g_sgu_v[0][None, :], w_in[0].astype(BF16))
    a = _attn(q, k, v)
    w_s = jnp.concatenate(list(w_sgu_spatial[0].astype(BF16)), axis=1)
    b_tile = jnp.repeat(b_sgu_spatial[0].T, SGU_GROUP, axis=1)
    return _mix(a, act, mem, g_mem[0][None, :], w_mem_kv[0].astype(BF16), x, w_s, b_tile, w_out[0].astype(BF16),
                g_final[None, :])
```

```python
import functools
import math
from typing import NamedTuple

import jax
import jax.numpy as jnp
from jax import lax
from jax.experimental import pallas as pl
from jax.experimental.pallas import tpu as pltpu

F32 = jnp.float32
BF16 = jnp.bfloat16

EPS = 1e-6
NEG_INF = -1e30
LOG2_E = math.log2(math.e)

HEAD_DIM = 64
N_ATTN_HEADS = 8
ATTN_WIDTH = HEAD_DIM * N_ATTN_HEADS
DILATED_CONFIGS = ((128, 1), (512, 4), (2048, 16))
RADIUS = 64
SGU_WIDTH = 256
N_SGU_GROUPS = 4
SGU_GROUP = SGU_WIDTH // N_SGU_GROUPS
SGU_CHUNK = 128
MEM_WIDTH = 256
N_MEM_HEADS = 4
MEM_HEAD_DIM = MEM_WIDTH // N_MEM_HEADS
ACT_SPLITS = (ATTN_WIDTH, SGU_WIDTH, SGU_WIDTH, MEM_WIDTH, MEM_WIDTH)
ACT_WIDTH = sum(ACT_SPLITS)

LANES = 128
BF16_ROWS = 16
MXU_COLS = 256
Q_BLOCK = 128
KEY_WINDOW = 2 * Q_BLOCK
HEADS_PER_SLAB = LANES // HEAD_DIM
SLABS_PER_STEP = 2
SCORE_LAG = 4 * SLABS_PER_STEP
SCORE_SLOTS = 2 * SCORE_LAG
ROW_TILE = 1024
ORDER_BUFFERS = 6
EMIT_ROWS = 1024
ACT_ROWS = 256
MIX_TILE = 1024
PROJECT_ROWS = 256
BRANCH_PARTS = 2
VMEM_LIMIT_BYTES = 56 * 1024 * 1024


def _rms(x, g):
    return x * lax.rsqrt(jnp.mean(x * x, axis=-1, keepdims=True) + EPS) * g


def _silu(x):
    half = 0.5 * x
    return half + half * jnp.tanh(half)


def _ones_rows(cols):
    first = lax.broadcasted_iota(jnp.int32, (BF16_ROWS, cols), 0) == 0
    return jnp.where(first, 1.0, 0.0).astype(BF16)


def _emit_orders(val, row0, outs, lo, order_scrs):
    rows = val.shape[0]
    n_slabs = MXU_COLS // LANES

    def write(ref, r, slabs):
        n = slabs[0].shape[0]
        at = row0 * n // rows
        ref[r, at:at + n, lo:lo + MXU_COLS] = jnp.concatenate(slabs, axis=1).astype(BF16)

    by_dil = sorted(range(len(DILATED_CONFIGS)), key=lambda c: DILATED_CONFIGS[c][1])
    prev = 1
    for level, c in enumerate(by_dil):
        dil = DILATED_CONFIGS[c][1]
        if dil == 1:
            slabs = [val[:, i * LANES:(i + 1) * LANES] for i in range(n_slabs)]
            write(outs[c], 0, slabs)
            for scr, x in zip(order_scrs, slabs):
                scr[0] = x
            continue
        step, cls_prev, cls = dil // prev, rows // prev, rows // dil
        for r in range(dil):
            slabs = [scr[level - 1, pl.ds((r % prev) * cls_prev + r // prev, cls, stride=step), :]
                     for scr in order_scrs]
            write(outs[c], r, slabs)
            if level + 1 < len(by_dil):
                for scr, x in zip(order_scrs, slabs):
                    scr[level, r * cls:(r + 1) * cls, :] = x
        prev = dil


def _inproj_kernel(x_ref, g_ref, gv_ref, w_ref, *refs):
    n_cfg = len(DILATED_CONFIGS)
    q_refs, k_refs, v_refs = refs[0:n_cfg], refs[n_cfg:2 * n_cfg], refs[2 * n_cfg:3 * n_cfg]
    act_ref = refs[3 * n_cfg]
    order_scrs = refs[3 * n_cfg + 1:]
    h = _rms(x_ref[0], g_ref[...]).astype(BF16)

    n_slabs = MXU_COLS // LANES
    at_scr = 0
    for t, (out_refs, scale) in enumerate(((q_refs, HEAD_DIM ** -0.5 * LOG2_E), (k_refs, None), (v_refs, None))):
        for piece in range(ATTN_WIDTH // MXU_COLS):
            cols = slice(t * ATTN_WIDTH + piece * MXU_COLS, t * ATTN_WIDTH + (piece + 1) * MXU_COLS)
            for r0 in range(0, h.shape[0], EMIT_ROWS):
                val = jnp.dot(h[r0:r0 + EMIT_ROWS], w_ref[:, cols], preferred_element_type=F32)
                if scale is not None:
                    val = val * scale
                _emit_orders(val, r0, out_refs, piece * MXU_COLS, order_scrs[at_scr:at_scr + n_slabs])
                at_scr = (at_scr + n_slabs) % len(order_scrs)
    for r0 in range(0, h.shape[0], ACT_ROWS):
        hp = h[r0:r0 + ACT_ROWS]
        lo = 3 * ATTN_WIDTH
        raw = []
        for width in (ATTN_WIDTH, SGU_WIDTH, SGU_WIDTH, SGU_WIDTH, MEM_WIDTH, MEM_WIDTH):
            raw.append(jnp.dot(hp, w_ref[:, lo:lo + width], preferred_element_type=F32))
            lo += width
        za, ub, vb, zb, qm, zm = raw
        acts = (_silu(za), _silu(zb) * jax.nn.gelu(ub), _rms(jax.nn.gelu(vb), gv_ref[...]),
                qm * (MEM_HEAD_DIM ** -0.5 * LOG2_E), _silu(zm))
        lo = 0
        for width, val in zip(ACT_SPLITS, acts):
            act_ref[0, r0:r0 + ACT_ROWS, lo:lo + width] = val.astype(BF16)
            lo += width


def _inproj(x, g_norm, g_v, w_in):
    b, s, d = x.shape
    cols = w_in.shape[1]
    assert cols == 4 * ATTN_WIDTH + 3 * SGU_WIDTH + 2 * MEM_WIDTH
    tile = lambda w: pl.BlockSpec((1, ROW_TILE, w), lambda i, j: (i, j, 0))
    fixed = lambda i, j: (0, 0)
    dils = [dil for _, dil in DILATED_CONFIGS]
    assert all(EMIT_ROWS % (dil * BF16_ROWS) == 0 for dil in dils) and ROW_TILE % EMIT_ROWS == 0
    ordered = lambda w: [pl.BlockSpec((dil, ROW_TILE // dil, w), lambda i, j: (i, j, 0)) for dil in dils]
    shaped = lambda w: [jax.ShapeDtypeStruct((b * dil, s // dil, w), BF16) for dil in dils]
    widths = (ATTN_WIDTH, ATTN_WIDTH, ATTN_WIDTH)
    outs = pl.pallas_call(
        _inproj_kernel,
        grid=(b, s // ROW_TILE),
        in_specs=[
            tile(d),
            pl.BlockSpec((1, d), fixed),
            pl.BlockSpec((1, SGU_WIDTH), fixed),
            pl.BlockSpec((d, cols), fixed, pipeline_mode=pl.Buffered(1)),
        ],
        out_specs=[spec for w in widths for spec in ordered(w)] + [tile(ACT_WIDTH)],
        out_shape=[shape for w in widths for shape in shaped(w)] + [jax.ShapeDtypeStruct((b, s, ACT_WIDTH), BF16)],
        scratch_shapes=[pltpu.VMEM((len(dils) - 1, EMIT_ROWS, LANES), F32)] * ORDER_BUFFERS,
        compiler_params=pltpu.CompilerParams(
            dimension_semantics=("arbitrary", "arbitrary"), vmem_limit_bytes=VMEM_LIMIT_BYTES),
        name="inproj",
    )(x, g_norm, g_v, w_in)
    n = len(dils)
    q, k, v = ([o.reshape(b, s, -1) for o in outs[i * n:(i + 1) * n]] for i in range(3))
    return q, k, v, outs[3 * n]


SHIFT, EDGE, WHOLE = "shift", "edge", "whole"


def _bias_tile(kind, dilation, first_head):
    shape = (KEY_WINDOW, HEADS_PER_SLAB * Q_BLOCK)
    key = lax.broadcasted_iota(jnp.int32, shape, 0)
    col = lax.broadcasted_iota(jnp.int32, shape, 1)
    second = col >= Q_BLOCK
    qi = jnp.where(second, col - Q_BLOCK, col)
    if kind == SHIFT:
        rel = jnp.abs(qi + RADIUS - key)
        valid = rel <= RADIUS
    elif kind == WHOLE:
        rel = jnp.abs(qi - key)
        valid = rel <= RADIUS
    else:
        upper = key >= Q_BLOCK
        rel = jnp.abs(qi - jnp.where(upper, key - Q_BLOCK, key))
        valid = (rel <= RADIUS) & (upper == (qi >= RADIUS))
    head = (first_head + jnp.where(second, 1, 0)).astype(F32)
    slope = jnp.exp2(-8.0 * (head + 1.0) / N_ATTN_HEADS)
    dist = (rel * dilation).astype(F32)
    return jnp.where(valid, -slope * dist * LOG2_E, NEG_INF)


class _Block(NamedTuple):
    cfg: int
    tile: int
    q_rows: tuple
    k_rows: tuple
    out_rows: tuple
    out_stride: int


def _cfg_order():
    return sorted(range(len(DILATED_CONFIGS)), key=lambda c: -DILATED_CONFIGS[c][1])


def _attn_plan(seq):
    blocks, tiles = [], []
    order = _cfg_order()
    for n, c in enumerate(order):
        dil = DILATED_CONFIGS[c][1]
        nxt = DILATED_CONFIGS[order[n + 1]][1] if n + 1 < len(order) else 1
        step = dil // nxt
        cls = seq // dil
        if cls == Q_BLOCK:
            tiles.append((WHOLE, dil))
            whole = len(tiles) - 1
        else:
            tiles.append((SHIFT, dil))
            tiles.append((EDGE, dil))
            shift, edge = len(tiles) - 2, len(tiles) - 1
        for r in range(dil):
            base = r * cls
            sink = (r % nxt) * (seq // nxt) + r // nxt
            if cls == Q_BLOCK:
                blocks.append(_Block(c, whole, ((base, Q_BLOCK),), ((base, Q_BLOCK),), ((sink, Q_BLOCK),), step))
                continue
            last = cls - RADIUS
            blocks.append(_Block(c, edge, ((base, RADIUS), (base + last, RADIUS)),
                                 ((base, Q_BLOCK), (base + cls - Q_BLOCK, Q_BLOCK)),
                                 ((sink, RADIUS), (sink + step * last, RADIUS)), step))
            for j in range(cls // Q_BLOCK - 1):
                u = RADIUS + j * Q_BLOCK
                blocks.append(_Block(c, shift, ((base + u, Q_BLOCK),), ((base + j * Q_BLOCK, KEY_WINDOW),),
                                     ((sink + step * u, Q_BLOCK),), step))
    return blocks, tiles


def _rows(ref, lead, ranges, lanes=slice(None)):
    parts = [ref[lead + (pl.ds(s, n), lanes)] for s, n in ranges]
    return parts[0] if len(parts) == 1 else jnp.concatenate(parts, axis=0)


def _score_block(blk, lanes, q_ref, k_ref, bias, s_scr):
    q = _rows(q_ref, (0,), blk.q_rows, lanes)
    first = lax.broadcasted_iota(jnp.int32, (1, LANES), 1) < HEAD_DIM
    zeros = jnp.zeros_like(q)
    q2 = jnp.concatenate([jnp.where(first, q, zeros), jnp.where(first, zeros, q)], axis=0)
    k_win = _rows(k_ref, (0,), blk.k_rows, lanes)
    window = k_win.shape[0]
    s = lax.dot_general(k_win, q2, (((1,), (1,)), ((), ())), preferred_element_type=F32)
    s_scr[0, 0:window, :] = s + bias[0:window, :]


def _value_block(blk, lanes, zero, vt_scr, s_scr, prev, sink, a_ref):
    window = sum(n for _, n in blk.k_rows)
    ms, ps = [], []
    for h in range(HEADS_PER_SLAB):
        cols = slice(h * Q_BLOCK, (h + 1) * Q_BLOCK)
        s = s_scr[zero, 0:window, cols]
        m = jnp.max(s, axis=0, keepdims=True)
        ms.append(m)
        ps.append(jnp.exp2(s - m).astype(BF16))
    vt = jnp.concatenate([vt_scr[zero, :, s0:s0 + n] for s0, n in blk.k_rows], axis=1)
    ot = jnp.dot(vt, jnp.concatenate(ps, axis=1), preferred_element_type=F32)
    if prev is not None:
        o_prev, lse_prev = (_rows(ref, (), blk.q_rows) for ref in prev)
    last = prev is not None and sink is None
    if last:
        lse_back = lse_prev.T
    outs, lses, keeps = [], [], []
    for h in range(HEADS_PER_SLAB):
        cols = slice(h * Q_BLOCK, (h + 1) * Q_BLOCK)
        l = ot[LANES:LANES + 1, cols]
        lses.append(ms[h] + jnp.log2(l))
        scale = 1.0 / l
        if last:
            w = 1.0 / (1.0 + jnp.exp2(lse_back[h * HEAD_DIM:h * HEAD_DIM + 1] - lses[h]))
            scale = w * scale
            keeps.append(1.0 - w)
        outs.append(ot[h * HEAD_DIM:(h + 1) * HEAD_DIM, cols] * scale)

    def onto_rows(rows):
        return jnp.concatenate([jnp.broadcast_to(x, (HEAD_DIM, Q_BLOCK)) for x in rows], axis=0).T

    o = jnp.concatenate(outs, axis=0).T
    if last:
        o = o + onto_rows(keeps) * o_prev
    else:
        lse = onto_rows(lses)
        if prev is not None:
            top = jnp.maximum(lse, lse_prev)
            e, e_prev = jnp.exp2(lse - top), jnp.exp2(lse_prev - top)
            den = e + e_prev
            o = (e * o + e_prev * o_prev) * (1.0 / den)
            lse = top + jnp.log2(den)
    at = 0
    for start, n in blk.out_rows:
        dst = pl.ds(start, n) if blk.out_stride == 1 else pl.ds(start, n, stride=blk.out_stride)
        if sink is None:
            a_ref[0, dst, lanes] = o[at:at + n].astype(a_ref.dtype)
        else:
            sink[0][dst, :] = o[at:at + n]
            sink[1][dst, :] = lse[at:at + n]
        at += n


def _attn_kernel(*refs, seq):
    n_cfg = len(DILATED_CONFIGS)
    q_refs, k_refs, v_refs = refs[0:n_cfg], refs[n_cfg:2 * n_cfg], refs[2 * n_cfg:3 * n_cfg]
    a_ref, bias_scr = refs[3 * n_cfg:3 * n_cfg + 2]
    scrs = refs[3 * n_cfg + 2:]
    n_vt, n_hand = SLABS_PER_STEP * n_cfg, SLABS_PER_STEP * (n_cfg - 1)
    vt_scrs = [scrs[p * n_cfg:(p + 1) * n_cfg] for p in range(SLABS_PER_STEP)]
    hand = [[scrs[n_vt + 2 * (p * (n_cfg - 1) + i):n_vt + 2 * (p * (n_cfg - 1) + i) + 2] for i in range(n_cfg - 1)]
            for p in range(SLABS_PER_STEP)]
    s_scrs = scrs[n_vt + 2 * n_hand:]
    blocks, tiles = _attn_plan(seq)
    order = _cfg_order()
    slabs = [slice(p * LANES, (p + 1) * LANES) for p in range(SLABS_PER_STEP)]

    @pl.when(pl.program_id(1) == 0)
    def _():
        for p in range(SLABS_PER_STEP):
            first_head = (pl.program_id(0) * SLABS_PER_STEP + p) * HEADS_PER_SLAB
            for t, (kind, dil) in enumerate(tiles):
                bias_scr[p * len(tiles) + t] = _bias_tile(kind, dil, first_head)

    for p in range(SLABS_PER_STEP):
        for c in order:
            vt_scrs[p][c][0, LANES:LANES + BF16_ROWS, :] = _ones_rows(seq)
            for i in range(seq // LANES):
                chunk = slice(i * LANES, (i + 1) * LANES)
                vt_scrs[p][c][0, 0:LANES, chunk] = v_refs[c][0, chunk, slabs[p]].T

    zero = jnp.minimum(pl.program_id(1), 0)
    work = [(p, blk) for blk in blocks for p in range(SLABS_PER_STEP)]
    for t in range(len(work) + SCORE_LAG):
        if t < len(work):
            p, blk = work[t]
            _score_block(blk, slabs[p], q_refs[blk.cfg], k_refs[blk.cfg], bias_scr.at[p * len(tiles) + blk.tile],
                         s_scrs[t % SCORE_SLOTS])
        if t >= SCORE_LAG:
            u = t - SCORE_LAG
            p, blk = work[u]
            n = order.index(blk.cfg)
            _value_block(blk, slabs[p], zero, vt_scrs[p][blk.cfg], s_scrs[u % SCORE_SLOTS],
                         hand[p][n - 1] if n > 0 else None, hand[p][n] if n + 1 < n_cfg else None, a_ref)


def _attn(q, k, v):
    b, s, w = k[0].shape
    n_cfg = len(DILATED_CONFIGS)
    assert all(s % (dil * Q_BLOCK) == 0 for _, dil in DILATED_CONFIGS)
    n_tiles = len(_attn_plan(s)[1])
    blk = pl.BlockSpec((1, s, SLABS_PER_STEP * LANES), lambda j, i: (i, 0, j))
    return pl.pallas_call(
        functools.partial(_attn_kernel, seq=s),
        grid=(w // (SLABS_PER_STEP * LANES), b),
        in_specs=[blk] * (3 * n_cfg),
        out_specs=blk,
        out_shape=jax.ShapeDtypeStruct((b, s, w), BF16),
        scratch_shapes=[
            pltpu.VMEM((SLABS_PER_STEP * n_tiles, KEY_WINDOW, HEADS_PER_SLAB * Q_BLOCK), F32),
        ] + [pltpu.VMEM((1, LANES + BF16_ROWS, s), BF16)] * (SLABS_PER_STEP * n_cfg)
        + [pltpu.VMEM((s, LANES), F32)] * (2 * SLABS_PER_STEP * (n_cfg - 1))
        + [pltpu.VMEM((1, KEY_WINDOW, HEADS_PER_SLAB * Q_BLOCK), F32)] * SCORE_SLOTS,
        compiler_params=pltpu.CompilerParams(
            dimension_semantics=("arbitrary", "arbitrary"), vmem_limit_bytes=VMEM_LIMIT_BYTES),
        name="attn",
    )(*q, *k, *v)


def _mix_kernel(a_ref, act_ref, mem_ref, gm_ref, wkv_ref, x_ref, ws_ref, bs_ref, wo_ref, gf_ref, o_ref,
                km_ref, vtm_ref):
    @pl.when(pl.program_id(1) == 0)
    def _():
        hm = _rms(mem_ref[0], gm_ref[...]).astype(BF16)
        kv = jnp.dot(hm, wkv_ref[...], preferred_element_type=F32)
        km_ref[...] = kv[:, 0:MEM_WIDTH].astype(BF16)
        vtm_ref[0:MEM_WIDTH, :] = kv[:, MEM_WIDTH:].T.astype(BF16)
        vtm_ref[MEM_WIDTH:, :] = _ones_rows(kv.shape[0])

    tm = x_ref.shape[1]
    parts = [pl.ds(lo, tm // BRANCH_PARTS) for lo in range(0, tm, tm // BRANCH_PARTS)]
    gated = _mix_branches(parts, a_ref, act_ref, km_ref, vtm_ref, ws_ref, bs_ref)
    for lo in range(0, tm, PROJECT_ROWS):
        _mix_project(pl.ds(lo, PROJECT_ROWS), [g[lo:lo + PROJECT_ROWS] for g in gated], x_ref, wo_ref, gf_ref, o_ref)


def _mix_branches(parts, a_ref, act_ref, km_ref, vtm_ref, ws_ref, bs_ref):
    tm = parts[0].size
    lo = 0
    cols = []
    for width in ACT_SPLITS:
        cols.append(slice(lo, lo + width))
        lo += width
    acts = [[act_ref[0, rows, c] for c in cols] for rows in parts]

    mhead = lax.broadcasted_iota(jnp.int32, (tm, MEM_WIDTH), 1) // MEM_HEAD_DIM
    scores = []
    for _, _, _, qm, _ in acts:
        q4 = jnp.concatenate([jnp.where(mhead == h, qm, jnp.zeros_like(qm)) for h in range(N_MEM_HEADS)], axis=0)
        scores.append(lax.dot_general(km_ref[...], q4, (((1,), (1,)), ((), ())), preferred_element_type=F32))
    ps = [jnp.exp2(s - jnp.max(s, axis=0, keepdims=True)).astype(BF16) for s in scores]
    ones = vtm_ref[MEM_WIDTH:, :]
    mos = [[] for _ in parts]
    for h in range(N_MEM_HEADS):
        vt = jnp.concatenate([vtm_ref[h * MEM_HEAD_DIM:(h + 1) * MEM_HEAD_DIM, :], ones], axis=0)
        for mo, p in zip(mos, ps):
            ot = jnp.dot(vt, p[:, h * tm:(h + 1) * tm], preferred_element_type=F32)
            mo.append(ot[0:MEM_HEAD_DIM] * (1.0 / ot[MEM_HEAD_DIM:MEM_HEAD_DIM + 1]))
    gm = [(act[4].astype(F32) * jnp.concatenate(mo, axis=0).T).astype(BF16) for act, mo in zip(acts, mos)]

    group = lax.broadcasted_iota(jnp.int32, (SGU_CHUNK, SGU_WIDTH), 1) // SGU_GROUP
    gb = []
    for _, gu, vn, _, _ in acts:
        mixed = []
        for c in range(tm // SGU_CHUNK):
            vc = vn[c * SGU_CHUNK:(c + 1) * SGU_CHUNK]
            stacked = jnp.concatenate([jnp.where(group == g, vc, jnp.zeros_like(vc)) for g in range(N_SGU_GROUPS)],
                                      axis=0)
            mixed.append(jnp.dot(ws_ref[...], stacked, preferred_element_type=F32) + bs_ref[...])
        gb.append((gu.astype(F32) * jnp.concatenate(mixed, axis=0)).astype(BF16))

    ga = [act[0] * a_ref[0, rows, :] for act, rows in zip(acts, parts)]
    return [jnp.concatenate(x, axis=0) for x in (ga, gb, gm)]


def _mix_project(rows, gated, x_ref, wo_ref, gf_ref, o_ref):
    ga, gb, gm = gated
    e1 = ATTN_WIDTH
    e2 = e1 + SGU_WIDTH
    y = jnp.dot(ga, wo_ref[0:e1, :], preferred_element_type=F32)
    y = y + jnp.dot(gb, wo_ref[e1:e2, :], preferred_element_type=F32)
    y = y + jnp.dot(gm, wo_ref[e2:, :], preferred_element_type=F32)
    o_ref[0, rows, :] = _rms(x_ref[0, rows, :] + y, gf_ref[...])


def _mix(a, act, mem, g_mem, w_kv, x, w_s, b_tile, w_out, g_final):
    b, s, d = x.shape
    m = mem.shape[1]
    tile = lambda w: pl.BlockSpec((1, MIX_TILE, w), lambda i, j: (i, j, 0))
    fixed = lambda arr: pl.BlockSpec(arr.shape, lambda i, j: (0, 0))
    return pl.pallas_call(
        _mix_kernel,
        grid=(b, s // MIX_TILE),
        in_specs=[tile(a.shape[2]), tile(act.shape[2]), pl.BlockSpec((1, m, d), lambda i, j: (i, 0, 0)),
                  fixed(g_mem), fixed(w_kv), tile(d), fixed(w_s), fixed(b_tile), fixed(w_out), fixed(g_final)],
        out_specs=tile(d),
        out_shape=jax.ShapeDtypeStruct((b, s, d), x.dtype),
        scratch_shapes=[
            pltpu.VMEM((m, MEM_WIDTH), BF16),
            pltpu.VMEM((MEM_WIDTH + BF16_ROWS, m), BF16),
        ],
        compiler_params=pltpu.CompilerParams(
            dimension_semantics=("arbitrary", "arbitrary"), vmem_limit_bytes=VMEM_LIMIT_BYTES),
        name="mix",
    )(a, act, mem, g_mem, w_kv, x, w_s, b_tile, w_out, g_final)


def kernel(x, mem, g_norm, w_in, w_sgu_spatial, b_sgu_spatial, g_sgu_v, g_mem, w_mem_kv, w_out, g_final):
    assert g_norm.shape[0] == 1, "the final norm is fused into the single layer's last kernel"
    b, s, d = x.shape
    q, k, v, act = _inproj(x, g_norm[0][None, :], g_sgu_v[0][None, :], w_in[0].astype(BF16))
    a = _attn(q, k, v)
    w_s = jnp.concatenate(list(w_sgu_spatial[0].astype(BF16)), axis=1)
    b_tile = jnp.repeat(b_sgu_spatial[0].T, SGU_GROUP, axis=1)
    return _mix(a, act, mem, g_mem[0][None, :], w_mem_kv[0].astype(BF16), x, w_s, b_tile, w_out[0].astype(BF16),
                g_final[None, :])
```

```python
import functools
import math
from typing import NamedTuple

import jax
import jax.numpy as jnp
from jax import lax
from jax.experimental import pallas as pl
from jax.experimental.pallas import tpu as pltpu

F32 = jnp.float32
BF16 = jnp.bfloat16

EPS = 1e-6
NEG_INF = -1e30
LOG2_E = math.log2(math.e)

HEAD_DIM = 64
N_ATTN_HEADS = 8
ATTN_WIDTH = HEAD_DIM * N_ATTN_HEADS
DILATED_CONFIGS = ((128, 1), (512, 4), (2048, 16))
RADIUS = 64
SGU_WIDTH = 256
N_SGU_GROUPS = 4
SGU_GROUP = SGU_WIDTH // N_SGU_GROUPS
SGU_CHUNK = 128
MEM_WIDTH = 256
N_MEM_HEADS = 4
MEM_HEAD_DIM = MEM_WIDTH // N_MEM_HEADS
ACT_SPLITS = (ATTN_WIDTH, SGU_WIDTH, SGU_WIDTH, MEM_WIDTH, MEM_WIDTH)
ACT_WIDTH = sum(ACT_SPLITS)

LANES = 128
BF16_ROWS = 16
MXU_COLS = 256
Q_BLOCK = 128
KEY_WINDOW = 2 * Q_BLOCK
HEADS_PER_SLAB = LANES // HEAD_DIM
SLABS_PER_STEP = 2
SCORE_LAG = 2 * SLABS_PER_STEP
SCORE_SLOTS = 2 * SCORE_LAG
ROW_TILE = 1024
ORDER_BUFFERS = 6
EMIT_ROWS = 512
ACT_ROWS = 256
MIX_TILE = 1024
PROJECT_ROWS = 256
BRANCH_PARTS = 4
VMEM_LIMIT_BYTES = 56 * 1024 * 1024


def _rms(x, g):
    return x * lax.rsqrt(jnp.mean(x * x, axis=-1, keepdims=True) + EPS) * g


def _silu(x):
    half = 0.5 * x
    return half + half * jnp.tanh(half)


def _ones_rows(cols):
    first = lax.broadcasted_iota(jnp.int32, (BF16_ROWS, cols), 0) == 0
    return jnp.where(first, 1.0, 0.0).astype(BF16)


def _emit_orders(val, row0, outs, lo, order_scrs):
    rows = val.shape[0]
    n_slabs = MXU_COLS // LANES

    def write(ref, r, slabs):
        n = slabs[0].shape[0]
        at = row0 * n // rows
        ref[r, at:at + n, lo:lo + MXU_COLS] = jnp.concatenate(slabs, axis=1).astype(BF16)

    by_dil = sorted(range(len(DILATED_CONFIGS)), key=lambda c: DILATED_CONFIGS[c][1])
    prev = 1
    for level, c in enumerate(by_dil):
        dil = DILATED_CONFIGS[c][1]
        if dil == 1:
            slabs = [val[:, i * LANES:(i + 1) * LANES] for i in range(n_slabs)]
            write(outs[c], 0, slabs)
            for scr, x in zip(order_scrs, slabs):
                scr[0] = x
            continue
        step, cls_prev, cls = dil // prev, rows // prev, rows // dil
        for r in range(dil):
            slabs = [scr[level - 1, pl.ds((r % prev) * cls_prev + r // prev, cls, stride=step), :]
                     for scr in order_scrs]
            write(outs[c], r, slabs)
            if level + 1 < len(by_dil):
                for scr, x in zip(order_scrs, slabs):
                    scr[level, r * cls:(r + 1) * cls, :] = x
        prev = dil


def _inproj_kernel(x_ref, g_ref, gv_ref, w_ref, *refs):
    n_cfg = len(DILATED_CONFIGS)
    q_refs, k_refs, v_refs = refs[0:n_cfg], refs[n_cfg:2 * n_cfg], refs[2 * n_cfg:3 * n_cfg]
    act_ref = refs[3 * n_cfg]
    order_scrs = refs[3 * n_cfg + 1:]
    h = _rms(x_ref[0], g_ref[...]).astype(BF16)

    n_slabs = MXU_COLS // LANES
    at_scr = 0
    for t, (out_refs, scale) in enumerate(((q_refs, HEAD_DIM ** -0.5 * LOG2_E), (k_refs, None), (v_refs, None))):
        for piece in range(ATTN_WIDTH // MXU_COLS):
            cols = slice(t * ATTN_WIDTH + piece * MXU_COLS, t * ATTN_WIDTH + (piece + 1) * MXU_COLS)
            for r0 in range(0, h.shape[0], EMIT_ROWS):
                val = jnp.dot(h[r0:r0 + EMIT_ROWS], w_ref[:, cols], preferred_element_type=F32)
                if scale is not None:
                    val = val * scale
                _emit_orders(val, r0, out_refs, piece * MXU_COLS, order_scrs[at_scr:at_scr + n_slabs])
                at_scr = (at_scr + n_slabs) % len(order_scrs)
    for r0 in range(0, h.shape[0], ACT_ROWS):
        hp = h[r0:r0 + ACT_ROWS]
        lo = 3 * ATTN_WIDTH
        raw = []
        for width in (ATTN_WIDTH, SGU_WIDTH, SGU_WIDTH, SGU_WIDTH, MEM_WIDTH, MEM_WIDTH):
            raw.append(jnp.dot(hp, w_ref[:, lo:lo + width], preferred_element_type=F32))
            lo += width
        za, ub, vb, zb, qm, zm = raw
        acts = (_silu(za), _silu(zb) * jax.nn.gelu(ub), _rms(jax.nn.gelu(vb), gv_ref[...]),
                qm * (MEM_HEAD_DIM ** -0.5 * LOG2_E), _silu(zm))
        lo = 0
        for width, val in zip(ACT_SPLITS, acts):
            act_ref[0, r0:r0 + ACT_ROWS, lo:lo + width] = val.astype(BF16)
            lo += width


def _inproj(x, g_norm, g_v, w_in):
    b, s, d = x.shape
    cols = w_in.shape[1]
    assert cols == 4 * ATTN_WIDTH + 3 * SGU_WIDTH + 2 * MEM_WIDTH
    tile = lambda w: pl.BlockSpec((1, ROW_TILE, w), lambda i, j: (i, j, 0))
    fixed = lambda i, j: (0, 0)
    dils = [dil for _, dil in DILATED_CONFIGS]
    assert all(EMIT_ROWS % (dil * BF16_ROWS) == 0 for dil in dils) and ROW_TILE % EMIT_ROWS == 0
    ordered = lambda w: [pl.BlockSpec((dil, ROW_TILE // dil, w), lambda i, j: (i, j, 0)) for dil in dils]
    shaped = lambda w: [jax.ShapeDtypeStruct((b * dil, s // dil, w), BF16) for dil in dils]
    widths = (ATTN_WIDTH, ATTN_WIDTH, ATTN_WIDTH)
    outs = pl.pallas_call(
        _inproj_kernel,
        grid=(b, s // ROW_TILE),
        in_specs=[
            tile(d),
            pl.BlockSpec((1, d), fixed),
            pl.BlockSpec((1, SGU_WIDTH), fixed),
            pl.BlockSpec((d, cols), fixed, pipeline_mode=pl.Buffered(1)),
        ],
        out_specs=[spec for w in widths for spec in ordered(w)] + [tile(ACT_WIDTH)],
        out_shape=[shape for w in widths for shape in shaped(w)] + [jax.ShapeDtypeStruct((b, s, ACT_WIDTH), BF16)],
        scratch_shapes=[pltpu.VMEM((len(dils) - 1, EMIT_ROWS, LANES), F32)] * ORDER_BUFFERS,
        compiler_params=pltpu.CompilerParams(
            dimension_semantics=("arbitrary", "arbitrary"), vmem_limit_bytes=VMEM_LIMIT_BYTES),
        name="inproj",
    )(x, g_norm, g_v, w_in)
    n = len(dils)
    q, k, v = ([o.reshape(b, s, -1) for o in outs[i * n:(i + 1) * n]] for i in range(3))
    return q, k, v, outs[3 * n]


SHIFT, EDGE, WHOLE = "shift", "edge", "whole"


def _bias_tile(kind, dilation, first_head):
    shape = (KEY_WINDOW, HEADS_PER_SLAB * Q_BLOCK)
    key = lax.broadcasted_iota(jnp.int32, shape, 0)
    col = lax.broadcasted_iota(jnp.int32, shape, 1)
    second = col >= Q_BLOCK
    qi = jnp.where(second, col - Q_BLOCK, col)
    if kind == SHIFT:
        rel = jnp.abs(qi + RADIUS - key)
        valid = rel <= RADIUS
    elif kind == WHOLE:
        rel = jnp.abs(qi - key)
        valid = rel <= RADIUS
    else:
        upper = key >= Q_BLOCK
        rel = jnp.abs(qi - jnp.where(upper, key - Q_BLOCK, key))
        valid = (rel <= RADIUS) & (upper == (qi >= RADIUS))
    head = (first_head + jnp.where(second, 1, 0)).astype(F32)
    slope = jnp.exp2(-8.0 * (head + 1.0) / N_ATTN_HEADS)
    dist = (rel * dilation).astype(F32)
    return jnp.where(valid, -slope * dist * LOG2_E, NEG_INF)


class _Block(NamedTuple):
    cfg: int
    tile: int
    q_rows: tuple
    k_rows: tuple
    out_rows: tuple
    out_stride: int


def _cfg_order():
    return sorted(range(len(DILATED_CONFIGS)), key=lambda c: -DILATED_CONFIGS[c][1])


def _attn_plan(seq):
    blocks, tiles = [], []
    order = _cfg_order()
    for n, c in enumerate(order):
        dil = DILATED_CONFIGS[c][1]
        nxt = DILATED_CONFIGS[order[n + 1]][1] if n + 1 < len(order) else 1
        step = dil // nxt
        cls = seq // dil
        if cls == Q_BLOCK:
            tiles.append((WHOLE, dil))
            whole = len(tiles) - 1
        else:
            tiles.append((SHIFT, dil))
            tiles.append((EDGE, dil))
            shift, edge = len(tiles) - 2, len(tiles) - 1
        for r in range(dil):
            base = r * cls
            sink = (r % nxt) * (seq // nxt) + r // nxt
            if cls == Q_BLOCK:
                blocks.append(_Block(c, whole, ((base, Q_BLOCK),), ((base, Q_BLOCK),), ((sink, Q_BLOCK),), step))
                continue
            last = cls - RADIUS
            blocks.append(_Block(c, edge, ((base, RADIUS), (base + last, RADIUS)),
                                 ((base, Q_BLOCK), (base + cls - Q_BLOCK, Q_BLOCK)),
                                 ((sink, RADIUS), (sink + step * last, RADIUS)), step))
            for j in range(cls // Q_BLOCK - 1):
                u = RADIUS + j * Q_BLOCK
                blocks.append(_Block(c, shift, ((base + u, Q_BLOCK),), ((base + j * Q_BLOCK, KEY_WINDOW),),
                                     ((sink + step * u, Q_BLOCK),), step))
    return blocks, tiles


def _rows(ref, lead, ranges, lanes=slice(None)):
    parts = [ref[lead + (pl.ds(s, n), lanes)] for s, n in ranges]
    return parts[0] if len(parts) == 1 else jnp.concatenate(parts, axis=0)


def _score_block(blk, lanes, q_ref, k_ref, bias, s_scr):
    q = _rows(q_ref, (0,), blk.q_rows, lanes)
    first = lax.broadcasted_iota(jnp.int32, (1, LANES), 1) < HEAD_DIM
    zeros = jnp.zeros_like(q)
    q2 = jnp.concatenate([jnp.where(first, q, zeros), jnp.where(first, zeros, q)], axis=0)
    k_win = _rows(k_ref, (0,), blk.k_rows, lanes)
    window = k_win.shape[0]
    s = lax.dot_general(k_win, q2, (((1,), (1,)), ((), ())), preferred_element_type=F32)
    s_scr[0, 0:window, :] = s + bias[0:window, :]


def _value_block(blk, lanes, zero, vt_scr, s_scr, prev, sink, a_ref):
    window = sum(n for _, n in blk.k_rows)
    ms, ps = [], []
    for h in range(HEADS_PER_SLAB):
        cols = slice(h * Q_BLOCK, (h + 1) * Q_BLOCK)
        s = s_scr[zero, 0:window, cols]
        m = jnp.max(s, axis=0, keepdims=True)
        ms.append(m)
        ps.append(jnp.exp2(s - m).astype(BF16))
    vt = jnp.concatenate([vt_scr[zero, :, s0:s0 + n] for s0, n in blk.k_rows], axis=1)
    ot = jnp.dot(vt, jnp.concatenate(ps, axis=1), preferred_element_type=F32)
    if prev is not None:
        o_prev, lse_prev = (_rows(ref, (), blk.q_rows) for ref in prev)
    last = prev is not None and sink is None
    if last:
        lse_back = lse_prev.T
    outs, lses, keeps = [], [], []
    for h in range(HEADS_PER_SLAB):
        cols = slice(h * Q_BLOCK, (h + 1) * Q_BLOCK)
        l = ot[LANES:LANES + 1, cols]
        lses.append(ms[h] + jnp.log2(l))
        scale = 1.0 / l
        if last:
            w = 1.0 / (1.0 + jnp.exp2(lse_back[h * HEAD_DIM:h * HEAD_DIM + 1] - lses[h]))
            scale = w * scale
            keeps.append(1.0 - w)
        outs.append(ot[h * HEAD_DIM:(h + 1) * HEAD_DIM, cols] * scale)

    def onto_rows(rows):
        return jnp.concatenate([jnp.broadcast_to(x, (HEAD_DIM, Q_BLOCK)) for x in rows], axis=0).T

    o = jnp.concatenate(outs, axis=0).T
    if last:
        o = o + onto_rows(keeps) * o_prev
    else:
        lse = onto_rows(lses)
        if prev is not None:
            top = jnp.maximum(lse, lse_prev)
            e, e_prev = jnp.exp2(lse - top), jnp.exp2(lse_prev - top)
            den = e + e_prev
            o = (e * o + e_prev * o_prev) * (1.0 / den)
            lse = top + jnp.log2(den)
    at = 0
    for start, n in blk.out_rows:
        dst = pl.ds(start, n) if blk.out_stride == 1 else pl.ds(start, n, stride=blk.out_stride)
        if sink is None:
            a_ref[0, dst, lanes] = o[at:at + n].astype(a_ref.dtype)
        else:
            sink[0][dst, :] = o[at:at + n]
            sink[1][dst, :] = lse[at:at + n]
        at += n


def _attn_kernel(*refs, seq):
    n_cfg = len(DILATED_CONFIGS)
    q_refs, k_refs, v_refs = refs[0:n_cfg], refs[n_cfg:2 * n_cfg], refs[2 * n_cfg:3 * n_cfg]
    a_ref, bias_scr = refs[3 * n_cfg:3 * n_cfg + 2]
    scrs = refs[3 * n_cfg + 2:]
    n_vt, n_hand = SLABS_PER_STEP * n_cfg, SLABS_PER_STEP * (n_cfg - 1)
    vt_scrs = [scrs[p * n_cfg:(p + 1) * n_cfg] for p in range(SLABS_PER_STEP)]
    hand = [[scrs[n_vt + 2 * (p * (n_cfg - 1) + i):n_vt + 2 * (p * (n_cfg - 1) + i) + 2] for i in range(n_cfg - 1)]
            for p in range(SLABS_PER_STEP)]
    s_scrs = scrs[n_vt + 2 * n_hand:]
    blocks, tiles = _attn_plan(seq)
    order = _cfg_order()
    slabs = [slice(p * LANES, (p + 1) * LANES) for p in range(SLABS_PER_STEP)]

    @pl.when(pl.program_id(1) == 0)
    def _():
        for p in range(SLABS_PER_STEP):
            first_head = (pl.program_id(0) * SLABS_PER_STEP + p) * HEADS_PER_SLAB
            for t, (kind, dil) in enumerate(tiles):
                bias_scr[p * len(tiles) + t] = _bias_tile(kind, dil, first_head)

    for p in range(SLABS_PER_STEP):
        for c in order:
            vt_scrs[p][c][0, LANES:LANES + BF16_ROWS, :] = _ones_rows(seq)
            for i in range(seq // LANES):
                chunk = slice(i * LANES, (i + 1) * LANES)
                vt_scrs[p][c][0, 0:LANES, chunk] = v_refs[c][0, chunk, slabs[p]].T

    zero = jnp.minimum(pl.program_id(1), 0)
    work = [(p, blk) for blk in blocks for p in range(SLABS_PER_STEP)]
    for t in range(len(work) + SCORE_LAG):
        if t < len(work):
            p, blk = work[t]
            _score_block(blk, slabs[p], q_refs[blk.cfg], k_refs[blk.cfg], bias_scr.at[p * len(tiles) + blk.tile],
                         s_scrs[t % SCORE_SLOTS])
        if t >= SCORE_LAG:
            u = t - SCORE_LAG
            p, blk = work[u]
            n = order.index(blk.cfg)
            _value_block(blk, slabs[p], zero, vt_scrs[p][blk.cfg], s_scrs[u % SCORE_SLOTS],
                         hand[p][n - 1] if n > 0 else None, hand[p][n] if n + 1 < n_cfg else None, a_ref)


def _attn(q, k, v):
    b, s, w = k[0].shape
    n_cfg = len(DILATED_CONFIGS)
    assert all(s % (dil * Q_BLOCK) == 0 for _, dil in DILATED_CONFIGS)
    n_tiles = len(_attn_plan(s)[1])
    blk = pl.BlockSpec((1, s, SLABS_PER_STEP * LANES), lambda j, i: (i, 0, j))
    return pl.pallas_call(
        functools.partial(_attn_kernel, seq=s),
        grid=(w // (SLABS_PER_STEP * LANES), b),
        in_specs=[blk] * (3 * n_cfg),
        out_specs=blk,
        out_shape=jax.ShapeDtypeStruct((b, s, w), BF16),
        scratch_shapes=[
            pltpu.VMEM((SLABS_PER_STEP * n_tiles, KEY_WINDOW, HEADS_PER_SLAB * Q_BLOCK), F32),
        ] + [pltpu.VMEM((1, LANES + BF16_ROWS, s), BF16)] * (SLABS_PER_STEP * n_cfg)
        + [pltpu.VMEM((s, LANES), F32)] * (2 * SLABS_PER_STEP * (n_cfg - 1))
        + [pltpu.VMEM((1, KEY_WINDOW, HEADS_PER_SLAB * Q_BLOCK), F32)] * SCORE_SLOTS,
        compiler_params=pltpu.CompilerParams(
            dimension_semantics=("arbitrary", "arbitrary"), vmem_limit_bytes=VMEM_LIMIT_BYTES),
        name="attn",
    )(*q, *k, *v)


def _mix_kernel(a_ref, act_ref, mem_ref, gm_ref, wkv_ref, x_ref, ws_ref, bs_ref, wo_ref, gf_ref, o_ref,
                km_ref, vtm_ref):
    @pl.when(pl.program_id(1) == 0)
    def _():
        hm = _rms(mem_ref[0], gm_ref[...]).astype(BF16)
        kv = jnp.dot(hm, wkv_ref[...], preferred_element_type=F32)
        km_ref[...] = kv[:, 0:MEM_WIDTH].astype(BF16)
        vtm_ref[0:MEM_WIDTH, :] = kv[:, MEM_WIDTH:].T.astype(BF16)
        vtm_ref[MEM_WIDTH:, :] = _ones_rows(kv.shape[0])

    tm = x_ref.shape[1]
    parts = [pl.ds(lo, tm // BRANCH_PARTS) for lo in range(0, tm, tm // BRANCH_PARTS)]
    gated = _mix_branches(parts, a_ref, act_ref, km_ref, vtm_ref, ws_ref, bs_ref)
    for lo in range(0, tm, PROJECT_ROWS):
        _mix_project(pl.ds(lo, PROJECT_ROWS), [g[lo:lo + PROJECT_ROWS] for g in gated], x_ref, wo_ref, gf_ref, o_ref)


def _mix_branches(parts, a_ref, act_ref, km_ref, vtm_ref, ws_ref, bs_ref):
    tm = parts[0].size
    lo = 0
    cols = []
    for width in ACT_SPLITS:
        cols.append(slice(lo, lo + width))
        lo += width
    acts = [[act_ref[0, rows, c] for c in cols] for rows in parts]

    mhead = lax.broadcasted_iota(jnp.int32, (tm, MEM_WIDTH), 1) // MEM_HEAD_DIM
    scores = []
    for _, _, _, qm, _ in acts:
        q4 = jnp.concatenate([jnp.where(mhead == h, qm, jnp.zeros_like(qm)) for h in range(N_MEM_HEADS)], axis=0)
        scores.append(lax.dot_general(km_ref[...], q4, (((1,), (1,)), ((), ())), preferred_element_type=F32))
    ps = [jnp.exp2(s - jnp.max(s, axis=0, keepdims=True)).astype(BF16) for s in scores]
    ones = vtm_ref[MEM_WIDTH:, :]
    mos = [[] for _ in parts]
    for h in range(N_MEM_HEADS):
        vt = jnp.concatenate([vtm_ref[h * MEM_HEAD_DIM:(h + 1) * MEM_HEAD_DIM, :], ones], axis=0)
        for mo, p in zip(mos, ps):
            ot = jnp.dot(vt, p[:, h * tm:(h + 1) * tm], preferred_element_type=F32)
            mo.append(ot[0:MEM_HEAD_DIM] * (1.0 / ot[MEM_HEAD_DIM:MEM_HEAD_DIM + 1]))
    gm = [(act[4].astype(F32) * jnp.concatenate(mo, axis=0).T).astype(BF16) for act, mo in zip(acts, mos)]

    group = lax.broadcasted_iota(jnp.int32, (SGU_CHUNK, SGU_WIDTH), 1) // SGU_GROUP
    gb = []
    for _, gu, vn, _, _ in acts:
        mixed = []
        for c in range(tm // SGU_CHUNK):
            vc = vn[c * SGU_CHUNK:(c + 1) * SGU_CHUNK]
            stacked = jnp.concatenate([jnp.where(group == g, vc, jnp.zeros_like(vc)) for g in range(N_SGU_GROUPS)],
                                      axis=0)
            mixed.append(jnp.dot(ws_ref[...], stacked, preferred_element_type=F32) + bs_ref[...])
        gb.append((gu.astype(F32) * jnp.concatenate(mixed, axis=0)).astype(BF16))

    ga = [act[0] * a_ref[0, rows, :] for act, rows in zip(acts, parts)]
    return [jnp.concatenate(x, axis=0) for x in (ga, gb, gm)]


def _mix_project(rows, gated, x_ref, wo_ref, gf_ref, o_ref):
    ga, gb, gm = gated
    e1 = ATTN_WIDTH
    e2 = e1 + SGU_WIDTH
    y = jnp.dot(ga, wo_ref[0:e1, :], preferred_element_type=F32)
    y = y + jnp.dot(gb, wo_ref[e1:e2, :], preferred_element_type=F32)
    y = y + jnp.dot(gm, wo_ref[e2:, :], preferred_element_type=F32)
    o_ref[0, rows, :] = _rms(x_ref[0, rows, :] + y, gf_ref[...])


def _mix(a, act, mem, g_mem, w_kv, x, w_s, b_tile, w_out, g_final):
    b, s, d = x.shape
    m = mem.shape[1]
    tile = lambda w: pl.BlockSpec((1, MIX_TILE, w), lambda i, j: (i, j, 0))
    fixed = lambda arr: pl.BlockSpec(arr.shape, lambda i, j: (0, 0))
    return pl.pallas_call(
        _mix_kernel,
        grid=(b, s // MIX_TILE),
        in_specs=[tile(a.shape[2]), tile(act.shape[2]), pl.BlockSpec((1, m, d), lambda i, j: (i, 0, 0)),
                  fixed(g_mem), fixed(w_kv), tile(d), fixed(w_s), fixed(b_tile), fixed(w_out), fixed(g_final)],
        out_specs=tile(d),
        out_shape=jax.ShapeDtypeStruct((b, s, d), x.dtype),
        scratch_shapes=[
            pltpu.VMEM((m, MEM_WIDTH), BF16),
            pltpu.VMEM((MEM_WIDTH + BF16_ROWS, m), BF16),
        ],
        compiler_params=pltpu.CompilerParams(
            dimension_semantics=("arbitrary", "arbitrary"), vmem_limit_bytes=VMEM_LIMIT_BYTES),
        name="mix",
    )(a, act, mem, g_mem, w_kv, x, w_s, b_tile, w_out, g_final)


def kernel(x, mem, g_norm, w_in, w_sgu_spatial, b_sgu_spatial, g_sgu_v, g_mem, w_mem_kv, w_out, g_final):
    assert g_norm.shape[0] == 1, "the final norm is fused into the single layer's last kernel"
    b, s, d = x.shape
    q, k, v, act = _inproj(x, g_norm[0][None, :], g_sgu_v[0][None, :], w_in[0].astype(BF16))
    a = _attn(q, k, v)
    w_s = jnp.concatenate(list(w_sgu_spatial[0].astype(BF16)), axis=1)
    b_tile = jnp.repeat(b_sgu_spatial[0].T, SGU_GROUP, axis=1)
    return _mix(a, act, mem, g_mem[0][None, :], w_mem_kv[0].astype(BF16), x, w_s, b_tile, w_out[0].astype(BF16),
                g_final[None, :])
```

```python
import functools
import math
from typing import NamedTuple

import jax
import jax.numpy as jnp
from jax import lax
from jax.experimental import pallas as pl
from jax.experimental.pallas import tpu as pltpu

F32 = jnp.float32
BF16 = jnp.bfloat16

EPS = 1e-6
NEG_INF = -1e30
LOG2_E = math.log2(math.e)

HEAD_DIM = 64
N_ATTN_HEADS = 8
ATTN_WIDTH = HEAD_DIM * N_ATTN_HEADS
DILATED_CONFIGS = ((128, 1), (512, 4), (2048, 16))
RADIUS = 64
SGU_WIDTH = 256
N_SGU_GROUPS = 4
SGU_GROUP = SGU_WIDTH // N_SGU_GROUPS
SGU_CHUNK = 128
MEM_WIDTH = 256
N_MEM_HEADS = 4
MEM_HEAD_DIM = MEM_WIDTH // N_MEM_HEADS
ACT_SPLITS = (ATTN_WIDTH, SGU_WIDTH, SGU_WIDTH, MEM_WIDTH, MEM_WIDTH)
ACT_WIDTH = sum(ACT_SPLITS)

LANES = 128
BF16_ROWS = 16
MXU_COLS = 256
Q_BLOCK = 128
KEY_WINDOW = 2 * Q_BLOCK
HEADS_PER_SLAB = LANES // HEAD_DIM
SLABS_PER_STEP = 2
SCORE_LAG = 6 * SLABS_PER_STEP
SCORE_SLOTS = 2 * SCORE_LAG
ROW_TILE = 1024
ORDER_BUFFERS = 6
EMIT_ROWS = 1024
ACT_ROWS = 256
MIX_TILE = 1024
PROJECT_ROWS = 256
BRANCH_PARTS = 2
VMEM_LIMIT_BYTES = 56 * 1024 * 1024


def _rms(x, g):
    return x * lax.rsqrt(jnp.mean(x * x, axis=-1, keepdims=True) + EPS) * g


def _silu(x):
    half = 0.5 * x
    return half + half * jnp.tanh(half)


def _ones_rows(cols):
    first = lax.broadcasted_iota(jnp.int32, (BF16_ROWS, cols), 0) == 0
    return jnp.where(first, 1.0, 0.0).astype(BF16)


def _emit_orders(val, row0, outs, lo, order_scrs):
    rows = val.shape[0]
    n_slabs = MXU_COLS // LANES

    def write(ref, r, slabs):
        n = slabs[0].shape[0]
        at = row0 * n // rows
        ref[r, at:at + n, lo:lo + MXU_COLS] = jnp.concatenate(slabs, axis=1).astype(BF16)

    by_dil = sorted(range(len(DILATED_CONFIGS)), key=lambda c: DILATED_CONFIGS[c][1])
    prev = 1
    for level, c in enumerate(by_dil):
        dil = DILATED_CONFIGS[c][1]
        if dil == 1:
            slabs = [val[:, i * LANES:(i + 1) * LANES] for i in range(n_slabs)]
            write(outs[c], 0, slabs)
            for scr, x in zip(order_scrs, slabs):
                scr[0] = x
            continue
        step, cls_prev, cls = dil // prev, rows // prev, rows // dil
        for r in range(dil):
            slabs = [scr[level - 1, pl.ds((r % prev) * cls_prev + r // prev, cls, stride=step), :]
                     for scr in order_scrs]
            write(outs[c], r, slabs)
            if level + 1 < len(by_dil):
                for scr, x in zip(order_scrs, slabs):
                    scr[level, r * cls:(r + 1) * cls, :] = x
        prev = dil


def _inproj_kernel(x_ref, g_ref, gv_ref, w_ref, *refs):
    n_cfg = len(DILATED_CONFIGS)
    q_refs, k_refs, v_refs = refs[0:n_cfg], refs[n_cfg:2 * n_cfg], refs[2 * n_cfg:3 * n_cfg]
    act_ref = refs[3 * n_cfg]
    order_scrs = refs[3 * n_cfg + 1:]
    h = _rms(x_ref[0], g_ref[...]).astype(BF16)

    n_slabs = MXU_COLS // LANES
    at_scr = 0
    for t, (out_refs, scale) in enumerate(((q_refs, HEAD_DIM ** -0.5 * LOG2_E), (k_refs, None), (v_refs, None))):
        for piece in range(ATTN_WIDTH // MXU_COLS):
            cols = slice(t * ATTN_WIDTH + piece * MXU_COLS, t * ATTN_WIDTH + (piece + 1) * MXU_COLS)
            for r0 in range(0, h.shape[0], EMIT_ROWS):
                val = jnp.dot(h[r0:r0 + EMIT_ROWS], w_ref[:, cols], preferred_element_type=F32)
                if scale is not None:
                    val = val * scale
                _emit_orders(val, r0, out_refs, piece * MXU_COLS, order_scrs[at_scr:at_scr + n_slabs])
                at_scr = (at_scr + n_slabs) % len(order_scrs)
    for r0 in range(0, h.shape[0], ACT_ROWS):
        hp = h[r0:r0 + ACT_ROWS]
        lo = 3 * ATTN_WIDTH
        raw = []
        for width in (ATTN_WIDTH, SGU_WIDTH, SGU_WIDTH, SGU_WIDTH, MEM_WIDTH, MEM_WIDTH):
            raw.append(jnp.dot(hp, w_ref[:, lo:lo + width], preferred_element_type=F32))
            lo += width
        za, ub, vb, zb, qm, zm = raw
        acts = (_silu(za), _silu(zb) * jax.nn.gelu(ub), _rms(jax.nn.gelu(vb), gv_ref[...]),
                qm * (MEM_HEAD_DIM ** -0.5 * LOG2_E), _silu(zm))
        lo = 0
        for width, val in zip(ACT_SPLITS, acts):
            act_ref[0, r0:r0 + ACT_ROWS, lo:lo + width] = val.astype(BF16)
            lo += width


def _inproj(x, g_norm, g_v, w_in):
    b, s, d = x.shape
    cols = w_in.shape[1]
    assert cols == 4 * ATTN_WIDTH + 3 * SGU_WIDTH + 2 * MEM_WIDTH
    tile = lambda w: pl.BlockSpec((1, ROW_TILE, w), lambda i, j: (i, j, 0))
    fixed = lambda i, j: (0, 0)
    dils = [dil for _, dil in DILATED_CONFIGS]
    assert all(EMIT_ROWS % (dil * BF16_ROWS) == 0 for dil in dils) and ROW_TILE % EMIT_ROWS == 0
    ordered = lambda w: [pl.BlockSpec((dil, ROW_TILE // dil, w), lambda i, j: (i, j, 0)) for dil in dils]
    shaped = lambda w: [jax.ShapeDtypeStruct((b * dil, s // dil, w), BF16) for dil in dils]
    widths = (ATTN_WIDTH, ATTN_WIDTH, ATTN_WIDTH)
    outs = pl.pallas_call(
        _inproj_kernel,
        grid=(b, s // ROW_TILE),
        in_specs=[
            tile(d),
            pl.BlockSpec((1, d), fixed),
            pl.BlockSpec((1, SGU_WIDTH), fixed),
            pl.BlockSpec((d, cols), fixed, pipeline_mode=pl.Buffered(1)),
        ],
        out_specs=[spec for w in widths for spec in ordered(w)] + [tile(ACT_WIDTH)],
        out_shape=[shape for w in widths for shape in shaped(w)] + [jax.ShapeDtypeStruct((b, s, ACT_WIDTH), BF16)],
        scratch_shapes=[pltpu.VMEM((len(dils) - 1, EMIT_ROWS, LANES), F32)] * ORDER_BUFFERS,
        compiler_params=pltpu.CompilerParams(
            dimension_semantics=("arbitrary", "arbitrary"), vmem_limit_bytes=VMEM_LIMIT_BYTES),
        name="inproj",
    )(x, g_norm, g_v, w_in)
    n = len(dils)
    q, k, v = ([o.reshape(b, s, -1) for o in outs[i * n:(i + 1) * n]] for i in range(3))
    return q, k, v, outs[3 * n]


SHIFT, EDGE, WHOLE = "shift", "edge", "whole"


def _bias_tile(kind, dilation, first_head):
    shape = (KEY_WINDOW, HEADS_PER_SLAB * Q_BLOCK)
    key = lax.broadcasted_iota(jnp.int32, shape, 0)
    col = lax.broadcasted_iota(jnp.int32, shape, 1)
    second = col >= Q_BLOCK
    qi = jnp.where(second, col - Q_BLOCK, col)
    if kind == SHIFT:
        rel = jnp.abs(qi + RADIUS - key)
        valid = rel <= RADIUS
    elif kind == WHOLE:
        rel = jnp.abs(qi - key)
        valid = rel <= RADIUS
    else:
        upper = key >= Q_BLOCK
        rel = jnp.abs(qi - jnp.where(upper, key - Q_BLOCK, key))
        valid = (rel <= RADIUS) & (upper == (qi >= RADIUS))
    head = (first_head + jnp.where(second, 1, 0)).astype(F32)
    slope = jnp.exp2(-8.0 * (head + 1.0) / N_ATTN_HEADS)
    dist = (rel * dilation).astype(F32)
    return jnp.where(valid, -slope * dist * LOG2_E, NEG_INF)


class _Block(NamedTuple):
    cfg: int
    tile: int
    q_rows: tuple
    k_rows: tuple
    out_rows: tuple
    out_stride: int


def _cfg_order():
    return sorted(range(len(DILATED_CONFIGS)), key=lambda c: -DILATED_CONFIGS[c][1])


def _attn_plan(seq):
    blocks, tiles = [], []
    order = _cfg_order()
    for n, c in enumerate(order):
        dil = DILATED_CONFIGS[c][1]
        nxt = DILATED_CONFIGS[order[n + 1]][1] if n + 1 < len(order) else 1
        step = dil // nxt
        cls = seq // dil
        if cls == Q_BLOCK:
            tiles.append((WHOLE, dil))
            whole = len(tiles) - 1
        else:
            tiles.append((SHIFT, dil))
            tiles.append((EDGE, dil))
            shift, edge = len(tiles) - 2, len(tiles) - 1
        for r in range(dil):
            base = r * cls
            sink = (r % nxt) * (seq // nxt) + r // nxt
            if cls == Q_BLOCK:
                blocks.append(_Block(c, whole, ((base, Q_BLOCK),), ((base, Q_BLOCK),), ((sink, Q_BLOCK),), step))
                continue
            last = cls - RADIUS
            blocks.append(_Block(c, edge, ((base, RADIUS), (base + last, RADIUS)),
                                 ((base, Q_BLOCK), (base + cls - Q_BLOCK, Q_BLOCK)),
                                 ((sink, RADIUS), (sink + step * last, RADIUS)), step))
            for j in range(cls // Q_BLOCK - 1):
                u = RADIUS + j * Q_BLOCK
                blocks.append(_Block(c, shift, ((base + u, Q_BLOCK),), ((base + j * Q_BLOCK, KEY_WINDOW),),
                                     ((sink + step * u, Q_BLOCK),), step))
    return blocks, tiles


def _rows(ref, lead, ranges, lanes=slice(None)):
    parts = [ref[lead + (pl.ds(s, n), lanes)] for s, n in ranges]
    return parts[0] if len(parts) == 1 else jnp.concatenate(parts, axis=0)


def _score_block(blk, lanes, q_ref, k_ref, bias, s_scr):
    q = _rows(q_ref, (0,), blk.q_rows, lanes)
    first = lax.broadcasted_iota(jnp.int32, (1, LANES), 1) < HEAD_DIM
    zeros = jnp.zeros_like(q)
    q2 = jnp.concatenate([jnp.where(first, q, zeros), jnp.where(first, zeros, q)], axis=0)
    k_win = _rows(k_ref, (0,), blk.k_rows, lanes)
    window = k_win.shape[0]
    s = lax.dot_general(k_win, q2, (((1,), (1,)), ((), ())), preferred_element_type=F32)
    s_scr[0, 0:window, :] = s + bias[0:window, :]


def _value_block(blk, lanes, zero, vt_scr, s_scr, prev, sink, a_ref):
    window = sum(n for _, n in blk.k_rows)
    ms, ps = [], []
    for h in range(HEADS_PER_SLAB):
        cols = slice(h * Q_BLOCK, (h + 1) * Q_BLOCK)
        s = s_scr[zero, 0:window, cols]
        m = jnp.max(s, axis=0, keepdims=True)
        ms.append(m)
        ps.append(jnp.exp2(s - m).astype(BF16))
    vt = jnp.concatenate([vt_scr[zero, :, s0:s0 + n] for s0, n in blk.k_rows], axis=1)
    ot = jnp.dot(vt, jnp.concatenate(ps, axis=1), preferred_element_type=F32)
    if prev is not None:
        o_prev, lse_prev = (_rows(ref, (), blk.q_rows) for ref in prev)
    last = prev is not None and sink is None
    if last:
        lse_back = lse_prev.T
    outs, lses, keeps = [], [], []
    for h in range(HEADS_PER_SLAB):
        cols = slice(h * Q_BLOCK, (h + 1) * Q_BLOCK)
        l = ot[LANES:LANES + 1, cols]
        lses.append(ms[h] + jnp.log2(l))
        scale = 1.0 / l
        if last:
            w = 1.0 / (1.0 + jnp.exp2(lse_back[h * HEAD_DIM:h * HEAD_DIM + 1] - lses[h]))
            scale = w * scale
            keeps.append(1.0 - w)
        outs.append(ot[h * HEAD_DIM:(h + 1) * HEAD_DIM, cols] * scale)

    def onto_rows(rows):
        return jnp.concatenate([jnp.broadcast_to(x, (HEAD_DIM, Q_BLOCK)) for x in rows], axis=0).T

    o = jnp.concatenate(outs, axis=0).T
    if last:
        o = o + onto_rows(keeps) * o_prev
    else:
        lse = onto_rows(lses)
        if prev is not None:
            top = jnp.maximum(lse, lse_prev)
            e, e_prev = jnp.exp2(lse - top), jnp.exp2(lse_prev - top)
            den = e + e_prev
            o = (e * o + e_prev * o_prev) * (1.0 / den)
            lse = top + jnp.log2(den)
    at = 0
    for start, n in blk.out_rows:
        dst = pl.ds(start, n) if blk.out_stride == 1 else pl.ds(start, n, stride=blk.out_stride)
        if sink is None:
            a_ref[0, dst, lanes] = o[at:at + n].astype(a_ref.dtype)
        else:
            sink[0][dst, :] = o[at:at + n]
            sink[1][dst, :] = lse[at:at + n]
        at += n


def _attn_kernel(*refs, seq):
    n_cfg = len(DILATED_CONFIGS)
    q_refs, k_refs, v_refs = refs[0:n_cfg], refs[n_cfg:2 * n_cfg], refs[2 * n_cfg:3 * n_cfg]
    a_ref, bias_scr = refs[3 * n_cfg:3 * n_cfg + 2]
    scrs = refs[3 * n_cfg + 2:]
    n_vt, n_hand = SLABS_PER_STEP * n_cfg, SLABS_PER_STEP * (n_cfg - 1)
    vt_scrs = [scrs[p * n_cfg:(p + 1) * n_cfg] for p in range(SLABS_PER_STEP)]
    hand = [[scrs[n_vt + 2 * (p * (n_cfg - 1) + i):n_vt + 2 * (p * (n_cfg - 1) + i) + 2] for i in range(n_cfg - 1)]
            for p in range(SLABS_PER_STEP)]
    s_scrs = scrs[n_vt + 2 * n_hand:]
    blocks, tiles = _attn_plan(seq)
    order = _cfg_order()
    slabs = [slice(p * LANES, (p + 1) * LANES) for p in range(SLABS_PER_STEP)]

    @pl.when(pl.program_id(1) == 0)
    def _():
        for p in range(SLABS_PER_STEP):
            first_head = (pl.program_id(0) * SLABS_PER_STEP + p) * HEADS_PER_SLAB
            for t, (kind, dil) in enumerate(tiles):
                bias_scr[p * len(tiles) + t] = _bias_tile(kind, dil, first_head)

    for p in range(SLABS_PER_STEP):
        for c in order:
            vt_scrs[p][c][0, LANES:LANES + BF16_ROWS, :] = _ones_rows(seq)
            for i in range(seq // LANES):
                chunk = slice(i * LANES, (i + 1) * LANES)
                vt_scrs[p][c][0, 0:LANES, chunk] = v_refs[c][0, chunk, slabs[p]].T

    zero = jnp.minimum(pl.program_id(1), 0)
    work = [(p, blk) for blk in blocks for p in range(SLABS_PER_STEP)]
    for t in range(len(work) + SCORE_LAG):
        if t < len(work):
            p, blk = work[t]
            _score_block(blk, slabs[p], q_refs[blk.cfg], k_refs[blk.cfg], bias_scr.at[p * len(tiles) + blk.tile],
                         s_scrs[t % SCORE_SLOTS])
        if t >= SCORE_LAG:
            u = t - SCORE_LAG
            p, blk = work[u]
            n = order.index(blk.cfg)
            _value_block(blk, slabs[p], zero, vt_scrs[p][blk.cfg], s_scrs[u % SCORE_SLOTS],
                         hand[p][n - 1] if n > 0 else None, hand[p][n] if n + 1 < n_cfg else None, a_ref)


def _attn(q, k, v):
    b, s, w = k[0].shape
    n_cfg = len(DILATED_CONFIGS)
    assert all(s % (dil * Q_BLOCK) == 0 for _, dil in DILATED_CONFIGS)
    n_tiles = len(_attn_plan(s)[1])
    blk = pl.BlockSpec((1, s, SLABS_PER_STEP * LANES), lambda j, i: (i, 0, j))
    return pl.pallas_call(
        functools.partial(_attn_kernel, seq=s),
        grid=(w // (SLABS_PER_STEP * LANES), b),
        in_specs=[blk] * (3 * n_cfg),
        out_specs=blk,
        out_shape=jax.ShapeDtypeStruct((b, s, w), BF16),
        scratch_shapes=[
            pltpu.VMEM((SLABS_PER_STEP * n_tiles, KEY_WINDOW, HEADS_PER_SLAB * Q_BLOCK), F32),
        ] + [pltpu.VMEM((1, LANES + BF16_ROWS, s), BF16)] * (SLABS_PER_STEP * n_cfg)
        + [pltpu.VMEM((s, LANES), F32)] * (2 * SLABS_PER_STEP * (n_cfg - 1))
        + [pltpu.VMEM((1, KEY_WINDOW, HEADS_PER_SLAB * Q_BLOCK), F32)] * SCORE_SLOTS,
        compiler_params=pltpu.CompilerParams(
            dimension_semantics=("arbitrary", "arbitrary"), vmem_limit_bytes=VMEM_LIMIT_BYTES),
        name="attn",
    )(*q, *k, *v)


def _mix_kernel(a_ref, act_ref, mem_ref, gm_ref, wkv_ref, x_ref, ws_ref, bs_ref, wo_ref, gf_ref, o_ref,
                km_ref, vtm_ref):
    @pl.when(pl.program_id(1) == 0)
    def _():
        hm = _rms(mem_ref[0], gm_ref[...]).astype(BF16)
        kv = jnp.dot(hm, wkv_ref[...], preferred_element_type=F32)
        km_ref[...] = kv[:, 0:MEM_WIDTH].astype(BF16)
        vtm_ref[0:MEM_WIDTH, :] = kv[:, MEM_WIDTH:].T.astype(BF16)
        vtm_ref[MEM_WIDTH:, :] = _ones_rows(kv.shape[0])

    tm = x_ref.shape[1]
    parts = [pl.ds(lo, tm // BRANCH_PARTS) for lo in range(0, tm, tm // BRANCH_PARTS)]
    gated = _mix_branches(parts, a_ref, act_ref, km_ref, vtm_ref, ws_ref, bs_ref)
    for lo in range(0, tm, PROJECT_ROWS):
        _mix_project(pl.ds(lo, PROJECT_ROWS), [g[lo:lo + PROJECT_ROWS] for g in gated], x_ref, wo_ref, gf_ref, o_ref)


def _mix_branches(parts, a_ref, act_ref, km_ref, vtm_ref, ws_ref, bs_ref):
    tm = parts[0].size
    lo = 0
    cols = []
    for width in ACT_SPLITS:
        cols.append(slice(lo, lo + width))
        lo += width
    acts = [[act_ref[0, rows, c] for c in cols] for rows in parts]

    mhead = lax.broadcasted_iota(jnp.int32, (tm, MEM_WIDTH), 1) // MEM_HEAD_DIM
    scores = []
    for _, _, _, qm, _ in acts:
        q4 = jnp.concatenate([jnp.where(mhead == h, qm, jnp.zeros_like(qm)) for h in range(N_MEM_HEADS)], axis=0)
        scores.append(lax.dot_general(km_ref[...], q4, (((1,), (1,)), ((), ())), preferred_element_type=F32))
    ps = [jnp.exp2(s - jnp.max(s, axis=0, keepdims=True)).astype(BF16) for s in scores]
    ones = vtm_ref[MEM_WIDTH:, :]
    mos = [[] for _ in parts]
    for h in range(N_MEM_HEADS):
        vt = jnp.concatenate([vtm_ref[h * MEM_HEAD_DIM:(h + 1) * MEM_HEAD_DIM, :], ones], axis=0)
        for mo, p in zip(mos, ps):
            ot = jnp.dot(vt, p[:, h * tm:(h + 1) * tm], preferred_element_type=F32)
            mo.append(ot[0:MEM_HEAD_DIM] * (1.0 / ot[MEM_HEAD_DIM:MEM_HEAD_DIM + 1]))
    gm = [(act[4].astype(F32) * jnp.concatenate(mo, axis=0).T).astype(BF16) for act, mo in zip(acts, mos)]

    group = lax.broadcasted_iota(jnp.int32, (SGU_CHUNK, SGU_WIDTH), 1) // SGU_GROUP
    gb = []
    for _, gu, vn, _, _ in acts:
        mixed = []
        for c in range(tm // SGU_CHUNK):
            vc = vn[c * SGU_CHUNK:(c + 1) * SGU_CHUNK]
            stacked = jnp.concatenate([jnp.where(group == g, vc, jnp.zeros_like(vc)) for g in range(N_SGU_GROUPS)],
                                      axis=0)
            mixed.append(jnp.dot(ws_ref[...], stacked, preferred_element_type=F32) + bs_ref[...])
        gb.append((gu.astype(F32) * jnp.concatenate(mixed, axis=0)).astype(BF16))

    ga = [act[0] * a_ref[0, rows, :] for act, rows in zip(acts, parts)]
    return [jnp.concatenate(x, axis=0) for x in (ga, gb, gm)]


def _mix_project(rows, gated, x_ref, wo_ref, gf_ref, o_ref):
    ga, gb, gm = gated
    e1 = ATTN_WIDTH
    e2 = e1 + SGU_WIDTH
    y = jnp.dot(ga, wo_ref[0:e1, :], preferred_element_type=F32)
    y = y + jnp.dot(gb, wo_ref[e1:e2, :], preferred_element_type=F32)
    y = y + jnp.dot(gm, wo_ref[e2:, :], preferred_element_type=F32)
    o_ref[0, rows, :] = _rms(x_ref[0, rows, :] + y, gf_ref[...])


def _mix(a, act, mem, g_mem, w_kv, x, w_s, b_tile, w_out, g_final):
    b, s, d = x.shape
    m = mem.shape[1]
    tile = lambda w: pl.BlockSpec((1, MIX_TILE, w), lambda i, j: (i, j, 0))
    fixed = lambda arr: pl.BlockSpec(arr.shape, lambda i, j: (0, 0))
    return pl.pallas_call(
        _mix_kernel,
        grid=(b, s // MIX_TILE),
        in_specs=[tile(a.shape[2]), tile(act.shape[2]), pl.BlockSpec((1, m, d), lambda i, j: (i, 0, 0)),
                  fixed(g_mem), fixed(w_kv), tile(d), fixed(w_s), fixed(b_tile), fixed(w_out), fixed(g_final)],
        out_specs=tile(d),
        out_shape=jax.ShapeDtypeStruct((b, s, d), x.dtype),
        scratch_shapes=[
            pltpu.VMEM((m, MEM_WIDTH), BF16),
            pltpu.VMEM((MEM_WIDTH + BF16_ROWS, m), BF16),
        ],
        compiler_params=pltpu.CompilerParams(
            dimension_semantics=("arbitrary", "arbitrary"), vmem_limit_bytes=VMEM_LIMIT_BYTES),
        name="mix",
    )(a, act, mem, g_mem, w_kv, x, w_s, b_tile, w_out, g_final)


def kernel(x, mem, g_norm, w_in, w_sgu_spatial, b_sgu_spatial, g_sgu_v, g_mem, w_mem_kv, w_out, g_final):
    assert g_norm.shape[0] == 1, "the final norm is fused into the single layer's last kernel"
    b, s, d = x.shape
    q, k, v, act = _inproj(x, g_norm[0][None, :], g_sgu_v[0][None, :], w_in[0].astype(BF16))
    a = _attn(q, k, v)
    w_s = jnp.concatenate(list(w_sgu_spatial[0].astype(BF16)), axis=1)
    b_tile = jnp.repeat(b_sgu_spatial[0].T, SGU_GROUP, axis=1)
    return _mix(a, act, mem, g_mem[0][None, :], w_mem_kv[0].astype(BF16), x, w_s, b_tile, w_out[0].astype(BF16),
                g_final[None, :])
```

```python
import functools
import math
from typing import NamedTuple

import jax
import jax.numpy as jnp
from jax import lax
from jax.experimental import pallas as pl
from jax.experimental.pallas import tpu as pltpu

F32 = jnp.float32
BF16 = jnp.bfloat16

EPS = 1e-6
NEG_INF = -1e30
LOG2_E = math.log2(math.e)

HEAD_DIM = 64
N_ATTN_HEADS = 8
ATTN_WIDTH = HEAD_DIM * N_ATTN_HEADS
DILATED_CONFIGS = ((128, 1), (512, 4), (2048, 16))
RADIUS = 64
SGU_WIDTH = 256
N_SGU_GROUPS = 4
SGU_GROUP = SGU_WIDTH // N_SGU_GROUPS
SGU_CHUNK = 128
MEM_WIDTH = 256
N_MEM_HEADS = 4
MEM_HEAD_DIM = MEM_WIDTH // N_MEM_HEADS
ACT_SPLITS = (ATTN_WIDTH, SGU_WIDTH, SGU_WIDTH, MEM_WIDTH, MEM_WIDTH)
ACT_WIDTH = sum(ACT_SPLITS)

LANES = 128
BF16_ROWS = 16
MXU_COLS = 256
Q_BLOCK = 128
KEY_WINDOW = 2 * Q_BLOCK
HEADS_PER_SLAB = LANES // HEAD_DIM
SLABS_PER_STEP = 2
SCORE_LAG = 4 * SLABS_PER_STEP
SCORE_SLOTS = 2 * SCORE_LAG
ROW_TILE = 1024
ORDER_BUFFERS = 6
EMIT_ROWS = 1024
ACT_ROWS = 256
MIX_TILE = 1024
PROJECT_ROWS = 256
X_RING = 3
BRANCH_PARTS = 2
VMEM_LIMIT_BYTES = 56 * 1024 * 1024


def _rms(x, g):
    return x * lax.rsqrt(jnp.mean(x * x, axis=-1, keepdims=True) + EPS) * g


def _silu(x):
    half = 0.5 * x
    return half + half * jnp.tanh(half)


def _ones_rows(cols):
    first = lax.broadcasted_iota(jnp.int32, (BF16_ROWS, cols), 0) == 0
    return jnp.where(first, 1.0, 0.0).astype(BF16)


def _emit_orders(val, row0, outs, lo, order_scrs):
    rows = val.shape[0]
    n_slabs = MXU_COLS // LANES

    def write(ref, r, slabs):
        n = slabs[0].shape[0]
        at = row0 * n // rows
        ref[r, at:at + n, lo:lo + MXU_COLS] = jnp.concatenate(slabs, axis=1).astype(BF16)

    by_dil = sorted(range(len(DILATED_CONFIGS)), key=lambda c: DILATED_CONFIGS[c][1])
    prev = 1
    for level, c in enumerate(by_dil):
        dil = DILATED_CONFIGS[c][1]
        if dil == 1:
            slabs = [val[:, i * LANES:(i + 1) * LANES] for i in range(n_slabs)]
            write(outs[c], 0, slabs)
            for scr, x in zip(order_scrs, slabs):
                scr[0] = x
            continue
        step, cls_prev, cls = dil // prev, rows // prev, rows // dil
        for r in range(dil):
            slabs = [scr[level - 1, pl.ds((r % prev) * cls_prev + r // prev, cls, stride=step), :]
                     for scr in order_scrs]
            write(outs[c], r, slabs)
            if level + 1 < len(by_dil):
                for scr, x in zip(order_scrs, slabs):
                    scr[level, r * cls:(r + 1) * cls, :] = x
        prev = dil


def _inproj_kernel(x_ref, g_ref, gv_ref, w_ref, *refs):
    n_cfg = len(DILATED_CONFIGS)
    q_refs, k_refs, v_refs = refs[0:n_cfg], refs[n_cfg:2 * n_cfg], refs[2 * n_cfg:3 * n_cfg]
    act_ref = refs[3 * n_cfg]
    order_scrs = refs[3 * n_cfg + 1:]
    h = _rms(x_ref[0], g_ref[...]).astype(BF16)

    n_slabs = MXU_COLS // LANES
    at_scr = 0
    for t, (out_refs, scale) in enumerate(((q_refs, HEAD_DIM ** -0.5 * LOG2_E), (k_refs, None), (v_refs, None))):
        for piece in range(ATTN_WIDTH // MXU_COLS):
            cols = slice(t * ATTN_WIDTH + piece * MXU_COLS, t * ATTN_WIDTH + (piece + 1) * MXU_COLS)
            for r0 in range(0, h.shape[0], EMIT_ROWS):
                val = jnp.dot(h[r0:r0 + EMIT_ROWS], w_ref[:, cols], preferred_element_type=F32)
                if scale is not None:
                    val = val * scale
                _emit_orders(val, r0, out_refs, piece * MXU_COLS, order_scrs[at_scr:at_scr + n_slabs])
                at_scr = (at_scr + n_slabs) % len(order_scrs)
    for r0 in range(0, h.shape[0], ACT_ROWS):
        hp = h[r0:r0 + ACT_ROWS]
        lo = 3 * ATTN_WIDTH
        raw = []
        for width in (ATTN_WIDTH, SGU_WIDTH, SGU_WIDTH, SGU_WIDTH, MEM_WIDTH, MEM_WIDTH):
            raw.append(jnp.dot(hp, w_ref[:, lo:lo + width], preferred_element_type=F32))
            lo += width
        za, ub, vb, zb, qm, zm = raw
        acts = (_silu(za), _silu(zb) * jax.nn.gelu(ub), _rms(jax.nn.gelu(vb), gv_ref[...]),
                qm * (MEM_HEAD_DIM ** -0.5 * LOG2_E), _silu(zm))
        lo = 0
        for width, val in zip(ACT_SPLITS, acts):
            act_ref[0, r0:r0 + ACT_ROWS, lo:lo + width] = val.astype(BF16)
            lo += width


def _inproj(x, g_norm, g_v, w_in):
    b, s, d = x.shape
    cols = w_in.shape[1]
    assert cols == 4 * ATTN_WIDTH + 3 * SGU_WIDTH + 2 * MEM_WIDTH
    tile = lambda w: pl.BlockSpec((1, ROW_TILE, w), lambda i, j: (i, j, 0))
    fixed = lambda i, j: (0, 0)
    dils = [dil for _, dil in DILATED_CONFIGS]
    assert all(EMIT_ROWS % (dil * BF16_ROWS) == 0 for dil in dils) and ROW_TILE % EMIT_ROWS == 0
    ordered = lambda w: [pl.BlockSpec((dil, ROW_TILE // dil, w), lambda i, j: (i, j, 0)) for dil in dils]
    shaped = lambda w: [jax.ShapeDtypeStruct((b * dil, s // dil, w), BF16) for dil in dils]
    widths = (ATTN_WIDTH, ATTN_WIDTH, ATTN_WIDTH)
    outs = pl.pallas_call(
        _inproj_kernel,
        grid=(b, s // ROW_TILE),
        in_specs=[
            tile(d),
            pl.BlockSpec((1, d), fixed),
            pl.BlockSpec((1, SGU_WIDTH), fixed),
            pl.BlockSpec((d, cols), fixed, pipeline_mode=pl.Buffered(1)),
        ],
        out_specs=[spec for w in widths for spec in ordered(w)] + [tile(ACT_WIDTH)],
        out_shape=[shape for w in widths for shape in shaped(w)] + [jax.ShapeDtypeStruct((b, s, ACT_WIDTH), BF16)],
        scratch_shapes=[pltpu.VMEM((len(dils) - 1, EMIT_ROWS, LANES), F32)] * ORDER_BUFFERS,
        compiler_params=pltpu.CompilerParams(
            dimension_semantics=("arbitrary", "arbitrary"), vmem_limit_bytes=VMEM_LIMIT_BYTES),
        name="inproj",
    )(x, g_norm, g_v, w_in)
    n = len(dils)
    q, k, v = ([o.reshape(b, s, -1) for o in outs[i * n:(i + 1) * n]] for i in range(3))
    return q, k, v, outs[3 * n]


SHIFT, EDGE, WHOLE = "shift", "edge", "whole"


def _bias_tile(kind, dilation, first_head):
    shape = (KEY_WINDOW, HEADS_PER_SLAB * Q_BLOCK)
    key = lax.broadcasted_iota(jnp.int32, shape, 0)
    col = lax.broadcasted_iota(jnp.int32, shape, 1)
    second = col >= Q_BLOCK
    qi = jnp.where(second, col - Q_BLOCK, col)
    if kind == SHIFT:
        rel = jnp.abs(qi + RADIUS - key)
        valid = rel <= RADIUS
    elif kind == WHOLE:
        rel = jnp.abs(qi - key)
        valid = rel <= RADIUS
    else:
        upper = key >= Q_BLOCK
        rel = jnp.abs(qi - jnp.where(upper, key - Q_BLOCK, key))
        valid = (rel <= RADIUS) & (upper == (qi >= RADIUS))
    head = (first_head + jnp.where(second, 1, 0)).astype(F32)
    slope = jnp.exp2(-8.0 * (head + 1.0) / N_ATTN_HEADS)
    dist = (rel * dilation).astype(F32)
    return jnp.where(valid, -slope * dist * LOG2_E, NEG_INF)


class _Block(NamedTuple):
    cfg: int
    tile: int
    q_rows: tuple
    k_rows: tuple
    out_rows: tuple
    out_stride: int


def _cfg_order():
    return sorted(range(len(DILATED_CONFIGS)), key=lambda c: -DILATED_CONFIGS[c][1])


def _attn_plan(seq):
    blocks, tiles = [], []
    order = _cfg_order()
    for n, c in enumerate(order):
        dil = DILATED_CONFIGS[c][1]
        nxt = DILATED_CONFIGS[order[n + 1]][1] if n + 1 < len(order) else 1
        step = dil // nxt
        cls = seq // dil
        if cls == Q_BLOCK:
            tiles.append((WHOLE, dil))
            whole = len(tiles) - 1
        else:
            tiles.append((SHIFT, dil))
            tiles.append((EDGE, dil))
            shift, edge = len(tiles) - 2, len(tiles) - 1
        for r in range(dil):
            base = r * cls
            sink = (r % nxt) * (seq // nxt) + r // nxt
            if cls == Q_BLOCK:
                blocks.append(_Block(c, whole, ((base, Q_BLOCK),), ((base, Q_BLOCK),), ((sink, Q_BLOCK),), step))
                continue
            last = cls - RADIUS
            blocks.append(_Block(c, edge, ((base, RADIUS), (base + last, RADIUS)),
                                 ((base, Q_BLOCK), (base + cls - Q_BLOCK, Q_BLOCK)),
                                 ((sink, RADIUS), (sink + step * last, RADIUS)), step))
            for j in range(cls // Q_BLOCK - 1):
                u = RADIUS + j * Q_BLOCK
                blocks.append(_Block(c, shift, ((base + u, Q_BLOCK),), ((base + j * Q_BLOCK, KEY_WINDOW),),
                                     ((sink + step * u, Q_BLOCK),), step))
    return blocks, tiles


def _rows(ref, lead, ranges, lanes=slice(None)):
    parts = [ref[lead + (pl.ds(s, n), lanes)] for s, n in ranges]
    return parts[0] if len(parts) == 1 else jnp.concatenate(parts, axis=0)


def _score_block(blk, lanes, q_ref, k_ref, bias, s_scr):
    q = _rows(q_ref, (0,), blk.q_rows, lanes)
    first = lax.broadcasted_iota(jnp.int32, (1, LANES), 1) < HEAD_DIM
    zeros = jnp.zeros_like(q)
    q2 = jnp.concatenate([jnp.where(first, q, zeros), jnp.where(first, zeros, q)], axis=0)
    k_win = _rows(k_ref, (0,), blk.k_rows, lanes)
    window = k_win.shape[0]
    s = lax.dot_general(k_win, q2, (((1,), (1,)), ((), ())), preferred_element_type=F32)
    s_scr[0, 0:window, :] = s + bias[0:window, :]


def _value_block(blk, lanes, zero, vt_scr, s_scr, prev, sink, a_ref):
    window = sum(n for _, n in blk.k_rows)
    ms, ps = [], []
    for h in range(HEADS_PER_SLAB):
        cols = slice(h * Q_BLOCK, (h + 1) * Q_BLOCK)
        s = s_scr[zero, 0:window, cols]
        m = jnp.max(s, axis=0, keepdims=True)
        ms.append(m)
        ps.append(jnp.exp2(s - m).astype(BF16))
    vt = jnp.concatenate([vt_scr[zero, :, s0:s0 + n] for s0, n in blk.k_rows], axis=1)
    ot = jnp.dot(vt, jnp.concatenate(ps, axis=1), preferred_element_type=F32)
    if prev is not None:
        o_prev, lse_prev = (_rows(ref, (), blk.q_rows) for ref in prev)
    last = prev is not None and sink is None
    if last:
        lse_back = lse_prev.T
    outs, lses, keeps = [], [], []
    for h in range(HEADS_PER_SLAB):
        cols = slice(h * Q_BLOCK, (h + 1) * Q_BLOCK)
        l = ot[LANES:LANES + 1, cols]
        lses.append(ms[h] + jnp.log2(l))
        scale = 1.0 / l
        if last:
            w = 1.0 / (1.0 + jnp.exp2(lse_back[h * HEAD_DIM:h * HEAD_DIM + 1] - lses[h]))
            scale = w * scale
            keeps.append(1.0 - w)
        outs.append(ot[h * HEAD_DIM:(h + 1) * HEAD_DIM, cols] * scale)

    def onto_rows(rows):
        return jnp.concatenate([jnp.broadcast_to(x, (HEAD_DIM, Q_BLOCK)) for x in rows], axis=0).T

    o = jnp.concatenate(outs, axis=0).T
    if last:
        o = o + onto_rows(keeps) * o_prev
    else:
        lse = onto_rows(lses)
        if prev is not None:
            top = jnp.maximum(lse, lse_prev)
            e, e_prev = jnp.exp2(lse - top), jnp.exp2(lse_prev - top)
            den = e + e_prev
            o = (e * o + e_prev * o_prev) * (1.0 / den)
            lse = top + jnp.log2(den)
    at = 0
    for start, n in blk.out_rows:
        dst = pl.ds(start, n) if blk.out_stride == 1 else pl.ds(start, n, stride=blk.out_stride)
        if sink is None:
            a_ref[0, dst, lanes] = o[at:at + n].astype(a_ref.dtype)
        else:
            sink[0][dst, :] = o[at:at + n]
            sink[1][dst, :] = lse[at:at + n]
        at += n


def _attn_kernel(*refs, seq):
    n_cfg = len(DILATED_CONFIGS)
    q_refs, k_refs, v_refs = refs[0:n_cfg], refs[n_cfg:2 * n_cfg], refs[2 * n_cfg:3 * n_cfg]
    a_ref, bias_scr = refs[3 * n_cfg:3 * n_cfg + 2]
    scrs = refs[3 * n_cfg + 2:]
    n_vt, n_hand = SLABS_PER_STEP * n_cfg, SLABS_PER_STEP * (n_cfg - 1)
    vt_scrs = [scrs[p * n_cfg:(p + 1) * n_cfg] for p in range(SLABS_PER_STEP)]
    hand = [[scrs[n_vt + 2 * (p * (n_cfg - 1) + i):n_vt + 2 * (p * (n_cfg - 1) + i) + 2] for i in range(n_cfg - 1)]
            for p in range(SLABS_PER_STEP)]
    s_scrs = scrs[n_vt + 2 * n_hand:]
    blocks, tiles = _attn_plan(seq)
    order = _cfg_order()
    slabs = [slice(p * LANES, (p + 1) * LANES) for p in range(SLABS_PER_STEP)]

    @pl.when(pl.program_id(1) == 0)
    def _():
        for p in range(SLABS_PER_STEP):
            first_head = (pl.program_id(0) * SLABS_PER_STEP + p) * HEADS_PER_SLAB
            for t, (kind, dil) in enumerate(tiles):
                bias_scr[p * len(tiles) + t] = _bias_tile(kind, dil, first_head)

    for p in range(SLABS_PER_STEP):
        for c in order:
            vt_scrs[p][c][0, LANES:LANES + BF16_ROWS, :] = _ones_rows(seq)
            for i in range(seq // LANES):
                chunk = slice(i * LANES, (i + 1) * LANES)
                vt_scrs[p][c][0, 0:LANES, chunk] = v_refs[c][0, chunk, slabs[p]].T

    zero = jnp.minimum(pl.program_id(1), 0)
    work = [(p, blk) for blk in blocks for p in range(SLABS_PER_STEP)]
    for t in range(len(work) + SCORE_LAG):
        if t < len(work):
            p, blk = work[t]
            _score_block(blk, slabs[p], q_refs[blk.cfg], k_refs[blk.cfg], bias_scr.at[p * len(tiles) + blk.tile],
                         s_scrs[t % SCORE_SLOTS])
        if t >= SCORE_LAG:
            u = t - SCORE_LAG
            p, blk = work[u]
            n = order.index(blk.cfg)
            _value_block(blk, slabs[p], zero, vt_scrs[p][blk.cfg], s_scrs[u % SCORE_SLOTS],
                         hand[p][n - 1] if n > 0 else None, hand[p][n] if n + 1 < n_cfg else None, a_ref)


def _attn(q, k, v):
    b, s, w = k[0].shape
    n_cfg = len(DILATED_CONFIGS)
    assert all(s % (dil * Q_BLOCK) == 0 for _, dil in DILATED_CONFIGS)
    n_tiles = len(_attn_plan(s)[1])
    blk = pl.BlockSpec((1, s, SLABS_PER_STEP * LANES), lambda j, i: (i, 0, j))
    return pl.pallas_call(
        functools.partial(_attn_kernel, seq=s),
        grid=(w // (SLABS_PER_STEP * LANES), b),
        in_specs=[blk] * (3 * n_cfg),
        out_specs=blk,
        out_shape=jax.ShapeDtypeStruct((b, s, w), BF16),
        scratch_shapes=[
            pltpu.VMEM((SLABS_PER_STEP * n_tiles, KEY_WINDOW, HEADS_PER_SLAB * Q_BLOCK), F32),
        ] + [pltpu.VMEM((1, LANES + BF16_ROWS, s), BF16)] * (SLABS_PER_STEP * n_cfg)
        + [pltpu.VMEM((s, LANES), F32)] * (2 * SLABS_PER_STEP * (n_cfg - 1))
        + [pltpu.VMEM((1, KEY_WINDOW, HEADS_PER_SLAB * Q_BLOCK), F32)] * SCORE_SLOTS,
        compiler_params=pltpu.CompilerParams(
            dimension_semantics=("arbitrary", "arbitrary"), vmem_limit_bytes=VMEM_LIMIT_BYTES),
        name="attn",
    )(*q, *k, *v)


def _x_tile_copy(x_hbm, x_ring, sem, step, tiles_per_batch):
    slot = step % X_RING
    tm = x_ring.shape[1]
    src = x_hbm.at[step // tiles_per_batch, pl.ds((step % tiles_per_batch) * tm, tm), :]
    return pltpu.make_async_copy(src, x_ring.at[slot], sem.at[slot])


def _mix_kernel(a_ref, act_ref, mem_ref, gm_ref, wkv_ref, x_hbm, ws_ref, bs_ref, wo_ref, gf_ref, o_ref,
                km_ref, vtm_ref, x_ring, x_sem):
    n_i, n_j = pl.num_programs(0), pl.num_programs(1)
    step = pl.program_id(0) * n_j + pl.program_id(1)
    ahead = X_RING - 1

    @pl.when(step == 0)
    def _():
        for s in range(ahead):
            _x_tile_copy(x_hbm, x_ring, x_sem, s, n_j).start()

    @pl.when(step + ahead < n_i * n_j)
    def _():
        _x_tile_copy(x_hbm, x_ring, x_sem, step + ahead, n_j).start()

    _x_tile_copy(x_hbm, x_ring, x_sem, step, n_j).wait()
    x_ref = x_ring.at[pl.ds(step % X_RING, 1)]

    @pl.when(pl.program_id(1) == 0)
    def _():
        hm = _rms(mem_ref[0], gm_ref[...]).astype(BF16)
        kv = jnp.dot(hm, wkv_ref[...], preferred_element_type=F32)
        km_ref[...] = kv[:, 0:MEM_WIDTH].astype(BF16)
        vtm_ref[0:MEM_WIDTH, :] = kv[:, MEM_WIDTH:].T.astype(BF16)
        vtm_ref[MEM_WIDTH:, :] = _ones_rows(kv.shape[0])

    tm = x_ref.shape[1]
    parts = [pl.ds(lo, tm // BRANCH_PARTS) for lo in range(0, tm, tm // BRANCH_PARTS)]
    gated = _mix_branches(parts, a_ref, act_ref, km_ref, vtm_ref, ws_ref, bs_ref)
    for lo in range(0, tm, PROJECT_ROWS):
        _mix_project(pl.ds(lo, PROJECT_ROWS), [g[lo:lo + PROJECT_ROWS] for g in gated], x_ref, wo_ref, gf_ref, o_ref)


def _mix_branches(parts, a_ref, act_ref, km_ref, vtm_ref, ws_ref, bs_ref):
    tm = parts[0].size
    lo = 0
    cols = []
    for width in ACT_SPLITS:
        cols.append(slice(lo, lo + width))
        lo += width
    acts = [[act_ref[0, rows, c] for c in cols] for rows in parts]

    mhead = lax.broadcasted_iota(jnp.int32, (tm, MEM_WIDTH), 1) // MEM_HEAD_DIM
    scores = []
    for _, _, _, qm, _ in acts:
        q4 = jnp.concatenate([jnp.where(mhead == h, qm, jnp.zeros_like(qm)) for h in range(N_MEM_HEADS)], axis=0)
        scores.append(lax.dot_general(km_ref[...], q4, (((1,), (1,)), ((), ())), preferred_element_type=F32))
    ps = [jnp.exp2(s - jnp.max(s, axis=0, keepdims=True)).astype(BF16) for s in scores]
    ones = vtm_ref[MEM_WIDTH:, :]
    mos = [[] for _ in parts]
    for h in range(N_MEM_HEADS):
        vt = jnp.concatenate([vtm_ref[h * MEM_HEAD_DIM:(h + 1) * MEM_HEAD_DIM, :], ones], axis=0)
        for mo, p in zip(mos, ps):
            ot = jnp.dot(vt, p[:, h * tm:(h + 1) * tm], preferred_element_type=F32)
            mo.append(ot[0:MEM_HEAD_DIM] * (1.0 / ot[MEM_HEAD_DIM:MEM_HEAD_DIM + 1]))
    gm = [(act[4].astype(F32) * jnp.concatenate(mo, axis=0).T).astype(BF16) for act, mo in zip(acts, mos)]

    group = lax.broadcasted_iota(jnp.int32, (SGU_CHUNK, SGU_WIDTH), 1) // SGU_GROUP
    gb = []
    for _, gu, vn, _, _ in acts:
        mixed = []
        for c in range(tm // SGU_CHUNK):
            vc = vn[c * SGU_CHUNK:(c + 1) * SGU_CHUNK]
            stacked = jnp.concatenate([jnp.where(group == g, vc, jnp.zeros_like(vc)) for g in range(N_SGU_GROUPS)],
                                      axis=0)
            mixed.append(jnp.dot(ws_ref[...], stacked, preferred_element_type=F32) + bs_ref[...])
        gb.append((gu.astype(F32) * jnp.concatenate(mixed, axis=0)).astype(BF16))

    ga = [act[0] * a_ref[0, rows, :] for act, rows in zip(acts, parts)]
    return [jnp.concatenate(x, axis=0) for x in (ga, gb, gm)]


def _mix_project(rows, gated, x_ref, wo_ref, gf_ref, o_ref):
    ga, gb, gm = gated
    e1 = ATTN_WIDTH
    e2 = e1 + SGU_WIDTH
    y = jnp.dot(ga, wo_ref[0:e1, :], preferred_element_type=F32)
    y = y + jnp.dot(gb, wo_ref[e1:e2, :], preferred_element_type=F32)
    y = y + jnp.dot(gm, wo_ref[e2:, :], preferred_element_type=F32)
    o_ref[0, rows, :] = _rms(x_ref[0, rows, :] + y, gf_ref[...])


def _mix(a, act, mem, g_mem, w_kv, x, w_s, b_tile, w_out, g_final):
    b, s, d = x.shape
    m = mem.shape[1]
    assert b * (s // MIX_TILE) >= X_RING - 1
    tile = lambda w: pl.BlockSpec((1, MIX_TILE, w), lambda i, j: (i, j, 0))
    fixed = lambda arr: pl.BlockSpec(arr.shape, lambda i, j: (0, 0))
    return pl.pallas_call(
        _mix_kernel,
        grid=(b, s // MIX_TILE),
        in_specs=[tile(a.shape[2]), tile(act.shape[2]), pl.BlockSpec((1, m, d), lambda i, j: (i, 0, 0)),
                  fixed(g_mem), fixed(w_kv), pl.BlockSpec(memory_space=pl.ANY), fixed(w_s), fixed(b_tile),
                  fixed(w_out), fixed(g_final)],
        out_specs=tile(d),
        out_shape=jax.ShapeDtypeStruct((b, s, d), x.dtype),
        scratch_shapes=[
            pltpu.VMEM((m, MEM_WIDTH), BF16),
            pltpu.VMEM((MEM_WIDTH + BF16_ROWS, m), BF16),
            pltpu.VMEM((X_RING, MIX_TILE, d), x.dtype),
            pltpu.SemaphoreType.DMA((X_RING,)),
        ],
        compiler_params=pltpu.CompilerParams(
            dimension_semantics=("arbitrary", "arbitrary"), vmem_limit_bytes=VMEM_LIMIT_BYTES),
        name="mix",
    )(a, act, mem, g_mem, w_kv, x, w_s, b_tile, w_out, g_final)


def kernel(x, mem, g_norm, w_in, w_sgu_spatial, b_sgu_spatial, g_sgu_v, g_mem, w_mem_kv, w_out, g_final):
    assert g_norm.shape[0] == 1, "the final norm is fused into the single layer's last kernel"
    b, s, d = x.shape
    q, k, v, act = _inproj(x, g_norm[0][None, :], g_sgu_v[0][None, :], w_in[0].astype(BF16))
    a = _attn(q, k, v)
    w_s = jnp.concatenate(list(w_sgu_spatial[0].astype(BF16)), axis=1)
    b_tile = jnp.repeat(b_sgu_spatial[0].T, SGU_GROUP, axis=1)
    return _mix(a, act, mem, g_mem[0][None, :], w_mem_kv[0].astype(BF16), x, w_s, b_tile, w_out[0].astype(BF16),
                g_final[None, :])
```

```python
import functools
import math
from typing import NamedTuple

import jax
import jax.numpy as jnp
from jax import lax
from jax.experimental import pallas as pl
from jax.experimental.pallas import tpu as pltpu

F32 = jnp.float32
BF16 = jnp.bfloat16

EPS = 1e-6
NEG_INF = -1e30
LOG2_E = math.log2(math.e)

HEAD_DIM = 64
N_ATTN_HEADS = 8
ATTN_WIDTH = HEAD_DIM * N_ATTN_HEADS
DILATED_CONFIGS = ((128, 1), (512, 4), (2048, 16))
RADIUS = 64
SGU_WIDTH = 256
N_SGU_GROUPS = 4
SGU_GROUP = SGU_WIDTH // N_SGU_GROUPS
SGU_CHUNK = 128
MEM_WIDTH = 256
N_MEM_HEADS = 4
MEM_HEAD_DIM = MEM_WIDTH // N_MEM_HEADS
ACT_SPLITS = (ATTN_WIDTH, SGU_WIDTH, SGU_WIDTH, MEM_WIDTH, MEM_WIDTH)
ACT_WIDTH = sum(ACT_SPLITS)

LANES = 128
BF16_ROWS = 16
MXU_COLS = 256
Q_BLOCK = 128
KEY_WINDOW = 2 * Q_BLOCK
HEADS_PER_SLAB = LANES // HEAD_DIM
SLABS_PER_STEP = 2
SCORE_LAG = 4 * SLABS_PER_STEP
SCORE_SLOTS = 2 * SCORE_LAG
ROW_TILE = 1024
ORDER_BUFFERS = 6
EMIT_ROWS = 1024
ACT_ROWS = 256
MIX_TILE = 1024
PROJECT_ROWS = 256
X_RING = 3
BRANCH_PARTS = 2
VMEM_LIMIT_BYTES = 56 * 1024 * 1024


def _rms(x, g):
    return x * lax.rsqrt(jnp.mean(x * x, axis=-1, keepdims=True) + EPS) * g


def _silu(x):
    half = 0.5 * x
    return half + half * jnp.tanh(half)


def _ones_rows(cols):
    first = lax.broadcasted_iota(jnp.int32, (BF16_ROWS, cols), 0) == 0
    return jnp.where(first, 1.0, 0.0).astype(BF16)


def _emit_orders(val, row0, outs, lo, order_scrs):
    rows = val.shape[0]
    n_slabs = MXU_COLS // LANES

    def write(ref, r, slabs):
        n = slabs[0].shape[0]
        at = row0 * n // rows
        ref[r, at:at + n, lo:lo + MXU_COLS] = jnp.concatenate(slabs, axis=1).astype(BF16)

    by_dil = sorted(range(len(DILATED_CONFIGS)), key=lambda c: DILATED_CONFIGS[c][1])
    prev = 1
    for level, c in enumerate(by_dil):
        dil = DILATED_CONFIGS[c][1]
        if dil == 1:
            slabs = [val[:, i * LANES:(i + 1) * LANES] for i in range(n_slabs)]
            write(outs[c], 0, slabs)
            for scr, x in zip(order_scrs, slabs):
                scr[0] = x
            continue
        step, cls_prev, cls = dil // prev, rows // prev, rows // dil
        for r in range(dil):
            slabs = [scr[level - 1, pl.ds((r % prev) * cls_prev + r // prev, cls, stride=step), :]
                     for scr in order_scrs]
            write(outs[c], r, slabs)
            if level + 1 < len(by_dil):
                for scr, x in zip(order_scrs, slabs):
                    scr[level, r * cls:(r + 1) * cls, :] = x
        prev = dil


def _inproj_kernel(x_ref, g_ref, gv_ref, w_ref, *refs):
    n_cfg = len(DILATED_CONFIGS)
    q_refs, k_refs, v_refs = refs[0:n_cfg], refs[n_cfg:2 * n_cfg], refs[2 * n_cfg:3 * n_cfg]
    act_ref = refs[3 * n_cfg]
    order_scrs = refs[3 * n_cfg + 1:]
    h = _rms(x_ref[0], g_ref[...]).astype(BF16)

    n_slabs = MXU_COLS // LANES
    at_scr = 0
    for t, (out_refs, scale) in enumerate(((q_refs, HEAD_DIM ** -0.5 * LOG2_E), (k_refs, None), (v_refs, None))):
        for piece in range(ATTN_WIDTH // MXU_COLS):
            cols = slice(t * ATTN_WIDTH + piece * MXU_COLS, t * ATTN_WIDTH + (piece + 1) * MXU_COLS)
            for r0 in range(0, h.shape[0], EMIT_ROWS):
                val = jnp.dot(h[r0:r0 + EMIT_ROWS], w_ref[:, cols], preferred_element_type=F32)
                if scale is not None:
                    val = val * scale
                _emit_orders(val, r0, out_refs, piece * MXU_COLS, order_scrs[at_scr:at_scr + n_slabs])
                at_scr = (at_scr + n_slabs) % len(order_scrs)
    for r0 in range(0, h.shape[0], ACT_ROWS):
        hp = h[r0:r0 + ACT_ROWS]
        lo = 3 * ATTN_WIDTH
        raw = []
        for width in (ATTN_WIDTH, SGU_WIDTH, SGU_WIDTH, SGU_WIDTH, MEM_WIDTH, MEM_WIDTH):
            raw.append(jnp.dot(hp, w_ref[:, lo:lo + width], preferred_element_type=F32))
            lo += width
        za, ub, vb, zb, qm, zm = raw
        acts = (_silu(za), _silu(zb) * jax.nn.gelu(ub), _rms(jax.nn.gelu(vb), gv_ref[...]),
                qm * (MEM_HEAD_DIM ** -0.5 * LOG2_E), _silu(zm))
        lo = 0
        for width, val in zip(ACT_SPLITS, acts):
            act_ref[0, r0:r0 + ACT_ROWS, lo:lo + width] = val.astype(BF16)
            lo += width


def _inproj(x, g_norm, g_v, w_in):
    b, s, d = x.shape
    cols = w_in.shape[1]
    assert cols == 4 * ATTN_WIDTH + 3 * SGU_WIDTH + 2 * MEM_WIDTH
    tile = lambda w: pl.BlockSpec((1, ROW_TILE, w), lambda i, j: (i, j, 0))
    fixed = lambda i, j: (0, 0)
    dils = [dil for _, dil in DILATED_CONFIGS]
    assert all(EMIT_ROWS % (dil * BF16_ROWS) == 0 for dil in dils) and ROW_TILE % EMIT_ROWS == 0
    ordered = lambda w: [pl.BlockSpec((dil, ROW_TILE // dil, w), lambda i, j: (i, j, 0)) for dil in dils]
    shaped = lambda w: [jax.ShapeDtypeStruct((b * dil, s // dil, w), BF16) for dil in dils]
    widths = (ATTN_WIDTH, ATTN_WIDTH, ATTN_WIDTH)
    outs = pl.pallas_call(
        _inproj_kernel,
        grid=(b, s // ROW_TILE),
        in_specs=[
            tile(d),
            pl.BlockSpec((1, d), fixed),
            pl.BlockSpec((1, SGU_WIDTH), fixed),
            pl.BlockSpec((d, cols), fixed, pipeline_mode=pl.Buffered(1)),
        ],
        out_specs=[spec for w in widths for spec in ordered(w)] + [tile(ACT_WIDTH)],
        out_shape=[shape for w in widths for shape in shaped(w)] + [jax.ShapeDtypeStruct((b, s, ACT_WIDTH), BF16)],
        scratch_shapes=[pltpu.VMEM((len(dils) - 1, EMIT_ROWS, LANES), F32)] * ORDER_BUFFERS,
        compiler_params=pltpu.CompilerParams(
            dimension_semantics=("arbitrary", "arbitrary"), vmem_limit_bytes=VMEM_LIMIT_BYTES),
        name="inproj",
    )(x, g_norm, g_v, w_in)
    n = len(dils)
    q, k, v = ([o.reshape(b, s, -1) for o in outs[i * n:(i + 1) * n]] for i in range(3))
    return q, k, v, outs[3 * n]


SHIFT, EDGE, WHOLE = "shift", "edge", "whole"


def _bias_tile(kind, dilation, first_head):
    shape = (KEY_WINDOW, HEADS_PER_SLAB * Q_BLOCK)
    key = lax.broadcasted_iota(jnp.int32, shape, 0)
    col = lax.broadcasted_iota(jnp.int32, shape, 1)
    second = col >= Q_BLOCK
    qi = jnp.where(second, col - Q_BLOCK, col)
    if kind == SHIFT:
        rel = jnp.abs(qi + RADIUS - key)
        valid = rel <= RADIUS
    elif kind == WHOLE:
        rel = jnp.abs(qi - key)
        valid = rel <= RADIUS
    else:
        upper = key >= Q_BLOCK
        rel = jnp.abs(qi - jnp.where(upper, key - Q_BLOCK, key))
        valid = (rel <= RADIUS) & (upper == (qi >= RADIUS))
    head = (first_head + jnp.where(second, 1, 0)).astype(F32)
    slope = jnp.exp2(-8.0 * (head + 1.0) / N_ATTN_HEADS)
    dist = (rel * dilation).astype(F32)
    return jnp.where(valid, -slope * dist * LOG2_E, NEG_INF)


class _Block(NamedTuple):
    cfg: int
    tile: int
    q_rows: tuple
    k_rows: tuple
    out_rows: tuple
    out_stride: int


def _cfg_order():
    return sorted(range(len(DILATED_CONFIGS)), key=lambda c: -DILATED_CONFIGS[c][1])


def _attn_plan(seq):
    blocks, tiles = [], []
    order = _cfg_order()
    for n, c in enumerate(order):
        dil = DILATED_CONFIGS[c][1]
        nxt = DILATED_CONFIGS[order[n + 1]][1] if n + 1 < len(order) else 1
        step = dil // nxt
        cls = seq // dil
        if cls == Q_BLOCK:
            tiles.append((WHOLE, dil))
            whole = len(tiles) - 1
        else:
            tiles.append((SHIFT, dil))
            tiles.append((EDGE, dil))
            shift, edge = len(tiles) - 2, len(tiles) - 1
        for r in range(dil):
            base = r * cls
            sink = (r % nxt) * (seq // nxt) + r // nxt
            if cls == Q_BLOCK:
                blocks.append(_Block(c, whole, ((base, Q_BLOCK),), ((base, Q_BLOCK),), ((sink, Q_BLOCK),), step))
                continue
            last = cls - RADIUS
            blocks.append(_Block(c, edge, ((base, RADIUS), (base + last, RADIUS)),
                                 ((base, Q_BLOCK), (base + cls - Q_BLOCK, Q_BLOCK)),
                                 ((sink, RADIUS), (sink + step * last, RADIUS)), step))
            for j in range(cls // Q_BLOCK - 1):
                u = RADIUS + j * Q_BLOCK
                blocks.append(_Block(c, shift, ((base + u, Q_BLOCK),), ((base + j * Q_BLOCK, KEY_WINDOW),),
                                     ((sink + step * u, Q_BLOCK),), step))
    return blocks, tiles


def _rows(ref, lead, ranges, lanes=slice(None)):
    parts = [ref[lead + (pl.ds(s, n), lanes)] for s, n in ranges]
    return parts[0] if len(parts) == 1 else jnp.concatenate(parts, axis=0)


def _score_block(blk, lanes, q_ref, k_ref, bias, s_scr):
    q = _rows(q_ref, (0,), blk.q_rows, lanes)
    first = lax.broadcasted_iota(jnp.int32, (1, LANES), 1) < HEAD_DIM
    zeros = jnp.zeros_like(q)
    q2 = jnp.concatenate([jnp.where(first, q, zeros), jnp.where(first, zeros, q)], axis=0)
    k_win = _rows(k_ref, (0,), blk.k_rows, lanes)
    window = k_win.shape[0]
    s = lax.dot_general(k_win, q2, (((1,), (1,)), ((), ())), preferred_element_type=F32)
    s_scr[0, 0:window, :] = s + bias[0:window, :]


def _value_block(blk, lanes, zero, vt_scr, s_scr, prev, sink, a_ref):
    window = sum(n for _, n in blk.k_rows)
    ms, ps = [], []
    for h in range(HEADS_PER_SLAB):
        cols = slice(h * Q_BLOCK, (h + 1) * Q_BLOCK)
        s = s_scr[zero, 0:window, cols]
        m = jnp.max(s, axis=0, keepdims=True)
        ms.append(m)
        ps.append(jnp.exp2(s - m).astype(BF16))
    vt = jnp.concatenate([vt_scr[zero, :, s0:s0 + n] for s0, n in blk.k_rows], axis=1)
    ot = jnp.dot(vt, jnp.concatenate(ps, axis=1), preferred_element_type=F32)
    if prev is not None:
        o_prev, lse_prev = (_rows(ref, (), blk.q_rows) for ref in prev)
    last = prev is not None and sink is None
    if last:
        lse_back = lse_prev.T
    outs, lses, keeps = [], [], []
    for h in range(HEADS_PER_SLAB):
        cols = slice(h * Q_BLOCK, (h + 1) * Q_BLOCK)
        l = ot[LANES:LANES + 1, cols]
        lses.append(ms[h] + jnp.log2(l))
        scale = 1.0 / l
        if last:
            w = 1.0 / (1.0 + jnp.exp2(lse_back[h * HEAD_DIM:h * HEAD_DIM + 1] - lses[h]))
            scale = w * scale
            keeps.append(1.0 - w)
        outs.append(ot[h * HEAD_DIM:(h + 1) * HEAD_DIM, cols] * scale)

    def onto_rows(rows):
        return jnp.concatenate([jnp.broadcast_to(x, (HEAD_DIM, Q_BLOCK)) for x in rows], axis=0).T

    o = jnp.concatenate(outs, axis=0).T
    if last:
        o = o + onto_rows(keeps) * o_prev
    else:
        lse = onto_rows(lses)
        if prev is not None:
            top = jnp.maximum(lse, lse_prev)
            e, e_prev = jnp.exp2(lse - top), jnp.exp2(lse_prev - top)
            den = e + e_prev
            o = (e * o + e_prev * o_prev) * (1.0 / den)
            lse = top + jnp.log2(den)
    at = 0
    for start, n in blk.out_rows:
        dst = pl.ds(start, n) if blk.out_stride == 1 else pl.ds(start, n, stride=blk.out_stride)
        if sink is None:
            a_ref[0, dst, lanes] = o[at:at + n].astype(a_ref.dtype)
        else:
            sink[0][dst, :] = o[at:at + n]
            sink[1][dst, :] = lse[at:at + n]
        at += n


def _attn_kernel(*refs, seq):
    n_cfg = len(DILATED_CONFIGS)
    q_refs, k_refs, v_refs = refs[0:n_cfg], refs[n_cfg:2 * n_cfg], refs[2 * n_cfg:3 * n_cfg]
    a_ref, bias_scr = refs[3 * n_cfg:3 * n_cfg + 2]
    scrs = refs[3 * n_cfg + 2:]
    n_vt, n_hand = SLABS_PER_STEP * n_cfg, SLABS_PER_STEP * (n_cfg - 1)
    vt_scrs = [scrs[p * n_cfg:(p + 1) * n_cfg] for p in range(SLABS_PER_STEP)]
    hand = [[scrs[n_vt + 2 * (p * (n_cfg - 1) + i):n_vt + 2 * (p * (n_cfg - 1) + i) + 2] for i in range(n_cfg - 1)]
            for p in range(SLABS_PER_STEP)]
    s_scrs = scrs[n_vt + 2 * n_hand:]
    blocks, tiles = _attn_plan(seq)
    order = _cfg_order()
    slabs = [slice(p * LANES, (p + 1) * LANES) for p in range(SLABS_PER_STEP)]

    @pl.when(pl.program_id(1) == 0)
    def _():
        for p in range(SLABS_PER_STEP):
            first_head = (pl.program_id(0) * SLABS_PER_STEP + p) * HEADS_PER_SLAB
            for t, (kind, dil) in enumerate(tiles):
                bias_scr[p * len(tiles) + t] = _bias_tile(kind, dil, first_head)

    for p in range(SLABS_PER_STEP):
        for c in order:
            vt_scrs[p][c][0, LANES:LANES + BF16_ROWS, :] = _ones_rows(seq)
            for i in range(seq // LANES):
                chunk = slice(i * LANES, (i + 1) * LANES)
                vt_scrs[p][c][0, 0:LANES, chunk] = v_refs[c][0, chunk, slabs[p]].T

    zero = jnp.minimum(pl.program_id(1), 0)
    work = [(p, blk) for blk in blocks for p in range(SLABS_PER_STEP)]
    for t in range(len(work) + SCORE_LAG):
        if t < len(work):
            p, blk = work[t]
            _score_block(blk, slabs[p], q_refs[blk.cfg], k_refs[blk.cfg], bias_scr.at[p * len(tiles) + blk.tile],
                         s_scrs[t % SCORE_SLOTS])
        if t >= SCORE_LAG:
            u = t - SCORE_LAG
            p, blk = work[u]
            n = order.index(blk.cfg)
            _value_block(blk, slabs[p], zero, vt_scrs[p][blk.cfg], s_scrs[u % SCORE_SLOTS],
                         hand[p][n - 1] if n > 0 else None, hand[p][n] if n + 1 < n_cfg else None, a_ref)


def _attn(q, k, v):
    b, s, w = k[0].shape
    n_cfg = len(DILATED_CONFIGS)
    assert all(s % (dil * Q_BLOCK) == 0 for _, dil in DILATED_CONFIGS)
    n_tiles = len(_attn_plan(s)[1])
    blk = pl.BlockSpec((1, s, SLABS_PER_STEP * LANES), lambda j, i: (i, 0, j))
    return pl.pallas_call(
        functools.partial(_attn_kernel, seq=s),
        grid=(w // (SLABS_PER_STEP * LANES), b),
        in_specs=[blk] * (3 * n_cfg),
        out_specs=blk,
        out_shape=jax.ShapeDtypeStruct((b, s, w), BF16),
        scratch_shapes=[
            pltpu.VMEM((SLABS_PER_STEP * n_tiles, KEY_WINDOW, HEADS_PER_SLAB * Q_BLOCK), F32),
        ] + [pltpu.VMEM((1, LANES + BF16_ROWS, s), BF16)] * (SLABS_PER_STEP * n_cfg)
        + [pltpu.VMEM((s, LANES), F32)] * (2 * SLABS_PER_STEP * (n_cfg - 1))
        + [pltpu.VMEM((1, KEY_WINDOW, HEADS_PER_SLAB * Q_BLOCK), F32)] * SCORE_SLOTS,
        compiler_params=pltpu.CompilerParams(
            dimension_semantics=("arbitrary", "arbitrary"), vmem_limit_bytes=VMEM_LIMIT_BYTES),
        name="attn",
    )(*q, *k, *v)


def _tile_copies(hbms, rings, sems, step, tiles_per_batch):
    slot = step % X_RING
    copies = []
    for hbm, ring, sem in zip(hbms, rings, sems):
        tm = ring.shape[1]
        src = hbm.at[step // tiles_per_batch, pl.ds((step % tiles_per_batch) * tm, tm), :]
        copies.append(pltpu.make_async_copy(src, ring.at[slot], sem.at[slot]))
    return copies


def _mix_kernel(a_hbm, act_hbm, mem_ref, gm_ref, wkv_ref, x_hbm, ws_ref, bs_ref, wo_ref, gf_ref, o_ref,
                km_ref, vtm_ref, a_ring, act_ring, x_ring, a_sem, act_sem, x_sem):
    n_i, n_j = pl.num_programs(0), pl.num_programs(1)
    step = pl.program_id(0) * n_j + pl.program_id(1)
    ahead = X_RING - 1
    streams = ((a_hbm, act_hbm, x_hbm), (a_ring, act_ring, x_ring), (a_sem, act_sem, x_sem))

    @pl.when(step == 0)
    def _():
        for s in range(ahead):
            for copy in _tile_copies(*streams, s, n_j):
                copy.start()

    @pl.when(step + ahead < n_i * n_j)
    def _():
        for copy in _tile_copies(*streams, step + ahead, n_j):
            copy.start()

    for copy in _tile_copies(*streams, step, n_j):
        copy.wait()
    a_ref, act_ref, x_ref = (ring.at[pl.ds(step % X_RING, 1)] for ring in (a_ring, act_ring, x_ring))

    @pl.when(pl.program_id(1) == 0)
    def _():
        hm = _rms(mem_ref[0], gm_ref[...]).astype(BF16)
        kv = jnp.dot(hm, wkv_ref[...], preferred_element_type=F32)
        km_ref[...] = kv[:, 0:MEM_WIDTH].astype(BF16)
        vtm_ref[0:MEM_WIDTH, :] = kv[:, MEM_WIDTH:].T.astype(BF16)
        vtm_ref[MEM_WIDTH:, :] = _ones_rows(kv.shape[0])

    tm = x_ref.shape[1]
    parts = [pl.ds(lo, tm // BRANCH_PARTS) for lo in range(0, tm, tm // BRANCH_PARTS)]
    gated = _mix_branches(parts, a_ref, act_ref, km_ref, vtm_ref, ws_ref, bs_ref)
    for lo in range(0, tm, PROJECT_ROWS):
        _mix_project(pl.ds(lo, PROJECT_ROWS), [g[lo:lo + PROJECT_ROWS] for g in gated], x_ref, wo_ref, gf_ref, o_ref)


def _mix_branches(parts, a_ref, act_ref, km_ref, vtm_ref, ws_ref, bs_ref):
    tm = parts[0].size
    lo = 0
    cols = []
    for width in ACT_SPLITS:
        cols.append(slice(lo, lo + width))
        lo += width
    acts = [[act_ref[0, rows, c] for c in cols] for rows in parts]

    mhead = lax.broadcasted_iota(jnp.int32, (tm, MEM_WIDTH), 1) // MEM_HEAD_DIM
    scores = []
    for _, _, _, qm, _ in acts:
        q4 = jnp.concatenate([jnp.where(mhead == h, qm, jnp.zeros_like(qm)) for h in range(N_MEM_HEADS)], axis=0)
        scores.append(lax.dot_general(km_ref[...], q4, (((1,), (1,)), ((), ())), preferred_element_type=F32))
    ps = [jnp.exp2(s - jnp.max(s, axis=0, keepdims=True)).astype(BF16) for s in scores]
    ones = vtm_ref[MEM_WIDTH:, :]
    mos = [[] for _ in parts]
    for h in range(N_MEM_HEADS):
        vt = jnp.concatenate([vtm_ref[h * MEM_HEAD_DIM:(h + 1) * MEM_HEAD_DIM, :], ones], axis=0)
        for mo, p in zip(mos, ps):
            ot = jnp.dot(vt, p[:, h * tm:(h + 1) * tm], preferred_element_type=F32)
            mo.append(ot[0:MEM_HEAD_DIM] * (1.0 / ot[MEM_HEAD_DIM:MEM_HEAD_DIM + 1]))
    gm = [(act[4].astype(F32) * jnp.concatenate(mo, axis=0).T).astype(BF16) for act, mo in zip(acts, mos)]

    group = lax.broadcasted_iota(jnp.int32, (SGU_CHUNK, SGU_WIDTH), 1) // SGU_GROUP
    gb = []
    for _, gu, vn, _, _ in acts:
        mixed = []
        for c in range(tm // SGU_CHUNK):
            vc = vn[c * SGU_CHUNK:(c + 1) * SGU_CHUNK]
            stacked = jnp.concatenate([jnp.where(group == g, vc, jnp.zeros_like(vc)) for g in range(N_SGU_GROUPS)],
                                      axis=0)
            mixed.append(jnp.dot(ws_ref[...], stacked, preferred_element_type=F32) + bs_ref[...])
        gb.append((gu.astype(F32) * jnp.concatenate(mixed, axis=0)).astype(BF16))

    ga = [act[0] * a_ref[0, rows, :] for act, rows in zip(acts, parts)]
    return [jnp.concatenate(x, axis=0) for x in (ga, gb, gm)]


def _mix_project(rows, gated, x_ref, wo_ref, gf_ref, o_ref):
    ga, gb, gm = gated
    e1 = ATTN_WIDTH
    e2 = e1 + SGU_WIDTH
    y = jnp.dot(ga, wo_ref[0:e1, :], preferred_element_type=F32)
    y = y + jnp.dot(gb, wo_ref[e1:e2, :], preferred_element_type=F32)
    y = y + jnp.dot(gm, wo_ref[e2:, :], preferred_element_type=F32)
    o_ref[0, rows, :] = _rms(x_ref[0, rows, :] + y, gf_ref[...])


def _mix(a, act, mem, g_mem, w_kv, x, w_s, b_tile, w_out, g_final):
    b, s, d = x.shape
    m = mem.shape[1]
    assert b * (s // MIX_TILE) >= X_RING - 1
    tile = lambda w: pl.BlockSpec((1, MIX_TILE, w), lambda i, j: (i, j, 0))
    fixed = lambda arr: pl.BlockSpec(arr.shape, lambda i, j: (0, 0))
    streamed = pl.BlockSpec(memory_space=pl.ANY)
    return pl.pallas_call(
        _mix_kernel,
        grid=(b, s // MIX_TILE),
        in_specs=[streamed, streamed, pl.BlockSpec((1, m, d), lambda i, j: (i, 0, 0)),
                  fixed(g_mem), fixed(w_kv), streamed, fixed(w_s), fixed(b_tile), fixed(w_out), fixed(g_final)],
        out_specs=tile(d),
        out_shape=jax.ShapeDtypeStruct((b, s, d), x.dtype),
        scratch_shapes=[
            pltpu.VMEM((m, MEM_WIDTH), BF16),
            pltpu.VMEM((MEM_WIDTH + BF16_ROWS, m), BF16),
        ] + [pltpu.VMEM((X_RING, MIX_TILE, arr.shape[2]), arr.dtype) for arr in (a, act, x)]
        + [pltpu.SemaphoreType.DMA((X_RING,))] * 3,
        compiler_params=pltpu.CompilerParams(
            dimension_semantics=("arbitrary", "arbitrary"), vmem_limit_bytes=VMEM_LIMIT_BYTES),
        name="mix",
    )(a, act, mem, g_mem, w_kv, x, w_s, b_tile, w_out, g_final)


def kernel(x, mem, g_norm, w_in, w_sgu_spatial, b_sgu_spatial, g_sgu_v, g_mem, w_mem_kv, w_out, g_final):
    assert g_norm.shape[0] == 1, "the final norm is fused into the single layer's last kernel"
    b, s, d = x.shape
    q, k, v, act = _inproj(x, g_norm[0][None, :], g_sgu_v[0][None, :], w_in[0].astype(BF16))
    a = _attn(q, k, v)
    w_s = jnp.concatenate(list(w_sgu_spatial[0].astype(BF16)), axis=1)
    b_tile = jnp.repeat(b_sgu_spatial[0].T, SGU_GROUP, axis=1)
    return _mix(a, act, mem, g_mem[0][None, :], w_mem_kv[0].astype(BF16), x, w_s, b_tile, w_out[0].astype(BF16),
                g_final[None, :])
```
